```python
import math
import jax
import jax.numpy as jnp
from jax import lax
import numpy as np

D_MODEL = 1024
BATCH = 32
SEQ = 256
DEPTH = 2
DEC_BATCH = 8
DEC_SEQ = 1024
PAST_LEN = 256

GRID_W = 64
HEAD_DIM = 64
A_HEADS = 8
A_KV_HEADS = 2
A_GROUP = A_HEADS // A_KV_HEADS
A_WIDTH = A_HEADS * HEAD_DIM
B_HEADS = 4
B_QK_DIM = 32
B_V_DIM = 64
B_WIDTH = B_HEADS * B_V_DIM
C_HEADS = 4
C_DK = 32
C_DV = 64
C_WIDTH = C_HEADS * C_DV
GATE_RANK = 16
GLA_TAU = 16.0
CHUNK = 64
D_MIX = A_WIDTH + B_WIDTH + C_WIDTH
Q_BLOCK = 128
ROPE_THETA = 10000.0
EPS = 1e-6
IN_SIZES = (A_WIDTH, A_KV_HEADS * HEAD_DIM, A_KV_HEADS * HEAD_DIM,
            B_HEADS * 2 * B_QK_DIM, B_HEADS * 2 * B_QK_DIM, B_WIDTH,
            C_HEADS * C_DK, C_HEADS * C_DK, C_WIDTH, 2 * GATE_RANK, D_MIX)
IN_WIDTH = sum(IN_SIZES)

kernel_name = 'hybrid_diffusion_parallel_heads_step'


def rms_norm(x, g):
    xf = x.astype(jnp.float32)
    y = xf * lax.rsqrt(jnp.mean(xf * xf, axis=-1, keepdims=True) + EPS)
    return (y * g.astype(jnp.float32)).astype(x.dtype)


def split_heads(t, n_heads):
    b, l, _ = t.shape
    return t.reshape(b, l, n_heads, -1).transpose(0, 2, 1, 3)


def merge_heads(t):
    b, n, l, d = t.shape
    return t.transpose(0, 2, 1, 3).reshape(b, l, n * d)


def rope_1d(x, pos):
    half = x.shape[-1] // 2
    freq = ROPE_THETA ** (-jnp.arange(half, dtype=jnp.float32) / half)
    ang = pos.astype(jnp.float32)[:, None] * freq[None, :]
    cos, sin = jnp.cos(ang), jnp.sin(ang)
    xf = x.astype(jnp.float32)
    x1, x2 = xf[..., :half], xf[..., half:]
    return jnp.concatenate([x1 * cos - x2 * sin, x2 * cos + x1 * sin], axis=-1)


def rope_2d(x, pos_row, pos_col):
    d = x.shape[-1] // 2
    return jnp.concatenate([rope_1d(x[..., :d], pos_row), rope_1d(x[..., d:], pos_col)], axis=-1).astype(x.dtype)


def sweep_query_blocks(attend, q):
    lead = q.shape[:-2]
    lq, d = q.shape[-2], q.shape[-1]
    nb = lq // Q_BLOCK
    qb = jnp.moveaxis(q.reshape(*lead, nb, Q_BLOCK, d), -3, 0)
    ob = lax.map(attend, qb)
    return jnp.moveaxis(ob, 0, -3).reshape(*lead, lq, ob.shape[-1])


def gqa_attention(q, k, v):
    scale = q.shape[-1] ** -0.5

    def attend(qb):
        s = jnp.einsum('bkgqd,bksd->bkgqs', qb, k).astype(jnp.float32) * scale
        p = jax.nn.softmax(s, axis=-1).astype(v.dtype)
        return jnp.einsum('bkgqs,bksd->bkgqd', p, v)

    return sweep_query_blocks(attend, q)


def diff_attention(q, k, v, lam):
    scale = B_QK_DIM ** -0.5
    k1, k2 = k[..., :B_QK_DIM], k[..., B_QK_DIM:]
    lam32 = lam.astype(jnp.float32)

    def attend(qb):
        s1 = jnp.einsum('bhqd,bhsd->bhqs', qb[..., :B_QK_DIM], k1).astype(jnp.float32) * scale
        s2 = jnp.einsum('bhqd,bhsd->bhqs', qb[..., B_QK_DIM:], k2).astype(jnp.float32) * scale
        a = jax.nn.softmax(s1, axis=-1) - lam32 * jax.nn.softmax(s2, axis=-1)
        return jnp.einsum('bhqs,bhsd->bhqd', a.astype(v.dtype), v)

    return sweep_query_blocks(attend, q)


def gla_scan(q, k, v, g, s0):
    b, h, l, _ = q.shape
    n = l // CHUNK

    def to_chunks(t):
        return jnp.moveaxis(t.reshape(b, h, n, CHUNK, t.shape[-1]), 2, 0)

    mask = jnp.tril(jnp.ones((CHUNK, CHUNK), dtype=bool))

    def step(s, inp):
        qc, kc, vc, gc = inp
        cum = jnp.cumsum(gc, axis=-2)
        q_t = qc * jnp.exp(cum)
        k_t = kc * jnp.exp(-cum)
        a = jnp.where(mask, jnp.einsum('bhid,bhjd->bhij', q_t, k_t), 0.0)
        o = jnp.einsum('bhid,bhde->bhie', q_t, s) + jnp.einsum('bhij,bhje->bhie', a, vc)
        last = cum[..., -1:, :]
        s_new = jnp.exp(last)[..., 0, :, None] * s + jnp.einsum('bhjd,bhje->bhde', kc * jnp.exp(last - cum), vc)
        return s_new, o

    s_fin, o = lax.scan(step, s0, (to_chunks(q), to_chunks(k), to_chunks(v), to_chunks(g)))
    o = jnp.moveaxis(o, 0, 2).reshape(b, h, l, v.shape[-1])
    return o, s_fin


def gla_bidirectional(q, k, v, g_f, g_b, s_f0, s_b0):
    o_f, s_f = gla_scan(q, k, v, g_f, s_f0)
    flip = lambda t: jnp.flip(t, axis=2)
    o_b, s_b = gla_scan(flip(q), flip(k), flip(v), flip(g_b), s_b0)
    return o_f + flip(o_b), s_f, s_b


def trunk_layer(x, mod, lp, lam, lam_init, pos, ctx):
    (g_pre, g_post, w_in, w_out, a_q_gain, a_k_gain, b_out_gain,
     c_w_f, c_b_f, c_w_b, c_b_b, c_out_gain) = lp
    bsz, l, _ = x.shape
    f32 = jnp.float32
    shift, scale, gate = jnp.split(mod, 3, axis=-1)
    h = rms_norm(x, g_pre) * (1.0 + scale[:, None, :]) + shift[:, None, :]
    z = h @ w_in
    (a_q, a_k, a_v, b_q, b_k, b_v, c_q, c_k, c_v, c_lr, u) = jnp.split(
        z, np.cumsum(IN_SIZES)[:-1].tolist(), axis=-1)

    a_q = rms_norm(split_heads(a_q, A_HEADS), a_q_gain)
    a_k = rms_norm(split_heads(a_k, A_KV_HEADS), a_k_gain)
    a_v = split_heads(a_v, A_KV_HEADS)
    b_q = split_heads(b_q, B_HEADS)
    b_k = split_heads(b_k, B_HEADS)
    b_v = split_heads(b_v, B_HEADS)
    c_q = split_heads(c_q, C_HEADS).astype(f32) * (C_DK ** -0.5)
    c_k = split_heads(c_k, C_HEADS).astype(f32)
    c_v = split_heads(c_v, C_HEADS).astype(f32)
    g_f = split_heads(jax.nn.log_sigmoid((c_lr[..., :GATE_RANK] @ c_w_f + c_b_f).astype(f32)) / GLA_TAU, C_HEADS)
    g_b = split_heads(jax.nn.log_sigmoid((c_lr[..., GATE_RANK:] @ c_w_b + c_b_b).astype(f32)) / GLA_TAU, C_HEADS)

    if ctx is None:
        a_k_all, a_v_all, b_k_all, b_v_all = a_k, a_v, b_k, b_v
        s_f0 = jnp.zeros((bsz, C_HEADS, C_DK, C_DV), f32)
        s_b0 = jnp.zeros((bsz, C_HEADS, C_DK, C_DV), f32)
    else:
        pos_row, pos_col = pos
        ck_a, cv_a, ck_b, cv_b, cs_f, cs_b = ctx
        a_q = rope_2d(a_q, pos_row, pos_col)
        a_k = rope_2d(a_k, pos_row, pos_col)
        b_q = jnp.concatenate([rope_2d(b_q[..., :B_QK_DIM], pos_row, pos_col),
                               rope_2d(b_q[..., B_QK_DIM:], pos_row, pos_col)], axis=-1)
        b_k = jnp.concatenate([rope_2d(b_k[..., :B_QK_DIM], pos_row, pos_col),
                               rope_2d(b_k[..., B_QK_DIM:], pos_row, pos_col)], axis=-1)
        a_k_all = jnp.concatenate([ck_a.astype(a_k.dtype), a_k], axis=2)
        a_v_all = jnp.concatenate([cv_a.astype(a_v.dtype), a_v], axis=2)
        b_k_all = jnp.concatenate([ck_b.astype(b_k.dtype), b_k], axis=2)
        b_v_all = jnp.concatenate([cv_b.astype(b_v.dtype), b_v], axis=2)
        s_f0 = cs_f.astype(f32)
        s_b0 = cs_b.astype(f32)

    o_a = gqa_attention(a_q.reshape(bsz, A_KV_HEADS, A_GROUP, l, HEAD_DIM), a_k_all, a_v_all)
    o_a = o_a.reshape(bsz, A_HEADS, l, HEAD_DIM)
    o_b = rms_norm(diff_attention(b_q, b_k_all, b_v_all, lam), b_out_gain) * (1.0 - lam_init)
    o_c, s_f, s_b = gla_bidirectional(c_q, c_k, c_v, g_f, g_b, s_f0, s_b0)
    o_c = rms_norm(o_c.astype(x.dtype), c_out_gain)

    mixed = jnp.concatenate([merge_heads(o_a), merge_heads(o_b), merge_heads(o_c)], axis=-1) * jax.nn.silu(u)
    y = mixed @ w_out
    x_new = x + gate[:, None, :] * rms_norm(y, g_post)
    if ctx is None:
        return x_new, (a_k, a_v, b_k, b_v, s_f.astype(x.dtype), s_b.astype(x.dtype))
    return x_new, None


def setup_inputs(seed: int = 0) -> dict:
    key = jax.random.key(seed)
    ks = jax.random.split(key, 28)
    f32 = jnp.float32

    def nrm(k, shape, s):
        return jax.random.normal(k, shape, f32) * s

    return {
        'x_prompt': nrm(ks[0], (BATCH, SEQ, D_MODEL), 1.0),
        'x_sample': nrm(ks[1], (DEC_BATCH, DEC_SEQ, D_MODEL), 1.0),
        'c': nrm(ks[2], (DEC_BATCH, D_MODEL), 1.0),
        'cache_a_k': nrm(ks[3], (DEC_BATCH, DEPTH, A_KV_HEADS, PAST_LEN, HEAD_DIM), 1.0),
        'cache_a_v': nrm(ks[4], (DEC_BATCH, DEPTH, A_KV_HEADS, PAST_LEN, HEAD_DIM), 1.0),
        'cache_b_k': nrm(ks[5], (DEC_BATCH, DEPTH, B_HEADS, PAST_LEN, 2 * B_QK_DIM), 1.0),
        'cache_b_v': nrm(ks[6], (DEC_BATCH, DEPTH, B_HEADS, PAST_LEN, B_V_DIM), 1.0),
        'state_c_fwd': nrm(ks[7], (DEC_BATCH, DEPTH, C_HEADS, C_DK, C_DV), 1.0),
        'state_c_bwd': nrm(ks[8], (DEC_BATCH, DEPTH, C_HEADS, C_DK, C_DV), 1.0),
        'c_ctx': nrm(ks[9], (D_MODEL,), 1.0),
        'w_mod': nrm(ks[10], (DEPTH, D_MODEL, 3 * D_MODEL), D_MODEL ** -0.5),
        'b_mod': nrm(ks[11], (DEPTH, 3 * D_MODEL), 0.02),
        'g_pre': 1.0 + nrm(ks[12], (DEPTH, D_MODEL), 0.02),
        'g_post': 1.0 + nrm(ks[13], (DEPTH, D_MODEL), 0.02),
        'w_in': nrm(ks[14], (DEPTH, D_MODEL, IN_WIDTH), D_MODEL ** -0.5),
        'w_out': nrm(ks[15], (DEPTH, D_MIX, D_MODEL), D_MIX ** -0.5),
        'a_q_gain': 1.0 + nrm(ks[16], (DEPTH, HEAD_DIM), 0.02),
        'a_k_gain': 1.0 + nrm(ks[17], (DEPTH, HEAD_DIM), 0.02),
        'b_lambda_q1': nrm(ks[18], (DEPTH, B_QK_DIM), 0.1),
        'b_lambda_k1': nrm(ks[19], (DEPTH, B_QK_DIM), 0.1),
        'b_lambda_q2': nrm(ks[20], (DEPTH, B_QK_DIM), 0.1),
        'b_lambda_k2': nrm(ks[21], (DEPTH, B_QK_DIM), 0.1),
        'b_out_gain': 1.0 + nrm(ks[22], (DEPTH, B_V_DIM), 0.02),
        'c_gate_w_fwd': nrm(ks[23], (DEPTH, GATE_RANK, C_HEADS * C_DK), GATE_RANK ** -0.5),
        'c_gate_b_fwd': nrm(ks[24], (DEPTH, C_HEADS * C_DK), 0.02),
        'c_gate_w_bwd': nrm(ks[25], (DEPTH, GATE_RANK, C_HEADS * C_DK), GATE_RANK ** -0.5),
        'c_gate_b_bwd': nrm(ks[26], (DEPTH, C_HEADS * C_DK), 0.02),
        'c_out_gain': 1.0 + nrm(ks[27], (DEPTH, C_DV), 0.02),
    }


def reference(x_prompt, x_sample, c, cache_a_k, cache_a_v, cache_b_k, cache_b_v, state_c_fwd, state_c_bwd,
              c_ctx, w_mod, b_mod, g_pre, g_post, w_in, w_out, a_q_gain, a_k_gain,
              b_lambda_q1, b_lambda_k1, b_lambda_q2, b_lambda_k2, b_out_gain,
              c_gate_w_fwd, c_gate_b_fwd, c_gate_w_bwd, c_gate_b_bwd, c_out_gain):
    rows = x_sample.shape[1] // GRID_W
    t = jnp.arange(rows * GRID_W)
    pos = (t // GRID_W, t % GRID_W)
    lam_base = (jnp.exp(jnp.sum(b_lambda_q1.astype(jnp.float32) * b_lambda_k1.astype(jnp.float32), axis=-1))
                - jnp.exp(jnp.sum(b_lambda_q2.astype(jnp.float32) * b_lambda_k2.astype(jnp.float32), axis=-1)))

    y_p, y_s = x_prompt, x_sample
    ctx_layers = []
    for l in range(DEPTH):
        lam_init = 0.8 - 0.6 * math.exp(-0.3 * l)
        lam = lam_base[l] + lam_init
        lp = (g_pre[l], g_post[l], w_in[l], w_out[l], a_q_gain[l], a_k_gain[l], b_out_gain[l],
              c_gate_w_fwd[l], c_gate_b_fwd[l], c_gate_w_bwd[l], c_gate_b_bwd[l], c_out_gain[l])
        mod_ctx = (jax.nn.silu(c_ctx) @ w_mod[l] + b_mod[l])[None, :]
        mod_lat = jax.nn.silu(c) @ w_mod[l] + b_mod[l]
        y_p, ctx_l = trunk_layer(y_p, mod_ctx, lp, lam, lam_init, None, None)
        ctx_cache = (cache_a_k[:, l], cache_a_v[:, l], cache_b_k[:, l], cache_b_v[:, l],
                     state_c_fwd[:, l], state_c_bwd[:, l])
        y_s, _ = trunk_layer(y_s, mod_lat, lp, lam, lam_init, pos, ctx_cache)
        ctx_layers.append(ctx_l)

    new_a_k, new_a_v, new_b_k, new_b_v, new_state_c_fwd, new_state_c_bwd = [
        jnp.stack(ts, axis=1) for ts in zip(*ctx_layers)]
    return (y_p, y_s, new_a_k, new_a_v, new_b_k, new_b_v, new_state_c_fwd, new_state_c_bwd)
```

```python
import functools
import math

import jax
import jax.numpy as jnp
from jax import lax
from jax.experimental import pallas as pl
from jax.experimental.pallas import tpu as pltpu

F32 = jnp.float32
BF16 = jnp.bfloat16

D_MODEL = 1024
DEPTH = 2
GRID_W = 64
HEAD_DIM = 64
A_HEADS = 8
A_KV_HEADS = 2
A_GROUP = A_HEADS // A_KV_HEADS
A_WIDTH = A_HEADS * HEAD_DIM
A_KV_WIDTH = A_KV_HEADS * HEAD_DIM
B_HEADS = 4
B_QK_DIM = 32
B_V_DIM = 64
B_WIDTH = B_HEADS * B_V_DIM
C_HEADS = 4
C_DK = 32
C_DV = 64
C_KW = C_HEADS * C_DK
C_WIDTH = C_HEADS * C_DV
GATE_RANK = 16
GLA_TAU = 16.0
CHUNK = 64
D_MIX = A_WIDTH + B_WIDTH + C_WIDTH
ROPE_THETA = 10000.0
EPS = 1e-6

LANES = 128
GLA_BLOCK = 256
LR_PAD = LANES

OFF_AQ = 0
OFF_AK = OFF_AQ + A_WIDTH
OFF_AV = OFF_AK + A_KV_WIDTH
OFF_BQ = OFF_AV + A_KV_WIDTH
OFF_BK = OFF_BQ + B_WIDTH
OFF_BV = OFF_BK + B_WIDTH
OFF_CQ = OFF_BV + B_WIDTH
OFF_CK = OFF_CQ + C_KW
OFF_CV = OFF_CK + C_KW
OFF_U = OFF_CV + C_WIDTH
OFF_LR = OFF_U + D_MIX
IN_PAD_WIDTH = OFF_LR + LR_PAD

VMEM_LIMIT = 56 * 1024 * 1024


def _dot(a, b):
    return jnp.dot(a, b, preferred_element_type=F32)


def _dot_nt(a, b):
    return lax.dot_general(a, b, (((1,), (1,)), ((), ())), preferred_element_type=F32)


def _dot_tn(a, b):
    return lax.dot_general(a, b, (((0,), (0,)), ((), ())), preferred_element_type=F32)


def _split_bf16(x):
    hi = x.astype(BF16)
    lo = (x - hi.astype(F32)).astype(BF16)
    return hi, lo


def _iota(shape, dim):
    return lax.broadcasted_iota(jnp.int32, shape, dim)


def _group_mean_sq(x, group_log2):
    width = x.shape[-1]
    r = lax.shift_right_logical(_iota((LANES, LANES), 0), group_log2)
    c = lax.shift_right_logical(_iota((LANES, LANES), 1), group_log2)
    ones = jnp.where(r == c, 1.0, 0.0).astype(BF16)
    hi, lo = _split_bf16(x * x)
    cols = []
    for j in range(width // LANES):
        sl = slice(LANES * j, LANES * (j + 1))
        cols.append(_dot(hi[:, sl], ones) + _dot(lo[:, sl], ones))
    ss = cols[0] if len(cols) == 1 else jnp.concatenate(cols, axis=-1)
    return ss * (1.0 / (1 << group_log2))


def _rope(x, cos, sin_signed, dist):
    width = x.shape[-1]
    lane = _iota(x.shape, 1)
    first = (lane & (2 * dist - 1)) < dist
    up = pltpu.roll(x, width - dist, 1)
    down = pltpu.roll(x, dist, 1)
    return x * cos + jnp.where(first, up, down) * sin_signed


def _log_sigmoid(x):
    return jnp.minimum(x, 0.0) - jnp.log1p(jnp.exp(-jnp.abs(x)))


def _silu(x):
    return x * (1.0 / (1.0 + jnp.exp(-x)))


def _mod_kernel(c_ref, w_ref, b_ref, o_ref):
    a = _silu(c_ref[...]).astype(BF16)
    o_ref[...] = _dot(a, w_ref[...].astype(BF16)) + b_ref[...]


def _modulation(cvec, w_mod, b_mod):
    rows = cvec.shape[0]
    nblk = 3
    return pl.pallas_call(
        _mod_kernel,
        grid=(DEPTH, nblk),
        in_specs=[
            pl.BlockSpec((rows, D_MODEL), lambda l, n: (0, 0)),
            pl.BlockSpec((None, D_MODEL, D_MODEL), lambda l, n: (l, 0, n)),
            pl.BlockSpec((None, 1, D_MODEL), lambda l, n: (l, 0, n)),
        ],
        out_specs=pl.BlockSpec((None, rows, D_MODEL), lambda l, n: (l, 0, n)),
        out_shape=jax.ShapeDtypeStruct((DEPTH, rows, 3 * D_MODEL), F32),
        compiler_params=pltpu.CompilerParams(
            dimension_semantics=("arbitrary", "arbitrary"), vmem_limit_bytes=VMEM_LIMIT),
        name="modulation",
    )(cvec, w_mod, b_mod.reshape(DEPTH, 1, 3 * D_MODEL))


def _in_kernel(rope, kv_dtype, *refs):
    if rope:
        (x_ref, mod_ref, gpre_ref, w_ref, aqg_ref, akg_ref, cwf_ref, cbf_ref, cwb_ref, cbb_ref,
         cosa_ref, sina_ref, cosb_ref, sinb_ref,
         qa_ref, ka_ref, va_ref, qb_ref, kb_ref, vb_ref,
         cq_ref, ck_ref, cv_ref, gf_ref, gb_ref, su_ref) = refs
    else:
        (x_ref, mod_ref, gpre_ref, w_ref, aqg_ref, akg_ref, cwf_ref, cbf_ref, cwb_ref, cbb_ref,
         qa_ref, ka_ref, va_ref, qb_ref, kb_ref, vb_ref,
         cq_ref, ck_ref, cv_ref, gf_ref, gb_ref, su_ref) = refs

    x = x_ref[...]
    shift = mod_ref[:, 0:D_MODEL]
    scale = mod_ref[:, D_MODEL:2 * D_MODEL]
    ms = jnp.mean(x * x, axis=-1, keepdims=True)
    h = (x * lax.rsqrt(ms + EPS)) * gpre_ref[...] * (1.0 + scale) + shift
    hb = h.astype(BF16)

    def proj(off, width):
        return _dot(hb, w_ref[:, off:off + width])

    def store_heads(ref, val, n_heads, dtype):
        for hd in range(n_heads):
            ref[hd] = val[:, HEAD_DIM * hd:HEAD_DIM * (hd + 1)].astype(dtype)

    aq = proj(OFF_AQ, A_WIDTH)
    aq = aq * lax.rsqrt(_group_mean_sq(aq, 6) + EPS) * aqg_ref[...]
    ak = proj(OFF_AK, A_KV_WIDTH)
    ak = ak * lax.rsqrt(_group_mean_sq(ak, 6) + EPS) * akg_ref[...]
    if rope:
        aq = _rope(aq, cosa_ref[...], sina_ref[...], 16)
        ak = _rope(ak, cosa_ref[:, 0:A_KV_WIDTH], sina_ref[:, 0:A_KV_WIDTH], 16)
    store_heads(qa_ref, aq * (HEAD_DIM ** -0.5), A_HEADS, BF16)
    store_heads(ka_ref, ak, A_KV_HEADS, kv_dtype)
    store_heads(va_ref, proj(OFF_AV, A_KV_WIDTH), A_KV_HEADS, kv_dtype)

    bq = proj(OFF_BQ, B_WIDTH)
    bk = proj(OFF_BK, B_WIDTH)
    if rope:
        bq = _rope(bq, cosb_ref[...], sinb_ref[...], 8)
        bk = _rope(bk, cosb_ref[...], sinb_ref[...], 8)
    store_heads(qb_ref, bq * (B_QK_DIM ** -0.5), B_HEADS, BF16)
    store_heads(kb_ref, bk, B_HEADS, kv_dtype)
    store_heads(vb_ref, proj(OFF_BV, B_WIDTH), B_HEADS, kv_dtype)

    cq_ref[...] = proj(OFF_CQ, C_KW) * (C_DK ** -0.5)
    ck_ref[...] = proj(OFF_CK, C_KW)
    cv_ref[...] = proj(OFF_CV, C_WIDTH)
    lr = proj(OFF_LR, LR_PAD).astype(BF16)
    gf_ref[...] = _log_sigmoid(_dot(lr, cwf_ref[...]) + cbf_ref[...]) * (1.0 / GLA_TAU)
    gb_ref[...] = _log_sigmoid(_dot(lr, cwb_ref[...]) + cbb_ref[...]) * (1.0 / GLA_TAU)

    su_ref[...] = _silu(proj(OFF_U, D_MIX))


def _in_projection(x, mod, per_batch_mod, lw, rope_tabs, kv_dtype, tm):
    bsz, seq, _ = x.shape
    rope = rope_tabs is not None
    grid = (bsz, seq // tm)

    def const(shape):
        return pl.BlockSpec(shape, lambda b, t: (0,) * len(shape))

    mod_idx = (lambda b, t: (b, 0, 0)) if per_batch_mod else (lambda b, t: (0, 0, 0))
    in_specs = [
        pl.BlockSpec((None, tm, D_MODEL), lambda b, t: (b, t, 0)),
        pl.BlockSpec((None, 1, 3 * D_MODEL), mod_idx),
        const((1, D_MODEL)),
        const((D_MODEL, IN_PAD_WIDTH)),
        const((1, A_WIDTH)),
        const((1, A_KV_WIDTH)),
        const((LR_PAD, C_KW)),
        const((1, C_KW)),
        const((LR_PAD, C_KW)),
        const((1, C_KW)),
    ]
    args = [x, mod, lw["g_pre"], lw["w_in"], lw["aq_gain"], lw["ak_gain"],
            lw["cw_f"], lw["cb_f"], lw["cw_b"], lw["cb_b"]]
    if rope:
        in_specs += [
            pl.BlockSpec((tm, A_WIDTH), lambda b, t: (t, 0)),
            pl.BlockSpec((tm, A_WIDTH), lambda b, t: (t, 0)),
            pl.BlockSpec((tm, B_WIDTH), lambda b, t: (t, 0)),
            pl.BlockSpec((tm, B_WIDTH), lambda b, t: (t, 0)),
        ]
        args += list(rope_tabs)

    def heads(n):
        return pl.BlockSpec((None, n, tm, HEAD_DIM), lambda b, t: (b, 0, t, 0))

    def rows(width):
        return pl.BlockSpec((None, tm, width), lambda b, t: (b, t, 0))

    out_specs = [heads(A_HEADS), heads(A_KV_HEADS), heads(A_KV_HEADS),
                 heads(B_HEADS), heads(B_HEADS), heads(B_HEADS),
                 rows(C_KW), rows(C_KW), rows(C_WIDTH), rows(C_KW), rows(C_KW), rows(D_MIX)]

    def hshape(n, dtype):
        return jax.ShapeDtypeStruct((bsz, n, seq, HEAD_DIM), dtype)

    def rshape(width):
        return jax.ShapeDtypeStruct((bsz, seq, width), F32)

    out_shape = [hshape(A_HEADS, BF16), hshape(A_KV_HEADS, kv_dtype), hshape(A_KV_HEADS, kv_dtype),
                 hshape(B_HEADS, BF16), hshape(B_HEADS, kv_dtype), hshape(B_HEADS, kv_dtype),
                 rshape(C_KW), rshape(C_KW), rshape(C_WIDTH), rshape(C_KW), rshape(C_KW),
                 rshape(D_MIX)]
    return pl.pallas_call(
        functools.partial(_in_kernel, rope, kv_dtype),
        grid=grid,
        in_specs=in_specs,
        out_specs=out_specs,
        out_shape=out_shape,
        compiler_params=pltpu.CompilerParams(
            dimension_semantics=("arbitrary", "arbitrary"), vmem_limit_bytes=VMEM_LIMIT),
        name="in_projection_rope" if rope else "in_projection",
    )(*args)


def _gla_direction(cq_ref, ck_ref, cv_ref, g_ref, s_t, reverse, seq, oc_ref, accumulate):
    bl = GLA_BLOCK
    ri = _iota((bl, bl), 0)
    ci = _iota((bl, bl), 1)
    same_chunk = lax.shift_right_logical(ri, 6) == lax.shift_right_logical(ci, 6)
    causal = same_chunk & ((ci >= ri) if reverse else (ci <= ri))
    tri = jnp.where(causal, 1.0, 0.0).astype(BF16)
    bd = (lax.shift_right_logical(_iota((C_WIDTH, C_KW), 0), 6)
          == lax.shift_right_logical(_iota((C_WIDTH, C_KW), 1), 5))
    khead = lax.shift_right_logical(_iota((1, C_KW), 1), 5)
    vhead = lax.shift_right_logical(_iota((1, C_WIDTH), 1), 6)
    n_sub = bl // CHUNK

    blocks = range(seq // bl)
    for blk in (reversed(blocks) if reverse else blocks):
        r0 = blk * bl
        q = cq_ref[r0:r0 + bl, :]
        k = ck_ref[r0:r0 + bl, :]
        v = cv_ref[r0:r0 + bl, :]
        g = g_ref[r0:r0 + bl, :]
        g_hi, g_lo = _split_bf16(g)
        cum = _dot(tri, g_hi) + _dot(tri, g_lo)
        qt = q * jnp.exp(cum)
        kt = k * jnp.exp(-cum)
        ktb = kt.astype(BF16)
        vb = v.astype(BF16)

        o = jnp.zeros((bl, C_WIDTH), F32)
        for hd in range(C_HEADS):
            qh = jnp.where(khead == hd, qt, 0.0).astype(BF16)
            a = jnp.where(causal, _dot_nt(qh, ktb), 0.0).astype(BF16)
            vh = jnp.where(vhead == hd, v, 0.0).astype(BF16)
            o = o + _dot(a, vh)

        qtb = qt.astype(BF16)
        inter = [None] * n_sub
        subs = range(n_sub)
        for c in (reversed(subs) if reverse else subs):
            c0 = CHUNK * c
            edge = c0 if reverse else c0 + CHUNK - 1
            last = cum[edge:edge + 1, :]
            inter[c] = _dot_nt(qtb[c0:c0 + CHUNK], s_t.astype(BF16))
            kdec = (k[c0:c0 + CHUNK] * jnp.exp(last - cum[c0:c0 + CHUNK])).astype(BF16)
            kv = _dot_tn(vb[c0:c0 + CHUNK], kdec)
            s_t = jnp.exp(last) * s_t + jnp.where(bd, kv, 0.0)
        o = o + jnp.concatenate(inter, axis=0)
        if accumulate:
            oc_ref[r0:r0 + bl, :] = oc_ref[r0:r0 + bl, :] + o
        else:
            oc_ref[r0:r0 + bl, :] = o
    return s_t


def _softmax_pv(s, v):
    m = jnp.max(s, axis=-1, keepdims=True)
    p = jnp.exp(s - m)
    l = jnp.sum(p, axis=-1, keepdims=True)
    return _dot(p.astype(BF16), v) * (1.0 / l)


def _mix_kernel(cached, lam_init, seq, qt, qb, *refs):
    it = iter(refs)
    x_ref, mod_ref = next(it), next(it)
    qa_ref, ka_ref, va_ref = next(it), next(it), next(it)
    qb_ref, kb_ref, vb_ref = next(it), next(it), next(it)
    cq_ref, ck_ref, cv_ref, gf_ref, gb_ref, su_ref = (next(it) for _ in range(6))
    if cached:
        cak_ref, cav_ref, cbk_ref, cbv_ref, s0f_ref, s0b_ref = (next(it) for _ in range(6))
    wout_ref, gpost_ref, bog_ref, cog_ref, lamp_ref = (next(it) for _ in range(5))
    y_ref = next(it)
    if not cached:
        sf_ref, sb_ref = next(it), next(it)
    kA_s, vA_s, kB_s, vB_s, oc_s, mixed_s = (next(it) for _ in range(6))

    past = kA_s.shape[1] - seq
    t = pl.program_id(1)

    @pl.when(t == 0)
    def _per_sequence():
        for ref_new, ref_cache, dst in ((ka_ref, cak_ref if cached else None, kA_s),
                                        (va_ref, cav_ref if cached else None, vA_s),
                                        (kb_ref, cbk_ref if cached else None, kB_s),
                                        (vb_ref, cbv_ref if cached else None, vB_s)):
            for hd in range(dst.shape[0]):
                if cached:
                    dst[hd, 0:past, :] = ref_cache[hd].astype(BF16)
                dst[hd, past:past + seq, :] = ref_new[hd].astype(BF16)

        if cached:
            s0f, s0b = s0f_ref[...], s0b_ref[...]
        else:
            s0f = jnp.zeros((C_WIDTH, C_KW), F32)
            s0b = s0f
        s_f = _gla_direction(cq_ref, ck_ref, cv_ref, gf_ref, s0f, False, seq, oc_s, False)
        s_b = _gla_direction(cq_ref, ck_ref, cv_ref, gb_ref, s0b, True, seq, oc_s, True)
        if not cached:
            sf_ref[...] = s_f
            sb_ref[...] = s_b
        for r0 in range(0, seq, GLA_BLOCK):
            oc = oc_s[r0:r0 + GLA_BLOCK, :]
            oc_s[r0:r0 + GLA_BLOCK, :] = oc * lax.rsqrt(_group_mean_sq(oc, 6) + EPS) * cog_ref[...]

    lam = (jnp.exp(jnp.sum(lamp_ref[0:1, :] * lamp_ref[1:2, :], axis=-1, keepdims=True))
           - jnp.exp(jnp.sum(lamp_ref[2:3, :] * lamp_ref[3:4, :], axis=-1, keepdims=True))
           + lam_init)

    def attn_block(i, carry):
        rows = pl.ds(pl.multiple_of(i * qb, qb), qb)
        for grp in range(A_KV_HEADS):
            q4 = qa_ref[A_GROUP * grp:A_GROUP * (grp + 1), rows, :].reshape(A_GROUP * qb, HEAD_DIM)
            o4 = _softmax_pv(_dot_nt(q4, kA_s[grp]), vA_s[grp])
            o = jnp.concatenate([o4[j * qb:(j + 1) * qb] for j in range(A_GROUP)], axis=-1)
            mixed_s[rows, A_GROUP * HEAD_DIM * grp:A_GROUP * HEAD_DIM * (grp + 1)] = o
        lane = _iota((qb, HEAD_DIM), 1)
        outs = []
        for hd in range(B_HEADS):
            q = qb_ref[hd, rows, :]
            zero = jnp.zeros_like(q)
            q2 = jnp.concatenate([jnp.where(lane < B_QK_DIM, q, zero),
                                  jnp.where(lane >= B_QK_DIM, q, zero)], axis=0)
            o2 = _softmax_pv(_dot_nt(q2, kB_s[hd]), vB_s[hd])
            ob = o2[0:qb] - lam * o2[qb:2 * qb]
            ob = ob * lax.rsqrt(jnp.mean(ob * ob, axis=-1, keepdims=True) + EPS)
            outs.append(ob * bog_ref[...] * (1.0 - lam_init))
        mixed_s[rows, A_WIDTH:A_WIDTH + B_WIDTH] = jnp.concatenate(outs, axis=-1)
        return carry

    lax.fori_loop(0, qt // qb, attn_block, 0)

    seq_rows = pl.ds(pl.multiple_of(t * qt, qt), qt)
    mixed_s[:, A_WIDTH + B_WIDTH:D_MIX] = oc_s[seq_rows, :]
    gate = mod_ref[:, 2 * D_MODEL:3 * D_MODEL]
    mixed = (mixed_s[...] * su_ref[...]).astype(BF16)
    y = _dot(mixed, wout_ref[...])
    yn = y * lax.rsqrt(jnp.mean(y * y, axis=-1, keepdims=True) + EPS) * gpost_ref[...]
    y_ref[...] = x_ref[...] + gate * yn


def _mixer(x, mod, per_batch_mod, proj, lw, lam_init, cache):
    bsz, seq, _ = x.shape
    cached = cache is not None
    past = cache[0].shape[2] if cached else 0
    qa, ka, va, qb, kb, vb, cq, ck, cv, gf, gb, su = proj
    qt = 256
    qb_rows = 128

    def const(shape):
        return pl.BlockSpec(shape, lambda b, t: (0,) * len(shape))

    def heads(n, length):
        return pl.BlockSpec((None, n, length, HEAD_DIM), lambda b, t: (b, 0, 0, 0))

    def head_tile(n):
        return pl.BlockSpec((None, n, qt, HEAD_DIM), lambda b, t: (b, 0, t, 0))

    def rows(width):
        return pl.BlockSpec((None, seq, width), lambda b, t: (b, 0, 0))

    def row_tile(width):
        return pl.BlockSpec((None, qt, width), lambda b, t: (b, t, 0))

    state = pl.BlockSpec((None, C_WIDTH, C_KW), lambda b, t: (b, 0, 0))
    mod_idx = (lambda b, t: (b, 0, 0)) if per_batch_mod else (lambda b, t: (0, 0, 0))
    in_specs = [row_tile(D_MODEL), pl.BlockSpec((None, 1, 3 * D_MODEL), mod_idx),
                head_tile(A_HEADS), heads(A_KV_HEADS, seq), heads(A_KV_HEADS, seq),
                head_tile(B_HEADS), heads(B_HEADS, seq), heads(B_HEADS, seq),
                rows(C_KW), rows(C_KW), rows(C_WIDTH), rows(C_KW), rows(C_KW), row_tile(D_MIX)]
    args = [x, mod, qa, ka, va, qb, kb, vb, cq, ck, cv, gf, gb, su]
    if cached:
        in_specs += [heads(A_KV_HEADS, past), heads(A_KV_HEADS, past),
                     heads(B_HEADS, past), heads(B_HEADS, past), state, state]
        args += list(cache)
    in_specs += [const((D_MIX, D_MODEL)), const((1, D_MODEL)), const((1, B_V_DIM)),
                 const((1, C_WIDTH)), const((4, B_QK_DIM))]
    args += [lw["w_out"], lw["g_post"], lw["b_out_gain"], lw["c_out_gain"], lw["lam_params"]]

    out_specs = [row_tile(D_MODEL)]
    out_shape = [jax.ShapeDtypeStruct((bsz, seq, D_MODEL), F32)]
    if not cached:
        out_specs += [state, state]
        out_shape += [jax.ShapeDtypeStruct((bsz, C_WIDTH, C_KW), F32)] * 2

    lk = past + seq
    scratch = [pltpu.VMEM((A_KV_HEADS, lk, HEAD_DIM), BF16), pltpu.VMEM((A_KV_HEADS, lk, HEAD_DIM), BF16),
               pltpu.VMEM((B_HEADS, lk, HEAD_DIM), BF16), pltpu.VMEM((B_HEADS, lk, B_V_DIM), BF16),
               pltpu.VMEM((seq, C_WIDTH), F32), pltpu.VMEM((qt, D_MIX), F32)]
    return pl.pallas_call(
        functools.partial(_mix_kernel, cached, lam_init, seq, qt, qb_rows),
        grid=(bsz, seq // qt),
        in_specs=in_specs,
        out_specs=out_specs,
        out_shape=out_shape,
        scratch_shapes=scratch,
        compiler_params=pltpu.CompilerParams(
            dimension_semantics=("arbitrary", "arbitrary"), vmem_limit_bytes=VMEM_LIMIT),
        name="mixer_cached" if cached else "mixer",
    )(*args)


def _rope_tables(seq):
    t = jnp.arange(seq)
    pos_row = (t // GRID_W).astype(F32)
    pos_col = (t % GRID_W).astype(F32)

    def tables(half, width):
        freq = ROPE_THETA ** (-jnp.arange(half, dtype=F32) / half)
        ang_r = pos_row[:, None] * freq[None, :]
        ang_c = pos_col[:, None] * freq[None, :]
        cos = jnp.concatenate([jnp.cos(ang_r), jnp.cos(ang_r), jnp.cos(ang_c), jnp.cos(ang_c)], axis=-1)
        sin = jnp.concatenate([-jnp.sin(ang_r), jnp.sin(ang_r), -jnp.sin(ang_c), jnp.sin(ang_c)], axis=-1)
        reps = width // (4 * half)
        return jnp.tile(cos, (1, reps)), jnp.tile(sin, (1, reps))

    cos_a, sin_a = tables(HEAD_DIM // 4, A_WIDTH)
    cos_b, sin_b = tables(B_QK_DIM // 4, B_WIDTH)
    return cos_a, sin_a, cos_b, sin_b


def _state_to_blockdiag(s):
    st = jnp.swapaxes(s, -1, -2)
    eye = jnp.eye(C_HEADS, dtype=s.dtype)
    full = st[:, :, :, None, :] * eye[None, :, None, :, None]
    return full.reshape(s.shape[0], C_WIDTH, C_KW)


def _blockdiag_to_state(s_t):
    blocks = [s_t[:, C_DV * h:C_DV * (h + 1), C_DK * h:C_DK * (h + 1)] for h in range(C_HEADS)]
    return jnp.swapaxes(jnp.stack(blocks, axis=1), -1, -2)


def _layer_weights(l, g_pre, g_post, w_in, w_out, a_q_gain, a_k_gain, b_lambda_q1, b_lambda_k1,
                   b_lambda_q2, b_lambda_k2, b_out_gain, c_gate_w_fwd, c_gate_b_fwd, c_gate_w_bwd,
                   c_gate_b_bwd, c_out_gain):
    w = w_in[l]
    lr0 = OFF_U
    w_perm = jnp.concatenate(
        [w[:, :lr0], w[:, lr0 + 2 * GATE_RANK:], w[:, lr0:lr0 + 2 * GATE_RANK],
         jnp.zeros((D_MODEL, LR_PAD - 2 * GATE_RANK), w.dtype)], axis=-1).astype(BF16)
    zeros = jnp.zeros((LR_PAD, C_KW), F32)
    cw_f = zeros.at[0:GATE_RANK].set(c_gate_w_fwd[l]).astype(BF16)
    cw_b = zeros.at[GATE_RANK:2 * GATE_RANK].set(c_gate_w_bwd[l]).astype(BF16)
    return {
        "g_pre": g_pre[l][None, :],
        "g_post": g_post[l][None, :],
        "w_in": w_perm,
        "w_out": w_out[l].astype(BF16),
        "aq_gain": jnp.tile(a_q_gain[l], A_HEADS)[None, :],
        "ak_gain": jnp.tile(a_k_gain[l], A_KV_HEADS)[None, :],
        "cw_f": cw_f,
        "cb_f": c_gate_b_fwd[l][None, :],
        "cw_b": cw_b,
        "cb_b": c_gate_b_bwd[l][None, :],
        "b_out_gain": b_out_gain[l][None, :],
        "c_out_gain": jnp.tile(c_out_gain[l], C_HEADS)[None, :],
        "lam_params": jnp.stack([b_lambda_q1[l], b_lambda_k1[l], b_lambda_q2[l], b_lambda_k2[l]]),
    }


def kernel(x_prompt, x_sample, c, cache_a_k, cache_a_v, cache_b_k, cache_b_v, state_c_fwd, state_c_bwd, c_ctx, w_mod, b_mod, g_pre, g_post, w_in, w_out, a_q_gain, a_k_gain, b_lambda_q1, b_lambda_k1, b_lambda_q2, b_lambda_k2, b_out_gain, c_gate_w_fwd, c_gate_b_fwd, c_gate_w_bwd, c_gate_b_bwd, c_out_gain):
    dec_batch = x_sample.shape[0]
    dec_seq = x_sample.shape[1]

    mod_rows = 16
    cvec = jnp.zeros((mod_rows, D_MODEL), F32).at[0:dec_batch].set(c).at[dec_batch].set(c_ctx)
    mod = _modulation(cvec, w_mod, b_mod)

    rope_tabs = _rope_tables(dec_seq)
    y_p, y_s = x_prompt, x_sample
    ctx_layers = []
    for l in range(DEPTH):
        lam_init = 0.8 - 0.6 * math.exp(-0.3 * l)
        lw = _layer_weights(l, g_pre, g_post, w_in, w_out, a_q_gain, a_k_gain, b_lambda_q1,
                            b_lambda_k1, b_lambda_q2, b_lambda_k2, b_out_gain, c_gate_w_fwd,
                            c_gate_b_fwd, c_gate_w_bwd, c_gate_b_bwd, c_out_gain)
        mod_ctx = mod[l, dec_batch:dec_batch + 1][:, None, :]
        mod_lat = mod[l, 0:dec_batch][:, None, :]

        proj_p = _in_projection(y_p, mod_ctx, False, lw, None, F32, 256)
        y_p, s_f, s_b = _mixer(y_p, mod_ctx, False, proj_p, lw, lam_init, None)
        ctx_layers.append((proj_p[1], proj_p[2], proj_p[4], proj_p[5],
                           _blockdiag_to_state(s_f), _blockdiag_to_state(s_b)))

        proj_s = _in_projection(y_s, mod_lat, True, lw, rope_tabs, BF16, 512)
        cache = (cache_a_k[:, l], cache_a_v[:, l], cache_b_k[:, l], cache_b_v[:, l],
                 _state_to_blockdiag(state_c_fwd[:, l]), _state_to_blockdiag(state_c_bwd[:, l]))
        (y_s,) = _mixer(y_s, mod_lat, True, proj_s, lw, lam_init, cache)

    outs = [jnp.stack(ts, axis=1) for ts in zip(*ctx_layers)]
    return (y_p, y_s, *outs)
```

```python
import functools
import math

import jax
import jax.numpy as jnp
from jax import lax
from jax.experimental import pallas as pl
from jax.experimental.pallas import tpu as pltpu

F32 = jnp.float32
BF16 = jnp.bfloat16

D_MODEL = 1024
DEPTH = 2
GRID_W = 64
HEAD_DIM = 64
A_HEADS = 8
A_KV_HEADS = 2
A_GROUP = A_HEADS // A_KV_HEADS
A_WIDTH = A_HEADS * HEAD_DIM
A_KV_WIDTH = A_KV_HEADS * HEAD_DIM
B_HEADS = 4
B_QK_DIM = 32
B_V_DIM = 64
B_WIDTH = B_HEADS * B_V_DIM
C_HEADS = 4
C_DK = 32
C_DV = 64
C_KW = C_HEADS * C_DK
C_WIDTH = C_HEADS * C_DV
GATE_RANK = 16
GLA_TAU = 16.0
CHUNK = 64
D_MIX = A_WIDTH + B_WIDTH + C_WIDTH
ROPE_THETA = 10000.0
EPS = 1e-6

LANES = 128
GLA_BLOCK = 256
LR_PAD = LANES

OFF_AQ = 0
OFF_AK = OFF_AQ + A_WIDTH
OFF_AV = OFF_AK + A_KV_WIDTH
OFF_BQ = OFF_AV + A_KV_WIDTH
OFF_BK = OFF_BQ + B_WIDTH
OFF_BV = OFF_BK + B_WIDTH
OFF_CQ = OFF_BV + B_WIDTH
OFF_CK = OFF_CQ + C_KW
OFF_CV = OFF_CK + C_KW
OFF_U = OFF_CV + C_WIDTH
OFF_LR = OFF_U + D_MIX
IN_PAD_WIDTH = OFF_LR + LR_PAD

VMEM_LIMIT = 56 * 1024 * 1024


def _dot(a, b):
    return jnp.dot(a, b, preferred_element_type=F32)


def _dot_nt(a, b):
    return lax.dot_general(a, b, (((1,), (1,)), ((), ())), preferred_element_type=F32)


def _dot_tn(a, b):
    return lax.dot_general(a, b, (((0,), (0,)), ((), ())), preferred_element_type=F32)


def _split_bf16(x):
    hi = x.astype(BF16)
    lo = (x - hi.astype(F32)).astype(BF16)
    return hi, lo


def _iota(shape, dim):
    return lax.broadcasted_iota(jnp.int32, shape, dim)


def _group_mean_sq(x, group_log2):
    width = x.shape[-1]
    r = lax.shift_right_logical(_iota((LANES, LANES), 0), group_log2)
    c = lax.shift_right_logical(_iota((LANES, LANES), 1), group_log2)
    ones = jnp.where(r == c, 1.0, 0.0).astype(BF16)
    hi, lo = _split_bf16(x * x)
    cols = []
    for j in range(width // LANES):
        sl = slice(LANES * j, LANES * (j + 1))
        cols.append(_dot(hi[:, sl], ones) + _dot(lo[:, sl], ones))
    ss = cols[0] if len(cols) == 1 else jnp.concatenate(cols, axis=-1)
    return ss * (1.0 / (1 << group_log2))


def _rope(x, cos, sin_signed, dist):
    width = x.shape[-1]
    lane = _iota(x.shape, 1)
    first = (lane & (2 * dist - 1)) < dist
    up = pltpu.roll(x, width - dist, 1)
    down = pltpu.roll(x, dist, 1)
    return x * cos + jnp.where(first, up, down) * sin_signed


def _log_sigmoid(x):
    return jnp.minimum(x, 0.0) - jnp.log1p(jnp.exp(-jnp.abs(x)))


def _silu(x):
    return x * (1.0 / (1.0 + jnp.exp(-x)))


def _mod_kernel(c_ref, w_ref, b_ref, o_ref):
    a = _silu(c_ref[...]).astype(BF16)
    o_ref[...] = _dot(a, w_ref[...].astype(BF16)) + b_ref[...]


def _modulation(cvec, w_mod, b_mod):
    rows = cvec.shape[0]
    nblk = 3
    return pl.pallas_call(
        _mod_kernel,
        grid=(DEPTH, nblk),
        in_specs=[
            pl.BlockSpec((rows, D_MODEL), lambda l, n: (0, 0)),
            pl.BlockSpec((None, D_MODEL, D_MODEL), lambda l, n: (l, 0, n)),
            pl.BlockSpec((None, 1, D_MODEL), lambda l, n: (l, 0, n)),
        ],
        out_specs=pl.BlockSpec((None, rows, D_MODEL), lambda l, n: (l, 0, n)),
        out_shape=jax.ShapeDtypeStruct((DEPTH, rows, 3 * D_MODEL), F32),
        compiler_params=pltpu.CompilerParams(
            dimension_semantics=("arbitrary", "arbitrary"), vmem_limit_bytes=VMEM_LIMIT),
        name="modulation",
    )(cvec, w_mod, b_mod.reshape(DEPTH, 1, 3 * D_MODEL))


def _in_kernel(rope, kv_dtype, n_aliased, *refs):
    (x_ref, mod_ref, gpre_ref, w_ref, aqg_ref, akg_ref, cwf_ref, cbf_ref, cwb_ref, cbb_ref) = refs[:10]
    refs = refs[10:]
    if rope:
        cosa_ref, sina_ref, cosb_ref, sinb_ref = refs[:4]
        refs = refs[4:]
    refs = refs[n_aliased:]
    (qa_ref, ka_ref, va_ref, qb_ref, kb_ref, vb_ref,
     cq_ref, ck_ref, cv_ref, gf_ref, gb_ref, su_ref) = refs

    x = x_ref[...]
    shift = mod_ref[:, 0:D_MODEL]
    scale = mod_ref[:, D_MODEL:2 * D_MODEL]
    ms = jnp.mean(x * x, axis=-1, keepdims=True)
    h = (x * lax.rsqrt(ms + EPS)) * gpre_ref[...] * (1.0 + scale) + shift
    hb = h.astype(BF16)

    def proj(off, width):
        return _dot(hb, w_ref[:, off:off + width])

    def store_heads(ref, val, n_heads, dtype):
        for hd in range(n_heads):
            ref[hd] = val[:, HEAD_DIM * hd:HEAD_DIM * (hd + 1)].astype(dtype)

    aq = proj(OFF_AQ, A_WIDTH)
    aq = aq * lax.rsqrt(_group_mean_sq(aq, 6) + EPS) * aqg_ref[...]
    ak = proj(OFF_AK, A_KV_WIDTH)
    ak = ak * lax.rsqrt(_group_mean_sq(ak, 6) + EPS) * akg_ref[...]
    if rope:
        aq = _rope(aq, cosa_ref[...], sina_ref[...], 16)
        ak = _rope(ak, cosa_ref[:, 0:A_KV_WIDTH], sina_ref[:, 0:A_KV_WIDTH], 16)
    store_heads(qa_ref, aq * (HEAD_DIM ** -0.5), A_HEADS, BF16)
    store_heads(ka_ref, ak, A_KV_HEADS, kv_dtype)
    store_heads(va_ref, proj(OFF_AV, A_KV_WIDTH), A_KV_HEADS, kv_dtype)

    bq = proj(OFF_BQ, B_WIDTH)
    bk = proj(OFF_BK, B_WIDTH)
    if rope:
        bq = _rope(bq, cosb_ref[...], sinb_ref[...], 8)
        bk = _rope(bk, cosb_ref[...], sinb_ref[...], 8)
    store_heads(qb_ref, bq * (B_QK_DIM ** -0.5), B_HEADS, BF16)
    store_heads(kb_ref, bk, B_HEADS, kv_dtype)
    store_heads(vb_ref, proj(OFF_BV, B_WIDTH), B_HEADS, kv_dtype)

    cq_ref[...] = proj(OFF_CQ, C_KW) * (C_DK ** -0.5)
    ck_ref[...] = proj(OFF_CK, C_KW)
    cv_ref[...] = proj(OFF_CV, C_WIDTH)
    lr = proj(OFF_LR, LR_PAD).astype(BF16)
    gf_ref[...] = _log_sigmoid(_dot(lr, cwf_ref[...]) + cbf_ref[...]) * (1.0 / GLA_TAU)
    gb_ref[...] = _log_sigmoid(_dot(lr, cwb_ref[...]) + cbb_ref[...]) * (1.0 / GLA_TAU)

    su_ref[...] = _silu(proj(OFF_U, D_MIX))


def _in_projection(x, mod, per_batch_mod, layer, wts, rope_tabs, tm, kv_prev=None):
    bsz, seq, _ = x.shape
    rope = rope_tabs is not None
    stacked = not rope
    kv_dtype = F32 if stacked else BF16
    grid = (bsz, seq // tm)

    def per_layer(shape):
        return pl.BlockSpec((None,) + shape, lambda b, t: (layer,) + (0,) * len(shape))

    mod_idx = (lambda b, t: (layer, b, 0, 0)) if per_batch_mod else (lambda b, t: (layer, 0, 0, 0))
    in_specs = [
        pl.BlockSpec((None, tm, D_MODEL), lambda b, t: (b, t, 0)),
        pl.BlockSpec((None, None, 1, 3 * D_MODEL), mod_idx),
        per_layer((1, D_MODEL)),
        per_layer((D_MODEL, IN_PAD_WIDTH)),
        per_layer((1, A_WIDTH)),
        per_layer((1, A_KV_WIDTH)),
        per_layer((LR_PAD, C_KW)),
        per_layer((1, C_KW)),
        per_layer((LR_PAD, C_KW)),
        per_layer((1, C_KW)),
    ]
    args = [x, mod, wts["g_pre"], wts["w_in"], wts["aq_gain"], wts["ak_gain"],
            wts["cw_f"], wts["cb_f"], wts["cw_b"], wts["cb_b"]]
    if rope:
        in_specs += [
            pl.BlockSpec((tm, A_WIDTH), lambda b, t: (t, 0)),
            pl.BlockSpec((tm, A_WIDTH), lambda b, t: (t, 0)),
            pl.BlockSpec((tm, B_WIDTH), lambda b, t: (t, 0)),
            pl.BlockSpec((tm, B_WIDTH), lambda b, t: (t, 0)),
        ]
        args += list(rope_tabs)
    aliases = {}
    if kv_prev is not None:
        for j, buf in enumerate(kv_prev):
            aliases[len(args)] = (1, 2, 4, 5)[j]
            in_specs.append(pl.BlockSpec(memory_space=pl.ANY))
            args.append(buf)

    def heads(n):
        return pl.BlockSpec((None, n, tm, HEAD_DIM), lambda b, t: (b, 0, t, 0))

    def kv_heads(n):
        if stacked:
            return pl.BlockSpec((None, None, n, tm, HEAD_DIM), lambda b, t: (b, layer, 0, t, 0))
        return heads(n)

    def rows(width):
        return pl.BlockSpec((None, tm, width), lambda b, t: (b, t, 0))

    out_specs = [heads(A_HEADS), kv_heads(A_KV_HEADS), kv_heads(A_KV_HEADS),
                 heads(B_HEADS), kv_heads(B_HEADS), kv_heads(B_HEADS),
                 rows(C_KW), rows(C_KW), rows(C_WIDTH), rows(C_KW), rows(C_KW), rows(D_MIX)]

    def hshape(n):
        return jax.ShapeDtypeStruct((bsz, n, seq, HEAD_DIM), BF16)

    def kvshape(n):
        if stacked:
            return jax.ShapeDtypeStruct((bsz, DEPTH, n, seq, HEAD_DIM), F32)
        return hshape(n)

    def rshape(width):
        return jax.ShapeDtypeStruct((bsz, seq, width), F32)

    out_shape = [hshape(A_HEADS), kvshape(A_KV_HEADS), kvshape(A_KV_HEADS),
                 hshape(B_HEADS), kvshape(B_HEADS), kvshape(B_HEADS),
                 rshape(C_KW), rshape(C_KW), rshape(C_WIDTH), rshape(C_KW), rshape(C_KW),
                 rshape(D_MIX)]
    return pl.pallas_call(
        functools.partial(_in_kernel, rope, kv_dtype, len(aliases)),
        grid=grid,
        in_specs=in_specs,
        out_specs=out_specs,
        out_shape=out_shape,
        input_output_aliases=aliases,
        compiler_params=pltpu.CompilerParams(
            dimension_semantics=("arbitrary", "arbitrary"), vmem_limit_bytes=VMEM_LIMIT),
        name="in_projection_rope" if rope else "in_projection",
    )(*args)


def _gla_direction(cq_ref, ck_ref, cv_ref, g_ref, s_t, reverse, seq, oc_ref, accumulate):
    bl = GLA_BLOCK
    ri = _iota((bl, bl), 0)
    ci = _iota((bl, bl), 1)
    same_chunk = lax.shift_right_logical(ri, 6) == lax.shift_right_logical(ci, 6)
    causal = same_chunk & ((ci >= ri) if reverse else (ci <= ri))
    tri = jnp.where(causal, 1.0, 0.0).astype(BF16)
    bd = (lax.shift_right_logical(_iota((C_WIDTH, C_KW), 0), 6)
          == lax.shift_right_logical(_iota((C_WIDTH, C_KW), 1), 5))
    khead = lax.shift_right_logical(_iota((1, C_KW), 1), 5)
    vhead = lax.shift_right_logical(_iota((1, C_WIDTH), 1), 6)
    n_sub = bl // CHUNK

    blocks = range(seq // bl)
    for blk in (reversed(blocks) if reverse else blocks):
        r0 = blk * bl
        q = cq_ref[r0:r0 + bl, :]
        k = ck_ref[r0:r0 + bl, :]
        v = cv_ref[r0:r0 + bl, :]
        g = g_ref[r0:r0 + bl, :]
        g_hi, g_lo = _split_bf16(g)
        cum = _dot(tri, g_hi) + _dot(tri, g_lo)
        qt = q * jnp.exp(cum)
        kt = k * jnp.exp(-cum)
        ktb = kt.astype(BF16)
        vb = v.astype(BF16)

        o = jnp.zeros((bl, C_WIDTH), F32)
        for hd in range(C_HEADS):
            qh = jnp.where(khead == hd, qt, 0.0).astype(BF16)
            a = jnp.where(causal, _dot_nt(qh, ktb), 0.0).astype(BF16)
            vh = jnp.where(vhead == hd, v, 0.0).astype(BF16)
            o = o + _dot(a, vh)

        qtb = qt.astype(BF16)
        inter = [None] * n_sub
        subs = range(n_sub)
        for c in (reversed(subs) if reverse else subs):
            c0 = CHUNK * c
            edge = c0 if reverse else c0 + CHUNK - 1
            last = cum[edge:edge + 1, :]
            inter[c] = _dot_nt(qtb[c0:c0 + CHUNK], s_t.astype(BF16))
            kdec = (k[c0:c0 + CHUNK] * jnp.exp(last - cum[c0:c0 + CHUNK])).astype(BF16)
            kv = _dot_tn(vb[c0:c0 + CHUNK], kdec)
            s_t = jnp.exp(last) * s_t + jnp.where(bd, kv, 0.0)
        o = o + jnp.concatenate(inter, axis=0)
        if accumulate:
            oc_ref[r0:r0 + bl, :] = oc_ref[r0:r0 + bl, :] + o
        else:
            oc_ref[r0:r0 + bl, :] = o
    return s_t


def _softmax_pv(s, v):
    m = jnp.max(s, axis=-1, keepdims=True)
    p = jnp.exp(s - m)
    l = jnp.sum(p, axis=-1, keepdims=True)
    return _dot(p.astype(BF16), v) * (1.0 / l)


def _mix_kernel(cached, lam_init, seq, qt, qb, *refs):
    it = iter(refs)
    x_ref, mod_ref = next(it), next(it)
    qa_ref, ka_ref, va_ref = next(it), next(it), next(it)
    qb_ref, kb_ref, vb_ref = next(it), next(it), next(it)
    cq_ref, ck_ref, cv_ref, gf_ref, gb_ref, su_ref = (next(it) for _ in range(6))
    if cached:
        cak_ref, cav_ref, cbk_ref, cbv_ref, s0f_ref, s0b_ref = (next(it) for _ in range(6))
    wout_ref, gpost_ref, bog_ref, cog_ref, lamp_ref = (next(it) for _ in range(5))
    y_ref = next(it)
    if not cached:
        sf_ref, sb_ref = next(it), next(it)
    kA_s, vA_s, kB_s, vB_s, oc_s, mixed_s = (next(it) for _ in range(6))

    past = kA_s.shape[1] - seq
    t = pl.program_id(1)

    @pl.when(t == 0)
    def _per_sequence():
        for ref_new, ref_cache, dst in ((ka_ref, cak_ref if cached else None, kA_s),
                                        (va_ref, cav_ref if cached else None, vA_s),
                                        (kb_ref, cbk_ref if cached else None, kB_s),
                                        (vb_ref, cbv_ref if cached else None, vB_s)):
            for hd in range(dst.shape[0]):
                if cached:
                    dst[hd, 0:past, :] = ref_cache[hd].astype(BF16)
                dst[hd, past:past + seq, :] = ref_new[hd].astype(BF16)

        if cached:
            s0f, s0b = s0f_ref[...], s0b_ref[...]
        else:
            s0f = jnp.zeros((C_WIDTH, C_KW), F32)
            s0b = s0f
        s_f = _gla_direction(cq_ref, ck_ref, cv_ref, gf_ref, s0f, False, seq, oc_s, False)
        s_b = _gla_direction(cq_ref, ck_ref, cv_ref, gb_ref, s0b, True, seq, oc_s, True)
        if not cached:
            sf_ref[...] = s_f
            sb_ref[...] = s_b
        for r0 in range(0, seq, GLA_BLOCK):
            oc = oc_s[r0:r0 + GLA_BLOCK, :]
            oc_s[r0:r0 + GLA_BLOCK, :] = oc * lax.rsqrt(_group_mean_sq(oc, 6) + EPS) * cog_ref[...]

    lam = (jnp.exp(jnp.sum(lamp_ref[0:1, :] * lamp_ref[1:2, :], axis=-1, keepdims=True))
           - jnp.exp(jnp.sum(lamp_ref[2:3, :] * lamp_ref[3:4, :], axis=-1, keepdims=True))
           + lam_init)

    def attn_block(i, carry):
        rows = pl.ds(pl.multiple_of(i * qb, qb), qb)
        for grp in range(A_KV_HEADS):
            q4 = qa_ref[A_GROUP * grp:A_GROUP * (grp + 1), rows, :].reshape(A_GROUP * qb, HEAD_DIM)
            o4 = _softmax_pv(_dot_nt(q4, kA_s[grp]), vA_s[grp])
            o = jnp.concatenate([o4[j * qb:(j + 1) * qb] for j in range(A_GROUP)], axis=-1)
            mixed_s[rows, A_GROUP * HEAD_DIM * grp:A_GROUP * HEAD_DIM * (grp + 1)] = o
        lane = _iota((qb, HEAD_DIM), 1)
        outs = []
        for hd in range(B_HEADS):
            q = qb_ref[hd, rows, :]
            zero = jnp.zeros_like(q)
            q2 = jnp.concatenate([jnp.where(lane < B_QK_DIM, q, zero),
                                  jnp.where(lane >= B_QK_DIM, q, zero)], axis=0)
            o2 = _softmax_pv(_dot_nt(q2, kB_s[hd]), vB_s[hd])
            ob = o2[0:qb] - lam * o2[qb:2 * qb]
            ob = ob * lax.rsqrt(jnp.mean(ob * ob, axis=-1, keepdims=True) + EPS)
            outs.append(ob * bog_ref[...] * (1.0 - lam_init))
        mixed_s[rows, A_WIDTH:A_WIDTH + B_WIDTH] = jnp.concatenate(outs, axis=-1)
        return carry

    lax.fori_loop(0, qt // qb, attn_block, 0)

    seq_rows = pl.ds(pl.multiple_of(t * qt, qt), qt)
    mixed_s[:, A_WIDTH + B_WIDTH:D_MIX] = oc_s[seq_rows, :]
    gate = mod_ref[:, 2 * D_MODEL:3 * D_MODEL]
    mixed = (mixed_s[...] * su_ref[...]).astype(BF16)
    y = _dot(mixed, wout_ref[...])
    yn = y * lax.rsqrt(jnp.mean(y * y, axis=-1, keepdims=True) + EPS) * gpost_ref[...]
    y_ref[...] = x_ref[...] + gate * yn


def _mixer(x, mod, per_batch_mod, layer, proj, wts, lam_init, cache):
    bsz, seq, _ = x.shape
    cached = cache is not None
    past = cache[0].shape[3] if cached else 0
    qa, ka, va, qb, kb, vb, cq, ck, cv, gf, gb, su = proj
    qt = 256
    qb_rows = 128

    def per_layer(shape):
        return pl.BlockSpec((None,) + shape, lambda b, t: (layer,) + (0,) * len(shape))

    def layer_heads(n, length):
        return pl.BlockSpec((None, None, n, length, HEAD_DIM), lambda b, t: (b, layer, 0, 0, 0))

    def kv_heads(n):
        if cached:
            return pl.BlockSpec((None, n, seq, HEAD_DIM), lambda b, t: (b, 0, 0, 0))
        return layer_heads(n, seq)

    def head_tile(n):
        return pl.BlockSpec((None, n, qt, HEAD_DIM), lambda b, t: (b, 0, t, 0))

    def rows(width):
        return pl.BlockSpec((None, seq, width), lambda b, t: (b, 0, 0))

    def row_tile(width):
        return pl.BlockSpec((None, qt, width), lambda b, t: (b, t, 0))

    mod_idx = (lambda b, t: (layer, b, 0, 0)) if per_batch_mod else (lambda b, t: (layer, 0, 0, 0))
    in_specs = [row_tile(D_MODEL), pl.BlockSpec((None, None, 1, 3 * D_MODEL), mod_idx),
                head_tile(A_HEADS), kv_heads(A_KV_HEADS), kv_heads(A_KV_HEADS),
                head_tile(B_HEADS), kv_heads(B_HEADS), kv_heads(B_HEADS),
                rows(C_KW), rows(C_KW), rows(C_WIDTH), rows(C_KW), rows(C_KW), row_tile(D_MIX)]
    args = [x, mod, qa, ka, va, qb, kb, vb, cq, ck, cv, gf, gb, su]
    if cached:
        state_in = pl.BlockSpec((None, None, C_WIDTH, C_KW), lambda b, t: (b, layer, 0, 0))
        in_specs += [layer_heads(A_KV_HEADS, past), layer_heads(A_KV_HEADS, past),
                     layer_heads(B_HEADS, past), layer_heads(B_HEADS, past), state_in, state_in]
        args += list(cache)
    in_specs += [per_layer((D_MIX, D_MODEL)), per_layer((1, D_MODEL)), per_layer((1, B_V_DIM)),
                 per_layer((1, C_WIDTH)), per_layer((4, B_QK_DIM))]
    args += [wts["w_out"], wts["g_post"], wts["b_out_gain"], wts["c_out_gain"], wts["lam_params"]]

    out_specs = [row_tile(D_MODEL)]
    out_shape = [jax.ShapeDtypeStruct((bsz, seq, D_MODEL), F32)]
    if not cached:
        state_out = pl.BlockSpec((None, C_WIDTH, C_KW), lambda b, t: (b, 0, 0))
        out_specs += [state_out, state_out]
        out_shape += [jax.ShapeDtypeStruct((bsz, C_WIDTH, C_KW), F32)] * 2

    lk = past + seq
    scratch = [pltpu.VMEM((A_KV_HEADS, lk, HEAD_DIM), BF16), pltpu.VMEM((A_KV_HEADS, lk, HEAD_DIM), BF16),
               pltpu.VMEM((B_HEADS, lk, HEAD_DIM), BF16), pltpu.VMEM((B_HEADS, lk, B_V_DIM), BF16),
               pltpu.VMEM((seq, C_WIDTH), F32), pltpu.VMEM((qt, D_MIX), F32)]
    return pl.pallas_call(
        functools.partial(_mix_kernel, cached, lam_init, seq, qt, qb_rows),
        grid=(bsz, seq // qt),
        in_specs=in_specs,
        out_specs=out_specs,
        out_shape=out_shape,
        scratch_shapes=scratch,
        compiler_params=pltpu.CompilerParams(
            dimension_semantics=("arbitrary", "arbitrary"), vmem_limit_bytes=VMEM_LIMIT),
        name="mixer_cached" if cached else "mixer",
    )(*args)


def _rope_tables(seq):
    t = jnp.arange(seq)
    pos_row = (t // GRID_W).astype(F32)
    pos_col = (t % GRID_W).astype(F32)

    def tables(half, width):
        freq = ROPE_THETA ** (-jnp.arange(half, dtype=F32) / half)
        ang_r = pos_row[:, None] * freq[None, :]
        ang_c = pos_col[:, None] * freq[None, :]
        cos = jnp.concatenate([jnp.cos(ang_r), jnp.cos(ang_r), jnp.cos(ang_c), jnp.cos(ang_c)], axis=-1)
        sin = jnp.concatenate([-jnp.sin(ang_r), jnp.sin(ang_r), -jnp.sin(ang_c), jnp.sin(ang_c)], axis=-1)
        reps = width // (4 * half)
        return jnp.tile(cos, (1, reps)), jnp.tile(sin, (1, reps))

    cos_a, sin_a = tables(HEAD_DIM // 4, A_WIDTH)
    cos_b, sin_b = tables(B_QK_DIM // 4, B_WIDTH)
    return cos_a, sin_a, cos_b, sin_b


def _state_to_blockdiag(s):
    st = jnp.swapaxes(s, -1, -2)
    eye = jnp.eye(C_HEADS, dtype=s.dtype)
    full = st[..., :, :, None, :] * eye[:, None, :, None]
    return full.reshape(s.shape[:-3] + (C_WIDTH, C_KW))


def _blockdiag_to_state(s_t):
    blocks = [s_t[..., C_DV * h:C_DV * (h + 1), C_DK * h:C_DK * (h + 1)] for h in range(C_HEADS)]
    return jnp.swapaxes(jnp.stack(blocks, axis=-3), -1, -2)


def _prepare_weights(g_pre, g_post, w_in, w_out, a_q_gain, a_k_gain, b_lambda_q1, b_lambda_k1,
                     b_lambda_q2, b_lambda_k2, b_out_gain, c_gate_w_fwd, c_gate_b_fwd, c_gate_w_bwd,
                     c_gate_b_bwd, c_out_gain):
    lr0 = OFF_U
    w_perm = jnp.concatenate(
        [w_in[:, :, :lr0], w_in[:, :, lr0 + 2 * GATE_RANK:], w_in[:, :, lr0:lr0 + 2 * GATE_RANK],
         jnp.zeros((DEPTH, D_MODEL, LR_PAD - 2 * GATE_RANK), w_in.dtype)], axis=-1).astype(BF16)
    pad_lo = jnp.zeros((DEPTH, GATE_RANK, C_KW), F32)
    pad_hi = jnp.zeros((DEPTH, LR_PAD - 2 * GATE_RANK, C_KW), F32)
    cw_f = jnp.concatenate([c_gate_w_fwd, pad_lo, pad_hi], axis=1).astype(BF16)
    cw_b = jnp.concatenate([pad_lo, c_gate_w_bwd, pad_hi], axis=1).astype(BF16)
    return {
        "g_pre": g_pre[:, None, :],
        "g_post": g_post[:, None, :],
        "w_in": w_perm,
        "w_out": w_out.astype(BF16),
        "aq_gain": jnp.tile(a_q_gain, (1, A_HEADS))[:, None, :],
        "ak_gain": jnp.tile(a_k_gain, (1, A_KV_HEADS))[:, None, :],
        "cw_f": cw_f,
        "cb_f": c_gate_b_fwd[:, None, :],
        "cw_b": cw_b,
        "cb_b": c_gate_b_bwd[:, None, :],
        "b_out_gain": b_out_gain[:, None, :],
        "c_out_gain": jnp.tile(c_out_gain, (1, C_HEADS))[:, None, :],
        "lam_params": jnp.stack([b_lambda_q1, b_lambda_k1, b_lambda_q2, b_lambda_k2], axis=1),
    }


def kernel(x_prompt, x_sample, c, cache_a_k, cache_a_v, cache_b_k, cache_b_v, state_c_fwd, state_c_bwd, c_ctx, w_mod, b_mod, g_pre, g_post, w_in, w_out, a_q_gain, a_k_gain, b_lambda_q1, b_lambda_k1, b_lambda_q2, b_lambda_k2, b_out_gain, c_gate_w_fwd, c_gate_b_fwd, c_gate_w_bwd, c_gate_b_bwd, c_out_gain):
    dec_batch = x_sample.shape[0]
    dec_seq = x_sample.shape[1]

    mod_rows = 16
    cvec = jnp.zeros((mod_rows, D_MODEL), F32).at[0:dec_batch].set(c).at[dec_batch].set(c_ctx)
    mod = _modulation(cvec, w_mod, b_mod)[:, :, None, :]
    mod_lat = mod[:, 0:dec_batch]
    mod_ctx = mod[:, dec_batch:dec_batch + 1]

    wts = _prepare_weights(g_pre, g_post, w_in, w_out, a_q_gain, a_k_gain, b_lambda_q1, b_lambda_k1,
                           b_lambda_q2, b_lambda_k2, b_out_gain, c_gate_w_fwd, c_gate_b_fwd,
                           c_gate_w_bwd, c_gate_b_bwd, c_out_gain)
    rope_tabs = _rope_tables(dec_seq)
    cache = (cache_a_k, cache_a_v, cache_b_k, cache_b_v,
             _state_to_blockdiag(state_c_fwd), _state_to_blockdiag(state_c_bwd))

    y_p, y_s = x_prompt, x_sample
    kv_ctx = None
    states_f, states_b = [], []
    for l in range(DEPTH):
        lam_init = 0.8 - 0.6 * math.exp(-0.3 * l)
        proj_p = _in_projection(y_p, mod_ctx, False, l, wts, None, 256, kv_ctx)
        kv_ctx = (proj_p[1], proj_p[2], proj_p[4], proj_p[5])
        y_p, s_f, s_b = _mixer(y_p, mod_ctx, False, l, proj_p, wts, lam_init, None)
        states_f.append(s_f)
        states_b.append(s_b)

        proj_s = _in_projection(y_s, mod_lat, True, l, wts, rope_tabs, 512)
        (y_s,) = _mixer(y_s, mod_lat, True, l, proj_s, wts, lam_init, cache)

    new_s_f = _blockdiag_to_state(jnp.stack(states_f, axis=1))
    new_s_b = _blockdiag_to_state(jnp.stack(states_b, axis=1))
    return (y_p, y_s, *kv_ctx, new_s_f, new_s_b)
```

```python
import functools
import math

import jax
import jax.numpy as jnp
from jax import lax
from jax.experimental import pallas as pl
from jax.experimental.pallas import tpu as pltpu

F32 = jnp.float32
BF16 = jnp.bfloat16

D_MODEL = 1024
DEPTH = 2
GRID_W = 64
HEAD_DIM = 64
A_HEADS = 8
A_KV_HEADS = 2
A_GROUP = A_HEADS // A_KV_HEADS
A_WIDTH = A_HEADS * HEAD_DIM
A_KV_WIDTH = A_KV_HEADS * HEAD_DIM
B_HEADS = 4
B_QK_DIM = 32
B_V_DIM = 64
B_WIDTH = B_HEADS * B_V_DIM
C_HEADS = 4
C_DK = 32
C_DV = 64
C_KW = C_HEADS * C_DK
C_WIDTH = C_HEADS * C_DV
GATE_RANK = 16
GLA_TAU = 16.0
CHUNK = 64
D_MIX = A_WIDTH + B_WIDTH + C_WIDTH
ROPE_THETA = 10000.0
EPS = 1e-6

LANES = 128
GLA_BLOCK = 256
LR_PAD = LANES
VT_ROWS = HEAD_DIM + 16
SCORE_LOOKAHEAD = 3
LOG2E = math.log2(math.e)

OFF_AQ = 0
OFF_AK = OFF_AQ + A_WIDTH
OFF_AV = OFF_AK + A_KV_WIDTH
OFF_BQ = OFF_AV + A_KV_WIDTH
OFF_BK = OFF_BQ + B_WIDTH
OFF_BV = OFF_BK + B_WIDTH
OFF_CQ = OFF_BV + B_WIDTH
OFF_CK = OFF_CQ + C_KW
OFF_CV = OFF_CK + C_KW
OFF_U = OFF_CV + C_WIDTH
OFF_LR = OFF_U + D_MIX
IN_PAD_WIDTH = OFF_LR + LR_PAD

VMEM_LIMIT = 56 * 1024 * 1024


def _dot(a, b):
    return jnp.dot(a, b, preferred_element_type=F32)


def _dot_nt(a, b):
    return lax.dot_general(a, b, (((1,), (1,)), ((), ())), preferred_element_type=F32)


def _dot_tn(a, b):
    return lax.dot_general(a, b, (((0,), (0,)), ((), ())), preferred_element_type=F32)


def _split_bf16(x):
    hi = x.astype(BF16)
    lo = (x - hi.astype(F32)).astype(BF16)
    return hi, lo


def _iota(shape, dim):
    return lax.broadcasted_iota(jnp.int32, shape, dim)


def _group_mean_sq(x, group_log2):
    width = x.shape[-1]
    r = lax.shift_right_logical(_iota((LANES, LANES), 0), group_log2)
    c = lax.shift_right_logical(_iota((LANES, LANES), 1), group_log2)
    ones = jnp.where(r == c, 1.0, 0.0).astype(BF16)
    hi, lo = _split_bf16(x * x)
    cols = []
    for j in range(width // LANES):
        sl = slice(LANES * j, LANES * (j + 1))
        cols.append(_dot(hi[:, sl], ones) + _dot(lo[:, sl], ones))
    ss = cols[0] if len(cols) == 1 else jnp.concatenate(cols, axis=-1)
    return ss * (1.0 / (1 << group_log2))


def _rope(x, cos, sin_signed, dist):
    width = x.shape[-1]
    lane = _iota(x.shape, 1)
    first = (lane & (2 * dist - 1)) < dist
    up = pltpu.roll(x, width - dist, 1)
    down = pltpu.roll(x, dist, 1)
    return x * cos + jnp.where(first, up, down) * sin_signed


def _log_sigmoid(x):
    return jnp.minimum(x, 0.0) - jnp.log1p(jnp.exp(-jnp.abs(x)))


def _silu(x):
    return x * (1.0 / (1.0 + jnp.exp(-x)))


def _mod_kernel(c_ref, w_ref, b_ref, o_ref):
    a = _silu(c_ref[...]).astype(BF16)
    o_ref[...] = _dot(a, w_ref[...].astype(BF16)) + b_ref[...]


def _modulation(cvec, w_mod, b_mod):
    rows = cvec.shape[0]
    nblk = 3
    return pl.pallas_call(
        _mod_kernel,
        grid=(DEPTH, nblk),
        in_specs=[
            pl.BlockSpec((rows, D_MODEL), lambda l, n: (0, 0)),
            pl.BlockSpec((None, D_MODEL, D_MODEL), lambda l, n: (l, 0, n)),
            pl.BlockSpec((None, 1, D_MODEL), lambda l, n: (l, 0, n)),
        ],
        out_specs=pl.BlockSpec((None, rows, D_MODEL), lambda l, n: (l, 0, n)),
        out_shape=jax.ShapeDtypeStruct((DEPTH, rows, 3 * D_MODEL), F32),
        compiler_params=pltpu.CompilerParams(
            dimension_semantics=("arbitrary", "arbitrary"), vmem_limit_bytes=VMEM_LIMIT),
        name="modulation",
    )(cvec, w_mod, b_mod.reshape(DEPTH, 1, 3 * D_MODEL))


def _in_kernel(rope, kv_dtype, n_aliased, *refs):
    (x_ref, mod_ref, gpre_ref, w_ref, aqg_ref, akg_ref, cwf_ref, cbf_ref, cwb_ref, cbb_ref) = refs[:10]
    refs = refs[10:]
    if rope:
        cosa_ref, sina_ref, cosb_ref, sinb_ref = refs[:4]
        refs = refs[4:]
    refs = refs[n_aliased:]
    (qa_ref, ka_ref, va_ref, qb_ref, kb_ref, vb_ref,
     cq_ref, ck_ref, cv_ref, gf_ref, gb_ref, su_ref) = refs

    x = x_ref[...]
    shift = mod_ref[:, 0:D_MODEL]
    scale = mod_ref[:, D_MODEL:2 * D_MODEL]
    ms = jnp.mean(x * x, axis=-1, keepdims=True)
    h = (x * lax.rsqrt(ms + EPS)) * gpre_ref[...] * (1.0 + scale) + shift
    hb = h.astype(BF16)

    def proj(off, width):
        return _dot(hb, w_ref[:, off:off + width])

    def store_heads(ref, val, n_heads, dtype):
        for hd in range(n_heads):
            ref[hd] = val[:, HEAD_DIM * hd:HEAD_DIM * (hd + 1)].astype(dtype)

    aq = proj(OFF_AQ, A_WIDTH)
    aq = aq * lax.rsqrt(_group_mean_sq(aq, 6) + EPS) * aqg_ref[...]
    ak = proj(OFF_AK, A_KV_WIDTH)
    ak = ak * lax.rsqrt(_group_mean_sq(ak, 6) + EPS) * akg_ref[...]
    if rope:
        aq = _rope(aq, cosa_ref[...], sina_ref[...], 16)
        ak = _rope(ak, cosa_ref[:, 0:A_KV_WIDTH], sina_ref[:, 0:A_KV_WIDTH], 16)
    store_heads(qa_ref, aq * (HEAD_DIM ** -0.5 * LOG2E), A_HEADS, BF16)
    store_heads(ka_ref, ak, A_KV_HEADS, kv_dtype)
    store_heads(va_ref, proj(OFF_AV, A_KV_WIDTH), A_KV_HEADS, kv_dtype)

    bq = proj(OFF_BQ, B_WIDTH)
    bk = proj(OFF_BK, B_WIDTH)
    if rope:
        bq = _rope(bq, cosb_ref[...], sinb_ref[...], 8)
        bk = _rope(bk, cosb_ref[...], sinb_ref[...], 8)
    store_heads(qb_ref, bq * (B_QK_DIM ** -0.5 * LOG2E), B_HEADS, BF16)
    store_heads(kb_ref, bk, B_HEADS, kv_dtype)
    store_heads(vb_ref, proj(OFF_BV, B_WIDTH), B_HEADS, kv_dtype)

    cq_ref[...] = proj(OFF_CQ, C_KW) * (C_DK ** -0.5)
    ck_ref[...] = proj(OFF_CK, C_KW)
    cv_ref[...] = proj(OFF_CV, C_WIDTH)
    lr = proj(OFF_LR, LR_PAD).astype(BF16)
    gf_ref[...] = _log_sigmoid(_dot(lr, cwf_ref[...]) + cbf_ref[...]) * (1.0 / GLA_TAU)
    gb_ref[...] = _log_sigmoid(_dot(lr, cwb_ref[...]) + cbb_ref[...]) * (1.0 / GLA_TAU)

    su_ref[...] = _silu(proj(OFF_U, D_MIX))


def _in_projection(x, mod, per_batch_mod, layer, wts, rope_tabs, tm, kv_prev=None):
    bsz, seq, _ = x.shape
    rope = rope_tabs is not None
    stacked = not rope
    kv_dtype = F32 if stacked else BF16
    grid = (bsz, seq // tm)

    def per_layer(shape):
        return pl.BlockSpec((None,) + shape, lambda b, t: (layer,) + (0,) * len(shape))

    mod_idx = (lambda b, t: (layer, b, 0, 0)) if per_batch_mod else (lambda b, t: (layer, 0, 0, 0))
    in_specs = [
        pl.BlockSpec((None, tm, D_MODEL), lambda b, t: (b, t, 0)),
        pl.BlockSpec((None, None, 1, 3 * D_MODEL), mod_idx),
        per_layer((1, D_MODEL)),
        per_layer((D_MODEL, IN_PAD_WIDTH)),
        per_layer((1, A_WIDTH)),
        per_layer((1, A_KV_WIDTH)),
        per_layer((LR_PAD, C_KW)),
        per_layer((1, C_KW)),
        per_layer((LR_PAD, C_KW)),
        per_layer((1, C_KW)),
    ]
    args = [x, mod, wts["g_pre"], wts["w_in"], wts["aq_gain"], wts["ak_gain"],
            wts["cw_f"], wts["cb_f"], wts["cw_b"], wts["cb_b"]]
    if rope:
        in_specs += [
            pl.BlockSpec((tm, A_WIDTH), lambda b, t: (t, 0)),
            pl.BlockSpec((tm, A_WIDTH), lambda b, t: (t, 0)),
            pl.BlockSpec((tm, B_WIDTH), lambda b, t: (t, 0)),
            pl.BlockSpec((tm, B_WIDTH), lambda b, t: (t, 0)),
        ]
        args += list(rope_tabs)
    aliases = {}
    if kv_prev is not None:
        for j, buf in enumerate(kv_prev):
            aliases[len(args)] = (1, 2, 4, 5)[j]
            in_specs.append(pl.BlockSpec(memory_space=pl.ANY))
            args.append(buf)

    def heads(n):
        return pl.BlockSpec((None, n, tm, HEAD_DIM), lambda b, t: (b, 0, t, 0))

    def kv_heads(n):
        if stacked:
            return pl.BlockSpec((None, None, n, tm, HEAD_DIM), lambda b, t: (b, layer, 0, t, 0))
        return heads(n)

    def rows(width):
        return pl.BlockSpec((None, tm, width), lambda b, t: (b, t, 0))

    out_specs = [heads(A_HEADS), kv_heads(A_KV_HEADS), kv_heads(A_KV_HEADS),
                 heads(B_HEADS), kv_heads(B_HEADS), kv_heads(B_HEADS),
                 rows(C_KW), rows(C_KW), rows(C_WIDTH), rows(C_KW), rows(C_KW), rows(D_MIX)]

    def hshape(n):
        return jax.ShapeDtypeStruct((bsz, n, seq, HEAD_DIM), BF16)

    def kvshape(n):
        if stacked:
            return jax.ShapeDtypeStruct((bsz, DEPTH, n, seq, HEAD_DIM), F32)
        return hshape(n)

    def rshape(width):
        return jax.ShapeDtypeStruct((bsz, seq, width), F32)

    out_shape = [hshape(A_HEADS), kvshape(A_KV_HEADS), kvshape(A_KV_HEADS),
                 hshape(B_HEADS), kvshape(B_HEADS), kvshape(B_HEADS),
                 rshape(C_KW), rshape(C_KW), rshape(C_WIDTH), rshape(C_KW), rshape(C_KW),
                 rshape(D_MIX)]
    return pl.pallas_call(
        functools.partial(_in_kernel, rope, kv_dtype, len(aliases)),
        grid=grid,
        in_specs=in_specs,
        out_specs=out_specs,
        out_shape=out_shape,
        input_output_aliases=aliases,
        compiler_params=pltpu.CompilerParams(
            dimension_semantics=("arbitrary", "arbitrary"), vmem_limit_bytes=VMEM_LIMIT),
        name="in_projection_rope" if rope else "in_projection",
    )(*args)


def _gla_direction(cq_ref, ck_ref, cv_ref, g_ref, s_t, reverse, seq, oc_ref, accumulate):
    bl = GLA_BLOCK
    ri = _iota((bl, bl), 0)
    ci = _iota((bl, bl), 1)
    same_chunk = lax.shift_right_logical(ri, 6) == lax.shift_right_logical(ci, 6)
    causal = same_chunk & ((ci >= ri) if reverse else (ci <= ri))
    tri = jnp.where(causal, 1.0, 0.0).astype(BF16)
    bd = (lax.shift_right_logical(_iota((C_WIDTH, C_KW), 0), 6)
          == lax.shift_right_logical(_iota((C_WIDTH, C_KW), 1), 5))
    khead = lax.shift_right_logical(_iota((1, C_KW), 1), 5)
    vhead = lax.shift_right_logical(_iota((1, C_WIDTH), 1), 6)
    n_sub = bl // CHUNK

    blocks = range(seq // bl)
    for blk in (reversed(blocks) if reverse else blocks):
        r0 = blk * bl
        q = cq_ref[r0:r0 + bl, :]
        k = ck_ref[r0:r0 + bl, :]
        v = cv_ref[r0:r0 + bl, :]
        g = g_ref[r0:r0 + bl, :]
        g_hi, g_lo = _split_bf16(g)
        cum = _dot(tri, g_hi) + _dot(tri, g_lo)
        qt = q * jnp.exp(cum)
        kt = k * jnp.exp(-cum)
        ktb = kt.astype(BF16)
        vb = v.astype(BF16)

        o = jnp.zeros((bl, C_WIDTH), F32)
        for hd in range(C_HEADS):
            qh = jnp.where(khead == hd, qt, 0.0).astype(BF16)
            a = jnp.where(causal, _dot_nt(qh, ktb), 0.0).astype(BF16)
            vh = jnp.where(vhead == hd, v, 0.0).astype(BF16)
            o = o + _dot(a, vh)

        qtb = qt.astype(BF16)
        inter = [None] * n_sub
        subs = range(n_sub)
        for c in (reversed(subs) if reverse else subs):
            c0 = CHUNK * c
            edge = c0 if reverse else c0 + CHUNK - 1
            last = cum[edge:edge + 1, :]
            inter[c] = _dot_nt(qtb[c0:c0 + CHUNK], s_t.astype(BF16))
            kdec = (k[c0:c0 + CHUNK] * jnp.exp(last - cum[c0:c0 + CHUNK])).astype(BF16)
            kv = _dot_tn(vb[c0:c0 + CHUNK], kdec)
            s_t = jnp.exp(last) * s_t + jnp.where(bd, kv, 0.0)
        o = o + jnp.concatenate(inter, axis=0)
        if accumulate:
            oc_ref[r0:r0 + bl, :] = oc_ref[r0:r0 + bl, :] + o
        else:
            oc_ref[r0:r0 + bl, :] = o
    return s_t


def _attend_t(jobs):
    def scores(i):
        return _dot_nt(jobs[i][0][...], jobs[i][1])

    outs = []
    pending = [scores(i) for i in range(min(SCORE_LOOKAHEAD, len(jobs)))]
    for i, (_, _, vt) in enumerate(jobs):
        st = pending.pop(0)
        if i + SCORE_LOOKAHEAD < len(jobs):
            pending.append(scores(i + SCORE_LOOKAHEAD))
        m = jnp.max(st, axis=0, keepdims=True)
        p = jnp.exp2(st - m).astype(BF16)
        ot = _dot(vt[...], p)
        outs.append(ot[0:HEAD_DIM] * (1.0 / ot[HEAD_DIM:HEAD_DIM + 1]))
    return outs


def _pair_rows(a, b):
    return jnp.concatenate([a, b], axis=0).T


def _mix_kernel(cached, lam_init, seq, qt, qb, *refs):
    it = iter(refs)
    x_ref, mod_ref = next(it), next(it)
    qa_ref, ka_ref, va_ref = next(it), next(it), next(it)
    qb_ref, kb_ref, vb_ref = next(it), next(it), next(it)
    cq_ref, ck_ref, cv_ref, gf_ref, gb_ref, su_ref = (next(it) for _ in range(6))
    if cached:
        cak_ref, cav_ref, cbk_ref, cbv_ref, s0f_ref, s0b_ref = (next(it) for _ in range(6))
    wout_ref, gpost_ref, bog_ref, cog_ref, lamp_ref = (next(it) for _ in range(5))
    y_ref = next(it)
    if not cached:
        sf_ref, sb_ref = next(it), next(it)
    kA_s, vtA_s, kB_s, vtB_s, oc_s, mixed_s = (next(it) for _ in range(6))

    lk = kA_s.shape[1]
    past = lk - seq
    t = pl.program_id(1)

    @pl.when(t == 0)
    def _per_sequence():
        eye = jnp.where(_iota((HEAD_DIM, HEAD_DIM), 0) == _iota((HEAD_DIM, HEAD_DIM), 1),
                        1.0, 0.0).astype(BF16)
        ones_row = jnp.where(_iota((VT_ROWS - HEAD_DIM, lk), 0) == 0, 1.0, 0.0).astype(BF16)
        for ref_new, ref_cache, dst in ((ka_ref, cak_ref if cached else None, kA_s),
                                        (kb_ref, cbk_ref if cached else None, kB_s)):
            for hd in range(dst.shape[0]):
                if cached:
                    dst[hd, 0:past, :] = ref_cache[hd].astype(BF16)
                dst[hd, past:lk, :] = ref_new[hd].astype(BF16)
        for ref_new, ref_cache, dst in ((va_ref, cav_ref if cached else None, vtA_s),
                                        (vb_ref, cbv_ref if cached else None, vtB_s)):
            for hd in range(dst.shape[0]):
                if cached:
                    dst[hd, 0:HEAD_DIM, 0:past] = _dot_nt(eye, ref_cache[hd].astype(BF16)).astype(BF16)
                dst[hd, 0:HEAD_DIM, past:lk] = _dot_nt(eye, ref_new[hd].astype(BF16)).astype(BF16)
                dst[hd, HEAD_DIM:VT_ROWS, :] = ones_row

        if cached:
            s0f, s0b = s0f_ref[...], s0b_ref[...]
        else:
            s0f = jnp.zeros((C_WIDTH, C_KW), F32)
            s0b = s0f
        s_f = _gla_direction(cq_ref, ck_ref, cv_ref, gf_ref, s0f, False, seq, oc_s, False)
        s_b = _gla_direction(cq_ref, ck_ref, cv_ref, gb_ref, s0b, True, seq, oc_s, True)
        if not cached:
            sf_ref[...] = s_f
            sb_ref[...] = s_b
        for r0 in range(0, seq, GLA_BLOCK):
            oc = oc_s[r0:r0 + GLA_BLOCK, :]
            oc_s[r0:r0 + GLA_BLOCK, :] = oc * lax.rsqrt(_group_mean_sq(oc, 6) + EPS) * cog_ref[...]

    lam = (jnp.exp(jnp.sum(lamp_ref[0:1, :] * lamp_ref[1:2, :], axis=-1, keepdims=True))
           - jnp.exp(jnp.sum(lamp_ref[2:3, :] * lamp_ref[3:4, :], axis=-1, keepdims=True))
           + lam_init)

    def attn_block(i, carry):
        rows = pl.ds(pl.multiple_of(i * qb, qb), qb)
        jobs = []
        for grp in range(A_KV_HEADS):
            q4 = qa_ref[A_GROUP * grp:A_GROUP * (grp + 1), rows, :].reshape(A_GROUP * qb, HEAD_DIM)
            jobs.append((kA_s.at[grp], q4, vtA_s.at[grp]))
        lane = _iota((qb, HEAD_DIM), 1)
        for hd in range(B_HEADS):
            q = qb_ref[hd, rows, :]
            zero = jnp.zeros_like(q)
            q2 = jnp.concatenate([jnp.where(lane < B_QK_DIM, q, zero),
                                  jnp.where(lane >= B_QK_DIM, q, zero)], axis=0)
            jobs.append((kB_s.at[hd], q2, vtB_s.at[hd]))
        outs = _attend_t(jobs)
        for grp in range(A_KV_HEADS):
            ot = outs[grp]
            for pair in range(A_GROUP // 2):
                c0 = 2 * pair * qb
                col = A_GROUP * HEAD_DIM * grp + 2 * HEAD_DIM * pair
                mixed_s[rows, col:col + 2 * HEAD_DIM] = _pair_rows(ot[:, c0:c0 + qb],
                                                                   ot[:, c0 + qb:c0 + 2 * qb])
        obs = []
        for hd in range(B_HEADS):
            ot = outs[A_KV_HEADS + hd]
            ob = ot[:, 0:qb] - lam * ot[:, qb:2 * qb]
            obs.append(ob * lax.rsqrt(jnp.mean(ob * ob, axis=0, keepdims=True) + EPS))
        for pair in range(B_HEADS // 2):
            col = A_WIDTH + 2 * B_V_DIM * pair
            mixed_s[rows, col:col + 2 * B_V_DIM] = (_pair_rows(obs[2 * pair], obs[2 * pair + 1])
                                                    * bog_ref[...] * (1.0 - lam_init))
        return carry

    lax.fori_loop(0, qt // qb, attn_block, 0)

    seq_rows = pl.ds(pl.multiple_of(t * qt, qt), qt)
    mixed_s[:, A_WIDTH + B_WIDTH:D_MIX] = oc_s[seq_rows, :]
    gate = mod_ref[:, 2 * D_MODEL:3 * D_MODEL]
    mixed = (mixed_s[...] * su_ref[...]).astype(BF16)
    y = _dot(mixed, wout_ref[...])
    yn = y * lax.rsqrt(jnp.mean(y * y, axis=-1, keepdims=True) + EPS) * gpost_ref[...]
    y_ref[...] = x_ref[...] + gate * yn


def _mixer(x, mod, per_batch_mod, layer, proj, wts, lam_init, cache):
    bsz, seq, _ = x.shape
    cached = cache is not None
    past = cache[0].shape[3] if cached else 0
    qa, ka, va, qb, kb, vb, cq, ck, cv, gf, gb, su = proj
    qt = 256
    qb_rows = 128

    def per_layer(shape):
        return pl.BlockSpec((None,) + shape, lambda b, t: (layer,) + (0,) * len(shape))

    def layer_heads(n, length):
        return pl.BlockSpec((None, None, n, length, HEAD_DIM), lambda b, t: (b, layer, 0, 0, 0))

    def kv_heads(n):
        if cached:
            return pl.BlockSpec((None, n, seq, HEAD_DIM), lambda b, t: (b, 0, 0, 0))
        return layer_heads(n, seq)

    def head_tile(n):
        return pl.BlockSpec((None, n, qt, HEAD_DIM), lambda b, t: (b, 0, t, 0))

    def rows(width):
        return pl.BlockSpec((None, seq, width), lambda b, t: (b, 0, 0))

    def row_tile(width):
        return pl.BlockSpec((None, qt, width), lambda b, t: (b, t, 0))

    mod_idx = (lambda b, t: (layer, b, 0, 0)) if per_batch_mod else (lambda b, t: (layer, 0, 0, 0))
    in_specs = [row_tile(D_MODEL), pl.BlockSpec((None, None, 1, 3 * D_MODEL), mod_idx),
                head_tile(A_HEADS), kv_heads(A_KV_HEADS), kv_heads(A_KV_HEADS),
                head_tile(B_HEADS), kv_heads(B_HEADS), kv_heads(B_HEADS),
                rows(C_KW), rows(C_KW), rows(C_WIDTH), rows(C_KW), rows(C_KW), row_tile(D_MIX)]
    args = [x, mod, qa, ka, va, qb, kb, vb, cq, ck, cv, gf, gb, su]
    if cached:
        state_in = pl.BlockSpec((None, None, C_WIDTH, C_KW), lambda b, t: (b, layer, 0, 0))
        in_specs += [layer_heads(A_KV_HEADS, past), layer_heads(A_KV_HEADS, past),
                     layer_heads(B_HEADS, past), layer_heads(B_HEADS, past), state_in, state_in]
        args += list(cache)
    in_specs += [per_layer((D_MIX, D_MODEL)), per_layer((1, D_MODEL)), per_layer((1, 2 * B_V_DIM)),
                 per_layer((1, C_WIDTH)), per_layer((4, B_QK_DIM))]
    args += [wts["w_out"], wts["g_post"], wts["b_out_gain"], wts["c_out_gain"], wts["lam_params"]]

    out_specs = [row_tile(D_MODEL)]
    out_shape = [jax.ShapeDtypeStruct((bsz, seq, D_MODEL), F32)]
    if not cached:
        state_out = pl.BlockSpec((None, C_WIDTH, C_KW), lambda b, t: (b, 0, 0))
        out_specs += [state_out, state_out]
        out_shape += [jax.ShapeDtypeStruct((bsz, C_WIDTH, C_KW), F32)] * 2

    lk = past + seq
    scratch = [pltpu.VMEM((A_KV_HEADS, lk, HEAD_DIM), BF16), pltpu.VMEM((A_KV_HEADS, VT_ROWS, lk), BF16),
               pltpu.VMEM((B_HEADS, lk, HEAD_DIM), BF16), pltpu.VMEM((B_HEADS, VT_ROWS, lk), BF16),
               pltpu.VMEM((seq, C_WIDTH), F32), pltpu.VMEM((qt, D_MIX), F32)]
    return pl.pallas_call(
        functools.partial(_mix_kernel, cached, lam_init, seq, qt, qb_rows),
        grid=(bsz, seq // qt),
        in_specs=in_specs,
        out_specs=out_specs,
        out_shape=out_shape,
        scratch_shapes=scratch,
        compiler_params=pltpu.CompilerParams(
            dimension_semantics=("arbitrary", "arbitrary"), vmem_limit_bytes=VMEM_LIMIT),
        name="mixer_cached" if cached else "mixer",
    )(*args)


def _rope_tables(seq):
    t = jnp.arange(seq)
    pos_row = (t // GRID_W).astype(F32)
    pos_col = (t % GRID_W).astype(F32)

    def tables(half, width):
        freq = ROPE_THETA ** (-jnp.arange(half, dtype=F32) / half)
        ang_r = pos_row[:, None] * freq[None, :]
        ang_c = pos_col[:, None] * freq[None, :]
        cos = jnp.concatenate([jnp.cos(ang_r), jnp.cos(ang_r), jnp.cos(ang_c), jnp.cos(ang_c)], axis=-1)
        sin = jnp.concatenate([-jnp.sin(ang_r), jnp.sin(ang_r), -jnp.sin(ang_c), jnp.sin(ang_c)], axis=-1)
        reps = width // (4 * half)
        return jnp.tile(cos, (1, reps)), jnp.tile(sin, (1, reps))

    cos_a, sin_a = tables(HEAD_DIM // 4, A_WIDTH)
    cos_b, sin_b = tables(B_QK_DIM // 4, B_WIDTH)
    return cos_a, sin_a, cos_b, sin_b


def _state_to_blockdiag(s):
    st = jnp.swapaxes(s, -1, -2)
    eye = jnp.eye(C_HEADS, dtype=s.dtype)
    full = st[..., :, :, None, :] * eye[:, None, :, None]
    return full.reshape(s.shape[:-3] + (C_WIDTH, C_KW))


def _blockdiag_to_state(s_t):
    blocks = [s_t[..., C_DV * h:C_DV * (h + 1), C_DK * h:C_DK * (h + 1)] for h in range(C_HEADS)]
    return jnp.swapaxes(jnp.stack(blocks, axis=-3), -1, -2)


def _prepare_weights(g_pre, g_post, w_in, w_out, a_q_gain, a_k_gain, b_lambda_q1, b_lambda_k1,
                     b_lambda_q2, b_lambda_k2, b_out_gain, c_gate_w_fwd, c_gate_b_fwd, c_gate_w_bwd,
                     c_gate_b_bwd, c_out_gain):
    lr0 = OFF_U
    w_perm = jnp.concatenate(
        [w_in[:, :, :lr0], w_in[:, :, lr0 + 2 * GATE_RANK:], w_in[:, :, lr0:lr0 + 2 * GATE_RANK],
         jnp.zeros((DEPTH, D_MODEL, LR_PAD - 2 * GATE_RANK), w_in.dtype)], axis=-1).astype(BF16)
    pad_lo = jnp.zeros((DEPTH, GATE_RANK, C_KW), F32)
    pad_hi = jnp.zeros((DEPTH, LR_PAD - 2 * GATE_RANK, C_KW), F32)
    cw_f = jnp.concatenate([c_gate_w_fwd, pad_lo, pad_hi], axis=1).astype(BF16)
    cw_b = jnp.concatenate([pad_lo, c_gate_w_bwd, pad_hi], axis=1).astype(BF16)
    return {
        "g_pre": g_pre[:, None, :],
        "g_post": g_post[:, None, :],
        "w_in": w_perm,
        "w_out": w_out.astype(BF16),
        "aq_gain": jnp.tile(a_q_gain, (1, A_HEADS))[:, None, :],
        "ak_gain": jnp.tile(a_k_gain, (1, A_KV_HEADS))[:, None, :],
        "cw_f": cw_f,
        "cb_f": c_gate_b_fwd[:, None, :],
        "cw_b": cw_b,
        "cb_b": c_gate_b_bwd[:, None, :],
        "b_out_gain": jnp.tile(b_out_gain, (1, 2))[:, None, :],
        "c_out_gain": jnp.tile(c_out_gain, (1, C_HEADS))[:, None, :],
        "lam_params": jnp.stack([b_lambda_q1, b_lambda_k1, b_lambda_q2, b_lambda_k2], axis=1),
    }


def kernel(x_prompt, x_sample, c, cache_a_k, cache_a_v, cache_b_k, cache_b_v, state_c_fwd, state_c_bwd, c_ctx, w_mod, b_mod, g_pre, g_post, w_in, w_out, a_q_gain, a_k_gain, b_lambda_q1, b_lambda_k1, b_lambda_q2, b_lambda_k2, b_out_gain, c_gate_w_fwd, c_gate_b_fwd, c_gate_w_bwd, c_gate_b_bwd, c_out_gain):
    dec_batch = x_sample.shape[0]
    dec_seq = x_sample.shape[1]

    mod_rows = 16
    cvec = jnp.zeros((mod_rows, D_MODEL), F32).at[0:dec_batch].set(c).at[dec_batch].set(c_ctx)
    mod = _modulation(cvec, w_mod, b_mod)[:, :, None, :]
    mod_lat = mod[:, 0:dec_batch]
    mod_ctx = mod[:, dec_batch:dec_batch + 1]

    wts = _prepare_weights(g_pre, g_post, w_in, w_out, a_q_gain, a_k_gain, b_lambda_q1, b_lambda_k1,
                           b_lambda_q2, b_lambda_k2, b_out_gain, c_gate_w_fwd, c_gate_b_fwd,
                           c_gate_w_bwd, c_gate_b_bwd, c_out_gain)
    rope_tabs = _rope_tables(dec_seq)
    cache = (cache_a_k, cache_a_v, cache_b_k, cache_b_v,
             _state_to_blockdiag(state_c_fwd), _state_to_blockdiag(state_c_bwd))

    y_p, y_s = x_prompt, x_sample
    kv_ctx = None
    states_f, states_b = [], []
    for l in range(DEPTH):
        lam_init = 0.8 - 0.6 * math.exp(-0.3 * l)
        proj_p = _in_projection(y_p, mod_ctx, False, l, wts, None, 256, kv_ctx)
        kv_ctx = (proj_p[1], proj_p[2], proj_p[4], proj_p[5])
        y_p, s_f, s_b = _mixer(y_p, mod_ctx, False, l, proj_p, wts, lam_init, None)
        states_f.append(s_f)
        states_b.append(s_b)

        proj_s = _in_projection(y_s, mod_lat, True, l, wts, rope_tabs, 512)
        (y_s,) = _mixer(y_s, mod_lat, True, l, proj_s, wts, lam_init, cache)

    new_s_f = _blockdiag_to_state(jnp.stack(states_f, axis=1))
    new_s_b = _blockdiag_to_state(jnp.stack(states_b, axis=1))
    return (y_p, y_s, *kv_ctx, new_s_f, new_s_b)
```

```python
import functools
import math

import jax
import jax.numpy as jnp
from jax import lax
from jax.experimental import pallas as pl
from jax.experimental.pallas import tpu as pltpu

F32 = jnp.float32
BF16 = jnp.bfloat16

D_MODEL = 1024
DEPTH = 2
GRID_W = 64
HEAD_DIM = 64
A_HEADS = 8
A_KV_HEADS = 2
A_GROUP = A_HEADS // A_KV_HEADS
A_WIDTH = A_HEADS * HEAD_DIM
A_KV_WIDTH = A_KV_HEADS * HEAD_DIM
B_HEADS = 4
B_QK_DIM = 32
B_V_DIM = 64
B_WIDTH = B_HEADS * B_V_DIM
C_HEADS = 4
C_DK = 32
C_DV = 64
C_KW = C_HEADS * C_DK
C_WIDTH = C_HEADS * C_DV
GATE_RANK = 16
GLA_TAU = 16.0
CHUNK = 64
D_MIX = A_WIDTH + B_WIDTH + C_WIDTH
ROPE_THETA = 10000.0
EPS = 1e-6

LANES = 128
GLA_BLOCK = 256
VT_ROWS = HEAD_DIM + 16
SCORE_LOOKAHEAD = 3
LOG2E = math.log2(math.e)

OFF_AQ = 0
OFF_AK = OFF_AQ + A_WIDTH
OFF_AV = OFF_AK + A_KV_WIDTH
OFF_BQ = OFF_AV + A_KV_WIDTH
OFF_BK = OFF_BQ + B_WIDTH
OFF_BV = OFF_BK + B_WIDTH
OFF_CQ = OFF_BV + B_WIDTH
OFF_CK = OFF_CQ + C_KW
OFF_CV = OFF_CK + C_KW
OFF_LR = OFF_CV + C_WIDTH
OFF_U = OFF_LR + 2 * GATE_RANK
IN_WIDTH = OFF_U + D_MIX

VMEM_LIMIT = 56 * 1024 * 1024


def _dot(a, b):
    return jnp.dot(a, b, preferred_element_type=F32)


def _dot_nt(a, b):
    return lax.dot_general(a, b, (((1,), (1,)), ((), ())), preferred_element_type=F32)


def _dot_tn(a, b):
    return lax.dot_general(a, b, (((0,), (0,)), ((), ())), preferred_element_type=F32)


def _split_bf16(x):
    hi = x.astype(BF16)
    lo = (x - hi.astype(F32)).astype(BF16)
    return hi, lo


def _iota(shape, dim):
    return lax.broadcasted_iota(jnp.int32, shape, dim)


def _group_mean_sq(x, group_log2):
    width = x.shape[-1]
    r = lax.shift_right_logical(_iota((LANES, LANES), 0), group_log2)
    c = lax.shift_right_logical(_iota((LANES, LANES), 1), group_log2)
    ones = jnp.where(r == c, 1.0, 0.0).astype(BF16)
    hi, lo = _split_bf16(x * x)
    cols = []
    for j in range(width // LANES):
        sl = slice(LANES * j, LANES * (j + 1))
        cols.append(_dot(hi[:, sl], ones) + _dot(lo[:, sl], ones))
    ss = cols[0] if len(cols) == 1 else jnp.concatenate(cols, axis=-1)
    return ss * (1.0 / (1 << group_log2))


def _rope(x, cos, sin_signed, dist):
    width = x.shape[-1]
    lane = _iota(x.shape, 1)
    first = (lane & (2 * dist - 1)) < dist
    up = pltpu.roll(x, width - dist, 1)
    down = pltpu.roll(x, dist, 1)
    return x * cos + jnp.where(first, up, down) * sin_signed


def _log_sigmoid(x):
    return jnp.minimum(x, 0.0) - jnp.log1p(jnp.exp(-jnp.abs(x)))


def _silu(x):
    return x * (1.0 / (1.0 + jnp.exp(-x)))


def _mod_kernel(c_ref, w_ref, b_ref, o_ref):
    a = _silu(c_ref[...]).astype(BF16)
    o_ref[...] = _dot(a, w_ref[...].astype(BF16)) + b_ref[...]


def _modulation(cvec, w_mod, b_mod):
    rows = cvec.shape[0]
    nblk = 3
    return pl.pallas_call(
        _mod_kernel,
        grid=(DEPTH, nblk),
        in_specs=[
            pl.BlockSpec((rows, D_MODEL), lambda l, n: (0, 0)),
            pl.BlockSpec((None, D_MODEL, D_MODEL), lambda l, n: (l, 0, n)),
            pl.BlockSpec((None, 1, D_MODEL), lambda l, n: (l, 0, n)),
        ],
        out_specs=pl.BlockSpec((None, rows, D_MODEL), lambda l, n: (l, 0, n)),
        out_shape=jax.ShapeDtypeStruct((DEPTH, rows, 3 * D_MODEL), F32),
        compiler_params=pltpu.CompilerParams(
            dimension_semantics=("arbitrary", "arbitrary"), vmem_limit_bytes=VMEM_LIMIT),
        name="modulation",
    )(cvec, w_mod, b_mod.reshape(DEPTH, 1, 3 * D_MODEL))


def _in_kernel(rope, layer, stacked_first, nb, tl, n_aliased, *refs):
    (x_ref, mod_ref, gpre_ref, wt_ref, aqg_ref, akg_ref, cwf_ref, cbf_ref, cwb_ref, cbb_ref) = refs[:10]
    refs = refs[10:]
    if rope:
        cosa_ref, sina_ref, cosb_ref, sinb_ref = refs[:4]
        refs = refs[4:]
    refs = refs[n_aliased:]
    (qa_ref, ka_ref, qb_ref, kb_ref, cq_ref, ck_ref, cv_ref, gf_ref, gb_ref, su_ref,
     vat_ref, vbt_ref) = refs[:12]
    kat_ref, kbt_ref = refs[12:] if len(refs) > 12 else (None, None)

    x = x_ref[...].reshape(nb * tl, D_MODEL)
    shift = mod_ref[:, 0:D_MODEL]
    scale = mod_ref[:, D_MODEL:2 * D_MODEL]
    ms = jnp.mean(x * x, axis=-1, keepdims=True)
    h = (x * lax.rsqrt(ms + EPS)) * gpre_ref[...] * (1.0 + scale) + shift
    hb = h.astype(BF16)

    def proj(off, width):
        return _dot_nt(hb, wt_ref[off:off + width, :])

    def put_rows(ref, val):
        for bi in range(nb):
            ref[bi] = val[bi * tl:(bi + 1) * tl].astype(ref.dtype)

    def put_heads(ref, val, n_heads):
        for bi in range(nb):
            for hd in range(n_heads):
                ref[bi, hd] = val[bi * tl:(bi + 1) * tl,
                                  HEAD_DIM * hd:HEAD_DIM * (hd + 1)].astype(ref.dtype)

    def put_heads_t(ref, val, n_heads):
        val_t = val.T
        for bi in range(nb):
            for hd in range(n_heads):
                blk = val_t[HEAD_DIM * hd:HEAD_DIM * (hd + 1), bi * tl:(bi + 1) * tl].astype(ref.dtype)
                if stacked_first:
                    for l2 in range(DEPTH):
                        ref[bi, l2, hd] = blk if l2 == layer else jnp.zeros_like(blk)
                else:
                    ref[bi, hd] = blk

    aq = proj(OFF_AQ, A_WIDTH)
    aq = aq * lax.rsqrt(_group_mean_sq(aq, 6) + EPS) * aqg_ref[...]
    ak = proj(OFF_AK, A_KV_WIDTH)
    ak = ak * lax.rsqrt(_group_mean_sq(ak, 6) + EPS) * akg_ref[...]
    if rope:
        aq = _rope(aq, cosa_ref[...], sina_ref[...], 16)
        ak = _rope(ak, cosa_ref[:, 0:A_KV_WIDTH], sina_ref[:, 0:A_KV_WIDTH], 16)
    put_heads(qa_ref, aq * (HEAD_DIM ** -0.5 * LOG2E), A_HEADS)
    put_heads(ka_ref, ak, A_KV_HEADS)
    put_heads_t(vat_ref, proj(OFF_AV, A_KV_WIDTH), A_KV_HEADS)
    if kat_ref is not None:
        put_heads_t(kat_ref, ak, A_KV_HEADS)

    bq = proj(OFF_BQ, B_WIDTH)
    bk = proj(OFF_BK, B_WIDTH)
    if rope:
        bq = _rope(bq, cosb_ref[...], sinb_ref[...], 8)
        bk = _rope(bk, cosb_ref[...], sinb_ref[...], 8)
    put_heads(qb_ref, bq * (B_QK_DIM ** -0.5 * LOG2E), B_HEADS)
    put_heads(kb_ref, bk, B_HEADS)
    put_heads_t(vbt_ref, proj(OFF_BV, B_WIDTH), B_HEADS)
    if kbt_ref is not None:
        put_heads_t(kbt_ref, bk, B_HEADS)

    put_rows(cq_ref, proj(OFF_CQ, C_KW) * (C_DK ** -0.5))
    put_rows(ck_ref, proj(OFF_CK, C_KW))
    put_rows(cv_ref, proj(OFF_CV, C_WIDTH))
    lr = proj(OFF_LR, 2 * GATE_RANK).astype(BF16)
    put_rows(gf_ref, _log_sigmoid(_dot(lr, cwf_ref[...]) + cbf_ref[...]) * (1.0 / GLA_TAU))
    put_rows(gb_ref, _log_sigmoid(_dot(lr, cwb_ref[...]) + cbb_ref[...]) * (1.0 / GLA_TAU))

    put_rows(su_ref, _silu(proj(OFF_U, D_MIX)))


def _in_projection(x, mod, per_batch_mod, layer, wts, rope_tabs, nb, tl, kv_prev=None):
    bsz, seq, _ = x.shape
    rope = rope_tabs is not None
    stacked = not rope
    stacked_first = stacked and kv_prev is None
    grid = (bsz // nb, seq // tl)

    def per_layer(shape):
        return pl.BlockSpec((None,) + shape, lambda b, t: (layer,) + (0,) * len(shape))

    mod_idx = (lambda b, t: (layer, b, 0, 0)) if per_batch_mod else (lambda b, t: (layer, 0, 0, 0))
    in_specs = [
        pl.BlockSpec((nb, tl, D_MODEL), lambda b, t: (b, t, 0)),
        pl.BlockSpec((None, None, 1, 3 * D_MODEL), mod_idx),
        per_layer((1, D_MODEL)),
        per_layer((IN_WIDTH, D_MODEL)),
        per_layer((1, A_WIDTH)),
        per_layer((1, A_KV_WIDTH)),
        per_layer((2 * GATE_RANK, C_KW)),
        per_layer((1, C_KW)),
        per_layer((2 * GATE_RANK, C_KW)),
        per_layer((1, C_KW)),
    ]
    args = [x, mod, wts["g_pre"], wts["w_in_t"], wts["aq_gain"], wts["ak_gain"],
            wts["cw_f"], wts["cb_f"], wts["cw_b"], wts["cb_b"]]
    if rope:
        assert nb == 1
        in_specs += [
            pl.BlockSpec((tl, A_WIDTH), lambda b, t: (t, 0)),
            pl.BlockSpec((tl, A_WIDTH), lambda b, t: (t, 0)),
            pl.BlockSpec((tl, B_WIDTH), lambda b, t: (t, 0)),
            pl.BlockSpec((tl, B_WIDTH), lambda b, t: (t, 0)),
        ]
        args += list(rope_tabs)

    def heads(n):
        return pl.BlockSpec((nb, n, tl, HEAD_DIM), lambda b, t: (b, 0, t, 0))

    def heads_t(n):
        if stacked_first:
            return pl.BlockSpec((nb, DEPTH, n, HEAD_DIM, tl), lambda b, t: (b, 0, 0, 0, t))
        if stacked:
            return pl.BlockSpec((nb, None, n, HEAD_DIM, tl), lambda b, t: (b, layer, 0, 0, t))
        return pl.BlockSpec((nb, n, HEAD_DIM, tl), lambda b, t: (b, 0, 0, t))

    def rows(width):
        return pl.BlockSpec((nb, tl, width), lambda b, t: (b, t, 0))

    def hshape(n):
        return jax.ShapeDtypeStruct((bsz, n, seq, HEAD_DIM), BF16)

    def tshape(n):
        if stacked:
            return jax.ShapeDtypeStruct((bsz, DEPTH, n, HEAD_DIM, seq), F32)
        return jax.ShapeDtypeStruct((bsz, n, HEAD_DIM, seq), BF16)

    def rshape(width):
        return jax.ShapeDtypeStruct((bsz, seq, width), F32)

    out_specs = [heads(A_HEADS), heads(A_KV_HEADS), heads(B_HEADS), heads(B_HEADS),
                 rows(C_KW), rows(C_KW), rows(C_WIDTH), rows(C_KW), rows(C_KW), rows(D_MIX),
                 heads_t(A_KV_HEADS), heads_t(B_HEADS)]
    out_shape = [hshape(A_HEADS), hshape(A_KV_HEADS), hshape(B_HEADS), hshape(B_HEADS),
                 rshape(C_KW), rshape(C_KW), rshape(C_WIDTH), rshape(C_KW), rshape(C_KW),
                 rshape(D_MIX), tshape(A_KV_HEADS), tshape(B_HEADS)]
    if stacked:
        out_specs += [heads_t(A_KV_HEADS), heads_t(B_HEADS)]
        out_shape += [tshape(A_KV_HEADS), tshape(B_HEADS)]
    aliases = {}
    if kv_prev is not None:
        for j, buf in enumerate(kv_prev):
            aliases[len(args)] = 10 + j
            in_specs.append(pl.BlockSpec(memory_space=pl.ANY))
            args.append(buf)
    return pl.pallas_call(
        functools.partial(_in_kernel, rope, layer, stacked_first, nb, tl, len(aliases)),
        grid=grid,
        in_specs=in_specs,
        out_specs=out_specs,
        out_shape=out_shape,
        input_output_aliases=aliases,
        compiler_params=pltpu.CompilerParams(
            dimension_semantics=("arbitrary", "arbitrary"), vmem_limit_bytes=VMEM_LIMIT),
        name="in_projection_rope" if rope else "in_projection",
    )(*args)


def _gla_direction(cq_ref, ck_ref, cv_ref, g_ref, s_t, reverse, seq, oc_ref, accumulate):
    bl = GLA_BLOCK
    ri = _iota((bl, bl), 0)
    ci = _iota((bl, bl), 1)
    same_chunk = lax.shift_right_logical(ri, 6) == lax.shift_right_logical(ci, 6)
    causal = same_chunk & ((ci >= ri) if reverse else (ci <= ri))
    tri = jnp.where(causal, 1.0, 0.0).astype(BF16)
    bd = (lax.shift_right_logical(_iota((C_WIDTH, C_KW), 0), 6)
          == lax.shift_right_logical(_iota((C_WIDTH, C_KW), 1), 5))
    khead = lax.shift_right_logical(_iota((1, C_KW), 1), 5)
    vhead = lax.shift_right_logical(_iota((1, C_WIDTH), 1), 6)
    n_sub = bl // CHUNK

    blocks = range(seq // bl)
    for blk in (reversed(blocks) if reverse else blocks):
        r0 = blk * bl
        q = cq_ref[r0:r0 + bl, :]
        k = ck_ref[r0:r0 + bl, :]
        v = cv_ref[r0:r0 + bl, :]
        g = g_ref[r0:r0 + bl, :]
        g_hi, g_lo = _split_bf16(g)
        cum = _dot(tri, g_hi) + _dot(tri, g_lo)
        qt = q * jnp.exp(cum)
        kt = k * jnp.exp(-cum)
        ktb = kt.astype(BF16)
        vb = v.astype(BF16)

        o = jnp.zeros((bl, C_WIDTH), F32)
        for hd in range(C_HEADS):
            qh = jnp.where(khead == hd, qt, 0.0).astype(BF16)
            a = jnp.where(causal, _dot_nt(qh, ktb), 0.0).astype(BF16)
            vh = jnp.where(vhead == hd, v, 0.0).astype(BF16)
            o = o + _dot(a, vh)

        qtb = qt.astype(BF16)
        inter = [None] * n_sub
        subs = range(n_sub)
        for c in (reversed(subs) if reverse else subs):
            c0 = CHUNK * c
            edge = c0 if reverse else c0 + CHUNK - 1
            last = cum[edge:edge + 1, :]
            inter[c] = _dot_nt(qtb[c0:c0 + CHUNK], s_t.astype(BF16))
            kdec = (k[c0:c0 + CHUNK] * jnp.exp(last - cum[c0:c0 + CHUNK])).astype(BF16)
            kv = _dot_tn(vb[c0:c0 + CHUNK], kdec)
            s_t = jnp.exp(last) * s_t + jnp.where(bd, kv, 0.0)
        o = o + jnp.concatenate(inter, axis=0)
        if accumulate:
            oc_ref[r0:r0 + bl, :] = oc_ref[r0:r0 + bl, :] + o
        else:
            oc_ref[r0:r0 + bl, :] = o
    return s_t


def _attend_t(jobs):
    def scores(i):
        return _dot_nt(jobs[i][0][...], jobs[i][1])

    outs = []
    pending = [scores(i) for i in range(min(SCORE_LOOKAHEAD, len(jobs)))]
    for i, (_, _, vt) in enumerate(jobs):
        st = pending.pop(0)
        if i + SCORE_LOOKAHEAD < len(jobs):
            pending.append(scores(i + SCORE_LOOKAHEAD))
        m = jnp.max(st, axis=0, keepdims=True)
        p = jnp.exp2(st - m).astype(BF16)
        ot = _dot(vt[...], p)
        outs.append(ot[0:HEAD_DIM] * (1.0 / ot[HEAD_DIM:HEAD_DIM + 1]))
    return outs


def _pair_rows(a, b):
    return jnp.concatenate([a, b], axis=0).T


def _mix_kernel(cached, lam_init, seq, qt, qb, *refs):
    it = iter(refs)
    x_ref, mod_ref = next(it), next(it)
    qa_ref, ka_ref, qb_ref, kb_ref = (next(it) for _ in range(4))
    cq_ref, ck_ref, cv_ref, gf_ref, gb_ref, su_ref = (next(it) for _ in range(6))
    vat_ref, vbt_ref = next(it), next(it)
    if cached:
        cakt_ref, cavt_ref, cbkt_ref, cbvt_ref, s0f_ref, s0b_ref = (next(it) for _ in range(6))
    wout_ref, gpost_ref, bog_ref, cog_ref, lamp_ref = (next(it) for _ in range(5))
    y_ref = next(it)
    if not cached:
        sf_ref, sb_ref = next(it), next(it)
    kA_s, vtA_s, kB_s, vtB_s, oc_s, mixed_s = (next(it) for _ in range(6))

    lk = kA_s.shape[1]
    past = lk - seq
    t = pl.program_id(1)

    @pl.when(t == 0)
    def _per_sequence():
        ones_row = jnp.where(_iota((VT_ROWS - HEAD_DIM, lk), 0) == 0, 1.0, 0.0).astype(BF16)
        for k_new, kt_cache, k_dst, vt_new, vt_cache, vt_dst in (
                (ka_ref, cakt_ref if cached else None, kA_s, vat_ref, cavt_ref if cached else None, vtA_s),
                (kb_ref, cbkt_ref if cached else None, kB_s, vbt_ref, cbvt_ref if cached else None, vtB_s)):
            n_heads = k_dst.shape[0]
            if cached:
                for h0 in range(0, n_heads, 2):
                    pair = jnp.concatenate([kt_cache[h0], kt_cache[h0 + 1]], axis=0).T
                    k_dst[h0, 0:past, :] = pair[:, 0:HEAD_DIM].astype(BF16)
                    k_dst[h0 + 1, 0:past, :] = pair[:, HEAD_DIM:2 * HEAD_DIM].astype(BF16)
            for hd in range(n_heads):
                k_dst[hd, past:lk, :] = k_new[hd]
                if cached:
                    vt_dst[hd, 0:HEAD_DIM, 0:past] = vt_cache[hd].astype(BF16)
                vt_dst[hd, 0:HEAD_DIM, past:lk] = vt_new[hd].astype(BF16)
                vt_dst[hd, HEAD_DIM:VT_ROWS, :] = ones_row

        if cached:
            s0f, s0b = s0f_ref[...], s0b_ref[...]
        else:
            s0f = jnp.zeros((C_WIDTH, C_KW), F32)
            s0b = s0f
        s_f = _gla_direction(cq_ref, ck_ref, cv_ref, gf_ref, s0f, False, seq, oc_s, False)
        s_b = _gla_direction(cq_ref, ck_ref, cv_ref, gb_ref, s0b, True, seq, oc_s, True)
        if not cached:
            sf_ref[...] = s_f
            sb_ref[...] = s_b
        for r0 in range(0, seq, GLA_BLOCK):
            oc = oc_s[r0:r0 + GLA_BLOCK, :]
            oc_s[r0:r0 + GLA_BLOCK, :] = oc * lax.rsqrt(_group_mean_sq(oc, 6) + EPS) * cog_ref[...]

    lam = (jnp.exp(jnp.sum(lamp_ref[0:1, :] * lamp_ref[1:2, :], axis=-1, keepdims=True))
           - jnp.exp(jnp.sum(lamp_ref[2:3, :] * lamp_ref[3:4, :], axis=-1, keepdims=True))
           + lam_init)

    def attn_block(i, carry):
        rows = pl.ds(pl.multiple_of(i * qb, qb), qb)
        jobs = []
        for grp in range(A_KV_HEADS):
            q4 = qa_ref[A_GROUP * grp:A_GROUP * (grp + 1), rows, :].reshape(A_GROUP * qb, HEAD_DIM)
            jobs.append((kA_s.at[grp], q4, vtA_s.at[grp]))
        lane = _iota((qb, HEAD_DIM), 1)
        for hd in range(B_HEADS):
            q = qb_ref[hd, rows, :]
            zero = jnp.zeros_like(q)
            q2 = jnp.concatenate([jnp.where(lane < B_QK_DIM, q, zero),
                                  jnp.where(lane >= B_QK_DIM, q, zero)], axis=0)
            jobs.append((kB_s.at[hd], q2, vtB_s.at[hd]))
        outs = _attend_t(jobs)
        for grp in range(A_KV_HEADS):
            ot = outs[grp]
            for pair in range(A_GROUP // 2):
                c0 = 2 * pair * qb
                col = A_GROUP * HEAD_DIM * grp + 2 * HEAD_DIM * pair
                mixed_s[rows, col:col + 2 * HEAD_DIM] = _pair_rows(ot[:, c0:c0 + qb],
                                                                   ot[:, c0 + qb:c0 + 2 * qb])
        obs = []
        for hd in range(B_HEADS):
            ot = outs[A_KV_HEADS + hd]
            ob = ot[:, 0:qb] - lam * ot[:, qb:2 * qb]
            obs.append(ob * lax.rsqrt(jnp.mean(ob * ob, axis=0, keepdims=True) + EPS))
        for pair in range(B_HEADS // 2):
            col = A_WIDTH + 2 * B_V_DIM * pair
            mixed_s[rows, col:col + 2 * B_V_DIM] = (_pair_rows(obs[2 * pair], obs[2 * pair + 1])
                                                    * bog_ref[...] * (1.0 - lam_init))
        return carry

    lax.fori_loop(0, qt // qb, attn_block, 0)

    seq_rows = pl.ds(pl.multiple_of(t * qt, qt), qt)
    mixed_s[:, A_WIDTH + B_WIDTH:D_MIX] = oc_s[seq_rows, :]
    gate = mod_ref[:, 2 * D_MODEL:3 * D_MODEL]
    mixed = (mixed_s[...] * su_ref[...]).astype(BF16)
    y = _dot(mixed, wout_ref[...])
    yn = y * lax.rsqrt(jnp.mean(y * y, axis=-1, keepdims=True) + EPS) * gpost_ref[...]
    y_ref[...] = x_ref[...] + gate * yn


def _mixer(x, mod, per_batch_mod, layer, proj, wts, lam_init, cache):
    bsz, seq, _ = x.shape
    cached = cache is not None
    past = cache[0].shape[4] if cached else 0
    qa, ka, qb, kb, cq, ck, cv, gf, gb, su, vat, vbt = proj[:12]
    qt = 256
    qb_rows = 128

    def per_layer(shape):
        return pl.BlockSpec((None,) + shape, lambda b, t: (layer,) + (0,) * len(shape))

    def layer_heads_t(n, length):
        return pl.BlockSpec((None, None, n, HEAD_DIM, length), lambda b, t: (b, layer, 0, 0, 0))

    def heads_t(n):
        if cached:
            return pl.BlockSpec((None, n, HEAD_DIM, seq), lambda b, t: (b, 0, 0, 0))
        return layer_heads_t(n, seq)

    def heads(n):
        return pl.BlockSpec((None, n, seq, HEAD_DIM), lambda b, t: (b, 0, 0, 0))

    def head_tile(n):
        return pl.BlockSpec((None, n, qt, HEAD_DIM), lambda b, t: (b, 0, t, 0))

    def rows(width):
        return pl.BlockSpec((None, seq, width), lambda b, t: (b, 0, 0))

    def row_tile(width):
        return pl.BlockSpec((None, qt, width), lambda b, t: (b, t, 0))

    mod_idx = (lambda b, t: (layer, b, 0, 0)) if per_batch_mod else (lambda b, t: (layer, 0, 0, 0))
    in_specs = [row_tile(D_MODEL), pl.BlockSpec((None, None, 1, 3 * D_MODEL), mod_idx),
                head_tile(A_HEADS), heads(A_KV_HEADS), head_tile(B_HEADS), heads(B_HEADS),
                rows(C_KW), rows(C_KW), rows(C_WIDTH), rows(C_KW), rows(C_KW), row_tile(D_MIX),
                heads_t(A_KV_HEADS), heads_t(B_HEADS)]
    args = [x, mod, qa, ka, qb, kb, cq, ck, cv, gf, gb, su, vat, vbt]
    if cached:
        state_in = pl.BlockSpec((None, None, C_WIDTH, C_KW), lambda b, t: (b, layer, 0, 0))
        in_specs += [layer_heads_t(A_KV_HEADS, past), layer_heads_t(A_KV_HEADS, past),
                     layer_heads_t(B_HEADS, past), layer_heads_t(B_HEADS, past), state_in, state_in]
        args += list(cache)
    in_specs += [per_layer((D_MIX, D_MODEL)), per_layer((1, D_MODEL)), per_layer((1, 2 * B_V_DIM)),
                 per_layer((1, C_WIDTH)), per_layer((4, B_QK_DIM))]
    args += [wts["w_out"], wts["g_post"], wts["b_out_gain"], wts["c_out_gain"], wts["lam_params"]]

    out_specs = [row_tile(D_MODEL)]
    out_shape = [jax.ShapeDtypeStruct((bsz, seq, D_MODEL), F32)]
    if not cached:
        state_out = pl.BlockSpec((None, C_WIDTH, C_KW), lambda b, t: (b, 0, 0))
        out_specs += [state_out, state_out]
        out_shape += [jax.ShapeDtypeStruct((bsz, C_WIDTH, C_KW), F32)] * 2

    lk = past + seq
    scratch = [pltpu.VMEM((A_KV_HEADS, lk, HEAD_DIM), BF16), pltpu.VMEM((A_KV_HEADS, VT_ROWS, lk), BF16),
               pltpu.VMEM((B_HEADS, lk, HEAD_DIM), BF16), pltpu.VMEM((B_HEADS, VT_ROWS, lk), BF16),
               pltpu.VMEM((seq, C_WIDTH), F32), pltpu.VMEM((qt, D_MIX), F32)]
    return pl.pallas_call(
        functools.partial(_mix_kernel, cached, lam_init, seq, qt, qb_rows),
        grid=(bsz, seq // qt),
        in_specs=in_specs,
        out_specs=out_specs,
        out_shape=out_shape,
        scratch_shapes=scratch,
        compiler_params=pltpu.CompilerParams(
            dimension_semantics=("arbitrary", "arbitrary"), vmem_limit_bytes=VMEM_LIMIT),
        name="mixer_cached" if cached else "mixer",
    )(*args)


def _rope_tables(seq):
    t = jnp.arange(seq)
    pos_row = (t // GRID_W).astype(F32)
    pos_col = (t % GRID_W).astype(F32)

    def tables(half, width):
        freq = ROPE_THETA ** (-jnp.arange(half, dtype=F32) / half)
        ang_r = pos_row[:, None] * freq[None, :]
        ang_c = pos_col[:, None] * freq[None, :]
        cos = jnp.concatenate([jnp.cos(ang_r), jnp.cos(ang_r), jnp.cos(ang_c), jnp.cos(ang_c)], axis=-1)
        sin = jnp.concatenate([-jnp.sin(ang_r), jnp.sin(ang_r), -jnp.sin(ang_c), jnp.sin(ang_c)], axis=-1)
        reps = width // (4 * half)
        return jnp.tile(cos, (1, reps)), jnp.tile(sin, (1, reps))

    cos_a, sin_a = tables(HEAD_DIM // 4, A_WIDTH)
    cos_b, sin_b = tables(B_QK_DIM // 4, B_WIDTH)
    return cos_a, sin_a, cos_b, sin_b


def _state_to_blockdiag(s):
    st = jnp.swapaxes(s, -1, -2)
    eye = jnp.eye(C_HEADS, dtype=s.dtype)
    full = st[..., :, :, None, :] * eye[:, None, :, None]
    return full.reshape(s.shape[:-3] + (C_WIDTH, C_KW))


def _blockdiag_to_state(s_t):
    blocks = [s_t[..., C_DV * h:C_DV * (h + 1), C_DK * h:C_DK * (h + 1)] for h in range(C_HEADS)]
    return jnp.swapaxes(jnp.stack(blocks, axis=-3), -1, -2)


def _prepare_weights(g_pre, g_post, w_in, w_out, a_q_gain, a_k_gain, b_lambda_q1, b_lambda_k1,
                     b_lambda_q2, b_lambda_k2, b_out_gain, c_gate_w_fwd, c_gate_b_fwd, c_gate_w_bwd,
                     c_gate_b_bwd, c_out_gain):
    w_in_t = jnp.swapaxes(w_in, 1, 2).astype(BF16)
    pad = jnp.zeros((DEPTH, GATE_RANK, C_KW), F32)
    cw_f = jnp.concatenate([c_gate_w_fwd, pad], axis=1).astype(BF16)
    cw_b = jnp.concatenate([pad, c_gate_w_bwd], axis=1).astype(BF16)
    return {
        "g_pre": g_pre[:, None, :],
        "g_post": g_post[:, None, :],
        "w_in_t": w_in_t,
        "w_out": w_out.astype(BF16),
        "aq_gain": jnp.tile(a_q_gain, (1, A_HEADS))[:, None, :],
        "ak_gain": jnp.tile(a_k_gain, (1, A_KV_HEADS))[:, None, :],
        "cw_f": cw_f,
        "cb_f": c_gate_b_fwd[:, None, :],
        "cw_b": cw_b,
        "cb_b": c_gate_b_bwd[:, None, :],
        "b_out_gain": jnp.tile(b_out_gain, (1, 2))[:, None, :],
        "c_out_gain": jnp.tile(c_out_gain, (1, C_HEADS))[:, None, :],
        "lam_params": jnp.stack([b_lambda_q1, b_lambda_k1, b_lambda_q2, b_lambda_k2], axis=1),
    }


def kernel(x_prompt, x_sample, c, cache_a_k, cache_a_v, cache_b_k, cache_b_v, state_c_fwd, state_c_bwd, c_ctx, w_mod, b_mod, g_pre, g_post, w_in, w_out, a_q_gain, a_k_gain, b_lambda_q1, b_lambda_k1, b_lambda_q2, b_lambda_k2, b_out_gain, c_gate_w_fwd, c_gate_b_fwd, c_gate_w_bwd, c_gate_b_bwd, c_out_gain):
    dec_batch = x_sample.shape[0]
    dec_seq = x_sample.shape[1]

    mod_rows = 16
    cvec = jnp.zeros((mod_rows, D_MODEL), F32).at[0:dec_batch].set(c).at[dec_batch].set(c_ctx)
    mod = _modulation(cvec, w_mod, b_mod)[:, :, None, :]
    mod_lat = mod[:, 0:dec_batch]
    mod_ctx = mod[:, dec_batch:dec_batch + 1]

    wts = _prepare_weights(g_pre, g_post, w_in, w_out, a_q_gain, a_k_gain, b_lambda_q1, b_lambda_k1,
                           b_lambda_q2, b_lambda_k2, b_out_gain, c_gate_w_fwd, c_gate_b_fwd,
                           c_gate_w_bwd, c_gate_b_bwd, c_out_gain)
    rope_tabs = _rope_tables(dec_seq)
    cache = tuple(jnp.swapaxes(a, -1, -2) for a in (cache_a_k, cache_a_v, cache_b_k, cache_b_v))
    cache += (_state_to_blockdiag(state_c_fwd), _state_to_blockdiag(state_c_bwd))

    y_p, y_s = x_prompt, x_sample
    kv_ctx = None
    states_f, states_b = [], []
    for l in range(DEPTH):
        lam_init = 0.8 - 0.6 * math.exp(-0.3 * l)
        proj_p = _in_projection(y_p, mod_ctx, False, l, wts, None, 2, x_prompt.shape[1], kv_ctx)
        kv_ctx = proj_p[10:14]
        y_p, s_f, s_b = _mixer(y_p, mod_ctx, False, l, proj_p, wts, lam_init, None)
        states_f.append(s_f)
        states_b.append(s_b)

        proj_s = _in_projection(y_s, mod_lat, True, l, wts, rope_tabs, 1, 512)
        (y_s,) = _mixer(y_s, mod_lat, True, l, proj_s, wts, lam_init, cache)

    va_t, vb_t, ka_t, kb_t = kv_ctx
    new_kv = [jnp.swapaxes(a, -1, -2) for a in (ka_t, va_t, kb_t, vb_t)]
    new_s_f = _blockdiag_to_state(jnp.stack(states_f, axis=1))
    new_s_b = _blockdiag_to_state(jnp.stack(states_b, axis=1))
    return (y_p, y_s, *new_kv, new_s_f, new_s_b)
```

```python
import functools
import math

import jax
import jax.numpy as jnp
from jax import lax
from jax.experimental import pallas as pl
from jax.experimental.pallas import tpu as pltpu

F32 = jnp.float32
BF16 = jnp.bfloat16

D_MODEL = 1024
DEPTH = 2
GRID_W = 64
HEAD_DIM = 64
A_HEADS = 8
A_KV_HEADS = 2
A_GROUP = A_HEADS // A_KV_HEADS
A_WIDTH = A_HEADS * HEAD_DIM
A_KV_WIDTH = A_KV_HEADS * HEAD_DIM
B_HEADS = 4
B_QK_DIM = 32
B_V_DIM = 64
B_WIDTH = B_HEADS * B_V_DIM
C_HEADS = 4
C_DK = 32
C_DV = 64
C_KW = C_HEADS * C_DK
C_WIDTH = C_HEADS * C_DV
GATE_RANK = 16
GLA_TAU = 16.0
CHUNK = 64
D_MIX = A_WIDTH + B_WIDTH + C_WIDTH
ROPE_THETA = 10000.0
EPS = 1e-6

LANES = 128
GLA_BLOCK = 256
VT_ROWS = HEAD_DIM + 16
SCORE_LOOKAHEAD = 3
LOG2E = math.log2(math.e)

OFF_AQ = 0
OFF_AK = OFF_AQ + A_WIDTH
OFF_AV = OFF_AK + A_KV_WIDTH
OFF_BQ = OFF_AV + A_KV_WIDTH
OFF_BK = OFF_BQ + B_WIDTH
OFF_BV = OFF_BK + B_WIDTH
OFF_CQ = OFF_BV + B_WIDTH
OFF_CK = OFF_CQ + C_KW
OFF_CV = OFF_CK + C_KW
OFF_LR = OFF_CV + C_WIDTH
OFF_U = OFF_LR + 2 * GATE_RANK
IN_WIDTH = OFF_U + D_MIX

VMEM_LIMIT = 56 * 1024 * 1024


def _dot(a, b):
    return jnp.dot(a, b, preferred_element_type=F32)


def _dot_nt(a, b):
    return lax.dot_general(a, b, (((1,), (1,)), ((), ())), preferred_element_type=F32)


def _dot_tn(a, b):
    return lax.dot_general(a, b, (((0,), (0,)), ((), ())), preferred_element_type=F32)


def _split_bf16(x):
    hi = x.astype(BF16)
    lo = (x - hi.astype(F32)).astype(BF16)
    return hi, lo


def _iota(shape, dim):
    return lax.broadcasted_iota(jnp.int32, shape, dim)


def _group_mean_sq(x, group_log2):
    width = x.shape[-1]
    r = lax.shift_right_logical(_iota((LANES, LANES), 0), group_log2)
    c = lax.shift_right_logical(_iota((LANES, LANES), 1), group_log2)
    ones = jnp.where(r == c, 1.0, 0.0).astype(BF16)
    hi, lo = _split_bf16(x * x)
    cols = []
    for j in range(width // LANES):
        sl = slice(LANES * j, LANES * (j + 1))
        cols.append(_dot(hi[:, sl], ones) + _dot(lo[:, sl], ones))
    ss = cols[0] if len(cols) == 1 else jnp.concatenate(cols, axis=-1)
    return ss * (1.0 / (1 << group_log2))


def _rope(x, cos, sin_signed, dist):
    width = x.shape[-1]
    lane = _iota(x.shape, 1)
    first = (lane & (2 * dist - 1)) < dist
    up = pltpu.roll(x, width - dist, 1)
    down = pltpu.roll(x, dist, 1)
    return x * cos + jnp.where(first, up, down) * sin_signed


def _log_sigmoid(x):
    return jnp.minimum(x, 0.0) - jnp.log1p(jnp.exp(-jnp.abs(x)))


def _silu(x):
    return x * (1.0 / (1.0 + jnp.exp(-x)))


def _mod_kernel(c_ref, w_ref, b_ref, o_ref):
    a = _silu(c_ref[...]).astype(BF16)
    o_ref[...] = _dot(a, w_ref[...].astype(BF16)) + b_ref[...]


def _modulation(cvec, w_mod, b_mod):
    rows = cvec.shape[0]
    nblk = 3
    return pl.pallas_call(
        _mod_kernel,
        grid=(DEPTH, nblk),
        in_specs=[
            pl.BlockSpec((rows, D_MODEL), lambda l, n: (0, 0)),
            pl.BlockSpec((None, D_MODEL, D_MODEL), lambda l, n: (l, 0, n)),
            pl.BlockSpec((None, 1, D_MODEL), lambda l, n: (l, 0, n)),
        ],
        out_specs=pl.BlockSpec((None, rows, D_MODEL), lambda l, n: (l, 0, n)),
        out_shape=jax.ShapeDtypeStruct((DEPTH, rows, 3 * D_MODEL), F32),
        compiler_params=pltpu.CompilerParams(
            dimension_semantics=("arbitrary", "arbitrary"), vmem_limit_bytes=VMEM_LIMIT),
        name="modulation",
    )(cvec, w_mod, b_mod.reshape(DEPTH, 1, 3 * D_MODEL))


def _in_kernel(rope, layer, stacked_first, nb, tl, n_aliased, *refs):
    (x_ref, mod_ref, gpre_ref, wt_ref, aqg_ref, akg_ref, cwf_ref, cbf_ref, cwb_ref, cbb_ref) = refs[:10]
    refs = refs[10:]
    if rope:
        cosa_ref, sina_ref, cosb_ref, sinb_ref = refs[:4]
        refs = refs[4:]
    refs = refs[n_aliased:]
    (qa_ref, ka_ref, qb_ref, kb_ref, cq_ref, ck_ref, cv_ref, gf_ref, gb_ref, su_ref,
     vat_ref, vbt_ref) = refs[:12]
    kat_ref, kbt_ref = refs[12:] if len(refs) > 12 else (None, None)

    x = x_ref[...].reshape(nb * tl, D_MODEL)
    shift = mod_ref[:, 0:D_MODEL]
    scale = mod_ref[:, D_MODEL:2 * D_MODEL]
    ms = jnp.mean(x * x, axis=-1, keepdims=True)
    h = (x * lax.rsqrt(ms + EPS)) * gpre_ref[...] * (1.0 + scale) + shift
    hb = h.astype(BF16)

    def proj(off, width):
        return _dot_nt(hb, wt_ref[off:off + width, :])

    def put_rows(ref, val):
        for bi in range(nb):
            ref[bi] = val[bi * tl:(bi + 1) * tl].astype(ref.dtype)

    def put_heads(ref, val, n_heads):
        for bi in range(nb):
            for hd in range(n_heads):
                ref[bi, hd] = val[bi * tl:(bi + 1) * tl,
                                  HEAD_DIM * hd:HEAD_DIM * (hd + 1)].astype(ref.dtype)

    def put_heads_t(ref, val, n_heads):
        val_t = val.T
        for bi in range(nb):
            for hd in range(n_heads):
                blk = val_t[HEAD_DIM * hd:HEAD_DIM * (hd + 1), bi * tl:(bi + 1) * tl].astype(ref.dtype)
                if stacked_first:
                    for l2 in range(DEPTH):
                        ref[bi, l2, hd] = blk if l2 == layer else jnp.zeros_like(blk)
                else:
                    ref[bi, hd] = blk

    aq = proj(OFF_AQ, A_WIDTH)
    aq = aq * lax.rsqrt(_group_mean_sq(aq, 6) + EPS) * aqg_ref[...]
    ak = proj(OFF_AK, A_KV_WIDTH)
    ak = ak * lax.rsqrt(_group_mean_sq(ak, 6) + EPS) * akg_ref[...]
    if rope:
        aq = _rope(aq, cosa_ref[...], sina_ref[...], 16)
        ak = _rope(ak, cosa_ref[:, 0:A_KV_WIDTH], sina_ref[:, 0:A_KV_WIDTH], 16)
    put_heads(qa_ref, aq * (HEAD_DIM ** -0.5 * LOG2E), A_HEADS)
    put_heads(ka_ref, ak, A_KV_HEADS)
    put_heads_t(vat_ref, proj(OFF_AV, A_KV_WIDTH), A_KV_HEADS)
    if kat_ref is not None:
        put_heads_t(kat_ref, ak, A_KV_HEADS)

    bq = proj(OFF_BQ, B_WIDTH)
    bk = proj(OFF_BK, B_WIDTH)
    if rope:
        bq = _rope(bq, cosb_ref[...], sinb_ref[...], 8)
        bk = _rope(bk, cosb_ref[...], sinb_ref[...], 8)
    put_heads(qb_ref, bq * (B_QK_DIM ** -0.5 * LOG2E), B_HEADS)
    put_heads(kb_ref, bk, B_HEADS)
    put_heads_t(vbt_ref, proj(OFF_BV, B_WIDTH), B_HEADS)
    if kbt_ref is not None:
        put_heads_t(kbt_ref, bk, B_HEADS)

    put_rows(cq_ref, proj(OFF_CQ, C_KW) * (C_DK ** -0.5))
    put_rows(ck_ref, proj(OFF_CK, C_KW))
    put_rows(cv_ref, proj(OFF_CV, C_WIDTH))
    lr = proj(OFF_LR, 2 * GATE_RANK).astype(BF16)
    put_rows(gf_ref, _log_sigmoid(_dot(lr, cwf_ref[...]) + cbf_ref[...]) * (1.0 / GLA_TAU))
    put_rows(gb_ref, _log_sigmoid(_dot(lr, cwb_ref[...]) + cbb_ref[...]) * (1.0 / GLA_TAU))

    put_rows(su_ref, _silu(proj(OFF_U, D_MIX)))


def _in_projection(x, mod, per_batch_mod, layer, wts, rope_tabs, nb, tl, kv_prev=None):
    bsz, seq, _ = x.shape
    rope = rope_tabs is not None
    stacked = not rope
    stacked_first = stacked and kv_prev is None
    grid = (bsz // nb, seq // tl)

    def per_layer(shape):
        return pl.BlockSpec((None,) + shape, lambda b, t: (layer,) + (0,) * len(shape))

    mod_idx = (lambda b, t: (layer, b, 0, 0)) if per_batch_mod else (lambda b, t: (layer, 0, 0, 0))
    in_specs = [
        pl.BlockSpec((nb, tl, D_MODEL), lambda b, t: (b, t, 0)),
        pl.BlockSpec((None, None, 1, 3 * D_MODEL), mod_idx),
        per_layer((1, D_MODEL)),
        per_layer((IN_WIDTH, D_MODEL)),
        per_layer((1, A_WIDTH)),
        per_layer((1, A_KV_WIDTH)),
        per_layer((2 * GATE_RANK, C_KW)),
        per_layer((1, C_KW)),
        per_layer((2 * GATE_RANK, C_KW)),
        per_layer((1, C_KW)),
    ]
    args = [x, mod, wts["g_pre"], wts["w_in_t"], wts["aq_gain"], wts["ak_gain"],
            wts["cw_f"], wts["cb_f"], wts["cw_b"], wts["cb_b"]]
    if rope:
        assert nb == 1
        in_specs += [
            pl.BlockSpec((tl, A_WIDTH), lambda b, t: (t, 0)),
            pl.BlockSpec((tl, A_WIDTH), lambda b, t: (t, 0)),
            pl.BlockSpec((tl, B_WIDTH), lambda b, t: (t, 0)),
            pl.BlockSpec((tl, B_WIDTH), lambda b, t: (t, 0)),
        ]
        args += list(rope_tabs)

    def heads(n):
        return pl.BlockSpec((nb, n, tl, HEAD_DIM), lambda b, t: (b, 0, t, 0))

    def heads_t(n):
        if stacked_first:
            return pl.BlockSpec((nb, DEPTH, n, HEAD_DIM, tl), lambda b, t: (b, 0, 0, 0, t))
        if stacked:
            return pl.BlockSpec((nb, None, n, HEAD_DIM, tl), lambda b, t: (b, layer, 0, 0, t))
        return pl.BlockSpec((nb, n, HEAD_DIM, tl), lambda b, t: (b, 0, 0, t))

    def rows(width):
        return pl.BlockSpec((nb, tl, width), lambda b, t: (b, t, 0))

    def hshape(n):
        return jax.ShapeDtypeStruct((bsz, n, seq, HEAD_DIM), BF16)

    def tshape(n):
        if stacked:
            return jax.ShapeDtypeStruct((bsz, DEPTH, n, HEAD_DIM, seq), F32)
        return jax.ShapeDtypeStruct((bsz, n, HEAD_DIM, seq), BF16)

    def rshape(width):
        return jax.ShapeDtypeStruct((bsz, seq, width), F32)

    out_specs = [heads(A_HEADS), heads(A_KV_HEADS), heads(B_HEADS), heads(B_HEADS),
                 rows(C_KW), rows(C_KW), rows(C_WIDTH), rows(C_KW), rows(C_KW), rows(D_MIX),
                 heads_t(A_KV_HEADS), heads_t(B_HEADS)]
    out_shape = [hshape(A_HEADS), hshape(A_KV_HEADS), hshape(B_HEADS), hshape(B_HEADS),
                 rshape(C_KW), rshape(C_KW), rshape(C_WIDTH), rshape(C_KW), rshape(C_KW),
                 rshape(D_MIX), tshape(A_KV_HEADS), tshape(B_HEADS)]
    if stacked:
        out_specs += [heads_t(A_KV_HEADS), heads_t(B_HEADS)]
        out_shape += [tshape(A_KV_HEADS), tshape(B_HEADS)]
    aliases = {}
    if kv_prev is not None:
        for j, buf in enumerate(kv_prev):
            aliases[len(args)] = 10 + j
            in_specs.append(pl.BlockSpec(memory_space=pl.ANY))
            args.append(buf)
    return pl.pallas_call(
        functools.partial(_in_kernel, rope, layer, stacked_first, nb, tl, len(aliases)),
        grid=grid,
        in_specs=in_specs,
        out_specs=out_specs,
        out_shape=out_shape,
        input_output_aliases=aliases,
        compiler_params=pltpu.CompilerParams(
            dimension_semantics=("arbitrary", "arbitrary"), vmem_limit_bytes=VMEM_LIMIT),
        name="in_projection_rope" if rope else "in_projection",
    )(*args)


def _gla_bidirectional(cq_ref, ck_ref, cv_ref, gf_ref, gb_ref, s_f, s_b, seq, oc_ref):
    bl = GLA_BLOCK
    n_sub = bl // CHUNK
    nblk = seq // bl
    ri = _iota((bl, bl), 0)
    ci = _iota((bl, bl), 1)
    same_chunk = lax.shift_right_logical(ri, 6) == lax.shift_right_logical(ci, 6)
    bd = (lax.shift_right_logical(_iota((C_WIDTH, C_KW), 0), 6)
          == lax.shift_right_logical(_iota((C_WIDTH, C_KW), 1), 5))
    khead = lax.shift_right_logical(_iota((1, C_KW), 1), 5)
    vhead = lax.shift_right_logical(_iota((1, C_WIDTH), 1), 6)
    scans = []
    for reverse, g_ref in ((False, gf_ref), (True, gb_ref)):
        causal = same_chunk & ((ci >= ri) if reverse else (ci <= ri))
        scans.append((reverse, g_ref, causal, jnp.where(causal, 1.0, 0.0).astype(BF16)))
    states = [s_f, s_b]
    written = set()

    for step in range(nblk):
        rows0 = [step * bl, (nblk - 1 - step) * bl]
        cums = []
        for (reverse, g_ref, causal, tri), r0 in zip(scans, rows0):
            g_hi, g_lo = _split_bf16(g_ref[r0:r0 + bl, :])
            cums.append(_dot(tri, g_hi) + _dot(tri, g_lo))
        prep = []
        for (reverse, g_ref, causal, tri), r0, cum in zip(scans, rows0, cums):
            q = cq_ref[r0:r0 + bl, :]
            k = ck_ref[r0:r0 + bl, :]
            v = cv_ref[r0:r0 + bl, :]
            qt = q * jnp.exp(cum)
            ktb = (k * jnp.exp(-cum)).astype(BF16)
            vb = v.astype(BF16)
            lasts, kdecs = [], []
            for c in range(n_sub):
                c0 = CHUNK * c
                edge = c0 if reverse else c0 + CHUNK - 1
                last = cum[edge:edge + 1, :]
                lasts.append(last)
                kdecs.append((k[c0:c0 + CHUNK] * jnp.exp(last - cum[c0:c0 + CHUNK])).astype(BF16))
            prep.append((qt, ktb, v, vb, lasts, kdecs))
        scores, incs = [], []
        for qt, ktb, v, vb, lasts, kdecs in prep:
            scores.append([_dot_nt(jnp.where(khead == hd, qt, 0.0).astype(BF16), ktb)
                           for hd in range(C_HEADS)])
            incs.append([_dot_tn(vb[CHUNK * c:CHUNK * (c + 1)], kdecs[c]) for c in range(n_sub)])
        probs, entering = [], []
        for si, ((reverse, g_ref, causal, tri), (qt, ktb, v, vb, lasts, kdecs)) in enumerate(
                zip(scans, prep)):
            probs.append([jnp.where(causal, s, 0.0).astype(BF16) for s in scores[si]])
            s_t = states[si]
            before = [None] * n_sub
            subs = range(n_sub)
            for c in (reversed(subs) if reverse else subs):
                before[c] = s_t.astype(BF16)
                s_t = jnp.exp(lasts[c]) * s_t + jnp.where(bd, incs[si][c], 0.0)
            states[si] = s_t
            entering.append(before)
        outs = []
        for si, (qt, ktb, v, vb, lasts, kdecs) in enumerate(prep):
            qtb = qt.astype(BF16)
            o = jnp.concatenate([_dot_nt(qtb[CHUNK * c:CHUNK * (c + 1)], entering[si][c])
                                 for c in range(n_sub)], axis=0)
            for hd in range(C_HEADS):
                o = o + _dot(probs[si][hd], jnp.where(vhead == hd, v, 0.0).astype(BF16))
            outs.append(o)
        if rows0[0] == rows0[1]:
            outs, rows0 = [outs[0] + outs[1]], rows0[:1]
        for o, r0 in zip(outs, rows0):
            if r0 in written:
                oc_ref[r0:r0 + bl, :] = oc_ref[r0:r0 + bl, :] + o
            else:
                oc_ref[r0:r0 + bl, :] = o
                written.add(r0)
    return states[0], states[1]


def _attend_t(jobs):
    def scores(i):
        return _dot_nt(jobs[i][0][...], jobs[i][1])

    outs = []
    pending = [scores(i) for i in range(min(SCORE_LOOKAHEAD, len(jobs)))]
    for i, (_, _, vt) in enumerate(jobs):
        st = pending.pop(0)
        if i + SCORE_LOOKAHEAD < len(jobs):
            pending.append(scores(i + SCORE_LOOKAHEAD))
        m = jnp.max(st, axis=0, keepdims=True)
        p = jnp.exp2(st - m).astype(BF16)
        ot = _dot(vt[...], p)
        outs.append(ot[0:HEAD_DIM] * (1.0 / ot[HEAD_DIM:HEAD_DIM + 1]))
    return outs


def _pair_rows(a, b):
    return jnp.concatenate([a, b], axis=0).T


def _mix_kernel(cached, lam_init, seq, qt, qb, *refs):
    it = iter(refs)
    x_ref, mod_ref = next(it), next(it)
    qa_ref, ka_ref, qb_ref, kb_ref = (next(it) for _ in range(4))
    cq_ref, ck_ref, cv_ref, gf_ref, gb_ref, su_ref = (next(it) for _ in range(6))
    vat_ref, vbt_ref = next(it), next(it)
    if cached:
        cakt_ref, cavt_ref, cbkt_ref, cbvt_ref, s0f_ref, s0b_ref = (next(it) for _ in range(6))
    wout_ref, gpost_ref, bog_ref, cog_ref, lamp_ref = (next(it) for _ in range(5))
    y_ref = next(it)
    if not cached:
        sf_ref, sb_ref = next(it), next(it)
    kA_s, vtA_s, kB_s, vtB_s, oc_s, mixed_s = (next(it) for _ in range(6))

    lk = kA_s.shape[1]
    past = lk - seq
    t = pl.program_id(1)

    @pl.when(t == 0)
    def _per_sequence():
        ones_row = jnp.where(_iota((VT_ROWS - HEAD_DIM, lk), 0) == 0, 1.0, 0.0).astype(BF16)
        for k_new, kt_cache, k_dst, vt_new, vt_cache, vt_dst in (
                (ka_ref, cakt_ref if cached else None, kA_s, vat_ref, cavt_ref if cached else None, vtA_s),
                (kb_ref, cbkt_ref if cached else None, kB_s, vbt_ref, cbvt_ref if cached else None, vtB_s)):
            n_heads = k_dst.shape[0]
            if cached:
                for h0 in range(0, n_heads, 2):
                    pair = jnp.concatenate([kt_cache[h0], kt_cache[h0 + 1]], axis=0).T
                    k_dst[h0, 0:past, :] = pair[:, 0:HEAD_DIM].astype(BF16)
                    k_dst[h0 + 1, 0:past, :] = pair[:, HEAD_DIM:2 * HEAD_DIM].astype(BF16)
            for hd in range(n_heads):
                k_dst[hd, past:lk, :] = k_new[hd]
                if cached:
                    vt_dst[hd, 0:HEAD_DIM, 0:past] = vt_cache[hd].astype(BF16)
                vt_dst[hd, 0:HEAD_DIM, past:lk] = vt_new[hd].astype(BF16)
                vt_dst[hd, HEAD_DIM:VT_ROWS, :] = ones_row

        if cached:
            s0f, s0b = s0f_ref[...], s0b_ref[...]
        else:
            s0f = jnp.zeros((C_WIDTH, C_KW), F32)
            s0b = s0f
        s_f, s_b = _gla_bidirectional(cq_ref, ck_ref, cv_ref, gf_ref, gb_ref, s0f, s0b, seq, oc_s)
        if not cached:
            sf_ref[...] = s_f
            sb_ref[...] = s_b
        for r0 in range(0, seq, GLA_BLOCK):
            oc = oc_s[r0:r0 + GLA_BLOCK, :]
            oc_s[r0:r0 + GLA_BLOCK, :] = oc * lax.rsqrt(_group_mean_sq(oc, 6) + EPS) * cog_ref[...]

    lam = (jnp.exp(jnp.sum(lamp_ref[0:1, :] * lamp_ref[1:2, :], axis=-1, keepdims=True))
           - jnp.exp(jnp.sum(lamp_ref[2:3, :] * lamp_ref[3:4, :], axis=-1, keepdims=True))
           + lam_init)

    def attn_block(i, carry):
        rows = pl.ds(pl.multiple_of(i * qb, qb), qb)
        jobs = []
        for grp in range(A_KV_HEADS):
            q4 = qa_ref[A_GROUP * grp:A_GROUP * (grp + 1), rows, :].reshape(A_GROUP * qb, HEAD_DIM)
            jobs.append((kA_s.at[grp], q4, vtA_s.at[grp]))
        lane = _iota((qb, HEAD_DIM), 1)
        for hd in range(B_HEADS):
            q = qb_ref[hd, rows, :]
            zero = jnp.zeros_like(q)
            q2 = jnp.concatenate([jnp.where(lane < B_QK_DIM, q, zero),
                                  jnp.where(lane >= B_QK_DIM, q, zero)], axis=0)
            jobs.append((kB_s.at[hd], q2, vtB_s.at[hd]))
        outs = _attend_t(jobs)
        for grp in range(A_KV_HEADS):
            ot = outs[grp]
            for pair in range(A_GROUP // 2):
                c0 = 2 * pair * qb
                col = A_GROUP * HEAD_DIM * grp + 2 * HEAD_DIM * pair
                mixed_s[rows, col:col + 2 * HEAD_DIM] = _pair_rows(ot[:, c0:c0 + qb],
                                                                   ot[:, c0 + qb:c0 + 2 * qb])
        obs = []
        for hd in range(B_HEADS):
            ot = outs[A_KV_HEADS + hd]
            ob = ot[:, 0:qb] - lam * ot[:, qb:2 * qb]
            obs.append(ob * lax.rsqrt(jnp.mean(ob * ob, axis=0, keepdims=True) + EPS))
        for pair in range(B_HEADS // 2):
            col = A_WIDTH + 2 * B_V_DIM * pair
            mixed_s[rows, col:col + 2 * B_V_DIM] = (_pair_rows(obs[2 * pair], obs[2 * pair + 1])
                                                    * bog_ref[...] * (1.0 - lam_init))
        return carry

    lax.fori_loop(0, qt // qb, attn_block, 0)

    seq_rows = pl.ds(pl.multiple_of(t * qt, qt), qt)
    mixed_s[:, A_WIDTH + B_WIDTH:D_MIX] = oc_s[seq_rows, :]
    gate = mod_ref[:, 2 * D_MODEL:3 * D_MODEL]
    mixed = (mixed_s[...] * su_ref[...]).astype(BF16)
    y = _dot(mixed, wout_ref[...])
    yn = y * lax.rsqrt(jnp.mean(y * y, axis=-1, keepdims=True) + EPS) * gpost_ref[...]
    y_ref[...] = x_ref[...] + gate * yn


def _mixer(x, mod, per_batch_mod, layer, proj, wts, lam_init, cache):
    bsz, seq, _ = x.shape
    cached = cache is not None
    past = cache[0].shape[4] if cached else 0
    qa, ka, qb, kb, cq, ck, cv, gf, gb, su, vat, vbt = proj[:12]
    qt = 256
    qb_rows = 256

    def per_layer(shape):
        return pl.BlockSpec((None,) + shape, lambda b, t: (layer,) + (0,) * len(shape))

    def layer_heads_t(n, length):
        return pl.BlockSpec((None, None, n, HEAD_DIM, length), lambda b, t: (b, layer, 0, 0, 0))

    def heads_t(n):
        if cached:
            return pl.BlockSpec((None, n, HEAD_DIM, seq), lambda b, t: (b, 0, 0, 0))
        return layer_heads_t(n, seq)

    def heads(n):
        return pl.BlockSpec((None, n, seq, HEAD_DIM), lambda b, t: (b, 0, 0, 0))

    def head_tile(n):
        return pl.BlockSpec((None, n, qt, HEAD_DIM), lambda b, t: (b, 0, t, 0))

    def rows(width):
        return pl.BlockSpec((None, seq, width), lambda b, t: (b, 0, 0))

    def row_tile(width):
        return pl.BlockSpec((None, qt, width), lambda b, t: (b, t, 0))

    mod_idx = (lambda b, t: (layer, b, 0, 0)) if per_batch_mod else (lambda b, t: (layer, 0, 0, 0))
    in_specs = [row_tile(D_MODEL), pl.BlockSpec((None, None, 1, 3 * D_MODEL), mod_idx),
                head_tile(A_HEADS), heads(A_KV_HEADS), head_tile(B_HEADS), heads(B_HEADS),
                rows(C_KW), rows(C_KW), rows(C_WIDTH), rows(C_KW), rows(C_KW), row_tile(D_MIX),
                heads_t(A_KV_HEADS), heads_t(B_HEADS)]
    args = [x, mod, qa, ka, qb, kb, cq, ck, cv, gf, gb, su, vat, vbt]
    if cached:
        state_in = pl.BlockSpec((None, None, C_WIDTH, C_KW), lambda b, t: (b, layer, 0, 0))
        in_specs += [layer_heads_t(A_KV_HEADS, past), layer_heads_t(A_KV_HEADS, past),
                     layer_heads_t(B_HEADS, past), layer_heads_t(B_HEADS, past), state_in, state_in]
        args += list(cache)
    in_specs += [per_layer((D_MIX, D_MODEL)), per_layer((1, D_MODEL)), per_layer((1, 2 * B_V_DIM)),
                 per_layer((1, C_WIDTH)), per_layer((4, B_QK_DIM))]
    args += [wts["w_out"], wts["g_post"], wts["b_out_gain"], wts["c_out_gain"], wts["lam_params"]]

    out_specs = [row_tile(D_MODEL)]
    out_shape = [jax.ShapeDtypeStruct((bsz, seq, D_MODEL), F32)]
    if not cached:
        state_out = pl.BlockSpec((None, C_WIDTH, C_KW), lambda b, t: (b, 0, 0))
        out_specs += [state_out, state_out]
        out_shape += [jax.ShapeDtypeStruct((bsz, C_WIDTH, C_KW), F32)] * 2

    lk = past + seq
    scratch = [pltpu.VMEM((A_KV_HEADS, lk, HEAD_DIM), BF16), pltpu.VMEM((A_KV_HEADS, VT_ROWS, lk), BF16),
               pltpu.VMEM((B_HEADS, lk, HEAD_DIM), BF16), pltpu.VMEM((B_HEADS, VT_ROWS, lk), BF16),
               pltpu.VMEM((seq, C_WIDTH), F32), pltpu.VMEM((qt, D_MIX), F32)]
    return pl.pallas_call(
        functools.partial(_mix_kernel, cached, lam_init, seq, qt, qb_rows),
        grid=(bsz, seq // qt),
        in_specs=in_specs,
        out_specs=out_specs,
        out_shape=out_shape,
        scratch_shapes=scratch,
        compiler_params=pltpu.CompilerParams(
            dimension_semantics=("arbitrary", "arbitrary"), vmem_limit_bytes=VMEM_LIMIT),
        name="mixer_cached" if cached else "mixer",
    )(*args)


def _rope_tables(seq):
    t = jnp.arange(seq)
    pos_row = (t // GRID_W).astype(F32)
    pos_col = (t % GRID_W).astype(F32)

    def tables(half, width):
        freq = ROPE_THETA ** (-jnp.arange(half, dtype=F32) / half)
        ang_r = pos_row[:, None] * freq[None, :]
        ang_c = pos_col[:, None] * freq[None, :]
        cos = jnp.concatenate([jnp.cos(ang_r), jnp.cos(ang_r), jnp.cos(ang_c), jnp.cos(ang_c)], axis=-1)
        sin = jnp.concatenate([-jnp.sin(ang_r), jnp.sin(ang_r), -jnp.sin(ang_c), jnp.sin(ang_c)], axis=-1)
        reps = width // (4 * half)
        return jnp.tile(cos, (1, reps)), jnp.tile(sin, (1, reps))

    cos_a, sin_a = tables(HEAD_DIM // 4, A_WIDTH)
    cos_b, sin_b = tables(B_QK_DIM // 4, B_WIDTH)
    return cos_a, sin_a, cos_b, sin_b


def _state_to_blockdiag(s):
    st = jnp.swapaxes(s, -1, -2)
    eye = jnp.eye(C_HEADS, dtype=s.dtype)
    full = st[..., :, :, None, :] * eye[:, None, :, None]
    return full.reshape(s.shape[:-3] + (C_WIDTH, C_KW))


def _blockdiag_to_state(s_t):
    blocks = [s_t[..., C_DV * h:C_DV * (h + 1), C_DK * h:C_DK * (h + 1)] for h in range(C_HEADS)]
    return jnp.swapaxes(jnp.stack(blocks, axis=-3), -1, -2)


def _prepare_weights(g_pre, g_post, w_in, w_out, a_q_gain, a_k_gain, b_lambda_q1, b_lambda_k1,
                     b_lambda_q2, b_lambda_k2, b_out_gain, c_gate_w_fwd, c_gate_b_fwd, c_gate_w_bwd,
                     c_gate_b_bwd, c_out_gain):
    w_in_t = jnp.swapaxes(w_in, 1, 2).astype(BF16)
    pad = jnp.zeros((DEPTH, GATE_RANK, C_KW), F32)
    cw_f = jnp.concatenate([c_gate_w_fwd, pad], axis=1).astype(BF16)
    cw_b = jnp.concatenate([pad, c_gate_w_bwd], axis=1).astype(BF16)
    return {
        "g_pre": g_pre[:, None, :],
        "g_post": g_post[:, None, :],
        "w_in_t": w_in_t,
        "w_out": w_out.astype(BF16),
        "aq_gain": jnp.tile(a_q_gain, (1, A_HEADS))[:, None, :],
        "ak_gain": jnp.tile(a_k_gain, (1, A_KV_HEADS))[:, None, :],
        "cw_f": cw_f,
        "cb_f": c_gate_b_fwd[:, None, :],
        "cw_b": cw_b,
        "cb_b": c_gate_b_bwd[:, None, :],
        "b_out_gain": jnp.tile(b_out_gain, (1, 2))[:, None, :],
        "c_out_gain": jnp.tile(c_out_gain, (1, C_HEADS))[:, None, :],
        "lam_params": jnp.stack([b_lambda_q1, b_lambda_k1, b_lambda_q2, b_lambda_k2], axis=1),
    }


def kernel(x_prompt, x_sample, c, cache_a_k, cache_a_v, cache_b_k, cache_b_v, state_c_fwd, state_c_bwd, c_ctx, w_mod, b_mod, g_pre, g_post, w_in, w_out, a_q_gain, a_k_gain, b_lambda_q1, b_lambda_k1, b_lambda_q2, b_lambda_k2, b_out_gain, c_gate_w_fwd, c_gate_b_fwd, c_gate_w_bwd, c_gate_b_bwd, c_out_gain):
    dec_batch = x_sample.shape[0]
    dec_seq = x_sample.shape[1]

    mod_rows = 16
    cvec = jnp.zeros((mod_rows, D_MODEL), F32).at[0:dec_batch].set(c).at[dec_batch].set(c_ctx)
    mod = _modulation(cvec, w_mod, b_mod)[:, :, None, :]
    mod_lat = mod[:, 0:dec_batch]
    mod_ctx = mod[:, dec_batch:dec_batch + 1]

    wts = _prepare_weights(g_pre, g_post, w_in, w_out, a_q_gain, a_k_gain, b_lambda_q1, b_lambda_k1,
                           b_lambda_q2, b_lambda_k2, b_out_gain, c_gate_w_fwd, c_gate_b_fwd,
                           c_gate_w_bwd, c_gate_b_bwd, c_out_gain)
    rope_tabs = _rope_tables(dec_seq)
    cache = tuple(jnp.swapaxes(a, -1, -2) for a in (cache_a_k, cache_a_v, cache_b_k, cache_b_v))
    cache += (_state_to_blockdiag(state_c_fwd), _state_to_blockdiag(state_c_bwd))

    y_p, y_s = x_prompt, x_sample
    kv_ctx = None
    states_f, states_b = [], []
    for l in range(DEPTH):
        lam_init = 0.8 - 0.6 * math.exp(-0.3 * l)
        proj_p = _in_projection(y_p, mod_ctx, False, l, wts, None, 2, x_prompt.shape[1], kv_ctx)
        kv_ctx = proj_p[10:14]
        y_p, s_f, s_b = _mixer(y_p, mod_ctx, False, l, proj_p, wts, lam_init, None)
        states_f.append(s_f)
        states_b.append(s_b)

        proj_s = _in_projection(y_s, mod_lat, True, l, wts, rope_tabs, 1, 512)
        (y_s,) = _mixer(y_s, mod_lat, True, l, proj_s, wts, lam_init, cache)

    va_t, vb_t, ka_t, kb_t = kv_ctx
    new_kv = [jnp.swapaxes(a, -1, -2) for a in (ka_t, va_t, kb_t, vb_t)]
    new_s_f = _blockdiag_to_state(jnp.stack(states_f, axis=1))
    new_s_b = _blockdiag_to_state(jnp.stack(states_b, axis=1))
    return (y_p, y_s, *new_kv, new_s_f, new_s_b)
```

```python
import functools
import math

import jax
import jax.numpy as jnp
from jax import lax
from jax.experimental import pallas as pl
from jax.experimental.pallas import tpu as pltpu

F32 = jnp.float32
BF16 = jnp.bfloat16

D_MODEL = 1024
DEPTH = 2
GRID_W = 64
HEAD_DIM = 64
A_HEADS = 8
A_KV_HEADS = 2
A_GROUP = A_HEADS // A_KV_HEADS
A_WIDTH = A_HEADS * HEAD_DIM
A_KV_WIDTH = A_KV_HEADS * HEAD_DIM
B_HEADS = 4
B_QK_DIM = 32
B_V_DIM = 64
B_WIDTH = B_HEADS * B_V_DIM
C_HEADS = 4
C_DK = 32
C_DV = 64
C_KW = C_HEADS * C_DK
C_WIDTH = C_HEADS * C_DV
GATE_RANK = 16
GLA_TAU = 16.0
CHUNK = 64
D_MIX = A_WIDTH + B_WIDTH + C_WIDTH
ROPE_THETA = 10000.0
EPS = 1e-6

LANES = 128
GLA_BLOCK = 256
VT_ROWS = HEAD_DIM + 16
SCORE_LOOKAHEAD = 3
SAFE_GAP = 96.0
BOUND_SLACK = 1.02
KEY_TILE = 256
LOG2E = math.log2(math.e)

OFF_AQ = 0
OFF_AK = OFF_AQ + A_WIDTH
OFF_AV = OFF_AK + A_KV_WIDTH
OFF_BQ = OFF_AV + A_KV_WIDTH
OFF_BK = OFF_BQ + B_WIDTH
OFF_BV = OFF_BK + B_WIDTH
OFF_CQ = OFF_BV + B_WIDTH
OFF_CK = OFF_CQ + C_KW
OFF_CV = OFF_CK + C_KW
OFF_LR = OFF_CV + C_WIDTH
OFF_U = OFF_LR + 2 * GATE_RANK
IN_WIDTH = OFF_U + D_MIX

VMEM_LIMIT = 56 * 1024 * 1024


def _dot(a, b):
    return jnp.dot(a, b, preferred_element_type=F32)


def _dot_nt(a, b):
    return lax.dot_general(a, b, (((1,), (1,)), ((), ())), preferred_element_type=F32)


def _dot_tn(a, b):
    return lax.dot_general(a, b, (((0,), (0,)), ((), ())), preferred_element_type=F32)


def _split_bf16(x):
    hi = x.astype(BF16)
    lo = (x - hi.astype(F32)).astype(BF16)
    return hi, lo


def _iota(shape, dim):
    return lax.broadcasted_iota(jnp.int32, shape, dim)


def _group_mean_sq(x, group_log2, split=True):
    width = x.shape[-1]
    r = lax.shift_right_logical(_iota((LANES, LANES), 0), group_log2)
    c = lax.shift_right_logical(_iota((LANES, LANES), 1), group_log2)
    ones = jnp.where(r == c, 1.0, 0.0).astype(BF16)
    if split:
        hi, lo = _split_bf16(x * x)
    else:
        hi, lo = (x * x).astype(BF16), None
    cols = []
    for j in range(width // LANES):
        sl = slice(LANES * j, LANES * (j + 1))
        cols.append(_dot(hi[:, sl], ones) + (_dot(lo[:, sl], ones) if split else 0.0))
    ss = cols[0] if len(cols) == 1 else jnp.concatenate(cols, axis=-1)
    return ss * (1.0 / (1 << group_log2))


def _rope(x, cos, sin_signed, dist):
    width = x.shape[-1]
    lane = _iota(x.shape, 1)
    first = (lane & (2 * dist - 1)) < dist
    up = pltpu.roll(x, width - dist, 1)
    down = pltpu.roll(x, dist, 1)
    return x * cos + jnp.where(first, up, down) * sin_signed


def _log_sigmoid(x):
    return jnp.minimum(x, 0.0) - jnp.log1p(jnp.exp(-jnp.abs(x)))


def _silu(x):
    return x * (1.0 / (1.0 + jnp.exp(-x)))


def _mod_kernel(c_ref, w_ref, b_ref, o_ref):
    a = _silu(c_ref[...]).astype(BF16)
    o_ref[...] = _dot(a, w_ref[...].astype(BF16)) + b_ref[...]


def _modulation(cvec, w_mod, b_mod):
    rows = cvec.shape[0]
    nblk = 3
    return pl.pallas_call(
        _mod_kernel,
        grid=(DEPTH, nblk),
        in_specs=[
            pl.BlockSpec((rows, D_MODEL), lambda l, n: (0, 0)),
            pl.BlockSpec((None, D_MODEL, D_MODEL), lambda l, n: (l, 0, n)),
            pl.BlockSpec((None, 1, D_MODEL), lambda l, n: (l, 0, n)),
        ],
        out_specs=pl.BlockSpec((None, rows, D_MODEL), lambda l, n: (l, 0, n)),
        out_shape=jax.ShapeDtypeStruct((DEPTH, rows, 3 * D_MODEL), F32),
        compiler_params=pltpu.CompilerParams(
            dimension_semantics=("arbitrary", "arbitrary"), vmem_limit_bytes=VMEM_LIMIT),
        name="modulation",
    )(cvec, w_mod, b_mod.reshape(DEPTH, 1, 3 * D_MODEL))


def _in_kernel(rope, layer, stacked_first, nb, tl, n_aliased, *refs):
    (x_ref, mod_ref, gpre_ref, wt_ref, aqg_ref, akg_ref, cwf_ref, cbf_ref, cwb_ref, cbb_ref) = refs[:10]
    refs = refs[10:]
    if rope:
        cosa_ref, sina_ref, cosb_ref, sinb_ref = refs[:4]
        refs = refs[4:]
    refs = refs[n_aliased:]
    (qa_ref, ka_ref, qb_ref, kb_ref, cq_ref, ck_ref, cv_ref, gf_ref, gb_ref, su_ref,
     vat_ref, vbt_ref) = refs[:12]
    kat_ref, kbt_ref = refs[12:] if len(refs) > 12 else (None, None)

    x = x_ref[...].reshape(nb * tl, D_MODEL)
    shift = mod_ref[:, 0:D_MODEL]
    scale = mod_ref[:, D_MODEL:2 * D_MODEL]
    ms = jnp.mean(x * x, axis=-1, keepdims=True)
    h = (x * lax.rsqrt(ms + EPS)) * gpre_ref[...] * (1.0 + scale) + shift
    hb = h.astype(BF16)

    def proj(off, width):
        return _dot_nt(hb, wt_ref[off:off + width, :])

    def put_rows(ref, val):
        for bi in range(nb):
            ref[bi] = val[bi * tl:(bi + 1) * tl].astype(ref.dtype)

    def put_heads(ref, val, n_heads):
        for bi in range(nb):
            for hd in range(n_heads):
                ref[bi, hd] = val[bi * tl:(bi + 1) * tl,
                                  HEAD_DIM * hd:HEAD_DIM * (hd + 1)].astype(ref.dtype)

    def put_heads_t(ref, val, n_heads):
        val_t = val.T
        for bi in range(nb):
            for hd in range(n_heads):
                blk = val_t[HEAD_DIM * hd:HEAD_DIM * (hd + 1), bi * tl:(bi + 1) * tl].astype(ref.dtype)
                if stacked_first:
                    for l2 in range(DEPTH):
                        ref[bi, l2, hd] = blk if l2 == layer else jnp.zeros_like(blk)
                else:
                    ref[bi, hd] = blk

    aq = proj(OFF_AQ, A_WIDTH)
    aq = aq * lax.rsqrt(_group_mean_sq(aq, 6, split=False) + EPS) * aqg_ref[...]
    akv = proj(OFF_AK, 2 * A_KV_WIDTH)
    ak = akv[:, 0:A_KV_WIDTH]
    ak = ak * lax.rsqrt(_group_mean_sq(ak, 6, split=False) + EPS) * akg_ref[...]
    if rope:
        aq = _rope(aq, cosa_ref[...], sina_ref[...], 16)
        ak = _rope(ak, cosa_ref[:, 0:A_KV_WIDTH], sina_ref[:, 0:A_KV_WIDTH], 16)
    put_heads(qa_ref, aq * (HEAD_DIM ** -0.5 * LOG2E), A_HEADS)
    put_heads(ka_ref, ak, A_KV_HEADS)
    put_heads_t(vat_ref, akv[:, A_KV_WIDTH:2 * A_KV_WIDTH], A_KV_HEADS)
    if kat_ref is not None:
        put_heads_t(kat_ref, ak, A_KV_HEADS)

    bq = proj(OFF_BQ, B_WIDTH)
    bk = proj(OFF_BK, B_WIDTH)
    if rope:
        bq = _rope(bq, cosb_ref[...], sinb_ref[...], 8)
        bk = _rope(bk, cosb_ref[...], sinb_ref[...], 8)
    put_heads(qb_ref, bq * (B_QK_DIM ** -0.5 * LOG2E), B_HEADS)
    put_heads(kb_ref, bk, B_HEADS)
    put_heads_t(vbt_ref, proj(OFF_BV, B_WIDTH), B_HEADS)
    if kbt_ref is not None:
        put_heads_t(kbt_ref, bk, B_HEADS)

    cqk = proj(OFF_CQ, 2 * C_KW)
    put_rows(cq_ref, cqk[:, 0:C_KW] * (C_DK ** -0.5))
    put_rows(ck_ref, cqk[:, C_KW:2 * C_KW])
    put_rows(cv_ref, proj(OFF_CV, C_WIDTH))
    lr = proj(OFF_LR, 2 * GATE_RANK).astype(BF16)
    put_rows(gf_ref, _log_sigmoid(_dot(lr, cwf_ref[...]) + cbf_ref[...]) * (1.0 / GLA_TAU))
    put_rows(gb_ref, _log_sigmoid(_dot(lr, cwb_ref[...]) + cbb_ref[...]) * (1.0 / GLA_TAU))

    put_rows(su_ref, _silu(proj(OFF_U, D_MIX)))


def _in_projection(x, mod, per_batch_mod, layer, wts, rope_tabs, nb, tl, kv_prev=None):
    bsz, seq, _ = x.shape
    rope = rope_tabs is not None
    stacked = not rope
    stacked_first = stacked and kv_prev is None
    grid = (bsz // nb, seq // tl)

    def per_layer(shape):
        return pl.BlockSpec((None,) + shape, lambda b, t: (layer,) + (0,) * len(shape))

    mod_idx = (lambda b, t: (layer, b, 0, 0)) if per_batch_mod else (lambda b, t: (layer, 0, 0, 0))
    in_specs = [
        pl.BlockSpec((nb, tl, D_MODEL), lambda b, t: (b, t, 0)),
        pl.BlockSpec((None, None, 1, 3 * D_MODEL), mod_idx),
        per_layer((1, D_MODEL)),
        per_layer((IN_WIDTH, D_MODEL)),
        per_layer((1, A_WIDTH)),
        per_layer((1, A_KV_WIDTH)),
        per_layer((2 * GATE_RANK, C_KW)),
        per_layer((1, C_KW)),
        per_layer((2 * GATE_RANK, C_KW)),
        per_layer((1, C_KW)),
    ]
    args = [x, mod, wts["g_pre"], wts["w_in_t"], wts["aq_gain"], wts["ak_gain"],
            wts["cw_f"], wts["cb_f"], wts["cw_b"], wts["cb_b"]]
    if rope:
        assert nb == 1
        in_specs += [
            pl.BlockSpec((tl, A_WIDTH), lambda b, t: (t, 0)),
            pl.BlockSpec((tl, A_WIDTH), lambda b, t: (t, 0)),
            pl.BlockSpec((tl, B_WIDTH), lambda b, t: (t, 0)),
            pl.BlockSpec((tl, B_WIDTH), lambda b, t: (t, 0)),
        ]
        args += list(rope_tabs)

    def heads(n):
        return pl.BlockSpec((nb, n, tl, HEAD_DIM), lambda b, t: (b, 0, t, 0))

    def heads_t(n):
        if stacked_first:
            return pl.BlockSpec((nb, DEPTH, n, HEAD_DIM, tl), lambda b, t: (b, 0, 0, 0, t))
        if stacked:
            return pl.BlockSpec((nb, None, n, HEAD_DIM, tl), lambda b, t: (b, layer, 0, 0, t))
        return pl.BlockSpec((nb, n, HEAD_DIM, tl), lambda b, t: (b, 0, 0, t))

    def rows(width):
        return pl.BlockSpec((nb, tl, width), lambda b, t: (b, t, 0))

    def hshape(n):
        return jax.ShapeDtypeStruct((bsz, n, seq, HEAD_DIM), BF16)

    def tshape(n):
        if stacked:
            return jax.ShapeDtypeStruct((bsz, DEPTH, n, HEAD_DIM, seq), F32)
        return jax.ShapeDtypeStruct((bsz, n, HEAD_DIM, seq), BF16)

    def rshape(width):
        return jax.ShapeDtypeStruct((bsz, seq, width), F32)

    out_specs = [heads(A_HEADS), heads(A_KV_HEADS), heads(B_HEADS), heads(B_HEADS),
                 rows(C_KW), rows(C_KW), rows(C_WIDTH), rows(C_KW), rows(C_KW), rows(D_MIX),
                 heads_t(A_KV_HEADS), heads_t(B_HEADS)]
    out_shape = [hshape(A_HEADS), hshape(A_KV_HEADS), hshape(B_HEADS), hshape(B_HEADS),
                 rshape(C_KW), rshape(C_KW), rshape(C_WIDTH), rshape(C_KW), rshape(C_KW),
                 rshape(D_MIX), tshape(A_KV_HEADS), tshape(B_HEADS)]
    if stacked:
        out_specs += [heads_t(A_KV_HEADS), heads_t(B_HEADS)]
        out_shape += [tshape(A_KV_HEADS), tshape(B_HEADS)]
    aliases = {}
    if kv_prev is not None:
        for j, buf in enumerate(kv_prev):
            aliases[len(args)] = 10 + j
            in_specs.append(pl.BlockSpec(memory_space=pl.ANY))
            args.append(buf)
    return pl.pallas_call(
        functools.partial(_in_kernel, rope, layer, stacked_first, nb, tl, len(aliases)),
        grid=grid,
        in_specs=in_specs,
        out_specs=out_specs,
        out_shape=out_shape,
        input_output_aliases=aliases,
        compiler_params=pltpu.CompilerParams(
            dimension_semantics=("arbitrary", "arbitrary"), vmem_limit_bytes=VMEM_LIMIT),
        name="in_projection_rope" if rope else "in_projection",
    )(*args)


def _gla_bidirectional(cq_ref, ck_ref, cv_ref, gf_ref, gb_ref, s_f, s_b, seq, oc_ref):
    bl = GLA_BLOCK
    n_sub = bl // CHUNK
    nblk = seq // bl
    ri = _iota((bl, bl), 0)
    ci = _iota((bl, bl), 1)
    same_chunk = lax.shift_right_logical(ri, 6) == lax.shift_right_logical(ci, 6)
    bd = (lax.shift_right_logical(_iota((C_WIDTH, C_KW), 0), 6)
          == lax.shift_right_logical(_iota((C_WIDTH, C_KW), 1), 5))
    khead = lax.shift_right_logical(_iota((1, C_KW), 1), 5)
    vhead = lax.shift_right_logical(_iota((1, C_WIDTH), 1), 6)
    scans = []
    for reverse, g_ref in ((False, gf_ref), (True, gb_ref)):
        causal = same_chunk & ((ci >= ri) if reverse else (ci <= ri))
        scans.append((reverse, g_ref, causal, jnp.where(causal, 1.0, 0.0).astype(BF16)))
    states = [s_f, s_b]
    written = set()

    for step in range(nblk):
        rows0 = [step * bl, (nblk - 1 - step) * bl]
        cums = []
        for (reverse, g_ref, causal, tri), r0 in zip(scans, rows0):
            g_hi, g_lo = _split_bf16(g_ref[r0:r0 + bl, :])
            cums.append(_dot(tri, g_hi) + _dot(tri, g_lo))
        prep = []
        for (reverse, g_ref, causal, tri), r0, cum in zip(scans, rows0, cums):
            q = cq_ref[r0:r0 + bl, :]
            k = ck_ref[r0:r0 + bl, :]
            v = cv_ref[r0:r0 + bl, :]
            qt = q * jnp.exp(cum)
            ktb = (k * jnp.exp(-cum)).astype(BF16)
            vb = v.astype(BF16)
            lasts, kdecs = [], []
            for c in range(n_sub):
                c0 = CHUNK * c
                edge = c0 if reverse else c0 + CHUNK - 1
                last = cum[edge:edge + 1, :]
                lasts.append(last)
                kdecs.append((k[c0:c0 + CHUNK] * jnp.exp(last - cum[c0:c0 + CHUNK])).astype(BF16))
            prep.append((qt, ktb, v, vb, lasts, kdecs))
        scores, incs = [], []
        for qt, ktb, v, vb, lasts, kdecs in prep:
            scores.append([_dot_nt(jnp.where(khead == hd, qt, 0.0).astype(BF16), ktb)
                           for hd in range(C_HEADS)])
            incs.append([_dot_tn(vb[CHUNK * c:CHUNK * (c + 1)], kdecs[c]) for c in range(n_sub)])
        probs, entering = [], []
        for si, ((reverse, g_ref, causal, tri), (qt, ktb, v, vb, lasts, kdecs)) in enumerate(
                zip(scans, prep)):
            probs.append([jnp.where(causal, s, 0.0).astype(BF16) for s in scores[si]])
            s_t = states[si]
            before = [None] * n_sub
            subs = range(n_sub)
            for c in (reversed(subs) if reverse else subs):
                before[c] = s_t.astype(BF16)
                s_t = jnp.exp(lasts[c]) * s_t + jnp.where(bd, incs[si][c], 0.0)
            states[si] = s_t
            entering.append(before)
        outs = []
        for si, (qt, ktb, v, vb, lasts, kdecs) in enumerate(prep):
            qtb = qt.astype(BF16)
            o = jnp.concatenate([_dot_nt(qtb[CHUNK * c:CHUNK * (c + 1)], entering[si][c])
                                 for c in range(n_sub)], axis=0)
            for hd in range(C_HEADS):
                o = o + _dot(probs[si][hd], jnp.where(vhead == hd, v, 0.0).astype(BF16))
            outs.append(o)
        if rows0[0] == rows0[1]:
            outs, rows0 = [outs[0] + outs[1]], rows0[:1]
        for o, r0 in zip(outs, rows0):
            if r0 in written:
                oc_ref[r0:r0 + bl, :] = oc_ref[r0:r0 + bl, :] + o
            else:
                oc_ref[r0:r0 + bl, :] = o
                written.add(r0)
    return states[0], states[1]


def _attend_t(jobs, shifts=None):
    def scores(i):
        return _dot_nt(jobs[i][0][...], jobs[i][1])

    outs = []
    pending = [scores(i) for i in range(min(SCORE_LOOKAHEAD, len(jobs)))]
    for i, (_, _, vt) in enumerate(jobs):
        st = pending.pop(0)
        if i + SCORE_LOOKAHEAD < len(jobs):
            pending.append(scores(i + SCORE_LOOKAHEAD))
        m = jnp.max(st, axis=0, keepdims=True) if shifts is None else shifts[i]
        p = jnp.exp2(st - m).astype(BF16)
        ot = _dot(vt[...], p)
        outs.append(ot[0:HEAD_DIM] * (1.0 / ot[HEAD_DIM:HEAD_DIM + 1]))
    return outs


def _pair_rows(a, b):
    return jnp.concatenate([a, b], axis=0).T


def _mix_kernel(cached, lam_init, seq, qt, qb, *refs):
    it = iter(refs)
    x_ref, mod_ref = next(it), next(it)
    qa_ref, ka_ref, qb_ref, kb_ref = (next(it) for _ in range(4))
    cq_ref, ck_ref, cv_ref, gf_ref, gb_ref, su_ref = (next(it) for _ in range(6))
    vat_ref, vbt_ref = next(it), next(it)
    if cached:
        cakt_ref, cavt_ref, cbkt_ref, cbvt_ref, s0f_ref, s0b_ref = (next(it) for _ in range(6))
    wout_ref, gpost_ref, bog_ref, cog_ref, lamp_ref = (next(it) for _ in range(5))
    y_ref = next(it)
    if not cached:
        sf_ref, sb_ref = next(it), next(it)
    kA_s, vtA_s, kB_s, vtB_s, oc_s, mixed_s, kn2_s = (next(it) for _ in range(7))

    lk = kA_s.shape[1]
    past = lk - seq
    t = pl.program_id(1)

    @pl.when(t == 0)
    def _per_sequence():
        ones_row = jnp.where(_iota((VT_ROWS - HEAD_DIM, lk), 0) == 0, 1.0, 0.0).astype(BF16)
        for k_new, kt_cache, k_dst, vt_new, vt_cache, vt_dst in (
                (ka_ref, cakt_ref if cached else None, kA_s, vat_ref, cavt_ref if cached else None, vtA_s),
                (kb_ref, cbkt_ref if cached else None, kB_s, vbt_ref, cbvt_ref if cached else None, vtB_s)):
            n_heads = k_dst.shape[0]
            if cached:
                for h0 in range(0, n_heads, 2):
                    pair = jnp.concatenate([kt_cache[h0], kt_cache[h0 + 1]], axis=0).T
                    k_dst[h0, 0:past, :] = pair[:, 0:HEAD_DIM].astype(BF16)
                    k_dst[h0 + 1, 0:past, :] = pair[:, HEAD_DIM:2 * HEAD_DIM].astype(BF16)
            for hd in range(n_heads):
                k_dst[hd, past:lk, :] = k_new[hd]
                if cached:
                    vt_dst[hd, 0:HEAD_DIM, 0:past] = vt_cache[hd].astype(BF16)
                vt_dst[hd, 0:HEAD_DIM, past:lk] = vt_new[hd].astype(BF16)
                vt_dst[hd, HEAD_DIM:VT_ROWS, :] = ones_row
        bounded_jobs = ([(kA_s, g) for g in range(A_KV_HEADS)] + [(kB_s, h) for h in range(B_HEADS)]
                        if lk > KEY_TILE else [])
        for j, (k_dst, hd) in enumerate(bounded_jobs):
            kf = k_dst[hd].astype(F32)
            kn2 = jnp.max(jnp.sum(kf * kf, axis=-1, keepdims=True), axis=0, keepdims=True)
            kn2_s[j:j + 1, :] = jnp.broadcast_to(kn2, (1, LANES))

        if cached:
            s0f, s0b = s0f_ref[...], s0b_ref[...]
        else:
            s0f = jnp.zeros((C_WIDTH, C_KW), F32)
            s0b = s0f
        s_f, s_b = _gla_bidirectional(cq_ref, ck_ref, cv_ref, gf_ref, gb_ref, s0f, s0b, seq, oc_s)
        if not cached:
            sf_ref[...] = s_f
            sb_ref[...] = s_b
        for r0 in range(0, seq, GLA_BLOCK):
            oc = oc_s[r0:r0 + GLA_BLOCK, :]
            oc_s[r0:r0 + GLA_BLOCK, :] = oc * lax.rsqrt(_group_mean_sq(oc, 6) + EPS) * cog_ref[...]

    lam = (jnp.exp(jnp.sum(lamp_ref[0:1, :] * lamp_ref[1:2, :], axis=-1, keepdims=True))
           - jnp.exp(jnp.sum(lamp_ref[2:3, :] * lamp_ref[3:4, :], axis=-1, keepdims=True))
           + lam_init)

    def attn_block(i, carry):
        rows = pl.ds(pl.multiple_of(i * qb, qb), qb)
        jobs = []
        for grp in range(A_KV_HEADS):
            q4 = qa_ref[A_GROUP * grp:A_GROUP * (grp + 1), rows, :].reshape(A_GROUP * qb, HEAD_DIM)
            jobs.append((kA_s.at[grp], q4, vtA_s.at[grp]))
        lane = _iota((qb, HEAD_DIM), 1)
        for hd in range(B_HEADS):
            q = qb_ref[hd, rows, :]
            zero = jnp.zeros_like(q)
            q2 = jnp.concatenate([jnp.where(lane < B_QK_DIM, q, zero),
                                  jnp.where(lane >= B_QK_DIM, q, zero)], axis=0)
            jobs.append((kB_s.at[hd], q2, vtB_s.at[hd]))

        def finish(outs):
            for grp in range(A_KV_HEADS):
                ot = outs[grp]
                for pair in range(A_GROUP // 2):
                    c0 = 2 * pair * qb
                    col = A_GROUP * HEAD_DIM * grp + 2 * HEAD_DIM * pair
                    mixed_s[rows, col:col + 2 * HEAD_DIM] = _pair_rows(ot[:, c0:c0 + qb],
                                                                       ot[:, c0 + qb:c0 + 2 * qb])
            obs = []
            for hd in range(B_HEADS):
                ot = outs[A_KV_HEADS + hd]
                ob = ot[:, 0:qb] - lam * ot[:, qb:2 * qb]
                obs.append(ob * lax.rsqrt(jnp.mean(ob * ob, axis=0, keepdims=True) + EPS))
            for pair in range(B_HEADS // 2):
                col = A_WIDTH + 2 * B_V_DIM * pair
                mixed_s[rows, col:col + 2 * B_V_DIM] = (_pair_rows(obs[2 * pair], obs[2 * pair + 1])
                                                        * bog_ref[...] * (1.0 - lam_init))

        if lk <= KEY_TILE:
            finish(_attend_t(jobs))
            return carry

        ones8 = jnp.ones((8, HEAD_DIM), BF16)
        own = pl.ds(pl.multiple_of(past + t * qt + i * qb, qb), qb)
        shifts, gaps = [], []
        for j, (k_ref, q, _) in enumerate(jobs):
            qf = q.astype(F32)
            k_own = k_ref[own, :].astype(F32)
            k_own = jnp.concatenate([k_own] * (q.shape[0] // qb), axis=0)
            qn2 = _dot_nt(ones8, (qf * qf).astype(BF16))[0:1]
            lower = _dot_nt(ones8, (qf * k_own).astype(BF16))[0:1]
            upper = jnp.sqrt(qn2 * kn2_s[j:j + 1, 0:1]) * BOUND_SLACK
            shifts.append(upper)
            gaps.append(jnp.max(upper - lower, axis=-1, keepdims=True))
        worst = functools.reduce(jnp.maximum, gaps)
        safe = worst[0, 0] <= SAFE_GAP

        @pl.when(safe)
        def _bounded():
            finish(_attend_t(jobs, shifts))

        @pl.when(jnp.logical_not(safe))
        def _exact_max():
            finish(_attend_t(jobs))

        return carry

    lax.fori_loop(0, qt // qb, attn_block, 0)

    seq_rows = pl.ds(pl.multiple_of(t * qt, qt), qt)
    mixed_s[:, A_WIDTH + B_WIDTH:D_MIX] = oc_s[seq_rows, :]
    gate = mod_ref[:, 2 * D_MODEL:3 * D_MODEL]
    mixed = (mixed_s[...] * su_ref[...]).astype(BF16)
    y = _dot(mixed, wout_ref[...])
    yn = y * lax.rsqrt(jnp.mean(y * y, axis=-1, keepdims=True) + EPS) * gpost_ref[...]
    y_ref[...] = x_ref[...] + gate * yn


def _mixer(x, mod, per_batch_mod, layer, proj, wts, lam_init, cache):
    bsz, seq, _ = x.shape
    cached = cache is not None
    past = cache[0].shape[4] if cached else 0
    qa, ka, qb, kb, cq, ck, cv, gf, gb, su, vat, vbt = proj[:12]
    qt = 256
    qb_rows = 256

    def per_layer(shape):
        return pl.BlockSpec((None,) + shape, lambda b, t: (layer,) + (0,) * len(shape))

    def layer_heads_t(n, length):
        return pl.BlockSpec((None, None, n, HEAD_DIM, length), lambda b, t: (b, layer, 0, 0, 0))

    def heads_t(n):
        if cached:
            return pl.BlockSpec((None, n, HEAD_DIM, seq), lambda b, t: (b, 0, 0, 0))
        return layer_heads_t(n, seq)

    def heads(n):
        return pl.BlockSpec((None, n, seq, HEAD_DIM), lambda b, t: (b, 0, 0, 0))

    def head_tile(n):
        return pl.BlockSpec((None, n, qt, HEAD_DIM), lambda b, t: (b, 0, t, 0))

    def rows(width):
        return pl.BlockSpec((None, seq, width), lambda b, t: (b, 0, 0))

    def row_tile(width):
        return pl.BlockSpec((None, qt, width), lambda b, t: (b, t, 0))

    mod_idx = (lambda b, t: (layer, b, 0, 0)) if per_batch_mod else (lambda b, t: (layer, 0, 0, 0))
    in_specs = [row_tile(D_MODEL), pl.BlockSpec((None, None, 1, 3 * D_MODEL), mod_idx),
                head_tile(A_HEADS), heads(A_KV_HEADS), head_tile(B_HEADS), heads(B_HEADS),
                rows(C_KW), rows(C_KW), rows(C_WIDTH), rows(C_KW), rows(C_KW), row_tile(D_MIX),
                heads_t(A_KV_HEADS), heads_t(B_HEADS)]
    args = [x, mod, qa, ka, qb, kb, cq, ck, cv, gf, gb, su, vat, vbt]
    if cached:
        state_in = pl.BlockSpec((None, None, C_WIDTH, C_KW), lambda b, t: (b, layer, 0, 0))
        in_specs += [layer_heads_t(A_KV_HEADS, past), layer_heads_t(A_KV_HEADS, past),
                     layer_heads_t(B_HEADS, past), layer_heads_t(B_HEADS, past), state_in, state_in]
        args += list(cache)
    in_specs += [per_layer((D_MIX, D_MODEL)), per_layer((1, D_MODEL)), per_layer((1, 2 * B_V_DIM)),
                 per_layer((1, C_WIDTH)), per_layer((4, B_QK_DIM))]
    args += [wts["w_out"], wts["g_post"], wts["b_out_gain"], wts["c_out_gain"], wts["lam_params"]]

    out_specs = [row_tile(D_MODEL)]
    out_shape = [jax.ShapeDtypeStruct((bsz, seq, D_MODEL), F32)]
    if not cached:
        state_out = pl.BlockSpec((None, C_WIDTH, C_KW), lambda b, t: (b, 0, 0))
        out_specs += [state_out, state_out]
        out_shape += [jax.ShapeDtypeStruct((bsz, C_WIDTH, C_KW), F32)] * 2

    lk = past + seq
    scratch = [pltpu.VMEM((A_KV_HEADS, lk, HEAD_DIM), BF16), pltpu.VMEM((A_KV_HEADS, VT_ROWS, lk), BF16),
               pltpu.VMEM((B_HEADS, lk, HEAD_DIM), BF16), pltpu.VMEM((B_HEADS, VT_ROWS, lk), BF16),
               pltpu.VMEM((seq, C_WIDTH), F32), pltpu.VMEM((qt, D_MIX), F32),
               pltpu.VMEM((8, LANES), F32)]
    return pl.pallas_call(
        functools.partial(_mix_kernel, cached, lam_init, seq, qt, qb_rows),
        grid=(bsz, seq // qt),
        in_specs=in_specs,
        out_specs=out_specs,
        out_shape=out_shape,
        scratch_shapes=scratch,
        compiler_params=pltpu.CompilerParams(
            dimension_semantics=("arbitrary", "arbitrary"), vmem_limit_bytes=VMEM_LIMIT),
        name="mixer_cached" if cached else "mixer",
    )(*args)


def _rope_tables(seq):
    t = jnp.arange(seq)
    pos_row = (t // GRID_W).astype(F32)
    pos_col = (t % GRID_W).astype(F32)

    def tables(half, width):
        freq = ROPE_THETA ** (-jnp.arange(half, dtype=F32) / half)
        ang_r = pos_row[:, None] * freq[None, :]
        ang_c = pos_col[:, None] * freq[None, :]
        cos = jnp.concatenate([jnp.cos(ang_r), jnp.cos(ang_r), jnp.cos(ang_c), jnp.cos(ang_c)], axis=-1)
        sin = jnp.concatenate([-jnp.sin(ang_r), jnp.sin(ang_r), -jnp.sin(ang_c), jnp.sin(ang_c)], axis=-1)
        reps = width // (4 * half)
        return jnp.tile(cos, (1, reps)), jnp.tile(sin, (1, reps))

    cos_a, sin_a = tables(HEAD_DIM // 4, A_WIDTH)
    cos_b, sin_b = tables(B_QK_DIM // 4, B_WIDTH)
    return cos_a, sin_a, cos_b, sin_b


def _state_to_blockdiag(s):
    st = jnp.swapaxes(s, -1, -2)
    eye = jnp.eye(C_HEADS, dtype=s.dtype)
    full = st[..., :, :, None, :] * eye[:, None, :, None]
    return full.reshape(s.shape[:-3] + (C_WIDTH, C_KW))


def _blockdiag_to_state(s_t):
    blocks = [s_t[..., C_DV * h:C_DV * (h + 1), C_DK * h:C_DK * (h + 1)] for h in range(C_HEADS)]
    return jnp.swapaxes(jnp.stack(blocks, axis=-3), -1, -2)


def _prepare_weights(g_pre, g_post, w_in, w_out, a_q_gain, a_k_gain, b_lambda_q1, b_lambda_k1,
                     b_lambda_q2, b_lambda_k2, b_out_gain, c_gate_w_fwd, c_gate_b_fwd, c_gate_w_bwd,
                     c_gate_b_bwd, c_out_gain):
    w_in_t = jnp.swapaxes(w_in, 1, 2).astype(BF16)
    pad = jnp.zeros((DEPTH, GATE_RANK, C_KW), F32)
    cw_f = jnp.concatenate([c_gate_w_fwd, pad], axis=1).astype(BF16)
    cw_b = jnp.concatenate([pad, c_gate_w_bwd], axis=1).astype(BF16)
    return {
        "g_pre": g_pre[:, None, :],
        "g_post": g_post[:, None, :],
        "w_in_t": w_in_t,
        "w_out": w_out.astype(BF16),
        "aq_gain": jnp.tile(a_q_gain, (1, A_HEADS))[:, None, :],
        "ak_gain": jnp.tile(a_k_gain, (1, A_KV_HEADS))[:, None, :],
        "cw_f": cw_f,
        "cb_f": c_gate_b_fwd[:, None, :],
        "cw_b": cw_b,
        "cb_b": c_gate_b_bwd[:, None, :],
        "b_out_gain": jnp.tile(b_out_gain, (1, 2))[:, None, :],
        "c_out_gain": jnp.tile(c_out_gain, (1, C_HEADS))[:, None, :],
        "lam_params": jnp.stack([b_lambda_q1, b_lambda_k1, b_lambda_q2, b_lambda_k2], axis=1),
    }


def kernel(x_prompt, x_sample, c, cache_a_k, cache_a_v, cache_b_k, cache_b_v, state_c_fwd, state_c_bwd, c_ctx, w_mod, b_mod, g_pre, g_post, w_in, w_out, a_q_gain, a_k_gain, b_lambda_q1, b_lambda_k1, b_lambda_q2, b_lambda_k2, b_out_gain, c_gate_w_fwd, c_gate_b_fwd, c_gate_w_bwd, c_gate_b_bwd, c_out_gain):
    dec_batch = x_sample.shape[0]
    dec_seq = x_sample.shape[1]

    mod_rows = 16
    cvec = jnp.zeros((mod_rows, D_MODEL), F32).at[0:dec_batch].set(c).at[dec_batch].set(c_ctx)
    mod = _modulation(cvec, w_mod, b_mod)[:, :, None, :]
    mod_lat = mod[:, 0:dec_batch]
    mod_ctx = mod[:, dec_batch:dec_batch + 1]

    wts = _prepare_weights(g_pre, g_post, w_in, w_out, a_q_gain, a_k_gain, b_lambda_q1, b_lambda_k1,
                           b_lambda_q2, b_lambda_k2, b_out_gain, c_gate_w_fwd, c_gate_b_fwd,
                           c_gate_w_bwd, c_gate_b_bwd, c_out_gain)
    rope_tabs = _rope_tables(dec_seq)
    cache = tuple(jnp.swapaxes(a, -1, -2) for a in (cache_a_k, cache_a_v, cache_b_k, cache_b_v))
    cache += (_state_to_blockdiag(state_c_fwd), _state_to_blockdiag(state_c_bwd))

    y_p, y_s = x_prompt, x_sample
    kv_ctx = None
    states_f, states_b = [], []
    for l in range(DEPTH):
        lam_init = 0.8 - 0.6 * math.exp(-0.3 * l)
        proj_p = _in_projection(y_p, mod_ctx, False, l, wts, None, 2, x_prompt.shape[1], kv_ctx)
        kv_ctx = proj_p[10:14]
        y_p, s_f, s_b = _mixer(y_p, mod_ctx, False, l, proj_p, wts, lam_init, None)
        states_f.append(s_f)
        states_b.append(s_b)

        proj_s = _in_projection(y_s, mod_lat, True, l, wts, rope_tabs, 1, 512)
        (y_s,) = _mixer(y_s, mod_lat, True, l, proj_s, wts, lam_init, cache)

    va_t, vb_t, ka_t, kb_t = kv_ctx
    new_kv = [jnp.swapaxes(a, -1, -2) for a in (ka_t, va_t, kb_t, vb_t)]
    new_s_f = _blockdiag_to_state(jnp.stack(states_f, axis=1))
    new_s_b = _blockdiag_to_state(jnp.stack(states_b, axis=1))
    return (y_p, y_s, *new_kv, new_s_f, new_s_b)
```

```python
import functools
import math

import jax
import jax.numpy as jnp
from jax import lax
from jax.experimental import pallas as pl
from jax.experimental.pallas import tpu as pltpu

F32 = jnp.float32
BF16 = jnp.bfloat16

D_MODEL = 1024
DEPTH = 2
GRID_W = 64
HEAD_DIM = 64
A_HEADS = 8
A_KV_HEADS = 2
A_GROUP = A_HEADS // A_KV_HEADS
A_WIDTH = A_HEADS * HEAD_DIM
A_KV_WIDTH = A_KV_HEADS * HEAD_DIM
B_HEADS = 4
B_QK_DIM = 32
B_V_DIM = 64
B_WIDTH = B_HEADS * B_V_DIM
C_HEADS = 4
C_DK = 32
C_DV = 64
C_KW = C_HEADS * C_DK
C_WIDTH = C_HEADS * C_DV
GATE_RANK = 16
GLA_TAU = 16.0
CHUNK = 64
D_MIX = A_WIDTH + B_WIDTH + C_WIDTH
ROPE_THETA = 10000.0
EPS = 1e-6

LANES = 128
GLA_BLOCK = 256
VT_ROWS = HEAD_DIM + 16
SCORE_LOOKAHEAD = 3
SAFE_GAP = 96.0
BOUND_SLACK = 1.02
QSTAT_ROWS = 4 * B_HEADS
KEY_TILE = 256
LOG2E = math.log2(math.e)

OFF_AQ = 0
OFF_AK = OFF_AQ + A_WIDTH
OFF_AV = OFF_AK + A_KV_WIDTH
OFF_BQ = OFF_AV + A_KV_WIDTH
OFF_BK = OFF_BQ + B_WIDTH
OFF_BV = OFF_BK + B_WIDTH
OFF_CQ = OFF_BV + B_WIDTH
OFF_CK = OFF_CQ + C_KW
OFF_CV = OFF_CK + C_KW
OFF_LR = OFF_CV + C_WIDTH
OFF_U = OFF_LR + 2 * GATE_RANK
IN_WIDTH = OFF_U + D_MIX

VMEM_LIMIT = 56 * 1024 * 1024


def _dot(a, b):
    return jnp.dot(a, b, preferred_element_type=F32)


def _dot_nt(a, b):
    return lax.dot_general(a, b, (((1,), (1,)), ((), ())), preferred_element_type=F32)


def _dot_tn(a, b):
    return lax.dot_general(a, b, (((0,), (0,)), ((), ())), preferred_element_type=F32)


def _split_bf16(x):
    hi = x.astype(BF16)
    lo = (x - hi.astype(F32)).astype(BF16)
    return hi, lo


def _iota(shape, dim):
    return lax.broadcasted_iota(jnp.int32, shape, dim)


def _group_mean_sq(x, group_log2, split=True):
    width = x.shape[-1]
    r = lax.shift_right_logical(_iota((LANES, LANES), 0), group_log2)
    c = lax.shift_right_logical(_iota((LANES, LANES), 1), group_log2)
    ones = jnp.where(r == c, 1.0, 0.0).astype(BF16)
    if split:
        hi, lo = _split_bf16(x * x)
    else:
        hi, lo = (x * x).astype(BF16), None
    cols = []
    for j in range(width // LANES):
        sl = slice(LANES * j, LANES * (j + 1))
        cols.append(_dot(hi[:, sl], ones) + (_dot(lo[:, sl], ones) if split else 0.0))
    ss = cols[0] if len(cols) == 1 else jnp.concatenate(cols, axis=-1)
    return ss * (1.0 / (1 << group_log2))


def _rope(x, cos, sin_signed, dist):
    width = x.shape[-1]
    lane = _iota(x.shape, 1)
    first = (lane & (2 * dist - 1)) < dist
    up = pltpu.roll(x, width - dist, 1)
    down = pltpu.roll(x, dist, 1)
    return x * cos + jnp.where(first, up, down) * sin_signed


def _log_sigmoid(x):
    return jnp.minimum(x, 0.0) - jnp.log1p(jnp.exp(-jnp.abs(x)))


def _silu(x):
    return x * (1.0 / (1.0 + jnp.exp(-x)))


def _mod_kernel(c_ref, w_ref, b_ref, o_ref):
    a = _silu(c_ref[...]).astype(BF16)
    o_ref[...] = _dot(a, w_ref[...].astype(BF16)) + b_ref[...]


def _modulation(cvec, w_mod, b_mod):
    rows = cvec.shape[0]
    nblk = 3
    return pl.pallas_call(
        _mod_kernel,
        grid=(DEPTH, nblk),
        in_specs=[
            pl.BlockSpec((rows, D_MODEL), lambda l, n: (0, 0)),
            pl.BlockSpec((None, D_MODEL, D_MODEL), lambda l, n: (l, 0, n)),
            pl.BlockSpec((None, 1, D_MODEL), lambda l, n: (l, 0, n)),
        ],
        out_specs=pl.BlockSpec((None, rows, D_MODEL), lambda l, n: (l, 0, n)),
        out_shape=jax.ShapeDtypeStruct((DEPTH, rows, 3 * D_MODEL), F32),
        compiler_params=pltpu.CompilerParams(
            dimension_semantics=("arbitrary", "arbitrary"), vmem_limit_bytes=VMEM_LIMIT),
        name="modulation",
    )(cvec, w_mod, b_mod.reshape(DEPTH, 1, 3 * D_MODEL))


def _in_kernel(rope, layer, stacked_first, nb, tl, n_aliased, *refs):
    (x_ref, mod_ref, gpre_ref, wt_ref, aqg_ref, akg_ref, cwf_ref, cbf_ref, cwb_ref, cbb_ref) = refs[:10]
    refs = refs[10:]
    if rope:
        cosa_ref, sina_ref, cosb_ref, sinb_ref = refs[:4]
        refs = refs[4:]
    refs = refs[n_aliased:]
    (qa_ref, ka_ref, qb_ref, kb_ref, cq_ref, ck_ref, cv_ref, gf_ref, gb_ref, su_ref,
     vat_ref, vbt_ref) = refs[:12]
    kat_ref, kbt_ref = (None, None) if rope else refs[12:]
    qstat_ref = refs[12] if rope else None

    x = x_ref[...].reshape(nb * tl, D_MODEL)
    shift = mod_ref[:, 0:D_MODEL]
    scale = mod_ref[:, D_MODEL:2 * D_MODEL]
    ms = jnp.mean(x * x, axis=-1, keepdims=True)
    h = (x * lax.rsqrt(ms + EPS)) * gpre_ref[...] * (1.0 + scale) + shift
    hb = h.astype(BF16)

    def proj(off, width):
        return _dot_nt(hb, wt_ref[off:off + width, :])

    def put_rows(ref, val):
        for bi in range(nb):
            ref[bi] = val[bi * tl:(bi + 1) * tl].astype(ref.dtype)

    def put_heads(ref, val, n_heads):
        for bi in range(nb):
            for hd in range(n_heads):
                ref[bi, hd] = val[bi * tl:(bi + 1) * tl,
                                  HEAD_DIM * hd:HEAD_DIM * (hd + 1)].astype(ref.dtype)

    def put_heads_t(ref, val, n_heads):
        val_t = val.T
        for bi in range(nb):
            for hd in range(n_heads):
                blk = val_t[HEAD_DIM * hd:HEAD_DIM * (hd + 1), bi * tl:(bi + 1) * tl].astype(ref.dtype)
                if stacked_first:
                    for l2 in range(DEPTH):
                        ref[bi, l2, hd] = blk if l2 == layer else jnp.zeros_like(blk)
                else:
                    ref[bi, hd] = blk

    aq = proj(OFF_AQ, A_WIDTH)
    aq = aq * lax.rsqrt(_group_mean_sq(aq, 6, split=False) + EPS) * aqg_ref[...]
    akv = proj(OFF_AK, 2 * A_KV_WIDTH)
    ak = akv[:, 0:A_KV_WIDTH]
    ak = ak * lax.rsqrt(_group_mean_sq(ak, 6, split=False) + EPS) * akg_ref[...]
    if rope:
        aq = _rope(aq, cosa_ref[...], sina_ref[...], 16)
        ak = _rope(ak, cosa_ref[:, 0:A_KV_WIDTH], sina_ref[:, 0:A_KV_WIDTH], 16)
    put_heads(qa_ref, aq * (HEAD_DIM ** -0.5 * LOG2E), A_HEADS)
    put_heads(ka_ref, ak, A_KV_HEADS)
    put_heads_t(vat_ref, akv[:, A_KV_WIDTH:2 * A_KV_WIDTH], A_KV_HEADS)
    if kat_ref is not None:
        put_heads_t(kat_ref, ak, A_KV_HEADS)

    bq = proj(OFF_BQ, B_WIDTH)
    bk = proj(OFF_BK, B_WIDTH)
    if rope:
        bq = _rope(bq, cosb_ref[...], sinb_ref[...], 8)
        bk = _rope(bk, cosb_ref[...], sinb_ref[...], 8)
    bq = bq * (B_QK_DIM ** -0.5 * LOG2E)
    put_heads(qb_ref, bq, B_HEADS)
    put_heads(kb_ref, bk, B_HEADS)
    if rope:
        bq_t = bq.T
        sq = bq_t * bq_t
        own = bq_t * bk.T
        halves = range(0, B_WIDTH, B_QK_DIM)
        qstat_ref[0] = jnp.concatenate(
            [jnp.sum(sq[r0:r0 + B_QK_DIM], axis=0, keepdims=True) for r0 in halves]
            + [jnp.sum(own[r0:r0 + B_QK_DIM], axis=0, keepdims=True) for r0 in halves], axis=0)
    put_heads_t(vbt_ref, proj(OFF_BV, B_WIDTH), B_HEADS)
    if kbt_ref is not None:
        put_heads_t(kbt_ref, bk, B_HEADS)

    cqk = proj(OFF_CQ, 2 * C_KW)
    put_rows(cq_ref, cqk[:, 0:C_KW] * (C_DK ** -0.5))
    put_rows(ck_ref, cqk[:, C_KW:2 * C_KW])
    put_rows(cv_ref, proj(OFF_CV, C_WIDTH))
    lr = proj(OFF_LR, 2 * GATE_RANK).astype(BF16)
    put_rows(gf_ref, _log_sigmoid(_dot(lr, cwf_ref[...]) + cbf_ref[...]) * (1.0 / GLA_TAU))
    put_rows(gb_ref, _log_sigmoid(_dot(lr, cwb_ref[...]) + cbb_ref[...]) * (1.0 / GLA_TAU))

    put_rows(su_ref, _silu(proj(OFF_U, D_MIX)))


def _in_projection(x, mod, per_batch_mod, layer, wts, rope_tabs, nb, tl, kv_prev=None):
    bsz, seq, _ = x.shape
    rope = rope_tabs is not None
    stacked = not rope
    stacked_first = stacked and kv_prev is None
    grid = (bsz // nb, seq // tl)

    def per_layer(shape):
        return pl.BlockSpec((None,) + shape, lambda b, t: (layer,) + (0,) * len(shape))

    mod_idx = (lambda b, t: (layer, b, 0, 0)) if per_batch_mod else (lambda b, t: (layer, 0, 0, 0))
    in_specs = [
        pl.BlockSpec((nb, tl, D_MODEL), lambda b, t: (b, t, 0)),
        pl.BlockSpec((None, None, 1, 3 * D_MODEL), mod_idx),
        per_layer((1, D_MODEL)),
        per_layer((IN_WIDTH, D_MODEL)),
        per_layer((1, A_WIDTH)),
        per_layer((1, A_KV_WIDTH)),
        per_layer((2 * GATE_RANK, C_KW)),
        per_layer((1, C_KW)),
        per_layer((2 * GATE_RANK, C_KW)),
        per_layer((1, C_KW)),
    ]
    args = [x, mod, wts["g_pre"], wts["w_in_t"], wts["aq_gain"], wts["ak_gain"],
            wts["cw_f"], wts["cb_f"], wts["cw_b"], wts["cb_b"]]
    if rope:
        assert nb == 1
        in_specs += [
            pl.BlockSpec((tl, A_WIDTH), lambda b, t: (t, 0)),
            pl.BlockSpec((tl, A_WIDTH), lambda b, t: (t, 0)),
            pl.BlockSpec((tl, B_WIDTH), lambda b, t: (t, 0)),
            pl.BlockSpec((tl, B_WIDTH), lambda b, t: (t, 0)),
        ]
        args += list(rope_tabs)

    def heads(n):
        return pl.BlockSpec((nb, n, tl, HEAD_DIM), lambda b, t: (b, 0, t, 0))

    def heads_t(n):
        if stacked_first:
            return pl.BlockSpec((nb, DEPTH, n, HEAD_DIM, tl), lambda b, t: (b, 0, 0, 0, t))
        if stacked:
            return pl.BlockSpec((nb, None, n, HEAD_DIM, tl), lambda b, t: (b, layer, 0, 0, t))
        return pl.BlockSpec((nb, n, HEAD_DIM, tl), lambda b, t: (b, 0, 0, t))

    def rows(width):
        return pl.BlockSpec((nb, tl, width), lambda b, t: (b, t, 0))

    def hshape(n):
        return jax.ShapeDtypeStruct((bsz, n, seq, HEAD_DIM), BF16)

    def tshape(n):
        if stacked:
            return jax.ShapeDtypeStruct((bsz, DEPTH, n, HEAD_DIM, seq), F32)
        return jax.ShapeDtypeStruct((bsz, n, HEAD_DIM, seq), BF16)

    def rshape(width):
        return jax.ShapeDtypeStruct((bsz, seq, width), F32)

    out_specs = [heads(A_HEADS), heads(A_KV_HEADS), heads(B_HEADS), heads(B_HEADS),
                 rows(C_KW), rows(C_KW), rows(C_WIDTH), rows(C_KW), rows(C_KW), rows(D_MIX),
                 heads_t(A_KV_HEADS), heads_t(B_HEADS)]
    out_shape = [hshape(A_HEADS), hshape(A_KV_HEADS), hshape(B_HEADS), hshape(B_HEADS),
                 rshape(C_KW), rshape(C_KW), rshape(C_WIDTH), rshape(C_KW), rshape(C_KW),
                 rshape(D_MIX), tshape(A_KV_HEADS), tshape(B_HEADS)]
    if stacked:
        out_specs += [heads_t(A_KV_HEADS), heads_t(B_HEADS)]
        out_shape += [tshape(A_KV_HEADS), tshape(B_HEADS)]
    else:
        out_specs += [pl.BlockSpec((nb, QSTAT_ROWS, tl), lambda b, t: (b, 0, t))]
        out_shape += [jax.ShapeDtypeStruct((bsz, QSTAT_ROWS, seq), F32)]
    aliases = {}
    if kv_prev is not None:
        for j, buf in enumerate(kv_prev):
            aliases[len(args)] = 10 + j
            in_specs.append(pl.BlockSpec(memory_space=pl.ANY))
            args.append(buf)
    return pl.pallas_call(
        functools.partial(_in_kernel, rope, layer, stacked_first, nb, tl, len(aliases)),
        grid=grid,
        in_specs=in_specs,
        out_specs=out_specs,
        out_shape=out_shape,
        input_output_aliases=aliases,
        compiler_params=pltpu.CompilerParams(
            dimension_semantics=("arbitrary", "arbitrary"), vmem_limit_bytes=VMEM_LIMIT),
        name="in_projection_rope" if rope else "in_projection",
    )(*args)


def _gla_bidirectional(cq_ref, ck_ref, cv_ref, gf_ref, gb_ref, s_f, s_b, seq, oc_ref):
    bl = GLA_BLOCK
    n_sub = bl // CHUNK
    nblk = seq // bl
    ri = _iota((bl, bl), 0)
    ci = _iota((bl, bl), 1)
    same_chunk = lax.shift_right_logical(ri, 6) == lax.shift_right_logical(ci, 6)
    bd = (lax.shift_right_logical(_iota((C_WIDTH, C_KW), 0), 6)
          == lax.shift_right_logical(_iota((C_WIDTH, C_KW), 1), 5))
    khead = lax.shift_right_logical(_iota((1, C_KW), 1), 5)
    vhead = lax.shift_right_logical(_iota((1, C_WIDTH), 1), 6)
    scans = []
    for reverse, g_ref in ((False, gf_ref), (True, gb_ref)):
        causal = same_chunk & ((ci >= ri) if reverse else (ci <= ri))
        scans.append((reverse, g_ref, causal, jnp.where(causal, 1.0, 0.0).astype(BF16)))
    states = [s_f, s_b]
    written = set()

    for step in range(nblk):
        rows0 = [step * bl, (nblk - 1 - step) * bl]
        cums = []
        for (reverse, g_ref, causal, tri), r0 in zip(scans, rows0):
            g_hi, g_lo = _split_bf16(g_ref[r0:r0 + bl, :])
            cums.append(_dot(tri, g_hi) + _dot(tri, g_lo))
        prep = []
        for (reverse, g_ref, causal, tri), r0, cum in zip(scans, rows0, cums):
            q = cq_ref[r0:r0 + bl, :]
            k = ck_ref[r0:r0 + bl, :]
            v = cv_ref[r0:r0 + bl, :]
            qt = q * jnp.exp(cum)
            ktb = (k * jnp.exp(-cum)).astype(BF16)
            vb = v.astype(BF16)
            lasts, kdecs = [], []
            for c in range(n_sub):
                c0 = CHUNK * c
                edge = c0 if reverse else c0 + CHUNK - 1
                last = cum[edge:edge + 1, :]
                lasts.append(last)
                kdecs.append((k[c0:c0 + CHUNK] * jnp.exp(last - cum[c0:c0 + CHUNK])).astype(BF16))
            prep.append((qt, ktb, v, vb, lasts, kdecs))
        scores, incs = [], []
        for qt, ktb, v, vb, lasts, kdecs in prep:
            scores.append([_dot_nt(jnp.where(khead == hd, qt, 0.0).astype(BF16), ktb)
                           for hd in range(C_HEADS)])
            incs.append([_dot_tn(vb[CHUNK * c:CHUNK * (c + 1)], kdecs[c]) for c in range(n_sub)])
        probs, entering = [], []
        for si, ((reverse, g_ref, causal, tri), (qt, ktb, v, vb, lasts, kdecs)) in enumerate(
                zip(scans, prep)):
            probs.append([jnp.where(causal, s, 0.0).astype(BF16) for s in scores[si]])
            s_t = states[si]
            before = [None] * n_sub
            subs = range(n_sub)
            for c in (reversed(subs) if reverse else subs):
                before[c] = s_t.astype(BF16)
                s_t = jnp.exp(lasts[c]) * s_t + jnp.where(bd, incs[si][c], 0.0)
            states[si] = s_t
            entering.append(before)
        outs = []
        for si, (qt, ktb, v, vb, lasts, kdecs) in enumerate(prep):
            qtb = qt.astype(BF16)
            o = jnp.concatenate([_dot_nt(qtb[CHUNK * c:CHUNK * (c + 1)], entering[si][c])
                                 for c in range(n_sub)], axis=0)
            for hd in range(C_HEADS):
                o = o + _dot(probs[si][hd], jnp.where(vhead == hd, v, 0.0).astype(BF16))
            outs.append(o)
        if rows0[0] == rows0[1]:
            outs, rows0 = [outs[0] + outs[1]], rows0[:1]
        for o, r0 in zip(outs, rows0):
            if r0 in written:
                oc_ref[r0:r0 + bl, :] = oc_ref[r0:r0 + bl, :] + o
            else:
                oc_ref[r0:r0 + bl, :] = o
                written.add(r0)
    return states[0], states[1]


def _attend_t(jobs, shifts=None):
    def scores(i):
        return _dot_nt(jobs[i][0][...], jobs[i][1])

    outs = []
    pending = [scores(i) for i in range(min(SCORE_LOOKAHEAD, len(jobs)))]
    for i, (_, _, vt) in enumerate(jobs):
        st = pending.pop(0)
        if i + SCORE_LOOKAHEAD < len(jobs):
            pending.append(scores(i + SCORE_LOOKAHEAD))
        m = jnp.max(st, axis=0, keepdims=True) if shifts is None else shifts[i]
        p = jnp.exp2(st - m).astype(BF16)
        ot = _dot(vt[...], p)
        outs.append(ot[0:HEAD_DIM] * (1.0 / ot[HEAD_DIM:HEAD_DIM + 1]))
    return outs


def _pair_rows(a, b):
    return jnp.concatenate([a, b], axis=0).T


def _state_to_blockdiag_t(s_ref):
    rows = []
    for hd in range(C_HEADS):
        pieces = []
        if hd:
            pieces.append(jnp.zeros((C_DK, C_DV * hd), F32))
        pieces.append(s_ref[hd])
        if hd < C_HEADS - 1:
            pieces.append(jnp.zeros((C_DK, C_DV * (C_HEADS - 1 - hd)), F32))
        rows.append(jnp.concatenate(pieces, axis=-1))
    return jnp.concatenate(rows, axis=0).T


def _blockdiag_t_to_state(s_t, out_ref):
    s = s_t.T
    for hd in range(C_HEADS):
        out_ref[hd] = s[C_DK * hd:C_DK * (hd + 1), C_DV * hd:C_DV * (hd + 1)]


def _mix_kernel(cached, layer, first_state, n_aliased, lam_init, seq, qt, qb, *refs):
    it = iter(refs)
    x_ref, mod_ref = next(it), next(it)
    qa_ref, ka_ref, qb_ref, kb_ref = (next(it) for _ in range(4))
    cq_ref, ck_ref, cv_ref, gf_ref, gb_ref, su_ref = (next(it) for _ in range(6))
    vat_ref, vbt_ref = next(it), next(it)
    if cached:
        cakt_ref, cavt_ref, cbkt_ref, cbvt_ref, s0f_ref, s0b_ref = (next(it) for _ in range(6))
        qstat_ref, aqg_ref = next(it), next(it)
    wout_ref, gpost_ref, bog_ref, cog_ref, lamp_ref = (next(it) for _ in range(5))
    for _ in range(n_aliased):
        next(it)
    y_ref = next(it)
    if not cached:
        sf_ref, sb_ref = next(it), next(it)
    kA_s, vtA_s, kB_s, vtB_s, oc_s, mixed_s, kn2_s = (next(it) for _ in range(7))

    lk = kA_s.shape[1]
    past = lk - seq
    t = pl.program_id(1)

    @pl.when(t == 0)
    def _per_sequence():
        ones_row = jnp.where(_iota((VT_ROWS - HEAD_DIM, lk), 0) == 0, 1.0, 0.0).astype(BF16)
        for k_new, kt_cache, k_dst, vt_new, vt_cache, vt_dst in (
                (ka_ref, cakt_ref if cached else None, kA_s, vat_ref, cavt_ref if cached else None, vtA_s),
                (kb_ref, cbkt_ref if cached else None, kB_s, vbt_ref, cbvt_ref if cached else None, vtB_s)):
            n_heads = k_dst.shape[0]
            if cached:
                for h0 in range(0, n_heads, 2):
                    pair = jnp.concatenate([kt_cache[h0], kt_cache[h0 + 1]], axis=0).T
                    k_dst[h0, 0:past, :] = pair[:, 0:HEAD_DIM].astype(BF16)
                    k_dst[h0 + 1, 0:past, :] = pair[:, HEAD_DIM:2 * HEAD_DIM].astype(BF16)
            for hd in range(n_heads):
                k_dst[hd, past:lk, :] = k_new[hd]
                if cached:
                    vt_dst[hd, 0:HEAD_DIM, 0:past] = vt_cache[hd].astype(BF16)
                vt_dst[hd, 0:HEAD_DIM, past:lk] = vt_new[hd].astype(BF16)
                vt_dst[hd, HEAD_DIM:VT_ROWS, :] = ones_row
        bounded_jobs = ([(kA_s, g) for g in range(A_KV_HEADS)] + [(kB_s, h) for h in range(B_HEADS)]
                        if lk > KEY_TILE else [])
        for j, (k_dst, hd) in enumerate(bounded_jobs):
            kf = k_dst[hd].astype(F32)
            kn2 = jnp.max(jnp.sum(kf * kf, axis=-1, keepdims=True), axis=0, keepdims=True)
            kn2_s[j:j + 1, :] = jnp.broadcast_to(kn2, (1, LANES))

        if cached:
            s0f, s0b = _state_to_blockdiag_t(s0f_ref), _state_to_blockdiag_t(s0b_ref)
        else:
            s0f = jnp.zeros((C_WIDTH, C_KW), F32)
            s0b = s0f
        s_f, s_b = _gla_bidirectional(cq_ref, ck_ref, cv_ref, gf_ref, gb_ref, s0f, s0b, seq, oc_s)
        if not cached:
            for ref, s_t in ((sf_ref, s_f), (sb_ref, s_b)):
                if first_state:
                    for l2 in range(DEPTH):
                        if l2 == layer:
                            _blockdiag_t_to_state(s_t, ref.at[l2])
                        else:
                            ref[l2] = jnp.zeros(ref.shape[1:], F32)
                else:
                    _blockdiag_t_to_state(s_t, ref)
        for r0 in range(0, seq, GLA_BLOCK):
            oc = oc_s[r0:r0 + GLA_BLOCK, :]
            oc_s[r0:r0 + GLA_BLOCK, :] = oc * lax.rsqrt(_group_mean_sq(oc, 6) + EPS) * cog_ref[...]

    lam = (jnp.exp(jnp.sum(lamp_ref[0:1, :] * lamp_ref[1:2, :], axis=-1, keepdims=True))
           - jnp.exp(jnp.sum(lamp_ref[2:3, :] * lamp_ref[3:4, :], axis=-1, keepdims=True))
           + lam_init)

    def project_out():
        seq_rows = pl.ds(pl.multiple_of(t * qt, qt), qt)
        mixed_s[:, A_WIDTH + B_WIDTH:D_MIX] = oc_s[seq_rows, :]
        gate = mod_ref[:, 2 * D_MODEL:3 * D_MODEL]
        mixed = (mixed_s[...] * su_ref[...]).astype(BF16)
        y = _dot(mixed, wout_ref[...])
        yn = y * lax.rsqrt(jnp.mean(y * y, axis=-1, keepdims=True) + EPS) * gpost_ref[...]
        y_ref[...] = x_ref[...] + gate * yn

    def attn_block(i, carry):
        rows = pl.ds(i * qb, qb) if isinstance(i, int) else pl.ds(pl.multiple_of(i * qb, qb), qb)
        jobs = []
        for grp in range(A_KV_HEADS):
            q4 = qa_ref[A_GROUP * grp:A_GROUP * (grp + 1), rows, :].reshape(A_GROUP * qb, HEAD_DIM)
            jobs.append((kA_s.at[grp], q4, vtA_s.at[grp]))
        lane = _iota((qb, HEAD_DIM), 1)
        for hd in range(B_HEADS):
            q = qb_ref[hd, rows, :]
            zero = jnp.zeros_like(q)
            q2 = jnp.concatenate([jnp.where(lane < B_QK_DIM, q, zero),
                                  jnp.where(lane >= B_QK_DIM, q, zero)], axis=0)
            jobs.append((kB_s.at[hd], q2, vtB_s.at[hd]))

        def finish(outs):
            for grp in range(A_KV_HEADS):
                ot = outs[grp]
                for pair in range(A_GROUP // 2):
                    c0 = 2 * pair * qb
                    col = A_GROUP * HEAD_DIM * grp + 2 * HEAD_DIM * pair
                    mixed_s[rows, col:col + 2 * HEAD_DIM] = _pair_rows(ot[:, c0:c0 + qb],
                                                                       ot[:, c0 + qb:c0 + 2 * qb])
            obs = []
            for hd in range(B_HEADS):
                ot = outs[A_KV_HEADS + hd]
                ob = ot[:, 0:qb] - lam * ot[:, qb:2 * qb]
                obs.append(ob * lax.rsqrt(jnp.mean(ob * ob, axis=0, keepdims=True) + EPS))
            for pair in range(B_HEADS // 2):
                col = A_WIDTH + 2 * B_V_DIM * pair
                mixed_s[rows, col:col + 2 * B_V_DIM] = (_pair_rows(obs[2 * pair], obs[2 * pair + 1])
                                                        * bog_ref[...] * (1.0 - lam_init))
            if qt == qb:
                project_out()

        if lk <= KEY_TILE:
            finish(_attend_t(jobs))
            return carry

        qa_norm = (jnp.max(jnp.abs(aqg_ref[...]), axis=-1, keepdims=True)
                   * (HEAD_DIM ** 0.5 * HEAD_DIM ** -0.5 * LOG2E))
        stat_t = qstat_ref[...]
        shifts, gaps = [], []
        for j, (_, q, _) in enumerate(jobs):
            k_norm = jnp.sqrt(kn2_s[j:j + 1, 0:1])
            if j < A_KV_HEADS:
                upper = qa_norm * k_norm * BOUND_SLACK
                shifts.append(jnp.broadcast_to(upper, (1, q.shape[0])))
                gaps.append(2.0 * upper)
            else:
                r0 = 2 * (j - A_KV_HEADS)
                r1 = r0 + 2 * B_HEADS
                qn2 = jnp.concatenate([stat_t[r0:r0 + 1], stat_t[r0 + 1:r0 + 2]], axis=-1)
                lower = jnp.concatenate([stat_t[r1:r1 + 1], stat_t[r1 + 1:r1 + 2]], axis=-1)
                upper = jnp.sqrt(qn2) * k_norm * BOUND_SLACK
                shifts.append(upper)
                gaps.append(jnp.max(upper - lower, axis=-1, keepdims=True))
        worst = functools.reduce(jnp.maximum, gaps)
        safe = worst[0, 0] <= SAFE_GAP

        @pl.when(safe)
        def _bounded():
            finish(_attend_t(jobs, shifts))

        @pl.when(jnp.logical_not(safe))
        def _exact_max():
            finish(_attend_t(jobs))

        return carry

    if qt != qb:
        lax.fori_loop(0, qt // qb, attn_block, 0)
        project_out()
    else:
        attn_block(0, 0)


def _mixer(x, mod, per_batch_mod, layer, proj, wts, lam_init, cache, state_prev=None):
    bsz, seq, _ = x.shape
    cached = cache is not None
    past = cache[0].shape[4] if cached else 0
    qa, ka, qb, kb, cq, ck, cv, gf, gb, su, vat, vbt = proj[:12]
    qt = 256
    qb_rows = 256

    def per_layer(shape):
        return pl.BlockSpec((None,) + shape, lambda b, t: (layer,) + (0,) * len(shape))

    def layer_heads_t(n, length):
        return pl.BlockSpec((None, None, n, HEAD_DIM, length), lambda b, t: (b, layer, 0, 0, 0))

    def heads_t(n):
        if cached:
            return pl.BlockSpec((None, n, HEAD_DIM, seq), lambda b, t: (b, 0, 0, 0))
        return layer_heads_t(n, seq)

    def heads(n):
        return pl.BlockSpec((None, n, seq, HEAD_DIM), lambda b, t: (b, 0, 0, 0))

    def head_tile(n):
        return pl.BlockSpec((None, n, qt, HEAD_DIM), lambda b, t: (b, 0, t, 0))

    def rows(width):
        return pl.BlockSpec((None, seq, width), lambda b, t: (b, 0, 0))

    def row_tile(width):
        return pl.BlockSpec((None, qt, width), lambda b, t: (b, t, 0))

    mod_idx = (lambda b, t: (layer, b, 0, 0)) if per_batch_mod else (lambda b, t: (layer, 0, 0, 0))
    in_specs = [row_tile(D_MODEL), pl.BlockSpec((None, None, 1, 3 * D_MODEL), mod_idx),
                head_tile(A_HEADS), heads(A_KV_HEADS), head_tile(B_HEADS), heads(B_HEADS),
                rows(C_KW), rows(C_KW), rows(C_WIDTH), rows(C_KW), rows(C_KW), row_tile(D_MIX),
                heads_t(A_KV_HEADS), heads_t(B_HEADS)]
    args = [x, mod, qa, ka, qb, kb, cq, ck, cv, gf, gb, su, vat, vbt]
    if cached:
        state_in = pl.BlockSpec((None, None, C_HEADS, C_DK, C_DV), lambda b, t: (b, layer, 0, 0, 0))
        in_specs += [layer_heads_t(A_KV_HEADS, past), layer_heads_t(A_KV_HEADS, past),
                     layer_heads_t(B_HEADS, past), layer_heads_t(B_HEADS, past), state_in, state_in,
                     pl.BlockSpec((None, QSTAT_ROWS, qt), lambda b, t: (b, 0, t)),
                     per_layer((1, A_WIDTH))]
        args += list(cache) + [proj[12], wts["aq_gain"]]
    in_specs += [per_layer((D_MIX, D_MODEL)), per_layer((1, D_MODEL)), per_layer((1, 2 * B_V_DIM)),
                 per_layer((1, C_WIDTH)), per_layer((4, B_QK_DIM))]
    args += [wts["w_out"], wts["g_post"], wts["b_out_gain"], wts["c_out_gain"], wts["lam_params"]]

    out_specs = [row_tile(D_MODEL)]
    out_shape = [jax.ShapeDtypeStruct((bsz, seq, D_MODEL), F32)]
    aliases = {}
    first_state = not cached and state_prev is None
    if not cached:
        if first_state:
            state_out = pl.BlockSpec((None, DEPTH, C_HEADS, C_DK, C_DV), lambda b, t: (b, 0, 0, 0, 0))
        else:
            state_out = pl.BlockSpec((None, None, C_HEADS, C_DK, C_DV),
                                     lambda b, t: (b, layer, 0, 0, 0))
            for j, buf in enumerate(state_prev):
                aliases[len(args)] = 1 + j
                in_specs.append(pl.BlockSpec(memory_space=pl.ANY))
                args.append(buf)
        out_specs += [state_out, state_out]
        out_shape += [jax.ShapeDtypeStruct((bsz, DEPTH, C_HEADS, C_DK, C_DV), F32)] * 2

    lk = past + seq
    scratch = [pltpu.VMEM((A_KV_HEADS, lk, HEAD_DIM), BF16), pltpu.VMEM((A_KV_HEADS, VT_ROWS, lk), BF16),
               pltpu.VMEM((B_HEADS, lk, HEAD_DIM), BF16), pltpu.VMEM((B_HEADS, VT_ROWS, lk), BF16),
               pltpu.VMEM((seq, C_WIDTH), F32), pltpu.VMEM((qt, D_MIX), F32),
               pltpu.VMEM((8, LANES), F32)]
    return pl.pallas_call(
        functools.partial(_mix_kernel, cached, layer, first_state, len(aliases), lam_init, seq, qt,
                          qb_rows),
        grid=(bsz, seq // qt),
        in_specs=in_specs,
        out_specs=out_specs,
        out_shape=out_shape,
        scratch_shapes=scratch,
        input_output_aliases=aliases,
        compiler_params=pltpu.CompilerParams(
            dimension_semantics=("arbitrary", "arbitrary"), vmem_limit_bytes=VMEM_LIMIT),
        name="mixer_cached" if cached else "mixer",
    )(*args)


def _rope_tables(seq):
    t = jnp.arange(seq)
    pos_row = (t // GRID_W).astype(F32)
    pos_col = (t % GRID_W).astype(F32)

    def tables(half, width):
        freq = ROPE_THETA ** (-jnp.arange(half, dtype=F32) / half)
        ang_r = pos_row[:, None] * freq[None, :]
        ang_c = pos_col[:, None] * freq[None, :]
        cos = jnp.concatenate([jnp.cos(ang_r), jnp.cos(ang_r), jnp.cos(ang_c), jnp.cos(ang_c)], axis=-1)
        sin = jnp.concatenate([-jnp.sin(ang_r), jnp.sin(ang_r), -jnp.sin(ang_c), jnp.sin(ang_c)], axis=-1)
        reps = width // (4 * half)
        return jnp.tile(cos, (1, reps)), jnp.tile(sin, (1, reps))

    cos_a, sin_a = tables(HEAD_DIM // 4, A_WIDTH)
    cos_b, sin_b = tables(B_QK_DIM // 4, B_WIDTH)
    return cos_a, sin_a, cos_b, sin_b


def _prepare_weights(g_pre, g_post, w_in, w_out, a_q_gain, a_k_gain, b_lambda_q1, b_lambda_k1,
                     b_lambda_q2, b_lambda_k2, b_out_gain, c_gate_w_fwd, c_gate_b_fwd, c_gate_w_bwd,
                     c_gate_b_bwd, c_out_gain):
    w_in_t = jnp.swapaxes(w_in, 1, 2).astype(BF16)
    pad = jnp.zeros((DEPTH, GATE_RANK, C_KW), F32)
    cw_f = jnp.concatenate([c_gate_w_fwd, pad], axis=1).astype(BF16)
    cw_b = jnp.concatenate([pad, c_gate_w_bwd], axis=1).astype(BF16)
    return {
        "g_pre": g_pre[:, None, :],
        "g_post": g_post[:, None, :],
        "w_in_t": w_in_t,
        "w_out": w_out.astype(BF16),
        "aq_gain": jnp.tile(a_q_gain, (1, A_HEADS))[:, None, :],
        "ak_gain": jnp.tile(a_k_gain, (1, A_KV_HEADS))[:, None, :],
        "cw_f": cw_f,
        "cb_f": c_gate_b_fwd[:, None, :],
        "cw_b": cw_b,
        "cb_b": c_gate_b_bwd[:, None, :],
        "b_out_gain": jnp.tile(b_out_gain, (1, 2))[:, None, :],
        "c_out_gain": jnp.tile(c_out_gain, (1, C_HEADS))[:, None, :],
        "lam_params": jnp.stack([b_lambda_q1, b_lambda_k1, b_lambda_q2, b_lambda_k2], axis=1),
    }


def kernel(x_prompt, x_sample, c, cache_a_k, cache_a_v, cache_b_k, cache_b_v, state_c_fwd, state_c_bwd, c_ctx, w_mod, b_mod, g_pre, g_post, w_in, w_out, a_q_gain, a_k_gain, b_lambda_q1, b_lambda_k1, b_lambda_q2, b_lambda_k2, b_out_gain, c_gate_w_fwd, c_gate_b_fwd, c_gate_w_bwd, c_gate_b_bwd, c_out_gain):
    dec_batch = x_sample.shape[0]
    dec_seq = x_sample.shape[1]

    mod_rows = 16
    cvec = jnp.zeros((mod_rows, D_MODEL), F32).at[0:dec_batch].set(c).at[dec_batch].set(c_ctx)
    mod = _modulation(cvec, w_mod, b_mod)[:, :, None, :]
    mod_lat = mod[:, 0:dec_batch]
    mod_ctx = mod[:, dec_batch:dec_batch + 1]

    wts = _prepare_weights(g_pre, g_post, w_in, w_out, a_q_gain, a_k_gain, b_lambda_q1, b_lambda_k1,
                           b_lambda_q2, b_lambda_k2, b_out_gain, c_gate_w_fwd, c_gate_b_fwd,
                           c_gate_w_bwd, c_gate_b_bwd, c_out_gain)
    rope_tabs = _rope_tables(dec_seq)
    cache = tuple(jnp.swapaxes(a, -1, -2) for a in (cache_a_k, cache_a_v, cache_b_k, cache_b_v))
    cache += (state_c_fwd, state_c_bwd)

    y_p, y_s = x_prompt, x_sample
    kv_ctx = None
    states = None
    for l in range(DEPTH):
        lam_init = 0.8 - 0.6 * math.exp(-0.3 * l)
        proj_p = _in_projection(y_p, mod_ctx, False, l, wts, None, 2, x_prompt.shape[1], kv_ctx)
        kv_ctx = proj_p[10:14]
        y_p, *states = _mixer(y_p, mod_ctx, False, l, proj_p, wts, lam_init, None, states)

        proj_s = _in_projection(y_s, mod_lat, True, l, wts, rope_tabs, 1, 512)
        (y_s,) = _mixer(y_s, mod_lat, True, l, proj_s, wts, lam_init, cache)

    va_t, vb_t, ka_t, kb_t = kv_ctx
    new_kv = [jnp.swapaxes(a, -1, -2) for a in (ka_t, va_t, kb_t, vb_t)]
    return (y_p, y_s, *new_kv, *states)
```

```python
import functools
import math

import jax
import jax.numpy as jnp
from jax import lax
from jax.experimental import pallas as pl
from jax.experimental.pallas import tpu as pltpu

F32 = jnp.float32
BF16 = jnp.bfloat16

D_MODEL = 1024
DEPTH = 2
GRID_W = 64
HEAD_DIM = 64
A_HEADS = 8
A_KV_HEADS = 2
A_GROUP = A_HEADS // A_KV_HEADS
A_WIDTH = A_HEADS * HEAD_DIM
A_KV_WIDTH = A_KV_HEADS * HEAD_DIM
B_HEADS = 4
B_QK_DIM = 32
B_V_DIM = 64
B_WIDTH = B_HEADS * B_V_DIM
C_HEADS = 4
C_DK = 32
C_DV = 64
C_KW = C_HEADS * C_DK
C_WIDTH = C_HEADS * C_DV
GATE_RANK = 16
GLA_TAU = 16.0
CHUNK = 64
D_MIX = A_WIDTH + B_WIDTH + C_WIDTH
ROPE_THETA = 10000.0
EPS = 1e-6

LANES = 128
GLA_BLOCK = 256
VT_ROWS = HEAD_DIM + 16
SCORE_LOOKAHEAD = 3
SAFE_GAP = 96.0
BOUND_SLACK = 1.02
QSTAT_ROWS = 4 * B_HEADS
U_TILE = D_MIX // 4
KEY_TILE = 256
LOG2E = math.log2(math.e)

OFF_AQ = 0
OFF_AK = OFF_AQ + A_WIDTH
OFF_AV = OFF_AK + A_KV_WIDTH
OFF_BQ = OFF_AV + A_KV_WIDTH
OFF_BK = OFF_BQ + B_WIDTH
OFF_BV = OFF_BK + B_WIDTH
OFF_CQ = OFF_BV + B_WIDTH
OFF_CK = OFF_CQ + C_KW
OFF_CV = OFF_CK + C_KW
OFF_LR = OFF_CV + C_WIDTH
OFF_U = OFF_LR + 2 * GATE_RANK
IN_WIDTH = OFF_U + D_MIX

VMEM_LIMIT = 56 * 1024 * 1024


def _dot(a, b):
    return jnp.dot(a, b, preferred_element_type=F32)


def _dot_nt(a, b):
    return lax.dot_general(a, b, (((1,), (1,)), ((), ())), preferred_element_type=F32)


def _dot_tn(a, b):
    return lax.dot_general(a, b, (((0,), (0,)), ((), ())), preferred_element_type=F32)


def _split_bf16(x):
    hi = x.astype(BF16)
    lo = (x - hi.astype(F32)).astype(BF16)
    return hi, lo


def _iota(shape, dim):
    return lax.broadcasted_iota(jnp.int32, shape, dim)


def _group_mean_sq(x, group_log2, split=True):
    width = x.shape[-1]
    r = lax.shift_right_logical(_iota((LANES, LANES), 0), group_log2)
    c = lax.shift_right_logical(_iota((LANES, LANES), 1), group_log2)
    ones = jnp.where(r == c, 1.0, 0.0).astype(BF16)
    if split:
        hi, lo = _split_bf16(x * x)
    else:
        hi, lo = (x * x).astype(BF16), None
    cols = []
    for j in range(width // LANES):
        sl = slice(LANES * j, LANES * (j + 1))
        cols.append(_dot(hi[:, sl], ones) + (_dot(lo[:, sl], ones) if split else 0.0))
    ss = cols[0] if len(cols) == 1 else jnp.concatenate(cols, axis=-1)
    return ss * (1.0 / (1 << group_log2))


def _rope(x, cos, sin_signed, dist):
    width = x.shape[-1]
    lane = _iota(x.shape, 1)
    first = (lane & (2 * dist - 1)) < dist
    up = pltpu.roll(x, width - dist, 1)
    down = pltpu.roll(x, dist, 1)
    return x * cos + jnp.where(first, up, down) * sin_signed


def _log_sigmoid(x):
    return jnp.minimum(x, 0.0) - jnp.log1p(jnp.exp(-jnp.abs(x)))


def _silu(x):
    return x * (1.0 / (1.0 + jnp.exp(-x)))


def _mod_kernel(c_ref, w_ref, b_ref, o_ref):
    a = _silu(c_ref[...]).astype(BF16)
    o_ref[...] = _dot(a, w_ref[...].astype(BF16)) + b_ref[...]


def _modulation(cvec, w_mod, b_mod):
    rows = cvec.shape[0]
    nblk = 3
    return pl.pallas_call(
        _mod_kernel,
        grid=(DEPTH, nblk),
        in_specs=[
            pl.BlockSpec((rows, D_MODEL), lambda l, n: (0, 0)),
            pl.BlockSpec((None, D_MODEL, D_MODEL), lambda l, n: (l, 0, n)),
            pl.BlockSpec((None, 1, D_MODEL), lambda l, n: (l, 0, n)),
        ],
        out_specs=pl.BlockSpec((None, rows, D_MODEL), lambda l, n: (l, 0, n)),
        out_shape=jax.ShapeDtypeStruct((DEPTH, rows, 3 * D_MODEL), F32),
        compiler_params=pltpu.CompilerParams(
            dimension_semantics=("arbitrary", "arbitrary"), vmem_limit_bytes=VMEM_LIMIT),
        name="modulation",
    )(cvec, w_mod, b_mod.reshape(DEPTH, 1, 3 * D_MODEL))


def _in_kernel(rope, layer, stacked_first, nb, tl, n_aliased, *refs):
    (x_ref, mod_ref, gpre_ref, wt_ref, aqg_ref, akg_ref, cwf_ref, cbf_ref, cwb_ref, cbb_ref) = refs[:10]
    refs = refs[10:]
    if rope:
        cosa_ref, sina_ref, cosb_ref, sinb_ref = refs[:4]
        refs = refs[4:]
    refs = refs[n_aliased:]
    (qa_ref, ka_ref, qb_ref, kb_ref, cq_ref, ck_ref, cv_ref, gf_ref, gb_ref, su_ref,
     vat_ref, vbt_ref) = refs[:12]
    kat_ref, kbt_ref = (None, None) if rope else refs[12:]
    qstat_ref = refs[12] if rope else None

    x = x_ref[...].reshape(nb * tl, D_MODEL)
    shift = mod_ref[:, 0:D_MODEL]
    scale = mod_ref[:, D_MODEL:2 * D_MODEL]
    ms = jnp.mean(x * x, axis=-1, keepdims=True)
    h = (x * lax.rsqrt(ms + EPS)) * gpre_ref[...] * (1.0 + scale) + shift
    hb = h.astype(BF16)

    def proj(off, width):
        return _dot_nt(hb, wt_ref[off:off + width, :])

    def put_rows(ref, val):
        for bi in range(nb):
            ref[bi] = val[bi * tl:(bi + 1) * tl].astype(ref.dtype)

    def put_heads(ref, val, n_heads):
        for bi in range(nb):
            for hd in range(n_heads):
                ref[bi, hd] = val[bi * tl:(bi + 1) * tl,
                                  HEAD_DIM * hd:HEAD_DIM * (hd + 1)].astype(ref.dtype)

    def put_heads_t(ref, val, n_heads):
        val_t = val.T
        for bi in range(nb):
            for hd in range(n_heads):
                blk = val_t[HEAD_DIM * hd:HEAD_DIM * (hd + 1), bi * tl:(bi + 1) * tl].astype(ref.dtype)
                if stacked_first:
                    for l2 in range(DEPTH):
                        ref[bi, l2, hd] = blk if l2 == layer else jnp.zeros_like(blk)
                else:
                    ref[bi, hd] = blk

    def gate_tile(j):
        c0 = U_TILE * j
        val = _silu(proj(OFF_U + c0, U_TILE))
        for bi in range(nb):
            su_ref[bi, :, c0:c0 + U_TILE] = val[bi * tl:(bi + 1) * tl]

    aq = proj(OFF_AQ, A_WIDTH)
    akv = proj(OFF_AK, 2 * A_KV_WIDTH)
    bq = proj(OFF_BQ, B_WIDTH)
    bk = proj(OFF_BK, B_WIDTH)

    aq = aq * lax.rsqrt(_group_mean_sq(aq, 6, split=False) + EPS) * aqg_ref[...]
    ak = akv[:, 0:A_KV_WIDTH]
    ak = ak * lax.rsqrt(_group_mean_sq(ak, 6, split=False) + EPS) * akg_ref[...]
    gate_tile(0)
    bv = proj(OFF_BV, B_WIDTH)
    if rope:
        aq = _rope(aq, cosa_ref[...], sina_ref[...], 16)
        ak = _rope(ak, cosa_ref[:, 0:A_KV_WIDTH], sina_ref[:, 0:A_KV_WIDTH], 16)
    put_heads(qa_ref, aq * (HEAD_DIM ** -0.5 * LOG2E), A_HEADS)
    put_heads(ka_ref, ak, A_KV_HEADS)
    put_heads_t(vat_ref, akv[:, A_KV_WIDTH:2 * A_KV_WIDTH], A_KV_HEADS)
    if kat_ref is not None:
        put_heads_t(kat_ref, ak, A_KV_HEADS)
    gate_tile(1)
    cqk = proj(OFF_CQ, 2 * C_KW)
    cv = proj(OFF_CV, C_WIDTH)
    lr = proj(OFF_LR, 2 * GATE_RANK).astype(BF16)

    if rope:
        bq = _rope(bq, cosb_ref[...], sinb_ref[...], 8)
        bk = _rope(bk, cosb_ref[...], sinb_ref[...], 8)
    bq = bq * (B_QK_DIM ** -0.5 * LOG2E)
    put_heads(qb_ref, bq, B_HEADS)
    put_heads(kb_ref, bk, B_HEADS)
    if rope:
        bq_t = bq.T
        sq = bq_t * bq_t
        own = bq_t * bk.T
        halves = range(0, B_WIDTH, B_QK_DIM)
        qstat_ref[0] = jnp.concatenate(
            [jnp.sum(sq[r0:r0 + B_QK_DIM], axis=0, keepdims=True) for r0 in halves]
            + [jnp.sum(own[r0:r0 + B_QK_DIM], axis=0, keepdims=True) for r0 in halves], axis=0)
    put_heads_t(vbt_ref, bv, B_HEADS)
    if kbt_ref is not None:
        put_heads_t(kbt_ref, bk, B_HEADS)
    gate_tile(2)

    put_rows(cq_ref, cqk[:, 0:C_KW] * (C_DK ** -0.5))
    put_rows(ck_ref, cqk[:, C_KW:2 * C_KW])
    put_rows(cv_ref, cv)
    put_rows(gf_ref, _log_sigmoid(_dot(lr, cwf_ref[...]) + cbf_ref[...]) * (1.0 / GLA_TAU))
    put_rows(gb_ref, _log_sigmoid(_dot(lr, cwb_ref[...]) + cbb_ref[...]) * (1.0 / GLA_TAU))
    gate_tile(3)


def _in_projection(x, mod, per_batch_mod, layer, wts, rope_tabs, nb, tl, kv_prev=None):
    bsz, seq, _ = x.shape
    rope = rope_tabs is not None
    stacked = not rope
    stacked_first = stacked and kv_prev is None
    grid = (bsz // nb, seq // tl)

    def per_layer(shape):
        return pl.BlockSpec((None,) + shape, lambda b, t: (layer,) + (0,) * len(shape))

    mod_idx = (lambda b, t: (layer, b, 0, 0)) if per_batch_mod else (lambda b, t: (layer, 0, 0, 0))
    in_specs = [
        pl.BlockSpec((nb, tl, D_MODEL), lambda b, t: (b, t, 0)),
        pl.BlockSpec((None, None, 1, 3 * D_MODEL), mod_idx),
        per_layer((1, D_MODEL)),
        per_layer((IN_WIDTH, D_MODEL)),
        per_layer((1, A_WIDTH)),
        per_layer((1, A_KV_WIDTH)),
        per_layer((2 * GATE_RANK, C_KW)),
        per_layer((1, C_KW)),
        per_layer((2 * GATE_RANK, C_KW)),
        per_layer((1, C_KW)),
    ]
    args = [x, mod, wts["g_pre"], wts["w_in_t"], wts["aq_gain"], wts["ak_gain"],
            wts["cw_f"], wts["cb_f"], wts["cw_b"], wts["cb_b"]]
    if rope:
        assert nb == 1
        in_specs += [
            pl.BlockSpec((tl, A_WIDTH), lambda b, t: (t, 0)),
            pl.BlockSpec((tl, A_WIDTH), lambda b, t: (t, 0)),
            pl.BlockSpec((tl, B_WIDTH), lambda b, t: (t, 0)),
            pl.BlockSpec((tl, B_WIDTH), lambda b, t: (t, 0)),
        ]
        args += list(rope_tabs)

    def heads(n):
        return pl.BlockSpec((nb, n, tl, HEAD_DIM), lambda b, t: (b, 0, t, 0))

    def heads_t(n):
        if stacked_first:
            return pl.BlockSpec((nb, DEPTH, n, HEAD_DIM, tl), lambda b, t: (b, 0, 0, 0, t))
        if stacked:
            return pl.BlockSpec((nb, None, n, HEAD_DIM, tl), lambda b, t: (b, layer, 0, 0, t))
        return pl.BlockSpec((nb, n, HEAD_DIM, tl), lambda b, t: (b, 0, 0, t))

    def rows(width):
        return pl.BlockSpec((nb, tl, width), lambda b, t: (b, t, 0))

    def hshape(n):
        return jax.ShapeDtypeStruct((bsz, n, seq, HEAD_DIM), BF16)

    def tshape(n):
        if stacked:
            return jax.ShapeDtypeStruct((bsz, DEPTH, n, HEAD_DIM, seq), F32)
        return jax.ShapeDtypeStruct((bsz, n, HEAD_DIM, seq), BF16)

    def rshape(width):
        return jax.ShapeDtypeStruct((bsz, seq, width), F32)

    out_specs = [heads(A_HEADS), heads(A_KV_HEADS), heads(B_HEADS), heads(B_HEADS),
                 rows(C_KW), rows(C_KW), rows(C_WIDTH), rows(C_KW), rows(C_KW), rows(D_MIX),
                 heads_t(A_KV_HEADS), heads_t(B_HEADS)]
    out_shape = [hshape(A_HEADS), hshape(A_KV_HEADS), hshape(B_HEADS), hshape(B_HEADS),
                 rshape(C_KW), rshape(C_KW), rshape(C_WIDTH), rshape(C_KW), rshape(C_KW),
                 rshape(D_MIX), tshape(A_KV_HEADS), tshape(B_HEADS)]
    if stacked:
        out_specs += [heads_t(A_KV_HEADS), heads_t(B_HEADS)]
        out_shape += [tshape(A_KV_HEADS), tshape(B_HEADS)]
    else:
        out_specs += [pl.BlockSpec((nb, QSTAT_ROWS, tl), lambda b, t: (b, 0, t))]
        out_shape += [jax.ShapeDtypeStruct((bsz, QSTAT_ROWS, seq), F32)]
    aliases = {}
    if kv_prev is not None:
        for j, buf in enumerate(kv_prev):
            aliases[len(args)] = 10 + j
            in_specs.append(pl.BlockSpec(memory_space=pl.ANY))
            args.append(buf)
    return pl.pallas_call(
        functools.partial(_in_kernel, rope, layer, stacked_first, nb, tl, len(aliases)),
        grid=grid,
        in_specs=in_specs,
        out_specs=out_specs,
        out_shape=out_shape,
        input_output_aliases=aliases,
        compiler_params=pltpu.CompilerParams(
            dimension_semantics=("arbitrary", "arbitrary"), vmem_limit_bytes=VMEM_LIMIT),
        name="in_projection_rope" if rope else "in_projection",
    )(*args)


def _gla_bidirectional(cq_ref, ck_ref, cv_ref, gf_ref, gb_ref, s_f, s_b, seq, oc_ref):
    bl = GLA_BLOCK
    n_sub = bl // CHUNK
    nblk = seq // bl
    ri = _iota((bl, bl), 0)
    ci = _iota((bl, bl), 1)
    same_chunk = lax.shift_right_logical(ri, 6) == lax.shift_right_logical(ci, 6)
    bd = (lax.shift_right_logical(_iota((C_WIDTH, C_KW), 0), 6)
          == lax.shift_right_logical(_iota((C_WIDTH, C_KW), 1), 5))
    khead = lax.shift_right_logical(_iota((1, C_KW), 1), 5)
    vhead = lax.shift_right_logical(_iota((1, C_WIDTH), 1), 6)
    scans = []
    for reverse, g_ref in ((False, gf_ref), (True, gb_ref)):
        causal = same_chunk & ((ci >= ri) if reverse else (ci <= ri))
        scans.append((reverse, g_ref, causal, jnp.where(causal, 1.0, 0.0).astype(BF16)))
    states = [s_f, s_b]
    written = set()

    for step in range(nblk):
        rows0 = [step * bl, (nblk - 1 - step) * bl]
        cums = []
        for (reverse, g_ref, causal, tri), r0 in zip(scans, rows0):
            g_hi, g_lo = _split_bf16(g_ref[r0:r0 + bl, :])
            cums.append(_dot(tri, g_hi) + _dot(tri, g_lo))
        prep = []
        for (reverse, g_ref, causal, tri), r0, cum in zip(scans, rows0, cums):
            q = cq_ref[r0:r0 + bl, :]
            k = ck_ref[r0:r0 + bl, :]
            v = cv_ref[r0:r0 + bl, :]
            qt = q * jnp.exp(cum)
            ktb = (k * jnp.exp(-cum)).astype(BF16)
            vb = v.astype(BF16)
            lasts, kdecs = [], []
            for c in range(n_sub):
                c0 = CHUNK * c
                edge = c0 if reverse else c0 + CHUNK - 1
                last = cum[edge:edge + 1, :]
                lasts.append(last)
                kdecs.append((k[c0:c0 + CHUNK] * jnp.exp(last - cum[c0:c0 + CHUNK])).astype(BF16))
            prep.append((qt, ktb, v, vb, lasts, kdecs))
        scores, incs = [], []
        for qt, ktb, v, vb, lasts, kdecs in prep:
            scores.append([_dot_nt(jnp.where(khead == hd, qt, 0.0).astype(BF16), ktb)
                           for hd in range(C_HEADS)])
            incs.append([_dot_tn(vb[CHUNK * c:CHUNK * (c + 1)], kdecs[c]) for c in range(n_sub)])
        probs, entering = [], []
        for si, ((reverse, g_ref, causal, tri), (qt, ktb, v, vb, lasts, kdecs)) in enumerate(
                zip(scans, prep)):
            probs.append([jnp.where(causal, s, 0.0).astype(BF16) for s in scores[si]])
            s_t = states[si]
            before = [None] * n_sub
            subs = range(n_sub)
            for c in (reversed(subs) if reverse else subs):
                before[c] = s_t.astype(BF16)
                s_t = jnp.exp(lasts[c]) * s_t + jnp.where(bd, incs[si][c], 0.0)
            states[si] = s_t
            entering.append(before)
        outs = []
        for si, (qt, ktb, v, vb, lasts, kdecs) in enumerate(prep):
            qtb = qt.astype(BF16)
            o = jnp.concatenate([_dot_nt(qtb[CHUNK * c:CHUNK * (c + 1)], entering[si][c])
                                 for c in range(n_sub)], axis=0)
            for hd in range(C_HEADS):
                o = o + _dot(probs[si][hd], jnp.where(vhead == hd, v, 0.0).astype(BF16))
            outs.append(o)
        if rows0[0] == rows0[1]:
            outs, rows0 = [outs[0] + outs[1]], rows0[:1]
        for o, r0 in zip(outs, rows0):
            if r0 in written:
                oc_ref[r0:r0 + bl, :] = oc_ref[r0:r0 + bl, :] + o
            else:
                oc_ref[r0:r0 + bl, :] = o
                written.add(r0)
    return states[0], states[1]


def _attend_t(jobs, shifts=None):
    def scores(i):
        return _dot_nt(jobs[i][0][...], jobs[i][1])

    outs = []
    pending = [scores(i) for i in range(min(SCORE_LOOKAHEAD, len(jobs)))]
    for i, (_, _, vt) in enumerate(jobs):
        st = pending.pop(0)
        if i + SCORE_LOOKAHEAD < len(jobs):
            pending.append(scores(i + SCORE_LOOKAHEAD))
        m = jnp.max(st, axis=0, keepdims=True) if shifts is None else shifts[i]
        p = jnp.exp2(st - m).astype(BF16)
        ot = _dot(vt[...], p)
        outs.append(ot[0:HEAD_DIM] * (1.0 / ot[HEAD_DIM:HEAD_DIM + 1]))
    return outs


def _pair_rows(a, b):
    return jnp.concatenate([a, b], axis=0).T


def _state_to_blockdiag_t(s_ref):
    rows = []
    for hd in range(C_HEADS):
        pieces = []
        if hd:
            pieces.append(jnp.zeros((C_DK, C_DV * hd), F32))
        pieces.append(s_ref[hd])
        if hd < C_HEADS - 1:
            pieces.append(jnp.zeros((C_DK, C_DV * (C_HEADS - 1 - hd)), F32))
        rows.append(jnp.concatenate(pieces, axis=-1))
    return jnp.concatenate(rows, axis=0).T


def _blockdiag_t_to_state(s_t, out_ref):
    s = s_t.T
    for hd in range(C_HEADS):
        out_ref[hd] = s[C_DK * hd:C_DK * (hd + 1), C_DV * hd:C_DV * (hd + 1)]


def _mix_kernel(cached, layer, first_state, n_aliased, lam_init, seq, qt, qb, *refs):
    it = iter(refs)
    x_ref, mod_ref = next(it), next(it)
    qa_ref, ka_ref, qb_ref, kb_ref = (next(it) for _ in range(4))
    cq_ref, ck_ref, cv_ref, gf_ref, gb_ref, su_ref = (next(it) for _ in range(6))
    vat_ref, vbt_ref = next(it), next(it)
    if cached:
        cakt_ref, cavt_ref, cbkt_ref, cbvt_ref, s0f_ref, s0b_ref = (next(it) for _ in range(6))
        qstat_ref, aqg_ref = next(it), next(it)
    wout_ref, gpost_ref, bog_ref, cog_ref, lamp_ref = (next(it) for _ in range(5))
    for _ in range(n_aliased):
        next(it)
    y_ref = next(it)
    if not cached:
        sf_ref, sb_ref = next(it), next(it)
    kA_s, vtA_s, kB_s, vtB_s, oc_s, mixed_s, kn2_s = (next(it) for _ in range(7))

    lk = kA_s.shape[1]
    past = lk - seq
    t = pl.program_id(1)

    def once_per_sequence(body):
        return body() if seq == qt else pl.when(t == 0)(body)

    @once_per_sequence
    def _per_sequence():
        ones_row = jnp.where(_iota((VT_ROWS - HEAD_DIM, lk), 0) == 0, 1.0, 0.0).astype(BF16)
        for k_new, kt_cache, k_dst, vt_new, vt_cache, vt_dst in (
                (ka_ref, cakt_ref if cached else None, kA_s, vat_ref, cavt_ref if cached else None, vtA_s),
                (kb_ref, cbkt_ref if cached else None, kB_s, vbt_ref, cbvt_ref if cached else None, vtB_s)):
            n_heads = k_dst.shape[0]
            if cached:
                for h0 in range(0, n_heads, 2):
                    pair = jnp.concatenate([kt_cache[h0], kt_cache[h0 + 1]], axis=0).T
                    k_dst[h0, 0:past, :] = pair[:, 0:HEAD_DIM].astype(BF16)
                    k_dst[h0 + 1, 0:past, :] = pair[:, HEAD_DIM:2 * HEAD_DIM].astype(BF16)
            for hd in range(n_heads):
                k_dst[hd, past:lk, :] = k_new[hd]
                if cached:
                    vt_dst[hd, 0:HEAD_DIM, 0:past] = vt_cache[hd].astype(BF16)
                vt_dst[hd, 0:HEAD_DIM, past:lk] = vt_new[hd].astype(BF16)
                vt_dst[hd, HEAD_DIM:VT_ROWS, :] = ones_row
        bounded_jobs = ([(kA_s, g) for g in range(A_KV_HEADS)] + [(kB_s, h) for h in range(B_HEADS)]
                        if lk > KEY_TILE else [])
        for j, (k_dst, hd) in enumerate(bounded_jobs):
            kf = k_dst[hd].astype(F32)
            kn2 = jnp.max(jnp.sum(kf * kf, axis=-1, keepdims=True), axis=0, keepdims=True)
            kn2_s[j:j + 1, :] = jnp.broadcast_to(kn2, (1, LANES))

        if cached:
            s0f, s0b = _state_to_blockdiag_t(s0f_ref), _state_to_blockdiag_t(s0b_ref)
        else:
            s0f = jnp.zeros((C_WIDTH, C_KW), F32)
            s0b = s0f
        s_f, s_b = _gla_bidirectional(cq_ref, ck_ref, cv_ref, gf_ref, gb_ref, s0f, s0b, seq, oc_s)
        if not cached:
            for ref, s_t in ((sf_ref, s_f), (sb_ref, s_b)):
                if first_state:
                    for l2 in range(DEPTH):
                        if l2 == layer:
                            _blockdiag_t_to_state(s_t, ref.at[l2])
                        else:
                            ref[l2] = jnp.zeros(ref.shape[1:], F32)
                else:
                    _blockdiag_t_to_state(s_t, ref)
        for r0 in range(0, seq, GLA_BLOCK):
            oc = oc_s[r0:r0 + GLA_BLOCK, :]
            oc_s[r0:r0 + GLA_BLOCK, :] = oc * lax.rsqrt(_group_mean_sq(oc, 6) + EPS) * cog_ref[...]

    lam = (jnp.exp(jnp.sum(lamp_ref[0:1, :] * lamp_ref[1:2, :], axis=-1, keepdims=True))
           - jnp.exp(jnp.sum(lamp_ref[2:3, :] * lamp_ref[3:4, :], axis=-1, keepdims=True))
           + lam_init)

    def project_out():
        seq_rows = pl.ds(pl.multiple_of(t * qt, qt), qt)
        mixed_s[:, A_WIDTH + B_WIDTH:D_MIX] = oc_s[seq_rows, :]
        gate = mod_ref[:, 2 * D_MODEL:3 * D_MODEL]
        mixed = (mixed_s[...] * su_ref[...]).astype(BF16)
        y = _dot(mixed, wout_ref[...])
        yn = y * lax.rsqrt(jnp.mean(y * y, axis=-1, keepdims=True) + EPS) * gpost_ref[...]
        y_ref[...] = x_ref[...] + gate * yn

    def attn_block(i, carry):
        rows = pl.ds(i * qb, qb) if isinstance(i, int) else pl.ds(pl.multiple_of(i * qb, qb), qb)
        jobs = []
        for grp in range(A_KV_HEADS):
            q4 = qa_ref[A_GROUP * grp:A_GROUP * (grp + 1), rows, :].reshape(A_GROUP * qb, HEAD_DIM)
            jobs.append((kA_s.at[grp], q4, vtA_s.at[grp]))
        lane = _iota((qb, HEAD_DIM), 1)
        for hd in range(B_HEADS):
            q = qb_ref[hd, rows, :]
            zero = jnp.zeros_like(q)
            q2 = jnp.concatenate([jnp.where(lane < B_QK_DIM, q, zero),
                                  jnp.where(lane >= B_QK_DIM, q, zero)], axis=0)
            jobs.append((kB_s.at[hd], q2, vtB_s.at[hd]))

        def finish(outs):
            for grp in range(A_KV_HEADS):
                ot = outs[grp]
                for pair in range(A_GROUP // 2):
                    c0 = 2 * pair * qb
                    col = A_GROUP * HEAD_DIM * grp + 2 * HEAD_DIM * pair
                    mixed_s[rows, col:col + 2 * HEAD_DIM] = _pair_rows(ot[:, c0:c0 + qb],
                                                                       ot[:, c0 + qb:c0 + 2 * qb])
            obs = []
            for hd in range(B_HEADS):
                ot = outs[A_KV_HEADS + hd]
                ob = ot[:, 0:qb] - lam * ot[:, qb:2 * qb]
                obs.append(ob * lax.rsqrt(jnp.mean(ob * ob, axis=0, keepdims=True) + EPS))
            for pair in range(B_HEADS // 2):
                col = A_WIDTH + 2 * B_V_DIM * pair
                mixed_s[rows, col:col + 2 * B_V_DIM] = (_pair_rows(obs[2 * pair], obs[2 * pair + 1])
                                                        * bog_ref[...] * (1.0 - lam_init))
            if qt == qb:
                project_out()

        if lk <= KEY_TILE:
            finish(_attend_t(jobs))
            return carry

        qa_norm = (jnp.max(jnp.abs(aqg_ref[...]), axis=-1, keepdims=True)
                   * (HEAD_DIM ** 0.5 * HEAD_DIM ** -0.5 * LOG2E))
        stat_t = qstat_ref[...]
        shifts, gaps = [], []
        for j, (_, q, _) in enumerate(jobs):
            k_norm = jnp.sqrt(kn2_s[j:j + 1, 0:1])
            if j < A_KV_HEADS:
                upper = qa_norm * k_norm * BOUND_SLACK
                shifts.append(jnp.broadcast_to(upper, (1, q.shape[0])))
                gaps.append(2.0 * upper)
            else:
                r0 = 2 * (j - A_KV_HEADS)
                r1 = r0 + 2 * B_HEADS
                qn2 = jnp.concatenate([stat_t[r0:r0 + 1], stat_t[r0 + 1:r0 + 2]], axis=-1)
                lower = jnp.concatenate([stat_t[r1:r1 + 1], stat_t[r1 + 1:r1 + 2]], axis=-1)
                upper = jnp.sqrt(qn2) * k_norm * BOUND_SLACK
                shifts.append(upper)
                gaps.append(jnp.max(upper - lower, axis=-1, keepdims=True))
        worst = functools.reduce(jnp.maximum, gaps)
        safe = worst[0, 0] <= SAFE_GAP

        @pl.when(safe)
        def _bounded():
            finish(_attend_t(jobs, shifts))

        @pl.when(jnp.logical_not(safe))
        def _exact_max():
            finish(_attend_t(jobs))

        return carry

    if qt != qb:
        lax.fori_loop(0, qt // qb, attn_block, 0)
        project_out()
    else:
        attn_block(0, 0)


def _mixer(x, mod, per_batch_mod, layer, proj, wts, lam_init, cache, state_prev=None):
    bsz, seq, _ = x.shape
    cached = cache is not None
    past = cache[0].shape[4] if cached else 0
    qa, ka, qb, kb, cq, ck, cv, gf, gb, su, vat, vbt = proj[:12]
    qt = 256
    qb_rows = 256

    def per_layer(shape):
        return pl.BlockSpec((None,) + shape, lambda b, t: (layer,) + (0,) * len(shape))

    def layer_heads_t(n, length):
        return pl.BlockSpec((None, None, n, HEAD_DIM, length), lambda b, t: (b, layer, 0, 0, 0))

    def heads_t(n):
        if cached:
            return pl.BlockSpec((None, n, HEAD_DIM, seq), lambda b, t: (b, 0, 0, 0))
        return layer_heads_t(n, seq)

    def heads(n):
        return pl.BlockSpec((None, n, seq, HEAD_DIM), lambda b, t: (b, 0, 0, 0))

    def head_tile(n):
        return pl.BlockSpec((None, n, qt, HEAD_DIM), lambda b, t: (b, 0, t, 0))

    def rows(width):
        return pl.BlockSpec((None, seq, width), lambda b, t: (b, 0, 0))

    def row_tile(width):
        return pl.BlockSpec((None, qt, width), lambda b, t: (b, t, 0))

    mod_idx = (lambda b, t: (layer, b, 0, 0)) if per_batch_mod else (lambda b, t: (layer, 0, 0, 0))
    in_specs = [row_tile(D_MODEL), pl.BlockSpec((None, None, 1, 3 * D_MODEL), mod_idx),
                head_tile(A_HEADS), heads(A_KV_HEADS), head_tile(B_HEADS), heads(B_HEADS),
                rows(C_KW), rows(C_KW), rows(C_WIDTH), rows(C_KW), rows(C_KW), row_tile(D_MIX),
                heads_t(A_KV_HEADS), heads_t(B_HEADS)]
    args = [x, mod, qa, ka, qb, kb, cq, ck, cv, gf, gb, su, vat, vbt]
    if cached:
        state_in = pl.BlockSpec((None, None, C_HEADS, C_DK, C_DV), lambda b, t: (b, layer, 0, 0, 0))
        in_specs += [layer_heads_t(A_KV_HEADS, past), layer_heads_t(A_KV_HEADS, past),
                     layer_heads_t(B_HEADS, past), layer_heads_t(B_HEADS, past), state_in, state_in,
                     pl.BlockSpec((None, QSTAT_ROWS, qt), lambda b, t: (b, 0, t)),
                     per_layer((1, A_WIDTH))]
        args += list(cache) + [proj[12], wts["aq_gain"]]
    in_specs += [per_layer((D_MIX, D_MODEL)), per_layer((1, D_MODEL)), per_layer((1, 2 * B_V_DIM)),
                 per_layer((1, C_WIDTH)), per_layer((4, B_QK_DIM))]
    args += [wts["w_out"], wts["g_post"], wts["b_out_gain"], wts["c_out_gain"], wts["lam_params"]]

    out_specs = [row_tile(D_MODEL)]
    out_shape = [jax.ShapeDtypeStruct((bsz, seq, D_MODEL), F32)]
    aliases = {}
    first_state = not cached and state_prev is None
    if not cached:
        if first_state:
            state_out = pl.BlockSpec((None, DEPTH, C_HEADS, C_DK, C_DV), lambda b, t: (b, 0, 0, 0, 0))
        else:
            state_out = pl.BlockSpec((None, None, C_HEADS, C_DK, C_DV),
                                     lambda b, t: (b, layer, 0, 0, 0))
            for j, buf in enumerate(state_prev):
                aliases[len(args)] = 1 + j
                in_specs.append(pl.BlockSpec(memory_space=pl.ANY))
                args.append(buf)
        out_specs += [state_out, state_out]
        out_shape += [jax.ShapeDtypeStruct((bsz, DEPTH, C_HEADS, C_DK, C_DV), F32)] * 2

    lk = past + seq
    scratch = [pltpu.VMEM((A_KV_HEADS, lk, HEAD_DIM), BF16), pltpu.VMEM((A_KV_HEADS, VT_ROWS, lk), BF16),
               pltpu.VMEM((B_HEADS, lk, HEAD_DIM), BF16), pltpu.VMEM((B_HEADS, VT_ROWS, lk), BF16),
               pltpu.VMEM((seq, C_WIDTH), F32), pltpu.VMEM((qt, D_MIX), F32),
               pltpu.VMEM((8, LANES), F32)]
    return pl.pallas_call(
        functools.partial(_mix_kernel, cached, layer, first_state, len(aliases), lam_init, seq, qt,
                          qb_rows),
        grid=(bsz, seq // qt),
        in_specs=in_specs,
        out_specs=out_specs,
        out_shape=out_shape,
        scratch_shapes=scratch,
        input_output_aliases=aliases,
        compiler_params=pltpu.CompilerParams(
            dimension_semantics=("arbitrary", "arbitrary"), vmem_limit_bytes=VMEM_LIMIT),
        name="mixer_cached" if cached else "mixer",
    )(*args)


def _rope_tables(seq):
    t = jnp.arange(seq)
    pos_row = (t // GRID_W).astype(F32)
    pos_col = (t % GRID_W).astype(F32)

    def tables(half, width):
        freq = ROPE_THETA ** (-jnp.arange(half, dtype=F32) / half)
        ang_r = pos_row[:, None] * freq[None, :]
        ang_c = pos_col[:, None] * freq[None, :]
        cos = jnp.concatenate([jnp.cos(ang_r), jnp.cos(ang_r), jnp.cos(ang_c), jnp.cos(ang_c)], axis=-1)
        sin = jnp.concatenate([-jnp.sin(ang_r), jnp.sin(ang_r), -jnp.sin(ang_c), jnp.sin(ang_c)], axis=-1)
        reps = width // (4 * half)
        return jnp.tile(cos, (1, reps)), jnp.tile(sin, (1, reps))

    cos_a, sin_a = tables(HEAD_DIM // 4, A_WIDTH)
    cos_b, sin_b = tables(B_QK_DIM // 4, B_WIDTH)
    return cos_a, sin_a, cos_b, sin_b


def _prepare_weights(g_pre, g_post, w_in, w_out, a_q_gain, a_k_gain, b_lambda_q1, b_lambda_k1,
                     b_lambda_q2, b_lambda_k2, b_out_gain, c_gate_w_fwd, c_gate_b_fwd, c_gate_w_bwd,
                     c_gate_b_bwd, c_out_gain):
    w_in_t = jnp.swapaxes(w_in, 1, 2).astype(BF16)
    pad = jnp.zeros((DEPTH, GATE_RANK, C_KW), F32)
    cw_f = jnp.concatenate([c_gate_w_fwd, pad], axis=1).astype(BF16)
    cw_b = jnp.concatenate([pad, c_gate_w_bwd], axis=1).astype(BF16)
    return {
        "g_pre": g_pre[:, None, :],
        "g_post": g_post[:, None, :],
        "w_in_t": w_in_t,
        "w_out": w_out.astype(BF16),
        "aq_gain": jnp.tile(a_q_gain, (1, A_HEADS))[:, None, :],
        "ak_gain": jnp.tile(a_k_gain, (1, A_KV_HEADS))[:, None, :],
        "cw_f": cw_f,
        "cb_f": c_gate_b_fwd[:, None, :],
        "cw_b": cw_b,
        "cb_b": c_gate_b_bwd[:, None, :],
        "b_out_gain": jnp.tile(b_out_gain, (1, 2))[:, None, :],
        "c_out_gain": jnp.tile(c_out_gain, (1, C_HEADS))[:, None, :],
        "lam_params": jnp.stack([b_lambda_q1, b_lambda_k1, b_lambda_q2, b_lambda_k2], axis=1),
    }


def kernel(x_prompt, x_sample, c, cache_a_k, cache_a_v, cache_b_k, cache_b_v, state_c_fwd, state_c_bwd, c_ctx, w_mod, b_mod, g_pre, g_post, w_in, w_out, a_q_gain, a_k_gain, b_lambda_q1, b_lambda_k1, b_lambda_q2, b_lambda_k2, b_out_gain, c_gate_w_fwd, c_gate_b_fwd, c_gate_w_bwd, c_gate_b_bwd, c_out_gain):
    dec_batch = x_sample.shape[0]
    dec_seq = x_sample.shape[1]

    mod_rows = 16
    cvec = jnp.zeros((mod_rows, D_MODEL), F32).at[0:dec_batch].set(c).at[dec_batch].set(c_ctx)
    mod = _modulation(cvec, w_mod, b_mod)[:, :, None, :]
    mod_lat = mod[:, 0:dec_batch]
    mod_ctx = mod[:, dec_batch:dec_batch + 1]

    wts = _prepare_weights(g_pre, g_post, w_in, w_out, a_q_gain, a_k_gain, b_lambda_q1, b_lambda_k1,
                           b_lambda_q2, b_lambda_k2, b_out_gain, c_gate_w_fwd, c_gate_b_fwd,
                           c_gate_w_bwd, c_gate_b_bwd, c_out_gain)
    rope_tabs = _rope_tables(dec_seq)
    cache = tuple(jnp.swapaxes(a, -1, -2) for a in (cache_a_k, cache_a_v, cache_b_k, cache_b_v))
    cache += (state_c_fwd, state_c_bwd)

    y_p, y_s = x_prompt, x_sample
    kv_ctx = None
    states = None
    for l in range(DEPTH):
        lam_init = 0.8 - 0.6 * math.exp(-0.3 * l)
        proj_p = _in_projection(y_p, mod_ctx, False, l, wts, None, 2, x_prompt.shape[1], kv_ctx)
        kv_ctx = proj_p[10:14]
        y_p, *states = _mixer(y_p, mod_ctx, False, l, proj_p, wts, lam_init, None, states)

        proj_s = _in_projection(y_s, mod_lat, True, l, wts, rope_tabs, 1, 512)
        (y_s,) = _mixer(y_s, mod_lat, True, l, proj_s, wts, lam_init, cache)

    va_t, vb_t, ka_t, kb_t = kv_ctx
    new_kv = [jnp.swapaxes(a, -1, -2) for a in (ka_t, va_t, kb_t, vb_t)]
    return (y_p, y_s, *new_kv, *states)
```

```python
import functools
import math

import jax
import jax.numpy as jnp
from jax import lax
from jax.experimental import pallas as pl
from jax.experimental.pallas import tpu as pltpu

F32 = jnp.float32
BF16 = jnp.bfloat16

D_MODEL = 1024
DEPTH = 2
GRID_W = 64
HEAD_DIM = 64
A_HEADS = 8
A_KV_HEADS = 2
A_GROUP = A_HEADS // A_KV_HEADS
A_WIDTH = A_HEADS * HEAD_DIM
A_KV_WIDTH = A_KV_HEADS * HEAD_DIM
B_HEADS = 4
B_QK_DIM = 32
B_V_DIM = 64
B_WIDTH = B_HEADS * B_V_DIM
C_HEADS = 4
C_DK = 32
C_DV = 64
C_KW = C_HEADS * C_DK
C_WIDTH = C_HEADS * C_DV
GATE_RANK = 16
GLA_TAU = 16.0
CHUNK = 64
D_MIX = A_WIDTH + B_WIDTH + C_WIDTH
ROPE_THETA = 10000.0
EPS = 1e-6

LANES = 128
GLA_BLOCK = 256
VT_ROWS = HEAD_DIM + 16
SCORE_LOOKAHEAD = 6
SAFE_GAP = 96.0
BOUND_SLACK = 1.02
QSTAT_ROWS = 4 * B_HEADS
U_TILE = D_MIX // 4
N_MIX_PARAMS = 8
N_MIX_SEQ_INPUTS = 14
N_MIX_WEIGHTS = 5
KEY_TILE = 256
LOG2E = math.log2(math.e)

OFF_AQ = 0
OFF_AK = OFF_AQ + A_WIDTH
OFF_AV = OFF_AK + A_KV_WIDTH
OFF_BQ = OFF_AV + A_KV_WIDTH
OFF_BK = OFF_BQ + B_WIDTH
OFF_BV = OFF_BK + B_WIDTH
OFF_CQ = OFF_BV + B_WIDTH
OFF_CK = OFF_CQ + C_KW
OFF_CV = OFF_CK + C_KW
OFF_LR = OFF_CV + C_WIDTH
OFF_U = OFF_LR + 2 * GATE_RANK
IN_WIDTH = OFF_U + D_MIX

VMEM_LIMIT = 56 * 1024 * 1024


def _dot(a, b):
    return jnp.dot(a, b, preferred_element_type=F32)


def _dot_nt(a, b):
    return lax.dot_general(a, b, (((1,), (1,)), ((), ())), preferred_element_type=F32)


def _dot_tn(a, b):
    return lax.dot_general(a, b, (((0,), (0,)), ((), ())), preferred_element_type=F32)


def _split_bf16(x):
    hi = x.astype(BF16)
    lo = (x - hi.astype(F32)).astype(BF16)
    return hi, lo


def _iota(shape, dim):
    return lax.broadcasted_iota(jnp.int32, shape, dim)


def _group_mean_sq(x, group_log2, split=True):
    width = x.shape[-1]
    r = lax.shift_right_logical(_iota((LANES, LANES), 0), group_log2)
    c = lax.shift_right_logical(_iota((LANES, LANES), 1), group_log2)
    ones = jnp.where(r == c, 1.0, 0.0).astype(BF16)
    if split:
        hi, lo = _split_bf16(x * x)
    else:
        hi, lo = (x * x).astype(BF16), None
    cols = []
    for j in range(width // LANES):
        sl = slice(LANES * j, LANES * (j + 1))
        cols.append(_dot(hi[:, sl], ones) + (_dot(lo[:, sl], ones) if split else 0.0))
    ss = cols[0] if len(cols) == 1 else jnp.concatenate(cols, axis=-1)
    return ss * (1.0 / (1 << group_log2))


def _rope(x, cos, sin_signed, dist):
    width = x.shape[-1]
    lane = _iota(x.shape, 1)
    first = (lane & (2 * dist - 1)) < dist
    up = pltpu.roll(x, width - dist, 1)
    down = pltpu.roll(x, dist, 1)
    return x * cos + jnp.where(first, up, down) * sin_signed


def _log_sigmoid(x):
    return jnp.minimum(x, 0.0) - jnp.log1p(jnp.exp(-jnp.abs(x)))


def _silu(x):
    return x * (1.0 / (1.0 + jnp.exp(-x)))


def _mod_kernel(c_ref, w_ref, b_ref, o_ref):
    a = _silu(c_ref[...]).astype(BF16)
    o_ref[...] = _dot(a, w_ref[...].astype(BF16)) + b_ref[...]


def _modulation(cvec, w_mod, b_mod):
    rows = cvec.shape[0]
    nblk = 3
    return pl.pallas_call(
        _mod_kernel,
        grid=(DEPTH, nblk),
        in_specs=[
            pl.BlockSpec((rows, D_MODEL), lambda l, n: (0, 0)),
            pl.BlockSpec((None, D_MODEL, D_MODEL), lambda l, n: (l, 0, n)),
            pl.BlockSpec((None, 1, D_MODEL), lambda l, n: (l, 0, n)),
        ],
        out_specs=pl.BlockSpec((None, rows, D_MODEL), lambda l, n: (l, 0, n)),
        out_shape=jax.ShapeDtypeStruct((DEPTH, rows, 3 * D_MODEL), F32),
        compiler_params=pltpu.CompilerParams(
            dimension_semantics=("arbitrary", "arbitrary"), vmem_limit_bytes=VMEM_LIMIT),
        name="modulation",
    )(cvec, w_mod, b_mod.reshape(DEPTH, 1, 3 * D_MODEL))


def _in_kernel(rope, layer, stacked_first, nb, tl, n_aliased, *refs):
    (x_ref, mod_ref, gpre_ref, wt_ref, aqg_ref, akg_ref, cwf_ref, cbf_ref, cwb_ref, cbb_ref) = refs[:10]
    refs = refs[10:]
    if rope:
        cosa_ref, sina_ref, cosb_ref, sinb_ref = refs[:4]
        refs = refs[4:]
    refs = refs[n_aliased:]
    (qa_ref, ka_ref, qb_ref, kb_ref, cq_ref, ck_ref, cv_ref, gf_ref, gb_ref, su_ref,
     vat_ref, vbt_ref) = refs[:12]
    kat_ref, kbt_ref = (None, None) if rope else refs[12:]
    qstat_ref = refs[12] if rope else None

    x = x_ref[...].reshape(nb * tl, D_MODEL)
    shift = mod_ref[:, 0:D_MODEL]
    scale = mod_ref[:, D_MODEL:2 * D_MODEL]
    ms = jnp.mean(x * x, axis=-1, keepdims=True)
    h = (x * lax.rsqrt(ms + EPS)) * gpre_ref[...] * (1.0 + scale) + shift
    hb = h.astype(BF16)

    def proj(off, width):
        return _dot_nt(hb, wt_ref[off:off + width, :])

    def put_rows(ref, val):
        for bi in range(nb):
            ref[bi] = val[bi * tl:(bi + 1) * tl].astype(ref.dtype)

    def put_heads(ref, val, n_heads):
        for bi in range(nb):
            for hd in range(n_heads):
                ref[bi, hd] = val[bi * tl:(bi + 1) * tl,
                                  HEAD_DIM * hd:HEAD_DIM * (hd + 1)].astype(ref.dtype)

    def put_heads_t(ref, val, n_heads):
        val_t = val.T
        for bi in range(nb):
            for hd in range(n_heads):
                blk = val_t[HEAD_DIM * hd:HEAD_DIM * (hd + 1), bi * tl:(bi + 1) * tl].astype(ref.dtype)
                if stacked_first:
                    for l2 in range(DEPTH):
                        ref[bi, l2, hd] = blk if l2 == layer else jnp.zeros_like(blk)
                else:
                    ref[bi, hd] = blk

    def gate_tile(j):
        c0 = U_TILE * j
        val = _silu(proj(OFF_U + c0, U_TILE))
        for bi in range(nb):
            su_ref[bi, :, c0:c0 + U_TILE] = val[bi * tl:(bi + 1) * tl]

    aq = proj(OFF_AQ, A_WIDTH)
    akv = proj(OFF_AK, 2 * A_KV_WIDTH)
    bq = proj(OFF_BQ, B_WIDTH)
    bk = proj(OFF_BK, B_WIDTH)

    aq = aq * lax.rsqrt(_group_mean_sq(aq, 6, split=False) + EPS) * aqg_ref[...]
    ak = akv[:, 0:A_KV_WIDTH]
    ak = ak * lax.rsqrt(_group_mean_sq(ak, 6, split=False) + EPS) * akg_ref[...]
    gate_tile(0)
    bv = proj(OFF_BV, B_WIDTH)
    if rope:
        aq = _rope(aq, cosa_ref[...], sina_ref[...], 16)
        ak = _rope(ak, cosa_ref[:, 0:A_KV_WIDTH], sina_ref[:, 0:A_KV_WIDTH], 16)
    put_heads(qa_ref, aq * (HEAD_DIM ** -0.5 * LOG2E), A_HEADS)
    put_heads(ka_ref, ak, A_KV_HEADS)
    put_heads_t(vat_ref, akv[:, A_KV_WIDTH:2 * A_KV_WIDTH], A_KV_HEADS)
    if kat_ref is not None:
        put_heads_t(kat_ref, ak, A_KV_HEADS)
    gate_tile(1)
    cqk = proj(OFF_CQ, 2 * C_KW)
    cv = proj(OFF_CV, C_WIDTH)
    lr = proj(OFF_LR, 2 * GATE_RANK).astype(BF16)

    if rope:
        bq = _rope(bq, cosb_ref[...], sinb_ref[...], 8)
        bk = _rope(bk, cosb_ref[...], sinb_ref[...], 8)
    bq = bq * (B_QK_DIM ** -0.5 * LOG2E)
    put_heads(qb_ref, bq, B_HEADS)
    put_heads(kb_ref, bk, B_HEADS)
    if rope:
        bq_t = bq.T
        sq = bq_t * bq_t
        own = bq_t * bk.T
        halves = range(0, B_WIDTH, B_QK_DIM)
        qstat_ref[0] = jnp.concatenate(
            [jnp.sum(sq[r0:r0 + B_QK_DIM], axis=0, keepdims=True) for r0 in halves]
            + [jnp.sum(own[r0:r0 + B_QK_DIM], axis=0, keepdims=True) for r0 in halves], axis=0)
    put_heads_t(vbt_ref, bv, B_HEADS)
    if kbt_ref is not None:
        put_heads_t(kbt_ref, bk, B_HEADS)
    gate_tile(2)

    put_rows(cq_ref, cqk[:, 0:C_KW] * (C_DK ** -0.5))
    put_rows(ck_ref, cqk[:, C_KW:2 * C_KW])
    put_rows(cv_ref, cv)
    put_rows(gf_ref, _log_sigmoid(_dot(lr, cwf_ref[...]) + cbf_ref[...]) * (1.0 / GLA_TAU))
    put_rows(gb_ref, _log_sigmoid(_dot(lr, cwb_ref[...]) + cbb_ref[...]) * (1.0 / GLA_TAU))
    gate_tile(3)


def _in_projection(x, mod, per_batch_mod, layer, wts, rope_tabs, nb, tl, kv_prev=None):
    bsz, seq, _ = x.shape
    rope = rope_tabs is not None
    stacked = not rope
    stacked_first = stacked and kv_prev is None
    grid = (bsz // nb, seq // tl)

    def per_layer(shape):
        return pl.BlockSpec((None,) + shape, lambda b, t: (layer,) + (0,) * len(shape))

    mod_idx = (lambda b, t: (layer, b, 0, 0)) if per_batch_mod else (lambda b, t: (layer, 0, 0, 0))
    in_specs = [
        pl.BlockSpec((nb, tl, D_MODEL), lambda b, t: (b, t, 0)),
        pl.BlockSpec((None, None, 1, 3 * D_MODEL), mod_idx),
        per_layer((1, D_MODEL)),
        per_layer((IN_WIDTH, D_MODEL)),
        per_layer((1, A_WIDTH)),
        per_layer((1, A_KV_WIDTH)),
        per_layer((2 * GATE_RANK, C_KW)),
        per_layer((1, C_KW)),
        per_layer((2 * GATE_RANK, C_KW)),
        per_layer((1, C_KW)),
    ]
    args = [x, mod, wts["g_pre"], wts["w_in_t"], wts["aq_gain"], wts["ak_gain"],
            wts["cw_f"], wts["cb_f"], wts["cw_b"], wts["cb_b"]]
    if rope:
        assert nb == 1
        in_specs += [
            pl.BlockSpec((tl, A_WIDTH), lambda b, t: (t, 0)),
            pl.BlockSpec((tl, A_WIDTH), lambda b, t: (t, 0)),
            pl.BlockSpec((tl, B_WIDTH), lambda b, t: (t, 0)),
            pl.BlockSpec((tl, B_WIDTH), lambda b, t: (t, 0)),
        ]
        args += list(rope_tabs)

    def heads(n):
        return pl.BlockSpec((nb, n, tl, HEAD_DIM), lambda b, t: (b, 0, t, 0))

    def heads_t(n):
        if stacked_first:
            return pl.BlockSpec((nb, DEPTH, n, HEAD_DIM, tl), lambda b, t: (b, 0, 0, 0, t))
        if stacked:
            return pl.BlockSpec((nb, None, n, HEAD_DIM, tl), lambda b, t: (b, layer, 0, 0, t))
        return pl.BlockSpec((nb, n, HEAD_DIM, tl), lambda b, t: (b, 0, 0, t))

    def rows(width):
        return pl.BlockSpec((nb, tl, width), lambda b, t: (b, t, 0))

    def hshape(n):
        return jax.ShapeDtypeStruct((bsz, n, seq, HEAD_DIM), BF16)

    def tshape(n):
        if stacked:
            return jax.ShapeDtypeStruct((bsz, DEPTH, n, HEAD_DIM, seq), F32)
        return jax.ShapeDtypeStruct((bsz, n, HEAD_DIM, seq), BF16)

    def rshape(width):
        return jax.ShapeDtypeStruct((bsz, seq, width), F32)

    out_specs = [heads(A_HEADS), heads(A_KV_HEADS), heads(B_HEADS), heads(B_HEADS),
                 rows(C_KW), rows(C_KW), rows(C_WIDTH), rows(C_KW), rows(C_KW), rows(D_MIX),
                 heads_t(A_KV_HEADS), heads_t(B_HEADS)]
    out_shape = [hshape(A_HEADS), hshape(A_KV_HEADS), hshape(B_HEADS), hshape(B_HEADS),
                 rshape(C_KW), rshape(C_KW), rshape(C_WIDTH), rshape(C_KW), rshape(C_KW),
                 rshape(D_MIX), tshape(A_KV_HEADS), tshape(B_HEADS)]
    if stacked:
        out_specs += [heads_t(A_KV_HEADS), heads_t(B_HEADS)]
        out_shape += [tshape(A_KV_HEADS), tshape(B_HEADS)]
    else:
        out_specs += [pl.BlockSpec((nb, QSTAT_ROWS, tl), lambda b, t: (b, 0, t))]
        out_shape += [jax.ShapeDtypeStruct((bsz, QSTAT_ROWS, seq), F32)]
    aliases = {}
    if kv_prev is not None:
        for j, buf in enumerate(kv_prev):
            aliases[len(args)] = 10 + j
            in_specs.append(pl.BlockSpec(memory_space=pl.ANY))
            args.append(buf)
    return pl.pallas_call(
        functools.partial(_in_kernel, rope, layer, stacked_first, nb, tl, len(aliases)),
        grid=grid,
        in_specs=in_specs,
        out_specs=out_specs,
        out_shape=out_shape,
        input_output_aliases=aliases,
        compiler_params=pltpu.CompilerParams(
            dimension_semantics=("arbitrary", "arbitrary"), vmem_limit_bytes=VMEM_LIMIT),
        name="in_projection_rope" if rope else "in_projection",
    )(*args)


def _gla_bidirectional(cq_ref, ck_ref, cv_ref, gf_ref, gb_ref, s_f, s_b, seq, oc_ref):
    bl = GLA_BLOCK
    n_sub = bl // CHUNK
    nblk = seq // bl
    ri = _iota((bl, bl), 0)
    ci = _iota((bl, bl), 1)
    same_chunk = lax.shift_right_logical(ri, 6) == lax.shift_right_logical(ci, 6)
    bd = (lax.shift_right_logical(_iota((C_WIDTH, C_KW), 0), 6)
          == lax.shift_right_logical(_iota((C_WIDTH, C_KW), 1), 5))
    khead = lax.shift_right_logical(_iota((1, C_KW), 1), 5)
    vhead = lax.shift_right_logical(_iota((1, C_WIDTH), 1), 6)
    scans = []
    for reverse, g_ref in ((False, gf_ref), (True, gb_ref)):
        causal = same_chunk & ((ci >= ri) if reverse else (ci <= ri))
        scans.append((reverse, g_ref, causal, jnp.where(causal, 1.0, 0.0).astype(BF16)))
    states = [s_f, s_b]
    written = set()

    for step in range(nblk):
        rows0 = [step * bl, (nblk - 1 - step) * bl]
        cums = []
        for (reverse, g_ref, causal, tri), r0 in zip(scans, rows0):
            g_hi, g_lo = _split_bf16(g_ref[r0:r0 + bl, :])
            cums.append(_dot(tri, g_hi) + _dot(tri, g_lo))
        prep = []
        for (reverse, g_ref, causal, tri), r0, cum in zip(scans, rows0, cums):
            q = cq_ref[r0:r0 + bl, :]
            k = ck_ref[r0:r0 + bl, :]
            v = cv_ref[r0:r0 + bl, :]
            qt = q * jnp.exp(cum)
            ktb = (k * jnp.exp(-cum)).astype(BF16)
            vb = v.astype(BF16)
            lasts, kdecs = [], []
            for c in range(n_sub):
                c0 = CHUNK * c
                edge = c0 if reverse else c0 + CHUNK - 1
                last = cum[edge:edge + 1, :]
                lasts.append(last)
                kdecs.append((k[c0:c0 + CHUNK] * jnp.exp(last - cum[c0:c0 + CHUNK])).astype(BF16))
            prep.append((qt, ktb, v, vb, lasts, kdecs))
        scores, incs = [], []
        for qt, ktb, v, vb, lasts, kdecs in prep:
            scores.append([_dot_nt(jnp.where(khead == hd, qt, 0.0).astype(BF16), ktb)
                           for hd in range(C_HEADS)])
            incs.append([_dot_tn(vb[CHUNK * c:CHUNK * (c + 1)], kdecs[c]) for c in range(n_sub)])
        probs, entering = [], []
        for si, ((reverse, g_ref, causal, tri), (qt, ktb, v, vb, lasts, kdecs)) in enumerate(
                zip(scans, prep)):
            probs.append([jnp.where(causal, s, 0.0).astype(BF16) for s in scores[si]])
            s_t = states[si]
            before = [None] * n_sub
            subs = range(n_sub)
            for c in (reversed(subs) if reverse else subs):
                before[c] = s_t.astype(BF16)
                s_t = jnp.exp(lasts[c]) * s_t + jnp.where(bd, incs[si][c], 0.0)
            states[si] = s_t
            entering.append(before)
        outs = []
        for si, (qt, ktb, v, vb, lasts, kdecs) in enumerate(prep):
            qtb = qt.astype(BF16)
            o = jnp.concatenate([_dot_nt(qtb[CHUNK * c:CHUNK * (c + 1)], entering[si][c])
                                 for c in range(n_sub)], axis=0)
            for hd in range(C_HEADS):
                o = o + _dot(probs[si][hd], jnp.where(vhead == hd, v, 0.0).astype(BF16))
            outs.append(o)
        if rows0[0] == rows0[1]:
            outs, rows0 = [outs[0] + outs[1]], rows0[:1]
        for o, r0 in zip(outs, rows0):
            if r0 in written:
                oc_ref[r0:r0 + bl, :] = oc_ref[r0:r0 + bl, :] + o
            else:
                oc_ref[r0:r0 + bl, :] = o
                written.add(r0)
    return states[0], states[1]


def _attend_t(jobs, shifts=None):
    def scores(i):
        return _dot_nt(jobs[i][0][...], jobs[i][1])

    outs = []
    pending = [scores(i) for i in range(min(SCORE_LOOKAHEAD, len(jobs)))]
    for i, (_, _, vt) in enumerate(jobs):
        st = pending.pop(0)
        if i + SCORE_LOOKAHEAD < len(jobs):
            pending.append(scores(i + SCORE_LOOKAHEAD))
        m = jnp.max(st, axis=0, keepdims=True) if shifts is None else shifts[i]
        p = jnp.exp2(st - m).astype(BF16)
        ot = _dot(vt[...], p)
        outs.append(ot[0:HEAD_DIM] * (1.0 / ot[HEAD_DIM:HEAD_DIM + 1]))
    return outs


def _pair_rows(a, b):
    return jnp.concatenate([a, b], axis=0).T


def _state_to_blockdiag_t(s_ref):
    rows = []
    for hd in range(C_HEADS):
        pieces = []
        if hd:
            pieces.append(jnp.zeros((C_DK, C_DV * hd), F32))
        pieces.append(s_ref[hd])
        if hd < C_HEADS - 1:
            pieces.append(jnp.zeros((C_DK, C_DV * (C_HEADS - 1 - hd)), F32))
        rows.append(jnp.concatenate(pieces, axis=-1))
    return jnp.concatenate(rows, axis=0).T


def _blockdiag_t_to_state(s_t, out_ref):
    s = s_t.T
    for hd in range(C_HEADS):
        out_ref[hd] = s[C_DK * hd:C_DK * (hd + 1), C_DV * hd:C_DV * (hd + 1)]


def _mix_sequence(cached, layer, first_state, n_aliased, lam_init, seq, qt, qb, *refs):
    it = iter(refs)
    x_ref, mod_ref = next(it), next(it)
    qa_ref, ka_ref, qb_ref, kb_ref = (next(it) for _ in range(4))
    cq_ref, ck_ref, cv_ref, gf_ref, gb_ref, su_ref = (next(it) for _ in range(6))
    vat_ref, vbt_ref = next(it), next(it)
    if cached:
        cakt_ref, cavt_ref, cbkt_ref, cbvt_ref, s0f_ref, s0b_ref = (next(it) for _ in range(6))
        qstat_ref, aqg_ref = next(it), next(it)
    wout_ref, gpost_ref, bog_ref, cog_ref, lamp_ref = (next(it) for _ in range(5))
    for _ in range(n_aliased):
        next(it)
    y_ref = next(it)
    if not cached:
        sf_ref, sb_ref = next(it), next(it)
    kA_s, vtA_s, kB_s, vtB_s, oc_s, mixed_s, kn2_s = (next(it) for _ in range(7))

    lk = kA_s.shape[1]
    past = lk - seq
    t = pl.program_id(1)

    def once_per_sequence(body):
        return body() if seq == qt else pl.when(t == 0)(body)

    @once_per_sequence
    def _per_sequence():
        ones_row = jnp.where(_iota((VT_ROWS - HEAD_DIM, lk), 0) == 0, 1.0, 0.0).astype(BF16)
        for k_new, kt_cache, k_dst, vt_new, vt_cache, vt_dst in (
                (ka_ref, cakt_ref if cached else None, kA_s, vat_ref, cavt_ref if cached else None, vtA_s),
                (kb_ref, cbkt_ref if cached else None, kB_s, vbt_ref, cbvt_ref if cached else None, vtB_s)):
            n_heads = k_dst.shape[0]
            if cached:
                for h0 in range(0, n_heads, 2):
                    pair = jnp.concatenate([kt_cache[h0], kt_cache[h0 + 1]], axis=0).T
                    k_dst[h0, 0:past, :] = pair[:, 0:HEAD_DIM].astype(BF16)
                    k_dst[h0 + 1, 0:past, :] = pair[:, HEAD_DIM:2 * HEAD_DIM].astype(BF16)
            for hd in range(n_heads):
                k_dst[hd, past:lk, :] = k_new[hd]
                if cached:
                    vt_dst[hd, 0:HEAD_DIM, 0:past] = vt_cache[hd].astype(BF16)
                vt_dst[hd, 0:HEAD_DIM, past:lk] = vt_new[hd].astype(BF16)
                vt_dst[hd, HEAD_DIM:VT_ROWS, :] = ones_row
        bounded_jobs = ([(kA_s, g) for g in range(A_KV_HEADS)] + [(kB_s, h) for h in range(B_HEADS)]
                        if lk > KEY_TILE else [])
        for j, (k_dst, hd) in enumerate(bounded_jobs):
            kf = k_dst[hd].astype(F32)
            kn2 = jnp.max(jnp.sum(kf * kf, axis=-1, keepdims=True), axis=0, keepdims=True)
            kn2_s[j:j + 1, :] = jnp.broadcast_to(kn2, (1, LANES))

        if cached:
            s0f, s0b = _state_to_blockdiag_t(s0f_ref), _state_to_blockdiag_t(s0b_ref)
        else:
            s0f = jnp.zeros((C_WIDTH, C_KW), F32)
            s0b = s0f
        s_f, s_b = _gla_bidirectional(cq_ref, ck_ref, cv_ref, gf_ref, gb_ref, s0f, s0b, seq, oc_s)
        if not cached:
            for ref, s_t in ((sf_ref, s_f), (sb_ref, s_b)):
                if first_state:
                    for l2 in range(DEPTH):
                        if l2 == layer:
                            _blockdiag_t_to_state(s_t, ref.at[l2])
                        else:
                            ref[l2] = jnp.zeros(ref.shape[1:], F32)
                else:
                    _blockdiag_t_to_state(s_t, ref)
        for r0 in range(0, seq, GLA_BLOCK):
            oc = oc_s[r0:r0 + GLA_BLOCK, :]
            oc_s[r0:r0 + GLA_BLOCK, :] = oc * lax.rsqrt(_group_mean_sq(oc, 6) + EPS) * cog_ref[...]

    lam = (jnp.exp(jnp.sum(lamp_ref[0:1, :] * lamp_ref[1:2, :], axis=-1, keepdims=True))
           - jnp.exp(jnp.sum(lamp_ref[2:3, :] * lamp_ref[3:4, :], axis=-1, keepdims=True))
           + lam_init)

    def project_out():
        seq_rows = pl.ds(pl.multiple_of(t * qt, qt), qt)
        mixed_s[:, A_WIDTH + B_WIDTH:D_MIX] = oc_s[seq_rows, :]
        gate = mod_ref[:, 2 * D_MODEL:3 * D_MODEL]
        mixed = (mixed_s[...] * su_ref[...]).astype(BF16)
        y = _dot(mixed, wout_ref[...])
        yn = y * lax.rsqrt(jnp.mean(y * y, axis=-1, keepdims=True) + EPS) * gpost_ref[...]
        y_ref[...] = x_ref[...] + gate * yn

    def attn_block(i, carry):
        rows = pl.ds(i * qb, qb) if isinstance(i, int) else pl.ds(pl.multiple_of(i * qb, qb), qb)
        jobs = []
        for grp in range(A_KV_HEADS):
            q4 = qa_ref[A_GROUP * grp:A_GROUP * (grp + 1), rows, :].reshape(A_GROUP * qb, HEAD_DIM)
            jobs.append((kA_s.at[grp], q4, vtA_s.at[grp]))
        lane = _iota((qb, HEAD_DIM), 1)
        for hd in range(B_HEADS):
            q = qb_ref[hd, rows, :]
            zero = jnp.zeros_like(q)
            q2 = jnp.concatenate([jnp.where(lane < B_QK_DIM, q, zero),
                                  jnp.where(lane >= B_QK_DIM, q, zero)], axis=0)
            jobs.append((kB_s.at[hd], q2, vtB_s.at[hd]))

        def finish(outs):
            for grp in range(A_KV_HEADS):
                ot = outs[grp]
                for pair in range(A_GROUP // 2):
                    c0 = 2 * pair * qb
                    col = A_GROUP * HEAD_DIM * grp + 2 * HEAD_DIM * pair
                    mixed_s[rows, col:col + 2 * HEAD_DIM] = _pair_rows(ot[:, c0:c0 + qb],
                                                                       ot[:, c0 + qb:c0 + 2 * qb])
            obs = []
            for hd in range(B_HEADS):
                ot = outs[A_KV_HEADS + hd]
                ob = ot[:, 0:qb] - lam * ot[:, qb:2 * qb]
                obs.append(ob * lax.rsqrt(jnp.mean(ob * ob, axis=0, keepdims=True) + EPS))
            for pair in range(B_HEADS // 2):
                col = A_WIDTH + 2 * B_V_DIM * pair
                mixed_s[rows, col:col + 2 * B_V_DIM] = (_pair_rows(obs[2 * pair], obs[2 * pair + 1])
                                                        * bog_ref[...] * (1.0 - lam_init))
            if qt == qb:
                project_out()

        if lk <= KEY_TILE:
            finish(_attend_t(jobs))
            return carry

        qa_norm = (jnp.max(jnp.abs(aqg_ref[...]), axis=-1, keepdims=True)
                   * (HEAD_DIM ** 0.5 * HEAD_DIM ** -0.5 * LOG2E))
        stat_t = qstat_ref[...]
        shifts, gaps = [], []
        for j, (_, q, _) in enumerate(jobs):
            k_norm = jnp.sqrt(kn2_s[j:j + 1, 0:1])
            if j < A_KV_HEADS:
                upper = qa_norm * k_norm * BOUND_SLACK
                shifts.append(jnp.broadcast_to(upper, (1, q.shape[0])))
                gaps.append(2.0 * upper)
            else:
                r0 = 2 * (j - A_KV_HEADS)
                r1 = r0 + 2 * B_HEADS
                qn2 = jnp.concatenate([stat_t[r0:r0 + 1], stat_t[r0 + 1:r0 + 2]], axis=-1)
                lower = jnp.concatenate([stat_t[r1:r1 + 1], stat_t[r1 + 1:r1 + 2]], axis=-1)
                upper = jnp.sqrt(qn2) * k_norm * BOUND_SLACK
                shifts.append(upper)
                gaps.append(jnp.max(upper - lower, axis=-1, keepdims=True))
        worst = functools.reduce(jnp.maximum, gaps)
        safe = worst[0, 0] <= SAFE_GAP

        @pl.when(safe)
        def _bounded():
            finish(_attend_t(jobs, shifts))

        @pl.when(jnp.logical_not(safe))
        def _exact_max():
            finish(_attend_t(jobs))

        return carry

    if qt != qb:
        lax.fori_loop(0, qt // qb, attn_block, 0)
        project_out()
    else:
        attn_block(0, 0)


def _mix_kernel(nseq, *params_and_refs):
    params, refs = params_and_refs[:N_MIX_PARAMS], params_and_refs[N_MIX_PARAMS:]
    if nseq == 1:
        return _mix_sequence(*params, *refs)
    n_aliased = params[3]
    shared = {1} | set(range(N_MIX_SEQ_INPUTS, N_MIX_SEQ_INPUTS + N_MIX_WEIGHTS + n_aliased))
    for bi in range(nseq):
        _mix_sequence(*params, *[r if idx in shared else r.at[bi] for idx, r in enumerate(refs)])


def _mixer(x, mod, per_batch_mod, layer, proj, wts, lam_init, cache, state_prev=None):
    bsz, seq, _ = x.shape
    cached = cache is not None
    past = cache[0].shape[4] if cached else 0
    qa, ka, qb, kb, cq, ck, cv, gf, gb, su, vat, vbt = proj[:12]
    qt = 256
    qb_rows = 256
    nseq = 1 if cached else 4
    lead = None if nseq == 1 else nseq

    def per_layer(shape):
        return pl.BlockSpec((None,) + shape, lambda b, t: (layer,) + (0,) * len(shape))

    def layer_heads_t(n, length):
        return pl.BlockSpec((lead, None, n, HEAD_DIM, length), lambda b, t: (b, layer, 0, 0, 0))

    def heads_t(n):
        if cached:
            return pl.BlockSpec((lead, n, HEAD_DIM, seq), lambda b, t: (b, 0, 0, 0))
        return layer_heads_t(n, seq)

    def heads(n):
        return pl.BlockSpec((lead, n, seq, HEAD_DIM), lambda b, t: (b, 0, 0, 0))

    def head_tile(n):
        return pl.BlockSpec((lead, n, qt, HEAD_DIM), lambda b, t: (b, 0, t, 0))

    def rows(width):
        return pl.BlockSpec((lead, seq, width), lambda b, t: (b, 0, 0))

    def row_tile(width):
        return pl.BlockSpec((lead, qt, width), lambda b, t: (b, t, 0))

    mod_idx = (lambda b, t: (layer, b, 0, 0)) if per_batch_mod else (lambda b, t: (layer, 0, 0, 0))
    in_specs = [row_tile(D_MODEL), pl.BlockSpec((None, None, 1, 3 * D_MODEL), mod_idx),
                head_tile(A_HEADS), heads(A_KV_HEADS), head_tile(B_HEADS), heads(B_HEADS),
                rows(C_KW), rows(C_KW), rows(C_WIDTH), rows(C_KW), rows(C_KW), row_tile(D_MIX),
                heads_t(A_KV_HEADS), heads_t(B_HEADS)]
    args = [x, mod, qa, ka, qb, kb, cq, ck, cv, gf, gb, su, vat, vbt]
    if cached:
        state_in = pl.BlockSpec((None, None, C_HEADS, C_DK, C_DV), lambda b, t: (b, layer, 0, 0, 0))
        in_specs += [layer_heads_t(A_KV_HEADS, past), layer_heads_t(A_KV_HEADS, past),
                     layer_heads_t(B_HEADS, past), layer_heads_t(B_HEADS, past), state_in, state_in,
                     pl.BlockSpec((None, QSTAT_ROWS, qt), lambda b, t: (b, 0, t)),
                     per_layer((1, A_WIDTH))]
        args += list(cache) + [proj[12], wts["aq_gain"]]
    in_specs += [per_layer((D_MIX, D_MODEL)), per_layer((1, D_MODEL)), per_layer((1, 2 * B_V_DIM)),
                 per_layer((1, C_WIDTH)), per_layer((4, B_QK_DIM))]
    args += [wts["w_out"], wts["g_post"], wts["b_out_gain"], wts["c_out_gain"], wts["lam_params"]]

    out_specs = [row_tile(D_MODEL)]
    out_shape = [jax.ShapeDtypeStruct((bsz, seq, D_MODEL), F32)]
    aliases = {}
    first_state = not cached and state_prev is None
    if not cached:
        if first_state:
            state_out = pl.BlockSpec((lead, DEPTH, C_HEADS, C_DK, C_DV), lambda b, t: (b, 0, 0, 0, 0))
        else:
            state_out = pl.BlockSpec((lead, None, C_HEADS, C_DK, C_DV),
                                     lambda b, t: (b, layer, 0, 0, 0))
            for j, buf in enumerate(state_prev):
                aliases[len(args)] = 1 + j
                in_specs.append(pl.BlockSpec(memory_space=pl.ANY))
                args.append(buf)
        out_specs += [state_out, state_out]
        out_shape += [jax.ShapeDtypeStruct((bsz, DEPTH, C_HEADS, C_DK, C_DV), F32)] * 2

    lk = past + seq
    per_seq = () if nseq == 1 else (nseq,)
    scratch = [pltpu.VMEM(per_seq + shape, dtype) for shape, dtype in (
        ((A_KV_HEADS, lk, HEAD_DIM), BF16), ((A_KV_HEADS, VT_ROWS, lk), BF16),
        ((B_HEADS, lk, HEAD_DIM), BF16), ((B_HEADS, VT_ROWS, lk), BF16),
        ((seq, C_WIDTH), F32), ((qt, D_MIX), F32), ((8, LANES), F32))]
    return pl.pallas_call(
        functools.partial(_mix_kernel, nseq, cached, layer, first_state, len(aliases), lam_init, seq,
                          qt, qb_rows),
        grid=(bsz // nseq, seq // qt),
        in_specs=in_specs,
        out_specs=out_specs,
        out_shape=out_shape,
        scratch_shapes=scratch,
        input_output_aliases=aliases,
        compiler_params=pltpu.CompilerParams(
            dimension_semantics=("arbitrary", "arbitrary"), vmem_limit_bytes=VMEM_LIMIT),
        name="mixer_cached" if cached else "mixer",
    )(*args)


def _rope_tables(seq):
    t = jnp.arange(seq)
    pos_row = (t // GRID_W).astype(F32)
    pos_col = (t % GRID_W).astype(F32)

    def tables(half, width):
        freq = ROPE_THETA ** (-jnp.arange(half, dtype=F32) / half)
        ang_r = pos_row[:, None] * freq[None, :]
        ang_c = pos_col[:, None] * freq[None, :]
        cos = jnp.concatenate([jnp.cos(ang_r), jnp.cos(ang_r), jnp.cos(ang_c), jnp.cos(ang_c)], axis=-1)
        sin = jnp.concatenate([-jnp.sin(ang_r), jnp.sin(ang_r), -jnp.sin(ang_c), jnp.sin(ang_c)], axis=-1)
        reps = width // (4 * half)
        return jnp.tile(cos, (1, reps)), jnp.tile(sin, (1, reps))

    cos_a, sin_a = tables(HEAD_DIM // 4, A_WIDTH)
    cos_b, sin_b = tables(B_QK_DIM // 4, B_WIDTH)
    return cos_a, sin_a, cos_b, sin_b


def _prepare_weights(g_pre, g_post, w_in, w_out, a_q_gain, a_k_gain, b_lambda_q1, b_lambda_k1,
                     b_lambda_q2, b_lambda_k2, b_out_gain, c_gate_w_fwd, c_gate_b_fwd, c_gate_w_bwd,
                     c_gate_b_bwd, c_out_gain):
    w_in_t = jnp.swapaxes(w_in, 1, 2).astype(BF16)
    pad = jnp.zeros((DEPTH, GATE_RANK, C_KW), F32)
    cw_f = jnp.concatenate([c_gate_w_fwd, pad], axis=1).astype(BF16)
    cw_b = jnp.concatenate([pad, c_gate_w_bwd], axis=1).astype(BF16)
    return {
        "g_pre": g_pre[:, None, :],
        "g_post": g_post[:, None, :],
        "w_in_t": w_in_t,
        "w_out": w_out.astype(BF16),
        "aq_gain": jnp.tile(a_q_gain, (1, A_HEADS))[:, None, :],
        "ak_gain": jnp.tile(a_k_gain, (1, A_KV_HEADS))[:, None, :],
        "cw_f": cw_f,
        "cb_f": c_gate_b_fwd[:, None, :],
        "cw_b": cw_b,
        "cb_b": c_gate_b_bwd[:, None, :],
        "b_out_gain": jnp.tile(b_out_gain, (1, 2))[:, None, :],
        "c_out_gain": jnp.tile(c_out_gain, (1, C_HEADS))[:, None, :],
        "lam_params": jnp.stack([b_lambda_q1, b_lambda_k1, b_lambda_q2, b_lambda_k2], axis=1),
    }


def kernel(x_prompt, x_sample, c, cache_a_k, cache_a_v, cache_b_k, cache_b_v, state_c_fwd, state_c_bwd, c_ctx, w_mod, b_mod, g_pre, g_post, w_in, w_out, a_q_gain, a_k_gain, b_lambda_q1, b_lambda_k1, b_lambda_q2, b_lambda_k2, b_out_gain, c_gate_w_fwd, c_gate_b_fwd, c_gate_w_bwd, c_gate_b_bwd, c_out_gain):
    dec_batch = x_sample.shape[0]
    dec_seq = x_sample.shape[1]

    mod_rows = 16
    cvec = jnp.zeros((mod_rows, D_MODEL), F32).at[0:dec_batch].set(c).at[dec_batch].set(c_ctx)
    mod = _modulation(cvec, w_mod, b_mod)[:, :, None, :]
    mod_lat = mod[:, 0:dec_batch]
    mod_ctx = mod[:, dec_batch:dec_batch + 1]

    wts = _prepare_weights(g_pre, g_post, w_in, w_out, a_q_gain, a_k_gain, b_lambda_q1, b_lambda_k1,
                           b_lambda_q2, b_lambda_k2, b_out_gain, c_gate_w_fwd, c_gate_b_fwd,
                           c_gate_w_bwd, c_gate_b_bwd, c_out_gain)
    rope_tabs = _rope_tables(dec_seq)
    cache = tuple(jnp.swapaxes(a, -1, -2) for a in (cache_a_k, cache_a_v, cache_b_k, cache_b_v))
    cache += (state_c_fwd, state_c_bwd)

    y_p, y_s = x_prompt, x_sample
    kv_ctx = None
    states = None
    for l in range(DEPTH):
        lam_init = 0.8 - 0.6 * math.exp(-0.3 * l)
        proj_p = _in_projection(y_p, mod_ctx, False, l, wts, None, 2, x_prompt.shape[1], kv_ctx)
        kv_ctx = proj_p[10:14]
        y_p, *states = _mixer(y_p, mod_ctx, False, l, proj_p, wts, lam_init, None, states)

        proj_s = _in_projection(y_s, mod_lat, True, l, wts, rope_tabs, 1, 512)
        (y_s,) = _mixer(y_s, mod_lat, True, l, proj_s, wts, lam_init, cache)

    va_t, vb_t, ka_t, kb_t = kv_ctx
    new_kv = [jnp.swapaxes(a, -1, -2) for a in (ka_t, va_t, kb_t, vb_t)]
    return (y_p, y_s, *new_kv, *states)
```

```python
import functools
import math

import jax
import jax.numpy as jnp
from jax import lax
from jax.experimental import pallas as pl
from jax.experimental.pallas import tpu as pltpu

F32 = jnp.float32
BF16 = jnp.bfloat16

D_MODEL = 1024
DEPTH = 2
GRID_W = 64
HEAD_DIM = 64
A_HEADS = 8
A_KV_HEADS = 2
A_GROUP = A_HEADS // A_KV_HEADS
A_WIDTH = A_HEADS * HEAD_DIM
A_KV_WIDTH = A_KV_HEADS * HEAD_DIM
B_HEADS = 4
B_QK_DIM = 32
B_V_DIM = 64
B_WIDTH = B_HEADS * B_V_DIM
C_HEADS = 4
C_DK = 32
C_DV = 64
C_KW = C_HEADS * C_DK
C_WIDTH = C_HEADS * C_DV
GATE_RANK = 16
GLA_TAU = 16.0
CHUNK = 64
D_MIX = A_WIDTH + B_WIDTH + C_WIDTH
ROPE_THETA = 10000.0
EPS = 1e-6

LANES = 128
GLA_BLOCK = 256
VT_ROWS = HEAD_DIM + 16
SCORE_LOOKAHEAD = 6
SAFE_GAP = 96.0
BOUND_SLACK = 1.02
QSTAT_ROWS = 4 * B_HEADS
U_TILE = D_MIX // 4
N_MIX_PARAMS = 8
N_MIX_SEQ_INPUTS = 14
N_MIX_WEIGHTS = 5
KEY_TILE = 256
LOG2E = math.log2(math.e)

OFF_AQ = 0
OFF_AK = OFF_AQ + A_WIDTH
OFF_AV = OFF_AK + A_KV_WIDTH
OFF_BQ = OFF_AV + A_KV_WIDTH
OFF_BK = OFF_BQ + B_WIDTH
OFF_BV = OFF_BK + B_WIDTH
OFF_CQ = OFF_BV + B_WIDTH
OFF_CK = OFF_CQ + C_KW
OFF_CV = OFF_CK + C_KW
OFF_LR = OFF_CV + C_WIDTH
OFF_U = OFF_LR + 2 * GATE_RANK
IN_WIDTH = OFF_U + D_MIX

VMEM_LIMIT = 60 * 1024 * 1024


def _dot(a, b):
    return jnp.dot(a, b, preferred_element_type=F32)


def _dot_nt(a, b):
    return lax.dot_general(a, b, (((1,), (1,)), ((), ())), preferred_element_type=F32)


def _dot_tn(a, b):
    return lax.dot_general(a, b, (((0,), (0,)), ((), ())), preferred_element_type=F32)


def _split_bf16(x):
    hi = x.astype(BF16)
    lo = (x - hi.astype(F32)).astype(BF16)
    return hi, lo


def _iota(shape, dim):
    return lax.broadcasted_iota(jnp.int32, shape, dim)


def _group_mean_sq(x, group_log2, split=True):
    width = x.shape[-1]
    r = lax.shift_right_logical(_iota((LANES, LANES), 0), group_log2)
    c = lax.shift_right_logical(_iota((LANES, LANES), 1), group_log2)
    ones = jnp.where(r == c, 1.0, 0.0).astype(BF16)
    if split:
        hi, lo = _split_bf16(x * x)
    else:
        hi, lo = (x * x).astype(BF16), None
    cols = []
    for j in range(width // LANES):
        sl = slice(LANES * j, LANES * (j + 1))
        cols.append(_dot(hi[:, sl], ones) + (_dot(lo[:, sl], ones) if split else 0.0))
    ss = cols[0] if len(cols) == 1 else jnp.concatenate(cols, axis=-1)
    return ss * (1.0 / (1 << group_log2))


def _rope(x, cos, sin_signed, dist):
    width = x.shape[-1]
    lane = _iota(x.shape, 1)
    first = (lane & (2 * dist - 1)) < dist
    up = pltpu.roll(x, width - dist, 1)
    down = pltpu.roll(x, dist, 1)
    return x * cos + jnp.where(first, up, down) * sin_signed


def _log_sigmoid(x):
    return jnp.minimum(x, 0.0) - jnp.log1p(jnp.exp(-jnp.abs(x)))


def _silu(x):
    return x * (1.0 / (1.0 + jnp.exp(-x)))


def _mod_kernel(c_ref, w_ref, b_ref, o_ref):
    a = _silu(c_ref[...]).astype(BF16)
    o_ref[...] = _dot(a, w_ref[...].astype(BF16)) + b_ref[...]


def _modulation(cvec, w_mod, b_mod):
    rows = cvec.shape[0]
    nblk = 3
    return pl.pallas_call(
        _mod_kernel,
        grid=(DEPTH, nblk),
        in_specs=[
            pl.BlockSpec((rows, D_MODEL), lambda l, n: (0, 0)),
            pl.BlockSpec((None, D_MODEL, D_MODEL), lambda l, n: (l, 0, n)),
            pl.BlockSpec((None, 1, D_MODEL), lambda l, n: (l, 0, n)),
        ],
        out_specs=pl.BlockSpec((None, rows, D_MODEL), lambda l, n: (l, 0, n)),
        out_shape=jax.ShapeDtypeStruct((DEPTH, rows, 3 * D_MODEL), F32),
        compiler_params=pltpu.CompilerParams(
            dimension_semantics=("arbitrary", "arbitrary"), vmem_limit_bytes=VMEM_LIMIT),
        name="modulation",
    )(cvec, w_mod, b_mod.reshape(DEPTH, 1, 3 * D_MODEL))


def _in_kernel(rope, layer, stacked_first, nb, tl, n_aliased, *refs):
    (x_ref, mod_ref, gpre_ref, wt_ref, aqg_ref, akg_ref, cwf_ref, cbf_ref, cwb_ref, cbb_ref) = refs[:10]
    refs = refs[10:]
    if rope:
        cosa_ref, sina_ref, cosb_ref, sinb_ref = refs[:4]
        refs = refs[4:]
    refs = refs[n_aliased:]
    (qa_ref, ka_ref, qb_ref, kb_ref, cq_ref, ck_ref, cv_ref, gf_ref, gb_ref, su_ref,
     vat_ref, vbt_ref) = refs[:12]
    kat_ref, kbt_ref = (None, None) if rope else refs[12:]
    qstat_ref = refs[12] if rope else None

    x = x_ref[...].reshape(nb * tl, D_MODEL)
    shift = mod_ref[:, 0:D_MODEL]
    scale = mod_ref[:, D_MODEL:2 * D_MODEL]
    ms = jnp.mean(x * x, axis=-1, keepdims=True)
    h = (x * lax.rsqrt(ms + EPS)) * gpre_ref[...] * (1.0 + scale) + shift
    hb = h.astype(BF16)

    def proj(off, width):
        return _dot_nt(hb, wt_ref[off:off + width, :])

    def put_rows(ref, val):
        for bi in range(nb):
            ref[bi] = val[bi * tl:(bi + 1) * tl].astype(ref.dtype)

    def put_heads(ref, val, n_heads):
        for bi in range(nb):
            for hd in range(n_heads):
                ref[bi, hd] = val[bi * tl:(bi + 1) * tl,
                                  HEAD_DIM * hd:HEAD_DIM * (hd + 1)].astype(ref.dtype)

    def put_heads_t(ref, val, n_heads):
        val_t = val.T
        for bi in range(nb):
            for hd in range(n_heads):
                blk = val_t[HEAD_DIM * hd:HEAD_DIM * (hd + 1), bi * tl:(bi + 1) * tl].astype(ref.dtype)
                if stacked_first:
                    for l2 in range(DEPTH):
                        ref[bi, l2, hd] = blk if l2 == layer else jnp.zeros_like(blk)
                else:
                    ref[bi, hd] = blk

    def gate_tile(j):
        c0 = U_TILE * j
        val = _silu(proj(OFF_U + c0, U_TILE))
        for bi in range(nb):
            su_ref[bi, :, c0:c0 + U_TILE] = val[bi * tl:(bi + 1) * tl]

    aq = proj(OFF_AQ, A_WIDTH)
    akv = proj(OFF_AK, 2 * A_KV_WIDTH)
    bq = proj(OFF_BQ, B_WIDTH)
    bk = proj(OFF_BK, B_WIDTH)

    aq = aq * lax.rsqrt(_group_mean_sq(aq, 6, split=False) + EPS) * aqg_ref[...]
    ak = akv[:, 0:A_KV_WIDTH]
    ak = ak * lax.rsqrt(_group_mean_sq(ak, 6, split=False) + EPS) * akg_ref[...]
    gate_tile(0)
    bv = proj(OFF_BV, B_WIDTH)
    if rope:
        aq = _rope(aq, cosa_ref[...], sina_ref[...], 16)
        ak = _rope(ak, cosa_ref[:, 0:A_KV_WIDTH], sina_ref[:, 0:A_KV_WIDTH], 16)
    put_heads(qa_ref, aq * (HEAD_DIM ** -0.5 * LOG2E), A_HEADS)
    put_heads(ka_ref, ak, A_KV_HEADS)
    put_heads_t(vat_ref, akv[:, A_KV_WIDTH:2 * A_KV_WIDTH], A_KV_HEADS)
    if kat_ref is not None:
        put_heads_t(kat_ref, ak, A_KV_HEADS)
    gate_tile(1)
    cqk = proj(OFF_CQ, 2 * C_KW)
    cv = proj(OFF_CV, C_WIDTH)
    lr = proj(OFF_LR, 2 * GATE_RANK).astype(BF16)

    if rope:
        bq = _rope(bq, cosb_ref[...], sinb_ref[...], 8)
        bk = _rope(bk, cosb_ref[...], sinb_ref[...], 8)
    bq = bq * (B_QK_DIM ** -0.5 * LOG2E)
    put_heads(qb_ref, bq, B_HEADS)
    put_heads(kb_ref, bk, B_HEADS)
    if rope:
        bq_t = bq.T
        sq = bq_t * bq_t
        own = bq_t * bk.T
        halves = range(0, B_WIDTH, B_QK_DIM)
        qstat_ref[0] = jnp.concatenate(
            [jnp.sum(sq[r0:r0 + B_QK_DIM], axis=0, keepdims=True) for r0 in halves]
            + [jnp.sum(own[r0:r0 + B_QK_DIM], axis=0, keepdims=True) for r0 in halves], axis=0)
    put_heads_t(vbt_ref, bv, B_HEADS)
    if kbt_ref is not None:
        put_heads_t(kbt_ref, bk, B_HEADS)
    gate_tile(2)

    put_rows(cq_ref, cqk[:, 0:C_KW] * (C_DK ** -0.5))
    put_rows(ck_ref, cqk[:, C_KW:2 * C_KW])
    put_rows(cv_ref, cv)
    put_rows(gf_ref, _log_sigmoid(_dot(lr, cwf_ref[...]) + cbf_ref[...]) * (1.0 / GLA_TAU))
    put_rows(gb_ref, _log_sigmoid(_dot(lr, cwb_ref[...]) + cbb_ref[...]) * (1.0 / GLA_TAU))
    gate_tile(3)


def _in_projection(x, mod, per_batch_mod, layer, wts, rope_tabs, nb, tl, kv_prev=None):
    bsz, seq, _ = x.shape
    rope = rope_tabs is not None
    stacked = not rope
    stacked_first = stacked and kv_prev is None
    grid = (bsz // nb, seq // tl)

    def per_layer(shape):
        return pl.BlockSpec((None,) + shape, lambda b, t: (layer,) + (0,) * len(shape))

    mod_idx = (lambda b, t: (layer, b, 0, 0)) if per_batch_mod else (lambda b, t: (layer, 0, 0, 0))
    in_specs = [
        pl.BlockSpec((nb, tl, D_MODEL), lambda b, t: (b, t, 0)),
        pl.BlockSpec((None, None, 1, 3 * D_MODEL), mod_idx),
        per_layer((1, D_MODEL)),
        per_layer((IN_WIDTH, D_MODEL)),
        per_layer((1, A_WIDTH)),
        per_layer((1, A_KV_WIDTH)),
        per_layer((2 * GATE_RANK, C_KW)),
        per_layer((1, C_KW)),
        per_layer((2 * GATE_RANK, C_KW)),
        per_layer((1, C_KW)),
    ]
    args = [x, mod, wts["g_pre"], wts["w_in_t"], wts["aq_gain"], wts["ak_gain"],
            wts["cw_f"], wts["cb_f"], wts["cw_b"], wts["cb_b"]]
    if rope:
        assert nb == 1
        in_specs += [
            pl.BlockSpec((tl, A_WIDTH), lambda b, t: (t, 0)),
            pl.BlockSpec((tl, A_WIDTH), lambda b, t: (t, 0)),
            pl.BlockSpec((tl, B_WIDTH), lambda b, t: (t, 0)),
            pl.BlockSpec((tl, B_WIDTH), lambda b, t: (t, 0)),
        ]
        args += list(rope_tabs)

    def heads(n):
        return pl.BlockSpec((nb, n, tl, HEAD_DIM), lambda b, t: (b, 0, t, 0))

    def heads_t(n):
        if stacked_first:
            return pl.BlockSpec((nb, DEPTH, n, HEAD_DIM, tl), lambda b, t: (b, 0, 0, 0, t))
        if stacked:
            return pl.BlockSpec((nb, None, n, HEAD_DIM, tl), lambda b, t: (b, layer, 0, 0, t))
        return pl.BlockSpec((nb, n, HEAD_DIM, tl), lambda b, t: (b, 0, 0, t))

    def rows(width):
        return pl.BlockSpec((nb, tl, width), lambda b, t: (b, t, 0))

    def hshape(n):
        return jax.ShapeDtypeStruct((bsz, n, seq, HEAD_DIM), BF16)

    def tshape(n):
        if stacked:
            return jax.ShapeDtypeStruct((bsz, DEPTH, n, HEAD_DIM, seq), F32)
        return jax.ShapeDtypeStruct((bsz, n, HEAD_DIM, seq), BF16)

    def rshape(width):
        return jax.ShapeDtypeStruct((bsz, seq, width), F32)

    out_specs = [heads(A_HEADS), heads(A_KV_HEADS), heads(B_HEADS), heads(B_HEADS),
                 rows(C_KW), rows(C_KW), rows(C_WIDTH), rows(C_KW), rows(C_KW), rows(D_MIX),
                 heads_t(A_KV_HEADS), heads_t(B_HEADS)]
    out_shape = [hshape(A_HEADS), hshape(A_KV_HEADS), hshape(B_HEADS), hshape(B_HEADS),
                 rshape(C_KW), rshape(C_KW), rshape(C_WIDTH), rshape(C_KW), rshape(C_KW),
                 rshape(D_MIX), tshape(A_KV_HEADS), tshape(B_HEADS)]
    if stacked:
        out_specs += [heads_t(A_KV_HEADS), heads_t(B_HEADS)]
        out_shape += [tshape(A_KV_HEADS), tshape(B_HEADS)]
    else:
        out_specs += [pl.BlockSpec((nb, QSTAT_ROWS, tl), lambda b, t: (b, 0, t))]
        out_shape += [jax.ShapeDtypeStruct((bsz, QSTAT_ROWS, seq), F32)]
    aliases = {}
    if kv_prev is not None:
        for j, buf in enumerate(kv_prev):
            aliases[len(args)] = 10 + j
            in_specs.append(pl.BlockSpec(memory_space=pl.ANY))
            args.append(buf)
    return pl.pallas_call(
        functools.partial(_in_kernel, rope, layer, stacked_first, nb, tl, len(aliases)),
        grid=grid,
        in_specs=in_specs,
        out_specs=out_specs,
        out_shape=out_shape,
        input_output_aliases=aliases,
        compiler_params=pltpu.CompilerParams(
            dimension_semantics=("arbitrary", "arbitrary"), vmem_limit_bytes=VMEM_LIMIT),
        name="in_projection_rope" if rope else "in_projection",
    )(*args)


def _gla_bidirectional(cq_ref, ck_ref, cv_ref, gf_ref, gb_ref, s_f, s_b, seq, oc_ref):
    bl = GLA_BLOCK
    n_sub = bl // CHUNK
    nblk = seq // bl
    ri = _iota((bl, bl), 0)
    ci = _iota((bl, bl), 1)
    same_chunk = lax.shift_right_logical(ri, 6) == lax.shift_right_logical(ci, 6)
    bd = (lax.shift_right_logical(_iota((C_WIDTH, C_KW), 0), 6)
          == lax.shift_right_logical(_iota((C_WIDTH, C_KW), 1), 5))
    khead = lax.shift_right_logical(_iota((1, C_KW), 1), 5)
    vhead = lax.shift_right_logical(_iota((1, C_WIDTH), 1), 6)
    scans = []
    for reverse, g_ref in ((False, gf_ref), (True, gb_ref)):
        causal = same_chunk & ((ci >= ri) if reverse else (ci <= ri))
        scans.append((reverse, g_ref, causal, jnp.where(causal, 1.0, 0.0).astype(BF16)))
    states = [s_f, s_b]
    written = set()

    for step in range(nblk):
        rows0 = [step * bl, (nblk - 1 - step) * bl]
        cums = []
        for (reverse, g_ref, causal, tri), r0 in zip(scans, rows0):
            g_hi, g_lo = _split_bf16(g_ref[r0:r0 + bl, :])
            cums.append(_dot(tri, g_hi) + _dot(tri, g_lo))
        prep = []
        for (reverse, g_ref, causal, tri), r0, cum in zip(scans, rows0, cums):
            q = cq_ref[r0:r0 + bl, :]
            k = ck_ref[r0:r0 + bl, :]
            v = cv_ref[r0:r0 + bl, :]
            qt = q * jnp.exp(cum)
            ktb = (k * jnp.exp(-cum)).astype(BF16)
            vb = v.astype(BF16)
            lasts, kdecs = [], []
            for c in range(n_sub):
                c0 = CHUNK * c
                edge = c0 if reverse else c0 + CHUNK - 1
                last = cum[edge:edge + 1, :]
                lasts.append(last)
                kdecs.append((k[c0:c0 + CHUNK] * jnp.exp(last - cum[c0:c0 + CHUNK])).astype(BF16))
            prep.append((qt, ktb, v, vb, lasts, kdecs))
        scores, incs = [], []
        for qt, ktb, v, vb, lasts, kdecs in prep:
            scores.append([_dot_nt(jnp.where(khead == hd, qt, 0.0).astype(BF16), ktb)
                           for hd in range(C_HEADS)])
            incs.append([_dot_tn(vb[CHUNK * c:CHUNK * (c + 1)], kdecs[c]) for c in range(n_sub)])
        probs, entering = [], []
        for si, ((reverse, g_ref, causal, tri), (qt, ktb, v, vb, lasts, kdecs)) in enumerate(
                zip(scans, prep)):
            probs.append([jnp.where(causal, s, 0.0).astype(BF16) for s in scores[si]])
            s_t = states[si]
            before = [None] * n_sub
            subs = range(n_sub)
            for c in (reversed(subs) if reverse else subs):
                before[c] = s_t.astype(BF16)
                s_t = jnp.exp(lasts[c]) * s_t + jnp.where(bd, incs[si][c], 0.0)
            states[si] = s_t
            entering.append(before)
        outs = []
        for si, (qt, ktb, v, vb, lasts, kdecs) in enumerate(prep):
            qtb = qt.astype(BF16)
            o = jnp.concatenate([_dot_nt(qtb[CHUNK * c:CHUNK * (c + 1)], entering[si][c])
                                 for c in range(n_sub)], axis=0)
            for hd in range(C_HEADS):
                o = o + _dot(probs[si][hd], jnp.where(vhead == hd, v, 0.0).astype(BF16))
            outs.append(o)
        if rows0[0] == rows0[1]:
            outs, rows0 = [outs[0] + outs[1]], rows0[:1]
        for o, r0 in zip(outs, rows0):
            if r0 in written:
                oc_ref[r0:r0 + bl, :] = oc_ref[r0:r0 + bl, :] + o
            else:
                oc_ref[r0:r0 + bl, :] = o
                written.add(r0)
    return states[0], states[1]


def _attend_t(jobs, shifts=None):
    def scores(i):
        return _dot_nt(jobs[i][0][...], jobs[i][1])

    outs = []
    pending = [scores(i) for i in range(min(SCORE_LOOKAHEAD, len(jobs)))]
    for i, (_, _, vt) in enumerate(jobs):
        st = pending.pop(0)
        if i + SCORE_LOOKAHEAD < len(jobs):
            pending.append(scores(i + SCORE_LOOKAHEAD))
        m = jnp.max(st, axis=0, keepdims=True) if shifts is None else shifts[i]
        p = jnp.exp2(st - m).astype(BF16)
        ot = _dot(vt[...], p)
        outs.append(ot[0:HEAD_DIM] * (1.0 / ot[HEAD_DIM:HEAD_DIM + 1]))
    return outs


def _pair_rows(a, b):
    return jnp.concatenate([a, b], axis=0).T


def _state_to_blockdiag_t(s_ref):
    rows = []
    for hd in range(C_HEADS):
        pieces = []
        if hd:
            pieces.append(jnp.zeros((C_DK, C_DV * hd), F32))
        pieces.append(s_ref[hd])
        if hd < C_HEADS - 1:
            pieces.append(jnp.zeros((C_DK, C_DV * (C_HEADS - 1 - hd)), F32))
        rows.append(jnp.concatenate(pieces, axis=-1))
    return jnp.concatenate(rows, axis=0).T


def _blockdiag_t_to_state(s_t, out_ref):
    s = s_t.T
    for hd in range(C_HEADS):
        out_ref[hd] = s[C_DK * hd:C_DK * (hd + 1), C_DV * hd:C_DV * (hd + 1)]


def _mix_sequence(cached, layer, first_state, n_aliased, lam_init, seq, qt, qb, *refs):
    it = iter(refs)
    x_ref, mod_ref = next(it), next(it)
    qa_ref, ka_ref, qb_ref, kb_ref = (next(it) for _ in range(4))
    cq_ref, ck_ref, cv_ref, gf_ref, gb_ref, su_ref = (next(it) for _ in range(6))
    vat_ref, vbt_ref = next(it), next(it)
    if cached:
        cakt_ref, cavt_ref, cbkt_ref, cbvt_ref, s0f_ref, s0b_ref = (next(it) for _ in range(6))
        qstat_ref, aqg_ref = next(it), next(it)
    wout_ref, gpost_ref, bog_ref, cog_ref, lamp_ref = (next(it) for _ in range(5))
    for _ in range(n_aliased):
        next(it)
    y_ref = next(it)
    if not cached:
        sf_ref, sb_ref = next(it), next(it)
    kA_s, vtA_s, kB_s, vtB_s, oc_s, mixed_s, kn2_s = (next(it) for _ in range(7))

    lk = kA_s.shape[1]
    past = lk - seq
    t = pl.program_id(1)

    def once_per_sequence(body):
        return body() if seq == qt else pl.when(t == 0)(body)

    @once_per_sequence
    def _per_sequence():
        ones_row = jnp.where(_iota((VT_ROWS - HEAD_DIM, lk), 0) == 0, 1.0, 0.0).astype(BF16)
        for k_new, kt_cache, k_dst, vt_new, vt_cache, vt_dst in (
                (ka_ref, cakt_ref if cached else None, kA_s, vat_ref, cavt_ref if cached else None, vtA_s),
                (kb_ref, cbkt_ref if cached else None, kB_s, vbt_ref, cbvt_ref if cached else None, vtB_s)):
            n_heads = k_dst.shape[0]
            if cached:
                for h0 in range(0, n_heads, 2):
                    pair = jnp.concatenate([kt_cache[h0], kt_cache[h0 + 1]], axis=0).T
                    k_dst[h0, 0:past, :] = pair[:, 0:HEAD_DIM].astype(BF16)
                    k_dst[h0 + 1, 0:past, :] = pair[:, HEAD_DIM:2 * HEAD_DIM].astype(BF16)
            for hd in range(n_heads):
                k_dst[hd, past:lk, :] = k_new[hd]
                if cached:
                    vt_dst[hd, 0:HEAD_DIM, 0:past] = vt_cache[hd].astype(BF16)
                vt_dst[hd, 0:HEAD_DIM, past:lk] = vt_new[hd].astype(BF16)
                vt_dst[hd, HEAD_DIM:VT_ROWS, :] = ones_row
        bounded_jobs = ([(kA_s, g) for g in range(A_KV_HEADS)] + [(kB_s, h) for h in range(B_HEADS)]
                        if lk > KEY_TILE else [])
        for j, (k_dst, hd) in enumerate(bounded_jobs):
            kf = k_dst[hd].astype(F32)
            kn2 = jnp.max(jnp.sum(kf * kf, axis=-1, keepdims=True), axis=0, keepdims=True)
            kn2_s[j:j + 1, :] = jnp.broadcast_to(kn2, (1, LANES))

        if cached:
            s0f, s0b = _state_to_blockdiag_t(s0f_ref), _state_to_blockdiag_t(s0b_ref)
        else:
            s0f = jnp.zeros((C_WIDTH, C_KW), F32)
            s0b = s0f
        s_f, s_b = _gla_bidirectional(cq_ref, ck_ref, cv_ref, gf_ref, gb_ref, s0f, s0b, seq, oc_s)
        if not cached:
            for ref, s_t in ((sf_ref, s_f), (sb_ref, s_b)):
                if first_state:
                    for l2 in range(DEPTH):
                        if l2 == layer:
                            _blockdiag_t_to_state(s_t, ref.at[l2])
                        else:
                            ref[l2] = jnp.zeros(ref.shape[1:], F32)
                else:
                    _blockdiag_t_to_state(s_t, ref)
        for r0 in range(0, seq, GLA_BLOCK):
            oc = oc_s[r0:r0 + GLA_BLOCK, :]
            oc_s[r0:r0 + GLA_BLOCK, :] = oc * lax.rsqrt(_group_mean_sq(oc, 6) + EPS) * cog_ref[...]

    lam = (jnp.exp(jnp.sum(lamp_ref[0:1, :] * lamp_ref[1:2, :], axis=-1, keepdims=True))
           - jnp.exp(jnp.sum(lamp_ref[2:3, :] * lamp_ref[3:4, :], axis=-1, keepdims=True))
           + lam_init)

    def project_out():
        seq_rows = pl.ds(pl.multiple_of(t * qt, qt), qt)
        mixed_s[:, A_WIDTH + B_WIDTH:D_MIX] = oc_s[seq_rows, :]
        gate = mod_ref[:, 2 * D_MODEL:3 * D_MODEL]
        mixed = (mixed_s[...] * su_ref[...]).astype(BF16)
        y = _dot(mixed, wout_ref[...])
        yn = y * lax.rsqrt(jnp.mean(y * y, axis=-1, keepdims=True) + EPS) * gpost_ref[...]
        y_ref[...] = x_ref[...] + gate * yn

    def attn_block(i, carry):
        rows = pl.ds(i * qb, qb) if isinstance(i, int) else pl.ds(pl.multiple_of(i * qb, qb), qb)
        jobs = []
        for grp in range(A_KV_HEADS):
            q4 = qa_ref[A_GROUP * grp:A_GROUP * (grp + 1), rows, :].reshape(A_GROUP * qb, HEAD_DIM)
            jobs.append((kA_s.at[grp], q4, vtA_s.at[grp]))
        lane = _iota((qb, HEAD_DIM), 1)
        for hd in range(B_HEADS):
            q = qb_ref[hd, rows, :]
            zero = jnp.zeros_like(q)
            q2 = jnp.concatenate([jnp.where(lane < B_QK_DIM, q, zero),
                                  jnp.where(lane >= B_QK_DIM, q, zero)], axis=0)
            jobs.append((kB_s.at[hd], q2, vtB_s.at[hd]))

        def finish(outs):
            for grp in range(A_KV_HEADS):
                ot = outs[grp]
                for pair in range(A_GROUP // 2):
                    c0 = 2 * pair * qb
                    col = A_GROUP * HEAD_DIM * grp + 2 * HEAD_DIM * pair
                    mixed_s[rows, col:col + 2 * HEAD_DIM] = _pair_rows(ot[:, c0:c0 + qb],
                                                                       ot[:, c0 + qb:c0 + 2 * qb])
            obs = []
            for hd in range(B_HEADS):
                ot = outs[A_KV_HEADS + hd]
                ob = ot[:, 0:qb] - lam * ot[:, qb:2 * qb]
                obs.append(ob * lax.rsqrt(jnp.mean(ob * ob, axis=0, keepdims=True) + EPS))
            for pair in range(B_HEADS // 2):
                col = A_WIDTH + 2 * B_V_DIM * pair
                mixed_s[rows, col:col + 2 * B_V_DIM] = (_pair_rows(obs[2 * pair], obs[2 * pair + 1])
                                                        * bog_ref[...] * (1.0 - lam_init))
            if qt == qb:
                project_out()

        if lk <= KEY_TILE:
            finish(_attend_t(jobs))
            return carry

        qa_norm = (jnp.max(jnp.abs(aqg_ref[...]), axis=-1, keepdims=True)
                   * (HEAD_DIM ** 0.5 * HEAD_DIM ** -0.5 * LOG2E))
        stat_t = qstat_ref[...]
        shifts, gaps = [], []
        for j, (_, q, _) in enumerate(jobs):
            k_norm = jnp.sqrt(kn2_s[j:j + 1, 0:1])
            if j < A_KV_HEADS:
                upper = qa_norm * k_norm * BOUND_SLACK
                shifts.append(jnp.broadcast_to(upper, (1, q.shape[0])))
                gaps.append(2.0 * upper)
            else:
                r0 = 2 * (j - A_KV_HEADS)
                r1 = r0 + 2 * B_HEADS
                qn2 = jnp.concatenate([stat_t[r0:r0 + 1], stat_t[r0 + 1:r0 + 2]], axis=-1)
                lower = jnp.concatenate([stat_t[r1:r1 + 1], stat_t[r1 + 1:r1 + 2]], axis=-1)
                upper = jnp.sqrt(qn2) * k_norm * BOUND_SLACK
                shifts.append(upper)
                gaps.append(jnp.max(upper - lower, axis=-1, keepdims=True))
        worst = functools.reduce(jnp.maximum, gaps)
        safe = worst[0, 0] <= SAFE_GAP

        @pl.when(safe)
        def _bounded():
            finish(_attend_t(jobs, shifts))

        @pl.when(jnp.logical_not(safe))
        def _exact_max():
            finish(_attend_t(jobs))

        return carry

    if qt != qb:
        lax.fori_loop(0, qt // qb, attn_block, 0)
        project_out()
    else:
        attn_block(0, 0)


def _mix_kernel(nseq, *params_and_refs):
    params, refs = params_and_refs[:N_MIX_PARAMS], params_and_refs[N_MIX_PARAMS:]
    if nseq == 1:
        return _mix_sequence(*params, *refs)
    n_aliased = params[3]
    shared = {1} | set(range(N_MIX_SEQ_INPUTS, N_MIX_SEQ_INPUTS + N_MIX_WEIGHTS + n_aliased))
    for bi in range(nseq):
        _mix_sequence(*params, *[r if idx in shared else r.at[bi] for idx, r in enumerate(refs)])


def _mixer(x, mod, per_batch_mod, layer, proj, wts, lam_init, cache, state_prev=None):
    bsz, seq, _ = x.shape
    cached = cache is not None
    past = cache[0].shape[4] if cached else 0
    qa, ka, qb, kb, cq, ck, cv, gf, gb, su, vat, vbt = proj[:12]
    qt = 256
    qb_rows = 256
    nseq = 1 if cached else 4
    lead = None if nseq == 1 else nseq

    def per_layer(shape):
        return pl.BlockSpec((None,) + shape, lambda b, t: (layer,) + (0,) * len(shape))

    def layer_heads_t(n, length):
        return pl.BlockSpec((lead, None, n, HEAD_DIM, length), lambda b, t: (b, layer, 0, 0, 0))

    def heads_t(n):
        if cached:
            return pl.BlockSpec((lead, n, HEAD_DIM, seq), lambda b, t: (b, 0, 0, 0))
        return layer_heads_t(n, seq)

    def heads(n):
        return pl.BlockSpec((lead, n, seq, HEAD_DIM), lambda b, t: (b, 0, 0, 0))

    def head_tile(n):
        return pl.BlockSpec((lead, n, qt, HEAD_DIM), lambda b, t: (b, 0, t, 0))

    def rows(width):
        return pl.BlockSpec((lead, seq, width), lambda b, t: (b, 0, 0))

    def row_tile(width):
        return pl.BlockSpec((lead, qt, width), lambda b, t: (b, t, 0))

    mod_idx = (lambda b, t: (layer, b, 0, 0)) if per_batch_mod else (lambda b, t: (layer, 0, 0, 0))
    in_specs = [row_tile(D_MODEL), pl.BlockSpec((None, None, 1, 3 * D_MODEL), mod_idx),
                head_tile(A_HEADS), heads(A_KV_HEADS), head_tile(B_HEADS), heads(B_HEADS),
                rows(C_KW), rows(C_KW), rows(C_WIDTH), rows(C_KW), rows(C_KW), row_tile(D_MIX),
                heads_t(A_KV_HEADS), heads_t(B_HEADS)]
    args = [x, mod, qa, ka, qb, kb, cq, ck, cv, gf, gb, su, vat, vbt]
    if cached:
        state_in = pl.BlockSpec((None, None, C_HEADS, C_DK, C_DV), lambda b, t: (b, layer, 0, 0, 0))
        in_specs += [layer_heads_t(A_KV_HEADS, past), layer_heads_t(A_KV_HEADS, past),
                     layer_heads_t(B_HEADS, past), layer_heads_t(B_HEADS, past), state_in, state_in,
                     pl.BlockSpec((None, QSTAT_ROWS, qt), lambda b, t: (b, 0, t)),
                     per_layer((1, A_WIDTH))]
        args += list(cache) + [proj[12], wts["aq_gain"]]
    in_specs += [per_layer((D_MIX, D_MODEL)), per_layer((1, D_MODEL)), per_layer((1, 2 * B_V_DIM)),
                 per_layer((1, C_WIDTH)), per_layer((4, B_QK_DIM))]
    args += [wts["w_out"], wts["g_post"], wts["b_out_gain"], wts["c_out_gain"], wts["lam_params"]]

    out_specs = [row_tile(D_MODEL)]
    out_shape = [jax.ShapeDtypeStruct((bsz, seq, D_MODEL), F32)]
    aliases = {}
    first_state = not cached and state_prev is None
    if not cached:
        if first_state:
            state_out = pl.BlockSpec((lead, DEPTH, C_HEADS, C_DK, C_DV), lambda b, t: (b, 0, 0, 0, 0))
        else:
            state_out = pl.BlockSpec((lead, None, C_HEADS, C_DK, C_DV),
                                     lambda b, t: (b, layer, 0, 0, 0))
            for j, buf in enumerate(state_prev):
                aliases[len(args)] = 1 + j
                in_specs.append(pl.BlockSpec(memory_space=pl.ANY))
                args.append(buf)
        out_specs += [state_out, state_out]
        out_shape += [jax.ShapeDtypeStruct((bsz, DEPTH, C_HEADS, C_DK, C_DV), F32)] * 2

    lk = past + seq
    per_seq = () if nseq == 1 else (nseq,)
    scratch = [pltpu.VMEM(per_seq + shape, dtype) for shape, dtype in (
        ((A_KV_HEADS, lk, HEAD_DIM), BF16), ((A_KV_HEADS, VT_ROWS, lk), BF16),
        ((B_HEADS, lk, HEAD_DIM), BF16), ((B_HEADS, VT_ROWS, lk), BF16),
        ((seq, C_WIDTH), F32), ((qt, D_MIX), F32), ((8, LANES), F32))]
    return pl.pallas_call(
        functools.partial(_mix_kernel, nseq, cached, layer, first_state, len(aliases), lam_init, seq,
                          qt, qb_rows),
        grid=(bsz // nseq, seq // qt),
        in_specs=in_specs,
        out_specs=out_specs,
        out_shape=out_shape,
        scratch_shapes=scratch,
        input_output_aliases=aliases,
        compiler_params=pltpu.CompilerParams(
            dimension_semantics=("arbitrary", "arbitrary"), vmem_limit_bytes=VMEM_LIMIT),
        name="mixer_cached" if cached else "mixer",
    )(*args)


def _rope_tables(seq):
    t = jnp.arange(seq)
    pos_row = (t // GRID_W).astype(F32)
    pos_col = (t % GRID_W).astype(F32)

    def tables(half, width):
        freq = ROPE_THETA ** (-jnp.arange(half, dtype=F32) / half)
        ang_r = pos_row[:, None] * freq[None, :]
        ang_c = pos_col[:, None] * freq[None, :]
        cos = jnp.concatenate([jnp.cos(ang_r), jnp.cos(ang_r), jnp.cos(ang_c), jnp.cos(ang_c)], axis=-1)
        sin = jnp.concatenate([-jnp.sin(ang_r), jnp.sin(ang_r), -jnp.sin(ang_c), jnp.sin(ang_c)], axis=-1)
        reps = width // (4 * half)
        return jnp.tile(cos, (1, reps)), jnp.tile(sin, (1, reps))

    cos_a, sin_a = tables(HEAD_DIM // 4, A_WIDTH)
    cos_b, sin_b = tables(B_QK_DIM // 4, B_WIDTH)
    return cos_a, sin_a, cos_b, sin_b


def _prepare_weights(g_pre, g_post, w_in, w_out, a_q_gain, a_k_gain, b_lambda_q1, b_lambda_k1,
                     b_lambda_q2, b_lambda_k2, b_out_gain, c_gate_w_fwd, c_gate_b_fwd, c_gate_w_bwd,
                     c_gate_b_bwd, c_out_gain):
    w_in_t = jnp.swapaxes(w_in, 1, 2).astype(BF16)
    pad = jnp.zeros((DEPTH, GATE_RANK, C_KW), F32)
    cw_f = jnp.concatenate([c_gate_w_fwd, pad], axis=1).astype(BF16)
    cw_b = jnp.concatenate([pad, c_gate_w_bwd], axis=1).astype(BF16)
    return {
        "g_pre": g_pre[:, None, :],
        "g_post": g_post[:, None, :],
        "w_in_t": w_in_t,
        "w_out": w_out.astype(BF16),
        "aq_gain": jnp.tile(a_q_gain, (1, A_HEADS))[:, None, :],
        "ak_gain": jnp.tile(a_k_gain, (1, A_KV_HEADS))[:, None, :],
        "cw_f": cw_f,
        "cb_f": c_gate_b_fwd[:, None, :],
        "cw_b": cw_b,
        "cb_b": c_gate_b_bwd[:, None, :],
        "b_out_gain": jnp.tile(b_out_gain, (1, 2))[:, None, :],
        "c_out_gain": jnp.tile(c_out_gain, (1, C_HEADS))[:, None, :],
        "lam_params": jnp.stack([b_lambda_q1, b_lambda_k1, b_lambda_q2, b_lambda_k2], axis=1),
    }


def kernel(x_prompt, x_sample, c, cache_a_k, cache_a_v, cache_b_k, cache_b_v, state_c_fwd, state_c_bwd, c_ctx, w_mod, b_mod, g_pre, g_post, w_in, w_out, a_q_gain, a_k_gain, b_lambda_q1, b_lambda_k1, b_lambda_q2, b_lambda_k2, b_out_gain, c_gate_w_fwd, c_gate_b_fwd, c_gate_w_bwd, c_gate_b_bwd, c_out_gain):
    dec_batch = x_sample.shape[0]
    dec_seq = x_sample.shape[1]

    mod_rows = 16
    cvec = jnp.zeros((mod_rows, D_MODEL), F32).at[0:dec_batch].set(c).at[dec_batch].set(c_ctx)
    mod = _modulation(cvec, w_mod, b_mod)[:, :, None, :]
    mod_lat = mod[:, 0:dec_batch]
    mod_ctx = mod[:, dec_batch:dec_batch + 1]

    wts = _prepare_weights(g_pre, g_post, w_in, w_out, a_q_gain, a_k_gain, b_lambda_q1, b_lambda_k1,
                           b_lambda_q2, b_lambda_k2, b_out_gain, c_gate_w_fwd, c_gate_b_fwd,
                           c_gate_w_bwd, c_gate_b_bwd, c_out_gain)
    rope_tabs = _rope_tables(dec_seq)
    cache = tuple(jnp.swapaxes(a, -1, -2) for a in (cache_a_k, cache_a_v, cache_b_k, cache_b_v))
    cache += (state_c_fwd, state_c_bwd)

    y_p, y_s = x_prompt, x_sample
    kv_ctx = None
    states = None
    for l in range(DEPTH):
        lam_init = 0.8 - 0.6 * math.exp(-0.3 * l)
        proj_p = _in_projection(y_p, mod_ctx, False, l, wts, None, 4, x_prompt.shape[1], kv_ctx)
        kv_ctx = proj_p[10:14]
        y_p, *states = _mixer(y_p, mod_ctx, False, l, proj_p, wts, lam_init, None, states)

        proj_s = _in_projection(y_s, mod_lat, True, l, wts, rope_tabs, 1, 1024)
        (y_s,) = _mixer(y_s, mod_lat, True, l, proj_s, wts, lam_init, cache)

    va_t, vb_t, ka_t, kb_t = kv_ctx
    new_kv = [jnp.swapaxes(a, -1, -2) for a in (ka_t, va_t, kb_t, vb_t)]
    return (y_p, y_s, *new_kv, *states)
```

```python
import functools
import math

import jax
import jax.numpy as jnp
from jax import lax
from jax.experimental import pallas as pl
from jax.experimental.pallas import tpu as pltpu

F32 = jnp.float32
BF16 = jnp.bfloat16

D_MODEL = 1024
DEPTH = 2
GRID_W = 64
HEAD_DIM = 64
A_HEADS = 8
A_KV_HEADS = 2
A_GROUP = A_HEADS // A_KV_HEADS
A_WIDTH = A_HEADS * HEAD_DIM
A_KV_WIDTH = A_KV_HEADS * HEAD_DIM
B_HEADS = 4
B_QK_DIM = 32
B_V_DIM = 64
B_WIDTH = B_HEADS * B_V_DIM
C_HEADS = 4
C_DK = 32
C_DV = 64
C_KW = C_HEADS * C_DK
C_WIDTH = C_HEADS * C_DV
GATE_RANK = 16
GLA_TAU = 16.0
CHUNK = 64
D_MIX = A_WIDTH + B_WIDTH + C_WIDTH
ROPE_THETA = 10000.0
EPS = 1e-6

LANES = 128
GLA_BLOCK = 256
VT_ROWS = HEAD_DIM + 16
SCORE_LOOKAHEAD = 6
SAFE_GAP = 96.0
BOUND_SLACK = 1.02
QSTAT_ROWS = 4 * B_HEADS
U_TILE = D_MIX // 4
N_MIX_PARAMS = 8
N_MIX_SEQ_INPUTS = 14
N_MIX_WEIGHTS = 5
KEY_TILE = 256
LOG2E = math.log2(math.e)

OFF_AQ = 0
OFF_AK = OFF_AQ + A_WIDTH
OFF_AV = OFF_AK + A_KV_WIDTH
OFF_BQ = OFF_AV + A_KV_WIDTH
OFF_BK = OFF_BQ + B_WIDTH
OFF_BV = OFF_BK + B_WIDTH
OFF_CQ = OFF_BV + B_WIDTH
OFF_CK = OFF_CQ + C_KW
OFF_CV = OFF_CK + C_KW
OFF_LR = OFF_CV + C_WIDTH
OFF_U = OFF_LR + 2 * GATE_RANK
IN_WIDTH = OFF_U + D_MIX

VMEM_LIMIT = 60 * 1024 * 1024


def _dot(a, b):
    return jnp.dot(a, b, preferred_element_type=F32)


def _dot_nt(a, b):
    return lax.dot_general(a, b, (((1,), (1,)), ((), ())), preferred_element_type=F32)


def _dot_tn(a, b):
    return lax.dot_general(a, b, (((0,), (0,)), ((), ())), preferred_element_type=F32)


def _split_bf16(x):
    hi = x.astype(BF16)
    lo = (x - hi.astype(F32)).astype(BF16)
    return hi, lo


def _iota(shape, dim):
    return lax.broadcasted_iota(jnp.int32, shape, dim)


def _group_mean_sq(x, group_log2, split=True):
    width = x.shape[-1]
    r = lax.shift_right_logical(_iota((LANES, LANES), 0), group_log2)
    c = lax.shift_right_logical(_iota((LANES, LANES), 1), group_log2)
    ones = jnp.where(r == c, 1.0, 0.0).astype(BF16)
    if split:
        hi, lo = _split_bf16(x * x)
    else:
        hi, lo = (x * x).astype(BF16), None
    cols = []
    for j in range(width // LANES):
        sl = slice(LANES * j, LANES * (j + 1))
        cols.append(_dot(hi[:, sl], ones) + (_dot(lo[:, sl], ones) if split else 0.0))
    ss = cols[0] if len(cols) == 1 else jnp.concatenate(cols, axis=-1)
    return ss * (1.0 / (1 << group_log2))


def _log_sigmoid(x):
    return jnp.minimum(x, 0.0) - jnp.log1p(jnp.exp(-jnp.abs(x)))


def _silu(x):
    return x * (1.0 / (1.0 + jnp.exp(-x)))


def _mod_kernel(c_ref, w_ref, b_ref, o_ref):
    a = _silu(c_ref[...]).astype(BF16)
    o_ref[...] = _dot(a, w_ref[...].astype(BF16)) + b_ref[...]


def _modulation(cvec, w_mod, b_mod):
    rows = cvec.shape[0]
    nblk = 3
    return pl.pallas_call(
        _mod_kernel,
        grid=(DEPTH, nblk),
        in_specs=[
            pl.BlockSpec((rows, D_MODEL), lambda l, n: (0, 0)),
            pl.BlockSpec((None, D_MODEL, D_MODEL), lambda l, n: (l, 0, n)),
            pl.BlockSpec((None, 1, D_MODEL), lambda l, n: (l, 0, n)),
        ],
        out_specs=pl.BlockSpec((None, rows, D_MODEL), lambda l, n: (l, 0, n)),
        out_shape=jax.ShapeDtypeStruct((DEPTH, rows, 3 * D_MODEL), F32),
        compiler_params=pltpu.CompilerParams(
            dimension_semantics=("arbitrary", "arbitrary"), vmem_limit_bytes=VMEM_LIMIT),
        name="modulation",
    )(cvec, w_mod, b_mod.reshape(DEPTH, 1, 3 * D_MODEL))


def _in_kernel(rope, layer, stacked_first, nb, tl, n_aliased, *refs):
    (x_ref, mod_ref, gpre_ref, wt_ref, aqg_ref, akg_ref, cwf_ref, cbf_ref, cwb_ref, cbb_ref) = refs[:10]
    refs = refs[10:]
    if rope:
        cosa_ref, sina_ref, cosb_ref, sinb_ref = refs[:4]
        refs = refs[4:]
    refs = refs[n_aliased:]
    (qa_ref, ka_ref, qb_ref, kb_ref, cq_ref, ck_ref, cv_ref, gf_ref, gb_ref, su_ref,
     vat_ref, vbt_ref) = refs[:12]
    kat_ref, kbt_ref = (None, None) if rope else refs[12:]
    qstat_ref = refs[12] if rope else None

    x = x_ref[...].reshape(nb * tl, D_MODEL)
    shift = mod_ref[:, 0:D_MODEL]
    scale = mod_ref[:, D_MODEL:2 * D_MODEL]
    ms = jnp.mean(x * x, axis=-1, keepdims=True)
    h = (x * lax.rsqrt(ms + EPS)) * gpre_ref[...] * (1.0 + scale) + shift
    hb = h.astype(BF16)

    def proj(off, width):
        return _dot_nt(hb, wt_ref[off:off + width, :])

    def proj_t(off, width):
        return _dot_nt(wt_ref[off:off + width, :], hb)

    def put_rows(ref, val):
        for bi in range(nb):
            ref[bi] = val[bi * tl:(bi + 1) * tl].astype(ref.dtype)

    def put_heads(ref, val, n_heads):
        for bi in range(nb):
            for hd in range(n_heads):
                ref[bi, hd] = val[bi * tl:(bi + 1) * tl,
                                  HEAD_DIM * hd:HEAD_DIM * (hd + 1)].astype(ref.dtype)

    def put_heads_t(ref, heads_t, stacked=False):
        for bi in range(nb):
            for hd, val_t in enumerate(heads_t):
                blk = val_t[:, bi * tl:(bi + 1) * tl].astype(ref.dtype)
                if stacked and stacked_first:
                    for l2 in range(DEPTH):
                        ref[bi, l2, hd] = blk if l2 == layer else jnp.zeros_like(blk)
                else:
                    ref[bi, hd] = blk

    def split_heads_t(val_t, n_heads):
        return [val_t[HEAD_DIM * hd:HEAD_DIM * (hd + 1)] for hd in range(n_heads)]

    def rms_t(head_t, gain_col):
        ms_h = jnp.mean(head_t * head_t, axis=0, keepdims=True)
        return head_t * lax.rsqrt(ms_h + EPS) * gain_col

    def rope_t(head_t, cos_t, sin_t, dist):
        blocks = [head_t[r0:r0 + dist] for r0 in range(0, HEAD_DIM, dist)]
        partner = jnp.concatenate([blocks[j ^ 1] for j in range(len(blocks))], axis=0)
        return head_t * cos_t + partner * sin_t

    def gate_tile(j):
        c0 = U_TILE * j
        val = _silu(proj(OFF_U + c0, U_TILE))
        for bi in range(nb):
            su_ref[bi, :, c0:c0 + U_TILE] = val[bi * tl:(bi + 1) * tl]

    za_t = proj_t(OFF_AQ, A_WIDTH + 2 * A_KV_WIDTH)
    zb_t = proj_t(OFF_BQ, 3 * B_WIDTH)
    lr_t = proj_t(OFF_LR, 2 * GATE_RANK).astype(BF16)
    gate_tile(0)

    aq_t = [rms_t(h_t, aqg_ref[...]) for h_t in split_heads_t(za_t[0:A_WIDTH], A_HEADS)]
    ak_t = [rms_t(h_t, akg_ref[...])
            for h_t in split_heads_t(za_t[A_WIDTH:A_WIDTH + A_KV_WIDTH], A_KV_HEADS)]
    if rope:
        aq_t = [rope_t(h_t, cosa_ref[...], sina_ref[...], 16) for h_t in aq_t]
        ak_t = [rope_t(h_t, cosa_ref[...], sina_ref[...], 16) for h_t in ak_t]
    put_heads_t(qa_ref, [h_t * (HEAD_DIM ** -0.5 * LOG2E) for h_t in aq_t])
    put_heads(ka_ref, jnp.concatenate(ak_t, axis=0).T, A_KV_HEADS)
    put_heads_t(vat_ref, split_heads_t(za_t[A_WIDTH + A_KV_WIDTH:], A_KV_HEADS), stacked=not rope)
    if kat_ref is not None:
        put_heads_t(kat_ref, ak_t, stacked=True)
    gate_tile(1)
    cqk = proj(OFF_CQ, 2 * C_KW)
    cv = proj(OFF_CV, C_WIDTH)

    bq_t = split_heads_t(zb_t[0:B_WIDTH], B_HEADS)
    bk_t = split_heads_t(zb_t[B_WIDTH:2 * B_WIDTH], B_HEADS)
    if rope:
        bq_t = [rope_t(h_t, cosb_ref[...], sinb_ref[...], 8) for h_t in bq_t]
        bk_t = [rope_t(h_t, cosb_ref[...], sinb_ref[...], 8) for h_t in bk_t]
    bq_t = [h_t * (B_QK_DIM ** -0.5 * LOG2E) for h_t in bq_t]
    put_heads_t(qb_ref, bq_t)
    put_heads(kb_ref, jnp.concatenate(bk_t, axis=0).T, B_HEADS)
    if rope:
        halves = [(hd, r0) for hd in range(B_HEADS) for r0 in (0, B_QK_DIM)]
        qstat_ref[0] = jnp.concatenate(
            [jnp.sum(bq_t[hd][r0:r0 + B_QK_DIM] * bq_t[hd][r0:r0 + B_QK_DIM], axis=0, keepdims=True)
             for hd, r0 in halves]
            + [jnp.sum(bq_t[hd][r0:r0 + B_QK_DIM] * bk_t[hd][r0:r0 + B_QK_DIM], axis=0, keepdims=True)
               for hd, r0 in halves], axis=0)
    put_heads_t(vbt_ref, split_heads_t(zb_t[2 * B_WIDTH:], B_HEADS), stacked=not rope)
    if kbt_ref is not None:
        put_heads_t(kbt_ref, bk_t, stacked=True)
    gate_tile(2)
    gf_pre = _dot_tn(lr_t, cwf_ref[...])
    gb_pre = _dot_tn(lr_t, cwb_ref[...])

    put_rows(cq_ref, cqk[:, 0:C_KW] * (C_DK ** -0.5))
    put_rows(ck_ref, cqk[:, C_KW:2 * C_KW])
    put_rows(cv_ref, cv)
    put_rows(gf_ref, _log_sigmoid(gf_pre + cbf_ref[...]) * (1.0 / GLA_TAU))
    put_rows(gb_ref, _log_sigmoid(gb_pre + cbb_ref[...]) * (1.0 / GLA_TAU))
    gate_tile(3)


def _in_projection(x, mod, per_batch_mod, layer, wts, rope_tabs, nb, tl, kv_prev=None):
    bsz, seq, _ = x.shape
    rope = rope_tabs is not None
    stacked = not rope
    stacked_first = stacked and kv_prev is None
    grid = (bsz // nb, seq // tl)

    def per_layer(shape):
        return pl.BlockSpec((None,) + shape, lambda b, t: (layer,) + (0,) * len(shape))

    mod_idx = (lambda b, t: (layer, b, 0, 0)) if per_batch_mod else (lambda b, t: (layer, 0, 0, 0))
    in_specs = [
        pl.BlockSpec((nb, tl, D_MODEL), lambda b, t: (b, t, 0)),
        pl.BlockSpec((None, None, 1, 3 * D_MODEL), mod_idx),
        per_layer((1, D_MODEL)),
        per_layer((IN_WIDTH, D_MODEL)),
        per_layer((HEAD_DIM, 1)),
        per_layer((HEAD_DIM, 1)),
        per_layer((2 * GATE_RANK, C_KW)),
        per_layer((1, C_KW)),
        per_layer((2 * GATE_RANK, C_KW)),
        per_layer((1, C_KW)),
    ]
    args = [x, mod, wts["g_pre"], wts["w_in_t"], wts["aq_gain_col"], wts["ak_gain_col"],
            wts["cw_f"], wts["cb_f"], wts["cw_b"], wts["cb_b"]]
    if rope:
        assert nb == 1
        in_specs += [
            pl.BlockSpec((HEAD_DIM, tl), lambda b, t: (0, t)),
            pl.BlockSpec((HEAD_DIM, tl), lambda b, t: (0, t)),
            pl.BlockSpec((HEAD_DIM, tl), lambda b, t: (0, t)),
            pl.BlockSpec((HEAD_DIM, tl), lambda b, t: (0, t)),
        ]
        args += list(rope_tabs)

    def heads(n):
        return pl.BlockSpec((nb, n, tl, HEAD_DIM), lambda b, t: (b, 0, t, 0))

    def heads_t(n):
        if stacked_first:
            return pl.BlockSpec((nb, DEPTH, n, HEAD_DIM, tl), lambda b, t: (b, 0, 0, 0, t))
        if stacked:
            return pl.BlockSpec((nb, None, n, HEAD_DIM, tl), lambda b, t: (b, layer, 0, 0, t))
        return pl.BlockSpec((nb, n, HEAD_DIM, tl), lambda b, t: (b, 0, 0, t))

    def q_heads_t(n):
        return pl.BlockSpec((nb, n, HEAD_DIM, tl), lambda b, t: (b, 0, 0, t))

    def rows(width):
        return pl.BlockSpec((nb, tl, width), lambda b, t: (b, t, 0))

    def hshape(n):
        return jax.ShapeDtypeStruct((bsz, n, seq, HEAD_DIM), BF16)

    def qshape(n):
        return jax.ShapeDtypeStruct((bsz, n, HEAD_DIM, seq), BF16)

    def tshape(n):
        if stacked:
            return jax.ShapeDtypeStruct((bsz, DEPTH, n, HEAD_DIM, seq), F32)
        return jax.ShapeDtypeStruct((bsz, n, HEAD_DIM, seq), BF16)

    def rshape(width):
        return jax.ShapeDtypeStruct((bsz, seq, width), F32)

    out_specs = [q_heads_t(A_HEADS), heads(A_KV_HEADS), q_heads_t(B_HEADS), heads(B_HEADS),
                 rows(C_KW), rows(C_KW), rows(C_WIDTH), rows(C_KW), rows(C_KW), rows(D_MIX),
                 heads_t(A_KV_HEADS), heads_t(B_HEADS)]
    out_shape = [qshape(A_HEADS), hshape(A_KV_HEADS), qshape(B_HEADS), hshape(B_HEADS),
                 rshape(C_KW), rshape(C_KW), rshape(C_WIDTH), rshape(C_KW), rshape(C_KW),
                 rshape(D_MIX), tshape(A_KV_HEADS), tshape(B_HEADS)]
    if stacked:
        out_specs += [heads_t(A_KV_HEADS), heads_t(B_HEADS)]
        out_shape += [tshape(A_KV_HEADS), tshape(B_HEADS)]
    else:
        out_specs += [pl.BlockSpec((nb, QSTAT_ROWS, tl), lambda b, t: (b, 0, t))]
        out_shape += [jax.ShapeDtypeStruct((bsz, QSTAT_ROWS, seq), F32)]
    aliases = {}
    if kv_prev is not None:
        for j, buf in enumerate(kv_prev):
            aliases[len(args)] = 10 + j
            in_specs.append(pl.BlockSpec(memory_space=pl.ANY))
            args.append(buf)
    return pl.pallas_call(
        functools.partial(_in_kernel, rope, layer, stacked_first, nb, tl, len(aliases)),
        grid=grid,
        in_specs=in_specs,
        out_specs=out_specs,
        out_shape=out_shape,
        input_output_aliases=aliases,
        compiler_params=pltpu.CompilerParams(
            dimension_semantics=("arbitrary", "arbitrary"), vmem_limit_bytes=VMEM_LIMIT),
        name="in_projection_rope" if rope else "in_projection",
    )(*args)


def _gla_bidirectional(cq_ref, ck_ref, cv_ref, gf_ref, gb_ref, s_f, s_b, seq, oc_ref):
    bl = GLA_BLOCK
    n_sub = bl // CHUNK
    nblk = seq // bl
    ri = _iota((bl, bl), 0)
    ci = _iota((bl, bl), 1)
    same_chunk = lax.shift_right_logical(ri, 6) == lax.shift_right_logical(ci, 6)
    bd = (lax.shift_right_logical(_iota((C_WIDTH, C_KW), 0), 6)
          == lax.shift_right_logical(_iota((C_WIDTH, C_KW), 1), 5))
    khead = lax.shift_right_logical(_iota((1, C_KW), 1), 5)
    vhead = lax.shift_right_logical(_iota((1, C_WIDTH), 1), 6)
    scans = []
    for reverse, g_ref in ((False, gf_ref), (True, gb_ref)):
        causal = same_chunk & ((ci >= ri) if reverse else (ci <= ri))
        scans.append((reverse, g_ref, causal, jnp.where(causal, 1.0, 0.0).astype(BF16)))
    states = [s_f, s_b]
    written = set()

    for step in range(nblk):
        rows0 = [step * bl, (nblk - 1 - step) * bl]
        cums = []
        for (reverse, g_ref, causal, tri), r0 in zip(scans, rows0):
            g_hi, g_lo = _split_bf16(g_ref[r0:r0 + bl, :])
            cums.append(_dot(tri, g_hi) + _dot(tri, g_lo))
        prep = []
        for (reverse, g_ref, causal, tri), r0, cum in zip(scans, rows0, cums):
            q = cq_ref[r0:r0 + bl, :]
            k = ck_ref[r0:r0 + bl, :]
            v = cv_ref[r0:r0 + bl, :]
            qt = q * jnp.exp(cum)
            ktb = (k * jnp.exp(-cum)).astype(BF16)
            vb = v.astype(BF16)
            lasts, kdecs = [], []
            for c in range(n_sub):
                c0 = CHUNK * c
                edge = c0 if reverse else c0 + CHUNK - 1
                last = cum[edge:edge + 1, :]
                lasts.append(last)
                kdecs.append((k[c0:c0 + CHUNK] * jnp.exp(last - cum[c0:c0 + CHUNK])).astype(BF16))
            prep.append((qt, ktb, v, vb, lasts, kdecs))
        scores, incs = [], []
        for qt, ktb, v, vb, lasts, kdecs in prep:
            scores.append([_dot_nt(jnp.where(khead == hd, qt, 0.0).astype(BF16), ktb)
                           for hd in range(C_HEADS)])
            incs.append([_dot_tn(vb[CHUNK * c:CHUNK * (c + 1)], kdecs[c]) for c in range(n_sub)])
        probs, entering = [], []
        for si, ((reverse, g_ref, causal, tri), (qt, ktb, v, vb, lasts, kdecs)) in enumerate(
                zip(scans, prep)):
            probs.append([jnp.where(causal, s, 0.0).astype(BF16) for s in scores[si]])
            s_t = states[si]
            before = [None] * n_sub
            subs = range(n_sub)
            for c in (reversed(subs) if reverse else subs):
                before[c] = s_t.astype(BF16)
                s_t = jnp.exp(lasts[c]) * s_t + jnp.where(bd, incs[si][c], 0.0)
            states[si] = s_t
            entering.append(before)
        outs = []
        for si, (qt, ktb, v, vb, lasts, kdecs) in enumerate(prep):
            qtb = qt.astype(BF16)
            o = jnp.concatenate([_dot_nt(qtb[CHUNK * c:CHUNK * (c + 1)], entering[si][c])
                                 for c in range(n_sub)], axis=0)
            for hd in range(C_HEADS):
                o = o + _dot(probs[si][hd], jnp.where(vhead == hd, v, 0.0).astype(BF16))
            outs.append(o)
        if rows0[0] == rows0[1]:
            outs, rows0 = [outs[0] + outs[1]], rows0[:1]
        for o, r0 in zip(outs, rows0):
            if r0 in written:
                oc_ref[r0:r0 + bl, :] = oc_ref[r0:r0 + bl, :] + o
            else:
                oc_ref[r0:r0 + bl, :] = o
                written.add(r0)
    return states[0], states[1]


def _attend_t(jobs, shifts=None):
    def scores(i):
        return _dot(jobs[i][0][...], jobs[i][1])

    outs = []
    pending = [scores(i) for i in range(min(SCORE_LOOKAHEAD, len(jobs)))]
    for i, (_, _, vt) in enumerate(jobs):
        st = pending.pop(0)
        if i + SCORE_LOOKAHEAD < len(jobs):
            pending.append(scores(i + SCORE_LOOKAHEAD))
        m = jnp.max(st, axis=0, keepdims=True) if shifts is None else shifts[i]
        p = jnp.exp2(st - m).astype(BF16)
        ot = _dot(vt[...], p)
        outs.append(ot[0:HEAD_DIM] * (1.0 / ot[HEAD_DIM:HEAD_DIM + 1]))
    return outs


def _pair_rows(a, b):
    return jnp.concatenate([a, b], axis=0).T


def _state_to_blockdiag_t(s_ref):
    rows = []
    for hd in range(C_HEADS):
        pieces = []
        if hd:
            pieces.append(jnp.zeros((C_DK, C_DV * hd), F32))
        pieces.append(s_ref[hd])
        if hd < C_HEADS - 1:
            pieces.append(jnp.zeros((C_DK, C_DV * (C_HEADS - 1 - hd)), F32))
        rows.append(jnp.concatenate(pieces, axis=-1))
    return jnp.concatenate(rows, axis=0).T


def _blockdiag_t_to_state(s_t, out_ref):
    s = s_t.T
    for hd in range(C_HEADS):
        out_ref[hd] = s[C_DK * hd:C_DK * (hd + 1), C_DV * hd:C_DV * (hd + 1)]


def _mix_sequence(cached, layer, first_state, n_aliased, lam_init, seq, qt, qb, *refs):
    it = iter(refs)
    x_ref, mod_ref = next(it), next(it)
    qa_ref, ka_ref, qb_ref, kb_ref = (next(it) for _ in range(4))
    cq_ref, ck_ref, cv_ref, gf_ref, gb_ref, su_ref = (next(it) for _ in range(6))
    vat_ref, vbt_ref = next(it), next(it)
    if cached:
        cakt_ref, cavt_ref, cbkt_ref, cbvt_ref, s0f_ref, s0b_ref = (next(it) for _ in range(6))
        qstat_ref, aqg_ref = next(it), next(it)
    wout_ref, gpost_ref, bog_ref, cog_ref, lamp_ref = (next(it) for _ in range(5))
    for _ in range(n_aliased):
        next(it)
    y_ref = next(it)
    if not cached:
        sf_ref, sb_ref = next(it), next(it)
    kA_s, vtA_s, kB_s, vtB_s, oc_s, mixed_s, kn2_s = (next(it) for _ in range(7))

    lk = kA_s.shape[1]
    past = lk - seq
    t = pl.program_id(1)

    def once_per_sequence(body):
        return body() if seq == qt else pl.when(t == 0)(body)

    @once_per_sequence
    def _per_sequence():
        ones_row = jnp.where(_iota((VT_ROWS - HEAD_DIM, lk), 0) == 0, 1.0, 0.0).astype(BF16)
        for k_new, kt_cache, k_dst, vt_new, vt_cache, vt_dst in (
                (ka_ref, cakt_ref if cached else None, kA_s, vat_ref, cavt_ref if cached else None, vtA_s),
                (kb_ref, cbkt_ref if cached else None, kB_s, vbt_ref, cbvt_ref if cached else None, vtB_s)):
            n_heads = k_dst.shape[0]
            if cached:
                for h0 in range(0, n_heads, 2):
                    pair = jnp.concatenate([kt_cache[h0], kt_cache[h0 + 1]], axis=0).T
                    k_dst[h0, 0:past, :] = pair[:, 0:HEAD_DIM].astype(BF16)
                    k_dst[h0 + 1, 0:past, :] = pair[:, HEAD_DIM:2 * HEAD_DIM].astype(BF16)
            for hd in range(n_heads):
                k_dst[hd, past:lk, :] = k_new[hd]
                if cached:
                    vt_dst[hd, 0:HEAD_DIM, 0:past] = vt_cache[hd].astype(BF16)
                vt_dst[hd, 0:HEAD_DIM, past:lk] = vt_new[hd].astype(BF16)
                vt_dst[hd, HEAD_DIM:VT_ROWS, :] = ones_row
        bounded_jobs = ([(kA_s, g) for g in range(A_KV_HEADS)] + [(kB_s, h) for h in range(B_HEADS)]
                        if lk > KEY_TILE else [])
        for j, (k_dst, hd) in enumerate(bounded_jobs):
            kf = k_dst[hd].astype(F32)
            kn2 = jnp.max(jnp.sum(kf * kf, axis=-1, keepdims=True), axis=0, keepdims=True)
            kn2_s[j:j + 1, :] = jnp.broadcast_to(kn2, (1, LANES))

        if cached:
            s0f, s0b = _state_to_blockdiag_t(s0f_ref), _state_to_blockdiag_t(s0b_ref)
        else:
            s0f = jnp.zeros((C_WIDTH, C_KW), F32)
            s0b = s0f
        s_f, s_b = _gla_bidirectional(cq_ref, ck_ref, cv_ref, gf_ref, gb_ref, s0f, s0b, seq, oc_s)
        if not cached:
            for ref, s_t in ((sf_ref, s_f), (sb_ref, s_b)):
                if first_state:
                    for l2 in range(DEPTH):
                        if l2 == layer:
                            _blockdiag_t_to_state(s_t, ref.at[l2])
                        else:
                            ref[l2] = jnp.zeros(ref.shape[1:], F32)
                else:
                    _blockdiag_t_to_state(s_t, ref)
        for r0 in range(0, seq, GLA_BLOCK):
            oc = oc_s[r0:r0 + GLA_BLOCK, :]
            oc_s[r0:r0 + GLA_BLOCK, :] = oc * lax.rsqrt(_group_mean_sq(oc, 6) + EPS) * cog_ref[...]

    lam = (jnp.exp(jnp.sum(lamp_ref[0:1, :] * lamp_ref[1:2, :], axis=-1, keepdims=True))
           - jnp.exp(jnp.sum(lamp_ref[2:3, :] * lamp_ref[3:4, :], axis=-1, keepdims=True))
           + lam_init)

    def project_out():
        seq_rows = pl.ds(pl.multiple_of(t * qt, qt), qt)
        mixed_s[:, A_WIDTH + B_WIDTH:D_MIX] = oc_s[seq_rows, :]
        gate = mod_ref[:, 2 * D_MODEL:3 * D_MODEL]
        mixed = (mixed_s[...] * su_ref[...]).astype(BF16)
        y = _dot(mixed, wout_ref[...])
        yn = y * lax.rsqrt(jnp.mean(y * y, axis=-1, keepdims=True) + EPS) * gpost_ref[...]
        y_ref[...] = x_ref[...] + gate * yn

    def attn_block(i, carry):
        rows = pl.ds(i * qb, qb) if isinstance(i, int) else pl.ds(pl.multiple_of(i * qb, qb), qb)
        jobs = []
        for grp in range(A_KV_HEADS):
            q4 = jnp.concatenate([qa_ref[A_GROUP * grp + j, :, rows] for j in range(A_GROUP)], axis=-1)
            jobs.append((kA_s.at[grp], q4, vtA_s.at[grp]))
        dim = _iota((HEAD_DIM, qb), 0)
        for hd in range(B_HEADS):
            q = qb_ref[hd, :, rows]
            zero = jnp.zeros_like(q)
            q2 = jnp.concatenate([jnp.where(dim < B_QK_DIM, q, zero),
                                  jnp.where(dim >= B_QK_DIM, q, zero)], axis=-1)
            jobs.append((kB_s.at[hd], q2, vtB_s.at[hd]))

        def finish(outs):
            for grp in range(A_KV_HEADS):
                ot = outs[grp]
                for pair in range(A_GROUP // 2):
                    c0 = 2 * pair * qb
                    col = A_GROUP * HEAD_DIM * grp + 2 * HEAD_DIM * pair
                    mixed_s[rows, col:col + 2 * HEAD_DIM] = _pair_rows(ot[:, c0:c0 + qb],
                                                                       ot[:, c0 + qb:c0 + 2 * qb])
            obs = []
            for hd in range(B_HEADS):
                ot = outs[A_KV_HEADS + hd]
                ob = ot[:, 0:qb] - lam * ot[:, qb:2 * qb]
                obs.append(ob * lax.rsqrt(jnp.mean(ob * ob, axis=0, keepdims=True) + EPS))
            for pair in range(B_HEADS // 2):
                col = A_WIDTH + 2 * B_V_DIM * pair
                mixed_s[rows, col:col + 2 * B_V_DIM] = (_pair_rows(obs[2 * pair], obs[2 * pair + 1])
                                                        * bog_ref[...] * (1.0 - lam_init))
            if qt == qb:
                project_out()

        if lk <= KEY_TILE:
            finish(_attend_t(jobs))
            return carry

        qa_norm = (jnp.max(jnp.abs(aqg_ref[...]), axis=-1, keepdims=True)
                   * (HEAD_DIM ** 0.5 * HEAD_DIM ** -0.5 * LOG2E))
        stat_t = qstat_ref[...]
        shifts, gaps = [], []
        for j, (_, q, _) in enumerate(jobs):
            k_norm = jnp.sqrt(kn2_s[j:j + 1, 0:1])
            if j < A_KV_HEADS:
                upper = qa_norm * k_norm * BOUND_SLACK
                shifts.append(jnp.broadcast_to(upper, (1, q.shape[1])))
                gaps.append(2.0 * upper)
            else:
                r0 = 2 * (j - A_KV_HEADS)
                r1 = r0 + 2 * B_HEADS
                qn2 = jnp.concatenate([stat_t[r0:r0 + 1], stat_t[r0 + 1:r0 + 2]], axis=-1)
                lower = jnp.concatenate([stat_t[r1:r1 + 1], stat_t[r1 + 1:r1 + 2]], axis=-1)
                upper = jnp.sqrt(qn2) * k_norm * BOUND_SLACK
                shifts.append(upper)
                gaps.append(jnp.max(upper - lower, axis=-1, keepdims=True))
        worst = functools.reduce(jnp.maximum, gaps)
        safe = worst[0, 0] <= SAFE_GAP

        @pl.when(safe)
        def _bounded():
            finish(_attend_t(jobs, shifts))

        @pl.when(jnp.logical_not(safe))
        def _exact_max():
            finish(_attend_t(jobs))

        return carry

    if qt != qb:
        lax.fori_loop(0, qt // qb, attn_block, 0)
        project_out()
    else:
        attn_block(0, 0)


def _mix_kernel(nseq, *params_and_refs):
    params, refs = params_and_refs[:N_MIX_PARAMS], params_and_refs[N_MIX_PARAMS:]
    if nseq == 1:
        return _mix_sequence(*params, *refs)
    n_aliased = params[3]
    shared = {1} | set(range(N_MIX_SEQ_INPUTS, N_MIX_SEQ_INPUTS + N_MIX_WEIGHTS + n_aliased))
    for bi in range(nseq):
        _mix_sequence(*params, *[r if idx in shared else r.at[bi] for idx, r in enumerate(refs)])


def _mixer(x, mod, per_batch_mod, layer, proj, wts, lam_init, cache, state_prev=None):
    bsz, seq, _ = x.shape
    cached = cache is not None
    past = cache[0].shape[4] if cached else 0
    qa, ka, qb, kb, cq, ck, cv, gf, gb, su, vat, vbt = proj[:12]
    qt = 256
    qb_rows = 256
    nseq = 1 if cached else 4
    lead = None if nseq == 1 else nseq

    def per_layer(shape):
        return pl.BlockSpec((None,) + shape, lambda b, t: (layer,) + (0,) * len(shape))

    def layer_heads_t(n, length):
        return pl.BlockSpec((lead, None, n, HEAD_DIM, length), lambda b, t: (b, layer, 0, 0, 0))

    def heads_t(n):
        if cached:
            return pl.BlockSpec((lead, n, HEAD_DIM, seq), lambda b, t: (b, 0, 0, 0))
        return layer_heads_t(n, seq)

    def heads(n):
        return pl.BlockSpec((lead, n, seq, HEAD_DIM), lambda b, t: (b, 0, 0, 0))

    def head_tile(n):
        return pl.BlockSpec((lead, n, HEAD_DIM, qt), lambda b, t: (b, 0, 0, t))

    def rows(width):
        return pl.BlockSpec((lead, seq, width), lambda b, t: (b, 0, 0))

    def row_tile(width):
        return pl.BlockSpec((lead, qt, width), lambda b, t: (b, t, 0))

    mod_idx = (lambda b, t: (layer, b, 0, 0)) if per_batch_mod else (lambda b, t: (layer, 0, 0, 0))
    in_specs = [row_tile(D_MODEL), pl.BlockSpec((None, None, 1, 3 * D_MODEL), mod_idx),
                head_tile(A_HEADS), heads(A_KV_HEADS), head_tile(B_HEADS), heads(B_HEADS),
                rows(C_KW), rows(C_KW), rows(C_WIDTH), rows(C_KW), rows(C_KW), row_tile(D_MIX),
                heads_t(A_KV_HEADS), heads_t(B_HEADS)]
    args = [x, mod, qa, ka, qb, kb, cq, ck, cv, gf, gb, su, vat, vbt]
    if cached:
        state_in = pl.BlockSpec((None, None, C_HEADS, C_DK, C_DV), lambda b, t: (b, layer, 0, 0, 0))
        in_specs += [layer_heads_t(A_KV_HEADS, past), layer_heads_t(A_KV_HEADS, past),
                     layer_heads_t(B_HEADS, past), layer_heads_t(B_HEADS, past), state_in, state_in,
                     pl.BlockSpec((None, QSTAT_ROWS, qt), lambda b, t: (b, 0, t)),
                     per_layer((1, HEAD_DIM))]
        args += list(cache) + [proj[12], wts["aq_gain"]]
    in_specs += [per_layer((D_MIX, D_MODEL)), per_layer((1, D_MODEL)), per_layer((1, 2 * B_V_DIM)),
                 per_layer((1, C_WIDTH)), per_layer((4, B_QK_DIM))]
    args += [wts["w_out"], wts["g_post"], wts["b_out_gain"], wts["c_out_gain"], wts["lam_params"]]

    out_specs = [row_tile(D_MODEL)]
    out_shape = [jax.ShapeDtypeStruct((bsz, seq, D_MODEL), F32)]
    aliases = {}
    first_state = not cached and state_prev is None
    if not cached:
        if first_state:
            state_out = pl.BlockSpec((lead, DEPTH, C_HEADS, C_DK, C_DV), lambda b, t: (b, 0, 0, 0, 0))
        else:
            state_out = pl.BlockSpec((lead, None, C_HEADS, C_DK, C_DV),
                                     lambda b, t: (b, layer, 0, 0, 0))
            for j, buf in enumerate(state_prev):
                aliases[len(args)] = 1 + j
                in_specs.append(pl.BlockSpec(memory_space=pl.ANY))
                args.append(buf)
        out_specs += [state_out, state_out]
        out_shape += [jax.ShapeDtypeStruct((bsz, DEPTH, C_HEADS, C_DK, C_DV), F32)] * 2

    lk = past + seq
    per_seq = () if nseq == 1 else (nseq,)
    scratch = [pltpu.VMEM(per_seq + shape, dtype) for shape, dtype in (
        ((A_KV_HEADS, lk, HEAD_DIM), BF16), ((A_KV_HEADS, VT_ROWS, lk), BF16),
        ((B_HEADS, lk, HEAD_DIM), BF16), ((B_HEADS, VT_ROWS, lk), BF16),
        ((seq, C_WIDTH), F32), ((qt, D_MIX), F32), ((8, LANES), F32))]
    return pl.pallas_call(
        functools.partial(_mix_kernel, nseq, cached, layer, first_state, len(aliases), lam_init, seq,
                          qt, qb_rows),
        grid=(bsz // nseq, seq // qt),
        in_specs=in_specs,
        out_specs=out_specs,
        out_shape=out_shape,
        scratch_shapes=scratch,
        input_output_aliases=aliases,
        compiler_params=pltpu.CompilerParams(
            dimension_semantics=("arbitrary", "arbitrary"), vmem_limit_bytes=VMEM_LIMIT),
        name="mixer_cached" if cached else "mixer",
    )(*args)


def _rope_tables(seq):
    t = jnp.arange(seq)
    pos_row = (t // GRID_W).astype(F32)
    pos_col = (t % GRID_W).astype(F32)

    def tables(half):
        freq = ROPE_THETA ** (-jnp.arange(half, dtype=F32) / half)
        ang_r = freq[:, None] * pos_row[None, :]
        ang_c = freq[:, None] * pos_col[None, :]
        cos = jnp.concatenate([jnp.cos(ang_r), jnp.cos(ang_r), jnp.cos(ang_c), jnp.cos(ang_c)], axis=0)
        sin = jnp.concatenate([-jnp.sin(ang_r), jnp.sin(ang_r), -jnp.sin(ang_c), jnp.sin(ang_c)], axis=0)
        reps = HEAD_DIM // (4 * half)
        return jnp.tile(cos, (reps, 1)), jnp.tile(sin, (reps, 1))

    cos_a, sin_a = tables(HEAD_DIM // 4)
    cos_b, sin_b = tables(B_QK_DIM // 4)
    return cos_a, sin_a, cos_b, sin_b


def _prepare_weights(g_pre, g_post, w_in, w_out, a_q_gain, a_k_gain, b_lambda_q1, b_lambda_k1,
                     b_lambda_q2, b_lambda_k2, b_out_gain, c_gate_w_fwd, c_gate_b_fwd, c_gate_w_bwd,
                     c_gate_b_bwd, c_out_gain):
    w_in_t = jnp.swapaxes(w_in, 1, 2).astype(BF16)
    pad = jnp.zeros((DEPTH, GATE_RANK, C_KW), F32)
    cw_f = jnp.concatenate([c_gate_w_fwd, pad], axis=1).astype(BF16)
    cw_b = jnp.concatenate([pad, c_gate_w_bwd], axis=1).astype(BF16)
    return {
        "g_pre": g_pre[:, None, :],
        "g_post": g_post[:, None, :],
        "w_in_t": w_in_t,
        "w_out": w_out.astype(BF16),
        "aq_gain": a_q_gain[:, None, :],
        "aq_gain_col": a_q_gain[:, :, None],
        "ak_gain_col": a_k_gain[:, :, None],
        "cw_f": cw_f,
        "cb_f": c_gate_b_fwd[:, None, :],
        "cw_b": cw_b,
        "cb_b": c_gate_b_bwd[:, None, :],
        "b_out_gain": jnp.tile(b_out_gain, (1, 2))[:, None, :],
        "c_out_gain": jnp.tile(c_out_gain, (1, C_HEADS))[:, None, :],
        "lam_params": jnp.stack([b_lambda_q1, b_lambda_k1, b_lambda_q2, b_lambda_k2], axis=1),
    }


def kernel(x_prompt, x_sample, c, cache_a_k, cache_a_v, cache_b_k, cache_b_v, state_c_fwd, state_c_bwd, c_ctx, w_mod, b_mod, g_pre, g_post, w_in, w_out, a_q_gain, a_k_gain, b_lambda_q1, b_lambda_k1, b_lambda_q2, b_lambda_k2, b_out_gain, c_gate_w_fwd, c_gate_b_fwd, c_gate_w_bwd, c_gate_b_bwd, c_out_gain):
    dec_batch = x_sample.shape[0]
    dec_seq = x_sample.shape[1]

    mod_rows = 16
    cvec = jnp.zeros((mod_rows, D_MODEL), F32).at[0:dec_batch].set(c).at[dec_batch].set(c_ctx)
    mod = _modulation(cvec, w_mod, b_mod)[:, :, None, :]
    mod_lat = mod[:, 0:dec_batch]
    mod_ctx = mod[:, dec_batch:dec_batch + 1]

    wts = _prepare_weights(g_pre, g_post, w_in, w_out, a_q_gain, a_k_gain, b_lambda_q1, b_lambda_k1,
                           b_lambda_q2, b_lambda_k2, b_out_gain, c_gate_w_fwd, c_gate_b_fwd,
                           c_gate_w_bwd, c_gate_b_bwd, c_out_gain)
    rope_tabs = _rope_tables(dec_seq)
    cache = tuple(jnp.swapaxes(a, -1, -2) for a in (cache_a_k, cache_a_v, cache_b_k, cache_b_v))
    cache += (state_c_fwd, state_c_bwd)

    y_p, y_s = x_prompt, x_sample
    kv_ctx = None
    states = None
    for l in range(DEPTH):
        lam_init = 0.8 - 0.6 * math.exp(-0.3 * l)
        proj_p = _in_projection(y_p, mod_ctx, False, l, wts, None, 4, x_prompt.shape[1], kv_ctx)
        kv_ctx = proj_p[10:14]
        y_p, *states = _mixer(y_p, mod_ctx, False, l, proj_p, wts, lam_init, None, states)

        proj_s = _in_projection(y_s, mod_lat, True, l, wts, rope_tabs, 1, 1024)
        (y_s,) = _mixer(y_s, mod_lat, True, l, proj_s, wts, lam_init, cache)

    va_t, vb_t, ka_t, kb_t = kv_ctx
    new_kv = [jnp.swapaxes(a, -1, -2) for a in (ka_t, va_t, kb_t, vb_t)]
    return (y_p, y_s, *new_kv, *states)
```

```python
import functools
import math

import jax
import jax.numpy as jnp
from jax import lax
from jax.experimental import pallas as pl
from jax.experimental.pallas import tpu as pltpu

F32 = jnp.float32
BF16 = jnp.bfloat16

D_MODEL = 1024
DEPTH = 2
GRID_W = 64
HEAD_DIM = 64
A_HEADS = 8
A_KV_HEADS = 2
A_GROUP = A_HEADS // A_KV_HEADS
A_WIDTH = A_HEADS * HEAD_DIM
A_KV_WIDTH = A_KV_HEADS * HEAD_DIM
B_HEADS = 4
B_QK_DIM = 32
B_V_DIM = 64
B_WIDTH = B_HEADS * B_V_DIM
C_HEADS = 4
C_DK = 32
C_DV = 64
C_KW = C_HEADS * C_DK
C_WIDTH = C_HEADS * C_DV
GATE_RANK = 16
GLA_TAU = 16.0
CHUNK = 64
D_MIX = A_WIDTH + B_WIDTH + C_WIDTH
ROPE_THETA = 10000.0
EPS = 1e-6

LANES = 128
GLA_BLOCK = 256
VT_ROWS = HEAD_DIM + 16
SCORE_LOOKAHEAD = 6
SAFE_GAP = 96.0
BOUND_SLACK = 1.02
QSTAT_ROWS = 4 * B_HEADS
U_TILE = D_MIX // 4
N_MIX_PARAMS = 8
N_MIX_SEQ_INPUTS = 14
N_MIX_WEIGHTS = 5
KEY_TILE = 256
LOG2E = math.log2(math.e)

OFF_AQ = 0
OFF_AK = OFF_AQ + A_WIDTH
OFF_AV = OFF_AK + A_KV_WIDTH
OFF_BQ = OFF_AV + A_KV_WIDTH
OFF_BK = OFF_BQ + B_WIDTH
OFF_BV = OFF_BK + B_WIDTH
OFF_CQ = OFF_BV + B_WIDTH
OFF_CK = OFF_CQ + C_KW
OFF_CV = OFF_CK + C_KW
OFF_LR = OFF_CV + C_WIDTH
OFF_U = OFF_LR + 2 * GATE_RANK
IN_WIDTH = OFF_U + D_MIX

VMEM_LIMIT = 60 * 1024 * 1024


def _dot(a, b):
    return jnp.dot(a, b, preferred_element_type=F32)


def _dot_nt(a, b):
    return lax.dot_general(a, b, (((1,), (1,)), ((), ())), preferred_element_type=F32)


def _dot_tn(a, b):
    return lax.dot_general(a, b, (((0,), (0,)), ((), ())), preferred_element_type=F32)


def _split_bf16(x):
    hi = x.astype(BF16)
    lo = (x - hi.astype(F32)).astype(BF16)
    return hi, lo


def _iota(shape, dim):
    return lax.broadcasted_iota(jnp.int32, shape, dim)


def _group_mean_sq(x, group_log2, split=True):
    width = x.shape[-1]
    r = lax.shift_right_logical(_iota((LANES, LANES), 0), group_log2)
    c = lax.shift_right_logical(_iota((LANES, LANES), 1), group_log2)
    ones = jnp.where(r == c, 1.0, 0.0).astype(BF16)
    if split:
        hi, lo = _split_bf16(x * x)
    else:
        hi, lo = (x * x).astype(BF16), None
    cols = []
    for j in range(width // LANES):
        sl = slice(LANES * j, LANES * (j + 1))
        cols.append(_dot(hi[:, sl], ones) + (_dot(lo[:, sl], ones) if split else 0.0))
    ss = cols[0] if len(cols) == 1 else jnp.concatenate(cols, axis=-1)
    return ss * (1.0 / (1 << group_log2))


def _log_sigmoid(x):
    return jnp.minimum(x, 0.0) - jnp.log1p(jnp.exp(-jnp.abs(x)))


def _silu(x):
    return x * (1.0 / (1.0 + jnp.exp(-x)))


def _mod_kernel(c_ref, w_ref, b_ref, o_ref):
    a = _silu(c_ref[...]).astype(BF16)
    o_ref[...] = _dot(a, w_ref[...].astype(BF16)) + b_ref[...]


def _modulation(cvec, w_mod, b_mod):
    rows = cvec.shape[0]
    nblk = 3
    return pl.pallas_call(
        _mod_kernel,
        grid=(DEPTH, nblk),
        in_specs=[
            pl.BlockSpec((rows, D_MODEL), lambda l, n: (0, 0)),
            pl.BlockSpec((None, D_MODEL, D_MODEL), lambda l, n: (l, 0, n)),
            pl.BlockSpec((None, 1, D_MODEL), lambda l, n: (l, 0, n)),
        ],
        out_specs=pl.BlockSpec((None, rows, D_MODEL), lambda l, n: (l, 0, n)),
        out_shape=jax.ShapeDtypeStruct((DEPTH, rows, 3 * D_MODEL), F32),
        compiler_params=pltpu.CompilerParams(
            dimension_semantics=("arbitrary", "arbitrary"), vmem_limit_bytes=VMEM_LIMIT),
        name="modulation",
    )(cvec, w_mod, b_mod.reshape(DEPTH, 1, 3 * D_MODEL))


def _in_kernel(rope, layer, stacked_first, nb, tl, n_aliased, *refs):
    (x_ref, mod_ref, gpre_ref, wt_ref, aqg_ref, akg_ref, cwf_ref, cbf_ref, cwb_ref, cbb_ref) = refs[:10]
    refs = refs[10:]
    if rope:
        cosa_ref, sina_ref, cosb_ref, sinb_ref = refs[:4]
        refs = refs[4:]
    refs = refs[n_aliased:]
    (qa_ref, ka_ref, qb_ref, kb_ref, cq_ref, ck_ref, cv_ref, gf_ref, gb_ref, su_ref,
     vat_ref, vbt_ref) = refs[:12]
    kat_ref, kbt_ref = (None, None) if rope else refs[12:]
    qstat_ref = refs[12] if rope else None

    x = x_ref[...].reshape(nb * tl, D_MODEL)
    shift = mod_ref[:, 0:D_MODEL]
    scale = mod_ref[:, D_MODEL:2 * D_MODEL]
    ms = jnp.mean(x * x, axis=-1, keepdims=True)
    h = (x * lax.rsqrt(ms + EPS)) * gpre_ref[...] * (1.0 + scale) + shift
    hb = h.astype(BF16)

    def proj(off, width):
        return _dot_nt(hb, wt_ref[off:off + width, :])

    def proj_t(off, width):
        return _dot_nt(wt_ref[off:off + width, :], hb)

    def put_rows(ref, val):
        for bi in range(nb):
            ref[bi] = val[bi * tl:(bi + 1) * tl].astype(ref.dtype)

    def put_heads(ref, val, n_heads):
        for bi in range(nb):
            for hd in range(n_heads):
                ref[bi, hd] = val[bi * tl:(bi + 1) * tl,
                                  HEAD_DIM * hd:HEAD_DIM * (hd + 1)].astype(ref.dtype)

    def put_heads_t(ref, heads_t, stacked=False):
        for bi in range(nb):
            for hd, val_t in enumerate(heads_t):
                blk = val_t[:, bi * tl:(bi + 1) * tl].astype(ref.dtype)
                if stacked and stacked_first:
                    for l2 in range(DEPTH):
                        ref[bi, l2, hd] = blk if l2 == layer else jnp.zeros_like(blk)
                else:
                    ref[bi, hd] = blk

    def split_heads_t(val_t, n_heads):
        return [val_t[HEAD_DIM * hd:HEAD_DIM * (hd + 1)] for hd in range(n_heads)]

    def rms_t(head_t, gain_col):
        ms_h = jnp.mean(head_t * head_t, axis=0, keepdims=True)
        return head_t * lax.rsqrt(ms_h + EPS) * gain_col

    def rope_t(head_t, cos_t, sin_t, dist):
        blocks = [head_t[r0:r0 + dist] for r0 in range(0, HEAD_DIM, dist)]
        partner = jnp.concatenate([blocks[j ^ 1] for j in range(len(blocks))], axis=0)
        return head_t * cos_t + partner * sin_t

    def gate_tile(j):
        c0 = U_TILE * j
        val = _silu(proj(OFF_U + c0, U_TILE))
        for bi in range(nb):
            su_ref[bi, :, c0:c0 + U_TILE] = val[bi * tl:(bi + 1) * tl].astype(su_ref.dtype)

    za_t = proj_t(OFF_AQ, A_WIDTH + 2 * A_KV_WIDTH)
    zb_t = proj_t(OFF_BQ, 3 * B_WIDTH)
    lr_t = proj_t(OFF_LR, 2 * GATE_RANK).astype(BF16)
    gate_tile(0)

    aq_t = [rms_t(h_t, aqg_ref[...]) for h_t in split_heads_t(za_t[0:A_WIDTH], A_HEADS)]
    ak_t = [rms_t(h_t, akg_ref[...])
            for h_t in split_heads_t(za_t[A_WIDTH:A_WIDTH + A_KV_WIDTH], A_KV_HEADS)]
    if rope:
        aq_t = [rope_t(h_t, cosa_ref[...], sina_ref[...], 16) for h_t in aq_t]
        ak_t = [rope_t(h_t, cosa_ref[...], sina_ref[...], 16) for h_t in ak_t]
    put_heads_t(qa_ref, [h_t * (HEAD_DIM ** -0.5 * LOG2E) for h_t in aq_t])
    put_heads(ka_ref, jnp.concatenate(ak_t, axis=0).T, A_KV_HEADS)
    put_heads_t(vat_ref, split_heads_t(za_t[A_WIDTH + A_KV_WIDTH:], A_KV_HEADS), stacked=not rope)
    if kat_ref is not None:
        put_heads_t(kat_ref, ak_t, stacked=True)
    gate_tile(1)
    cqk = proj(OFF_CQ, 2 * C_KW)
    cv = proj(OFF_CV, C_WIDTH)

    bq_t = split_heads_t(zb_t[0:B_WIDTH], B_HEADS)
    bk_t = split_heads_t(zb_t[B_WIDTH:2 * B_WIDTH], B_HEADS)
    if rope:
        bq_t = [rope_t(h_t, cosb_ref[...], sinb_ref[...], 8) for h_t in bq_t]
        bk_t = [rope_t(h_t, cosb_ref[...], sinb_ref[...], 8) for h_t in bk_t]
    bq_t = [h_t * (B_QK_DIM ** -0.5 * LOG2E) for h_t in bq_t]
    put_heads_t(qb_ref, bq_t)
    put_heads(kb_ref, jnp.concatenate(bk_t, axis=0).T, B_HEADS)
    if rope:
        halves = [(hd, r0) for hd in range(B_HEADS) for r0 in (0, B_QK_DIM)]
        qstat_ref[0] = jnp.concatenate(
            [jnp.sum(bq_t[hd][r0:r0 + B_QK_DIM] * bq_t[hd][r0:r0 + B_QK_DIM], axis=0, keepdims=True)
             for hd, r0 in halves]
            + [jnp.sum(bq_t[hd][r0:r0 + B_QK_DIM] * bk_t[hd][r0:r0 + B_QK_DIM], axis=0, keepdims=True)
               for hd, r0 in halves], axis=0)
    put_heads_t(vbt_ref, split_heads_t(zb_t[2 * B_WIDTH:], B_HEADS), stacked=not rope)
    if kbt_ref is not None:
        put_heads_t(kbt_ref, bk_t, stacked=True)
    gate_tile(2)
    gf_pre = _dot_tn(lr_t, cwf_ref[...])
    gb_pre = _dot_tn(lr_t, cwb_ref[...])

    put_rows(cq_ref, cqk[:, 0:C_KW] * (C_DK ** -0.5))
    put_rows(ck_ref, cqk[:, C_KW:2 * C_KW])
    put_rows(cv_ref, cv)
    put_rows(gf_ref, _log_sigmoid(gf_pre + cbf_ref[...]) * (1.0 / GLA_TAU))
    put_rows(gb_ref, _log_sigmoid(gb_pre + cbb_ref[...]) * (1.0 / GLA_TAU))
    gate_tile(3)


def _in_projection(x, mod, per_batch_mod, layer, wts, rope_tabs, nb, tl, kv_prev=None):
    bsz, seq, _ = x.shape
    rope = rope_tabs is not None
    stacked = not rope
    stacked_first = stacked and kv_prev is None
    grid = (bsz // nb, seq // tl)

    def per_layer(shape):
        return pl.BlockSpec((None,) + shape, lambda b, t: (layer,) + (0,) * len(shape))

    mod_idx = (lambda b, t: (layer, b, 0, 0)) if per_batch_mod else (lambda b, t: (layer, 0, 0, 0))
    in_specs = [
        pl.BlockSpec((nb, tl, D_MODEL), lambda b, t: (b, t, 0)),
        pl.BlockSpec((None, None, 1, 3 * D_MODEL), mod_idx),
        per_layer((1, D_MODEL)),
        per_layer((IN_WIDTH, D_MODEL)),
        per_layer((HEAD_DIM, 1)),
        per_layer((HEAD_DIM, 1)),
        per_layer((2 * GATE_RANK, C_KW)),
        per_layer((1, C_KW)),
        per_layer((2 * GATE_RANK, C_KW)),
        per_layer((1, C_KW)),
    ]
    args = [x, mod, wts["g_pre"], wts["w_in_t"], wts["aq_gain_col"], wts["ak_gain_col"],
            wts["cw_f"], wts["cb_f"], wts["cw_b"], wts["cb_b"]]
    if rope:
        assert nb == 1
        in_specs += [
            pl.BlockSpec((HEAD_DIM, tl), lambda b, t: (0, t)),
            pl.BlockSpec((HEAD_DIM, tl), lambda b, t: (0, t)),
            pl.BlockSpec((HEAD_DIM, tl), lambda b, t: (0, t)),
            pl.BlockSpec((HEAD_DIM, tl), lambda b, t: (0, t)),
        ]
        args += list(rope_tabs)

    def heads(n):
        return pl.BlockSpec((nb, n, tl, HEAD_DIM), lambda b, t: (b, 0, t, 0))

    def heads_t(n):
        if stacked_first:
            return pl.BlockSpec((nb, DEPTH, n, HEAD_DIM, tl), lambda b, t: (b, 0, 0, 0, t))
        if stacked:
            return pl.BlockSpec((nb, None, n, HEAD_DIM, tl), lambda b, t: (b, layer, 0, 0, t))
        return pl.BlockSpec((nb, n, HEAD_DIM, tl), lambda b, t: (b, 0, 0, t))

    def q_heads_t(n):
        return pl.BlockSpec((nb, n, HEAD_DIM, tl), lambda b, t: (b, 0, 0, t))

    def rows(width):
        return pl.BlockSpec((nb, tl, width), lambda b, t: (b, t, 0))

    def hshape(n):
        return jax.ShapeDtypeStruct((bsz, n, seq, HEAD_DIM), BF16)

    def qshape(n):
        return jax.ShapeDtypeStruct((bsz, n, HEAD_DIM, seq), BF16)

    def tshape(n):
        if stacked:
            return jax.ShapeDtypeStruct((bsz, DEPTH, n, HEAD_DIM, seq), F32)
        return jax.ShapeDtypeStruct((bsz, n, HEAD_DIM, seq), BF16)

    def rshape(width, dtype=F32):
        return jax.ShapeDtypeStruct((bsz, seq, width), dtype)

    out_specs = [q_heads_t(A_HEADS), heads(A_KV_HEADS), q_heads_t(B_HEADS), heads(B_HEADS),
                 rows(C_KW), rows(C_KW), rows(C_WIDTH), rows(C_KW), rows(C_KW), rows(D_MIX),
                 heads_t(A_KV_HEADS), heads_t(B_HEADS)]
    out_shape = [qshape(A_HEADS), hshape(A_KV_HEADS), qshape(B_HEADS), hshape(B_HEADS),
                 rshape(C_KW), rshape(C_KW), rshape(C_WIDTH, BF16), rshape(C_KW), rshape(C_KW),
                 rshape(D_MIX, BF16), tshape(A_KV_HEADS), tshape(B_HEADS)]
    if stacked:
        out_specs += [heads_t(A_KV_HEADS), heads_t(B_HEADS)]
        out_shape += [tshape(A_KV_HEADS), tshape(B_HEADS)]
    else:
        out_specs += [pl.BlockSpec((nb, QSTAT_ROWS, tl), lambda b, t: (b, 0, t))]
        out_shape += [jax.ShapeDtypeStruct((bsz, QSTAT_ROWS, seq), F32)]
    aliases = {}
    if kv_prev is not None:
        for j, buf in enumerate(kv_prev):
            aliases[len(args)] = 10 + j
            in_specs.append(pl.BlockSpec(memory_space=pl.ANY))
            args.append(buf)
    return pl.pallas_call(
        functools.partial(_in_kernel, rope, layer, stacked_first, nb, tl, len(aliases)),
        grid=grid,
        in_specs=in_specs,
        out_specs=out_specs,
        out_shape=out_shape,
        input_output_aliases=aliases,
        compiler_params=pltpu.CompilerParams(
            dimension_semantics=("arbitrary", "arbitrary"), vmem_limit_bytes=VMEM_LIMIT),
        name="in_projection_rope" if rope else "in_projection",
    )(*args)


def _gla_bidirectional(cq_ref, ck_ref, cv_ref, gf_ref, gb_ref, s_f, s_b, seq, oc_ref):
    bl = GLA_BLOCK
    n_sub = bl // CHUNK
    nblk = seq // bl
    ri = _iota((bl, bl), 0)
    ci = _iota((bl, bl), 1)
    same_chunk = lax.shift_right_logical(ri, 6) == lax.shift_right_logical(ci, 6)
    bd = (lax.shift_right_logical(_iota((C_WIDTH, C_KW), 0), 6)
          == lax.shift_right_logical(_iota((C_WIDTH, C_KW), 1), 5))
    khead = lax.shift_right_logical(_iota((1, C_KW), 1), 5)
    vhead = lax.shift_right_logical(_iota((1, C_WIDTH), 1), 6)
    scans = []
    for reverse, g_ref in ((False, gf_ref), (True, gb_ref)):
        causal = same_chunk & ((ci >= ri) if reverse else (ci <= ri))
        scans.append((reverse, g_ref, causal, jnp.where(causal, 1.0, 0.0).astype(BF16)))
    states = [s_f, s_b]
    written = set()

    for step in range(nblk):
        rows0 = [step * bl, (nblk - 1 - step) * bl]
        cums = []
        for (reverse, g_ref, causal, tri), r0 in zip(scans, rows0):
            g_hi, g_lo = _split_bf16(g_ref[r0:r0 + bl, :])
            cums.append(_dot(tri, g_hi) + _dot(tri, g_lo))
        prep = []
        for (reverse, g_ref, causal, tri), r0, cum in zip(scans, rows0, cums):
            q = cq_ref[r0:r0 + bl, :]
            k = ck_ref[r0:r0 + bl, :]
            v = cv_ref[r0:r0 + bl, :]
            qt = q * jnp.exp(cum)
            ktb = (k * jnp.exp(-cum)).astype(BF16)
            vb = v.astype(BF16)
            lasts, kdecs = [], []
            for c in range(n_sub):
                c0 = CHUNK * c
                edge = c0 if reverse else c0 + CHUNK - 1
                last = cum[edge:edge + 1, :]
                lasts.append(last)
                kdecs.append((k[c0:c0 + CHUNK] * jnp.exp(last - cum[c0:c0 + CHUNK])).astype(BF16))
            prep.append((qt, ktb, v, vb, lasts, kdecs))
        scores, incs = [], []
        for qt, ktb, v, vb, lasts, kdecs in prep:
            scores.append([_dot_nt(jnp.where(khead == hd, qt, 0.0).astype(BF16), ktb)
                           for hd in range(C_HEADS)])
            incs.append([_dot_tn(vb[CHUNK * c:CHUNK * (c + 1)], kdecs[c]) for c in range(n_sub)])
        probs, entering = [], []
        for si, ((reverse, g_ref, causal, tri), (qt, ktb, v, vb, lasts, kdecs)) in enumerate(
                zip(scans, prep)):
            probs.append([jnp.where(causal, s, 0.0).astype(BF16) for s in scores[si]])
            s_t = states[si]
            before = [None] * n_sub
            subs = range(n_sub)
            for c in (reversed(subs) if reverse else subs):
                before[c] = s_t.astype(BF16)
                s_t = jnp.exp(lasts[c]) * s_t + jnp.where(bd, incs[si][c], 0.0)
            states[si] = s_t
            entering.append(before)
        outs = []
        for si, (qt, ktb, v, vb, lasts, kdecs) in enumerate(prep):
            qtb = qt.astype(BF16)
            o = jnp.concatenate([_dot_nt(qtb[CHUNK * c:CHUNK * (c + 1)], entering[si][c])
                                 for c in range(n_sub)], axis=0)
            for hd in range(C_HEADS):
                o = o + _dot(probs[si][hd], jnp.where(vhead == hd, v, 0.0).astype(BF16))
            outs.append(o)
        if rows0[0] == rows0[1]:
            outs, rows0 = [outs[0] + outs[1]], rows0[:1]
        for o, r0 in zip(outs, rows0):
            if r0 in written:
                oc_ref[r0:r0 + bl, :] = oc_ref[r0:r0 + bl, :] + o
            else:
                oc_ref[r0:r0 + bl, :] = o
                written.add(r0)
    return states[0], states[1]


def _attend_t(jobs, shifts=None):
    def scores(i):
        return _dot(jobs[i][0][...], jobs[i][1])

    outs = []
    pending = [scores(i) for i in range(min(SCORE_LOOKAHEAD, len(jobs)))]
    for i, (_, _, vt) in enumerate(jobs):
        st = pending.pop(0)
        if i + SCORE_LOOKAHEAD < len(jobs):
            pending.append(scores(i + SCORE_LOOKAHEAD))
        m = jnp.max(st, axis=0, keepdims=True) if shifts is None else shifts[i]
        p = jnp.exp2(st - m).astype(BF16)
        ot = _dot(vt[...], p)
        outs.append(ot[0:HEAD_DIM] * (1.0 / ot[HEAD_DIM:HEAD_DIM + 1]))
    return outs


def _pair_rows(a, b):
    return jnp.concatenate([a, b], axis=0).T


def _state_to_blockdiag_t(s_ref):
    rows = []
    for hd in range(C_HEADS):
        pieces = []
        if hd:
            pieces.append(jnp.zeros((C_DK, C_DV * hd), F32))
        pieces.append(s_ref[hd])
        if hd < C_HEADS - 1:
            pieces.append(jnp.zeros((C_DK, C_DV * (C_HEADS - 1 - hd)), F32))
        rows.append(jnp.concatenate(pieces, axis=-1))
    return jnp.concatenate(rows, axis=0).T


def _blockdiag_t_to_state(s_t, out_ref):
    s = s_t.T
    for hd in range(C_HEADS):
        out_ref[hd] = s[C_DK * hd:C_DK * (hd + 1), C_DV * hd:C_DV * (hd + 1)]


def _mix_sequence(cached, layer, first_state, n_aliased, lam_init, seq, qt, qb, *refs):
    it = iter(refs)
    x_ref, mod_ref = next(it), next(it)
    qa_ref, ka_ref, qb_ref, kb_ref = (next(it) for _ in range(4))
    cq_ref, ck_ref, cv_ref, gf_ref, gb_ref, su_ref = (next(it) for _ in range(6))
    vat_ref, vbt_ref = next(it), next(it)
    if cached:
        cakt_ref, cavt_ref, cbkt_ref, cbvt_ref, s0f_ref, s0b_ref = (next(it) for _ in range(6))
        qstat_ref, aqg_ref = next(it), next(it)
    wout_ref, gpost_ref, bog_ref, cog_ref, lamp_ref = (next(it) for _ in range(5))
    for _ in range(n_aliased):
        next(it)
    y_ref = next(it)
    if not cached:
        sf_ref, sb_ref = next(it), next(it)
    kA_s, vtA_s, kB_s, vtB_s, oc_s, mixed_s, kn2_s = (next(it) for _ in range(7))

    lk = kA_s.shape[1]
    past = lk - seq
    t = pl.program_id(1)

    def once_per_sequence(body):
        return body() if seq == qt else pl.when(t == 0)(body)

    @once_per_sequence
    def _per_sequence():
        ones_row = jnp.where(_iota((VT_ROWS - HEAD_DIM, lk), 0) == 0, 1.0, 0.0).astype(BF16)
        for k_new, kt_cache, k_dst, vt_new, vt_cache, vt_dst in (
                (ka_ref, cakt_ref if cached else None, kA_s, vat_ref, cavt_ref if cached else None, vtA_s),
                (kb_ref, cbkt_ref if cached else None, kB_s, vbt_ref, cbvt_ref if cached else None, vtB_s)):
            n_heads = k_dst.shape[0]
            if cached:
                for h0 in range(0, n_heads, 2):
                    pair = jnp.concatenate([kt_cache[h0], kt_cache[h0 + 1]], axis=0).T
                    k_dst[h0, 0:past, :] = pair[:, 0:HEAD_DIM].astype(BF16)
                    k_dst[h0 + 1, 0:past, :] = pair[:, HEAD_DIM:2 * HEAD_DIM].astype(BF16)
            for hd in range(n_heads):
                k_dst[hd, past:lk, :] = k_new[hd]
                if cached:
                    vt_dst[hd, 0:HEAD_DIM, 0:past] = vt_cache[hd].astype(BF16)
                vt_dst[hd, 0:HEAD_DIM, past:lk] = vt_new[hd].astype(BF16)
                vt_dst[hd, HEAD_DIM:VT_ROWS, :] = ones_row
        bounded_jobs = ([(kA_s, g) for g in range(A_KV_HEADS)] + [(kB_s, h) for h in range(B_HEADS)]
                        if lk > KEY_TILE else [])
        for j, (k_dst, hd) in enumerate(bounded_jobs):
            kf = k_dst[hd].astype(F32)
            kn2 = jnp.max(jnp.sum(kf * kf, axis=-1, keepdims=True), axis=0, keepdims=True)
            kn2_s[j:j + 1, :] = jnp.broadcast_to(kn2, (1, LANES))

        if cached:
            s0f, s0b = _state_to_blockdiag_t(s0f_ref), _state_to_blockdiag_t(s0b_ref)
        else:
            s0f = jnp.zeros((C_WIDTH, C_KW), F32)
            s0b = s0f
        s_f, s_b = _gla_bidirectional(cq_ref, ck_ref, cv_ref, gf_ref, gb_ref, s0f, s0b, seq, oc_s)
        if not cached:
            for ref, s_t in ((sf_ref, s_f), (sb_ref, s_b)):
                if first_state:
                    for l2 in range(DEPTH):
                        if l2 == layer:
                            _blockdiag_t_to_state(s_t, ref.at[l2])
                        else:
                            ref[l2] = jnp.zeros(ref.shape[1:], F32)
                else:
                    _blockdiag_t_to_state(s_t, ref)
        for r0 in range(0, seq, GLA_BLOCK):
            oc = oc_s[r0:r0 + GLA_BLOCK, :]
            oc_s[r0:r0 + GLA_BLOCK, :] = oc * lax.rsqrt(_group_mean_sq(oc, 6) + EPS) * cog_ref[...]

    lam = (jnp.exp(jnp.sum(lamp_ref[0:1, :] * lamp_ref[1:2, :], axis=-1, keepdims=True))
           - jnp.exp(jnp.sum(lamp_ref[2:3, :] * lamp_ref[3:4, :], axis=-1, keepdims=True))
           + lam_init)

    def project_out():
        seq_rows = pl.ds(pl.multiple_of(t * qt, qt), qt)
        mixed_s[:, A_WIDTH + B_WIDTH:D_MIX] = oc_s[seq_rows, :]
        gate = mod_ref[:, 2 * D_MODEL:3 * D_MODEL]
        mixed = (mixed_s[...] * su_ref[...]).astype(BF16)
        y = _dot(mixed, wout_ref[...])
        yn = y * lax.rsqrt(jnp.mean(y * y, axis=-1, keepdims=True) + EPS) * gpost_ref[...]
        y_ref[...] = x_ref[...] + gate * yn

    def attn_block(i, carry):
        rows = pl.ds(i * qb, qb) if isinstance(i, int) else pl.ds(pl.multiple_of(i * qb, qb), qb)
        jobs = []
        for grp in range(A_KV_HEADS):
            q4 = jnp.concatenate([qa_ref[A_GROUP * grp + j, :, rows] for j in range(A_GROUP)], axis=-1)
            jobs.append((kA_s.at[grp], q4, vtA_s.at[grp]))
        dim = _iota((HEAD_DIM, qb), 0)
        for hd in range(B_HEADS):
            q = qb_ref[hd, :, rows]
            zero = jnp.zeros_like(q)
            q2 = jnp.concatenate([jnp.where(dim < B_QK_DIM, q, zero),
                                  jnp.where(dim >= B_QK_DIM, q, zero)], axis=-1)
            jobs.append((kB_s.at[hd], q2, vtB_s.at[hd]))

        def finish(outs):
            for grp in range(A_KV_HEADS):
                ot = outs[grp]
                for pair in range(A_GROUP // 2):
                    c0 = 2 * pair * qb
                    col = A_GROUP * HEAD_DIM * grp + 2 * HEAD_DIM * pair
                    mixed_s[rows, col:col + 2 * HEAD_DIM] = _pair_rows(ot[:, c0:c0 + qb],
                                                                       ot[:, c0 + qb:c0 + 2 * qb])
            obs = []
            for hd in range(B_HEADS):
                ot = outs[A_KV_HEADS + hd]
                ob = ot[:, 0:qb] - lam * ot[:, qb:2 * qb]
                obs.append(ob * lax.rsqrt(jnp.mean(ob * ob, axis=0, keepdims=True) + EPS))
            for pair in range(B_HEADS // 2):
                col = A_WIDTH + 2 * B_V_DIM * pair
                mixed_s[rows, col:col + 2 * B_V_DIM] = (_pair_rows(obs[2 * pair], obs[2 * pair + 1])
                                                        * bog_ref[...] * (1.0 - lam_init))
            if qt == qb:
                project_out()

        if lk <= KEY_TILE:
            finish(_attend_t(jobs))
            return carry

        qa_norm = (jnp.max(jnp.abs(aqg_ref[...]), axis=-1, keepdims=True)
                   * (HEAD_DIM ** 0.5 * HEAD_DIM ** -0.5 * LOG2E))
        stat_t = qstat_ref[...]
        shifts, gaps = [], []
        for j, (_, q, _) in enumerate(jobs):
            k_norm = jnp.sqrt(kn2_s[j:j + 1, 0:1])
            if j < A_KV_HEADS:
                upper = qa_norm * k_norm * BOUND_SLACK
                shifts.append(jnp.broadcast_to(upper, (1, q.shape[1])))
                gaps.append(2.0 * upper)
            else:
                r0 = 2 * (j - A_KV_HEADS)
                r1 = r0 + 2 * B_HEADS
                qn2 = jnp.concatenate([stat_t[r0:r0 + 1], stat_t[r0 + 1:r0 + 2]], axis=-1)
                lower = jnp.concatenate([stat_t[r1:r1 + 1], stat_t[r1 + 1:r1 + 2]], axis=-1)
                upper = jnp.sqrt(qn2) * k_norm * BOUND_SLACK
                shifts.append(upper)
                gaps.append(jnp.max(upper - lower, axis=-1, keepdims=True))
        worst = functools.reduce(jnp.maximum, gaps)
        safe = worst[0, 0] <= SAFE_GAP

        @pl.when(safe)
        def _bounded():
            finish(_attend_t(jobs, shifts))

        @pl.when(jnp.logical_not(safe))
        def _exact_max():
            finish(_attend_t(jobs))

        return carry

    if qt != qb:
        lax.fori_loop(0, qt // qb, attn_block, 0)
        project_out()
    else:
        attn_block(0, 0)


def _mix_kernel(nseq, *params_and_refs):
    params, refs = params_and_refs[:N_MIX_PARAMS], params_and_refs[N_MIX_PARAMS:]
    if nseq == 1:
        return _mix_sequence(*params, *refs)
    n_aliased = params[3]
    shared = {1} | set(range(N_MIX_SEQ_INPUTS, N_MIX_SEQ_INPUTS + N_MIX_WEIGHTS + n_aliased))
    for bi in range(nseq):
        _mix_sequence(*params, *[r if idx in shared else r.at[bi] for idx, r in enumerate(refs)])


def _mixer(x, mod, per_batch_mod, layer, proj, wts, lam_init, cache, state_prev=None):
    bsz, seq, _ = x.shape
    cached = cache is not None
    past = cache[0].shape[4] if cached else 0
    qa, ka, qb, kb, cq, ck, cv, gf, gb, su, vat, vbt = proj[:12]
    qt = 256
    qb_rows = 256
    nseq = 1 if cached else 4
    lead = None if nseq == 1 else nseq

    def per_layer(shape):
        return pl.BlockSpec((None,) + shape, lambda b, t: (layer,) + (0,) * len(shape))

    def layer_heads_t(n, length):
        return pl.BlockSpec((lead, None, n, HEAD_DIM, length), lambda b, t: (b, layer, 0, 0, 0))

    def heads_t(n):
        if cached:
            return pl.BlockSpec((lead, n, HEAD_DIM, seq), lambda b, t: (b, 0, 0, 0))
        return layer_heads_t(n, seq)

    def heads(n):
        return pl.BlockSpec((lead, n, seq, HEAD_DIM), lambda b, t: (b, 0, 0, 0))

    def head_tile(n):
        return pl.BlockSpec((lead, n, HEAD_DIM, qt), lambda b, t: (b, 0, 0, t))

    def rows(width):
        return pl.BlockSpec((lead, seq, width), lambda b, t: (b, 0, 0))

    def row_tile(width):
        return pl.BlockSpec((lead, qt, width), lambda b, t: (b, t, 0))

    mod_idx = (lambda b, t: (layer, b, 0, 0)) if per_batch_mod else (lambda b, t: (layer, 0, 0, 0))
    in_specs = [row_tile(D_MODEL), pl.BlockSpec((None, None, 1, 3 * D_MODEL), mod_idx),
                head_tile(A_HEADS), heads(A_KV_HEADS), head_tile(B_HEADS), heads(B_HEADS),
                rows(C_KW), rows(C_KW), rows(C_WIDTH), rows(C_KW), rows(C_KW), row_tile(D_MIX),
                heads_t(A_KV_HEADS), heads_t(B_HEADS)]
    args = [x, mod, qa, ka, qb, kb, cq, ck, cv, gf, gb, su, vat, vbt]
    if cached:
        state_in = pl.BlockSpec((None, None, C_HEADS, C_DK, C_DV), lambda b, t: (b, layer, 0, 0, 0))
        in_specs += [layer_heads_t(A_KV_HEADS, past), layer_heads_t(A_KV_HEADS, past),
                     layer_heads_t(B_HEADS, past), layer_heads_t(B_HEADS, past), state_in, state_in,
                     pl.BlockSpec((None, QSTAT_ROWS, qt), lambda b, t: (b, 0, t)),
                     per_layer((1, HEAD_DIM))]
        args += list(cache) + [proj[12], wts["aq_gain"]]
    in_specs += [per_layer((D_MIX, D_MODEL)), per_layer((1, D_MODEL)), per_layer((1, 2 * B_V_DIM)),
                 per_layer((1, C_WIDTH)), per_layer((4, B_QK_DIM))]
    args += [wts["w_out"], wts["g_post"], wts["b_out_gain"], wts["c_out_gain"], wts["lam_params"]]

    out_specs = [row_tile(D_MODEL)]
    out_shape = [jax.ShapeDtypeStruct((bsz, seq, D_MODEL), F32)]
    aliases = {}
    first_state = not cached and state_prev is None
    if not cached:
        if first_state:
            state_out = pl.BlockSpec((lead, DEPTH, C_HEADS, C_DK, C_DV), lambda b, t: (b, 0, 0, 0, 0))
        else:
            state_out = pl.BlockSpec((lead, None, C_HEADS, C_DK, C_DV),
                                     lambda b, t: (b, layer, 0, 0, 0))
            for j, buf in enumerate(state_prev):
                aliases[len(args)] = 1 + j
                in_specs.append(pl.BlockSpec(memory_space=pl.ANY))
                args.append(buf)
        out_specs += [state_out, state_out]
        out_shape += [jax.ShapeDtypeStruct((bsz, DEPTH, C_HEADS, C_DK, C_DV), F32)] * 2

    lk = past + seq
    per_seq = () if nseq == 1 else (nseq,)
    scratch = [pltpu.VMEM(per_seq + shape, dtype) for shape, dtype in (
        ((A_KV_HEADS, lk, HEAD_DIM), BF16), ((A_KV_HEADS, VT_ROWS, lk), BF16),
        ((B_HEADS, lk, HEAD_DIM), BF16), ((B_HEADS, VT_ROWS, lk), BF16),
        ((seq, C_WIDTH), F32), ((qt, D_MIX), F32), ((8, LANES), F32))]
    return pl.pallas_call(
        functools.partial(_mix_kernel, nseq, cached, layer, first_state, len(aliases), lam_init, seq,
                          qt, qb_rows),
        grid=(bsz // nseq, seq // qt),
        in_specs=in_specs,
        out_specs=out_specs,
        out_shape=out_shape,
        scratch_shapes=scratch,
        input_output_aliases=aliases,
        compiler_params=pltpu.CompilerParams(
            dimension_semantics=("arbitrary", "arbitrary"), vmem_limit_bytes=VMEM_LIMIT),
        name="mixer_cached" if cached else "mixer",
    )(*args)


def _rope_tables(seq):
    t = jnp.arange(seq)
    pos_row = (t // GRID_W).astype(F32)
    pos_col = (t % GRID_W).astype(F32)

    def tables(half):
        freq = ROPE_THETA ** (-jnp.arange(half, dtype=F32) / half)
        ang_r = freq[:, None] * pos_row[None, :]
        ang_c = freq[:, None] * pos_col[None, :]
        cos = jnp.concatenate([jnp.cos(ang_r), jnp.cos(ang_r), jnp.cos(ang_c), jnp.cos(ang_c)], axis=0)
        sin = jnp.concatenate([-jnp.sin(ang_r), jnp.sin(ang_r), -jnp.sin(ang_c), jnp.sin(ang_c)], axis=0)
        reps = HEAD_DIM // (4 * half)
        return jnp.tile(cos, (reps, 1)), jnp.tile(sin, (reps, 1))

    cos_a, sin_a = tables(HEAD_DIM // 4)
    cos_b, sin_b = tables(B_QK_DIM // 4)
    return cos_a, sin_a, cos_b, sin_b


def _prepare_weights(g_pre, g_post, w_in, w_out, a_q_gain, a_k_gain, b_lambda_q1, b_lambda_k1,
                     b_lambda_q2, b_lambda_k2, b_out_gain, c_gate_w_fwd, c_gate_b_fwd, c_gate_w_bwd,
                     c_gate_b_bwd, c_out_gain):
    w_in_t = jnp.swapaxes(w_in, 1, 2).astype(BF16)
    pad = jnp.zeros((DEPTH, GATE_RANK, C_KW), F32)
    cw_f = jnp.concatenate([c_gate_w_fwd, pad], axis=1).astype(BF16)
    cw_b = jnp.concatenate([pad, c_gate_w_bwd], axis=1).astype(BF16)
    return {
        "g_pre": g_pre[:, None, :],
        "g_post": g_post[:, None, :],
        "w_in_t": w_in_t,
        "w_out": w_out.astype(BF16),
        "aq_gain": a_q_gain[:, None, :],
        "aq_gain_col": a_q_gain[:, :, None],
        "ak_gain_col": a_k_gain[:, :, None],
        "cw_f": cw_f,
        "cb_f": c_gate_b_fwd[:, None, :],
        "cw_b": cw_b,
        "cb_b": c_gate_b_bwd[:, None, :],
        "b_out_gain": jnp.tile(b_out_gain, (1, 2))[:, None, :],
        "c_out_gain": jnp.tile(c_out_gain, (1, C_HEADS))[:, None, :],
        "lam_params": jnp.stack([b_lambda_q1, b_lambda_k1, b_lambda_q2, b_lambda_k2], axis=1),
    }


def kernel(x_prompt, x_sample, c, cache_a_k, cache_a_v, cache_b_k, cache_b_v, state_c_fwd, state_c_bwd, c_ctx, w_mod, b_mod, g_pre, g_post, w_in, w_out, a_q_gain, a_k_gain, b_lambda_q1, b_lambda_k1, b_lambda_q2, b_lambda_k2, b_out_gain, c_gate_w_fwd, c_gate_b_fwd, c_gate_w_bwd, c_gate_b_bwd, c_out_gain):
    dec_batch = x_sample.shape[0]
    dec_seq = x_sample.shape[1]

    mod_rows = 16
    cvec = jnp.zeros((mod_rows, D_MODEL), F32).at[0:dec_batch].set(c).at[dec_batch].set(c_ctx)
    mod = _modulation(cvec, w_mod, b_mod)[:, :, None, :]
    mod_lat = mod[:, 0:dec_batch]
    mod_ctx = mod[:, dec_batch:dec_batch + 1]

    wts = _prepare_weights(g_pre, g_post, w_in, w_out, a_q_gain, a_k_gain, b_lambda_q1, b_lambda_k1,
                           b_lambda_q2, b_lambda_k2, b_out_gain, c_gate_w_fwd, c_gate_b_fwd,
                           c_gate_w_bwd, c_gate_b_bwd, c_out_gain)
    rope_tabs = _rope_tables(dec_seq)
    cache = tuple(jnp.swapaxes(a, -1, -2) for a in (cache_a_k, cache_a_v, cache_b_k, cache_b_v))
    cache += (state_c_fwd, state_c_bwd)

    y_p, y_s = x_prompt, x_sample
    kv_ctx = None
    states = None
    for l in range(DEPTH):
        lam_init = 0.8 - 0.6 * math.exp(-0.3 * l)
        proj_p = _in_projection(y_p, mod_ctx, False, l, wts, None, 4, x_prompt.shape[1], kv_ctx)
        kv_ctx = proj_p[10:14]
        y_p, *states = _mixer(y_p, mod_ctx, False, l, proj_p, wts, lam_init, None, states)

        proj_s = _in_projection(y_s, mod_lat, True, l, wts, rope_tabs, 1, 1024)
        (y_s,) = _mixer(y_s, mod_lat, True, l, proj_s, wts, lam_init, cache)

    va_t, vb_t, ka_t, kb_t = kv_ctx
    new_kv = [jnp.swapaxes(a, -1, -2) for a in (ka_t, va_t, kb_t, vb_t)]
    return (y_p, y_s, *new_kv, *states)
```

```python
import functools
import math

import jax
import jax.numpy as jnp
from jax import lax
from jax.experimental import pallas as pl
from jax.experimental.pallas import tpu as pltpu

F32 = jnp.float32
BF16 = jnp.bfloat16

D_MODEL = 1024
DEPTH = 2
GRID_W = 64
HEAD_DIM = 64
A_HEADS = 8
A_KV_HEADS = 2
A_GROUP = A_HEADS // A_KV_HEADS
A_WIDTH = A_HEADS * HEAD_DIM
A_KV_WIDTH = A_KV_HEADS * HEAD_DIM
B_HEADS = 4
B_QK_DIM = 32
B_V_DIM = 64
B_WIDTH = B_HEADS * B_V_DIM
C_HEADS = 4
C_DK = 32
C_DV = 64
C_KW = C_HEADS * C_DK
C_WIDTH = C_HEADS * C_DV
GATE_RANK = 16
GLA_TAU = 16.0
CHUNK = 64
D_MIX = A_WIDTH + B_WIDTH + C_WIDTH
ROPE_THETA = 10000.0
EPS = 1e-6

LANES = 128
GLA_BLOCK = 256
VT_ROWS = HEAD_DIM + 16
SCORE_LOOKAHEAD = 6
SAFE_GAP = 96.0
BOUND_SLACK = 1.02
QSTAT_ROWS = 4 * B_HEADS
U_TILE = D_MIX // 4
N_MIX_PARAMS = 8
N_MIX_SEQ_INPUTS = 14
N_MIX_WEIGHTS = 5
KEY_TILE = 256
LOG2E = math.log2(math.e)

OFF_AQ = 0
OFF_AK = OFF_AQ + A_WIDTH
OFF_AV = OFF_AK + A_KV_WIDTH
OFF_BQ = OFF_AV + A_KV_WIDTH
OFF_BK = OFF_BQ + B_WIDTH
OFF_BV = OFF_BK + B_WIDTH
OFF_CQ = OFF_BV + B_WIDTH
OFF_CK = OFF_CQ + C_KW
OFF_CV = OFF_CK + C_KW
OFF_LR = OFF_CV + C_WIDTH
OFF_U = OFF_LR + 2 * GATE_RANK
IN_WIDTH = OFF_U + D_MIX

VMEM_LIMIT = 60 * 1024 * 1024


def _dot(a, b):
    return jnp.dot(a, b, preferred_element_type=F32)


def _dot_nt(a, b):
    return lax.dot_general(a, b, (((1,), (1,)), ((), ())), preferred_element_type=F32)


def _dot_tn(a, b):
    return lax.dot_general(a, b, (((0,), (0,)), ((), ())), preferred_element_type=F32)


def _split_bf16(x):
    hi = x.astype(BF16)
    lo = (x - hi.astype(F32)).astype(BF16)
    return hi, lo


def _iota(shape, dim):
    return lax.broadcasted_iota(jnp.int32, shape, dim)


def _group_mean_sq(x, group_log2, split=True):
    width = x.shape[-1]
    r = lax.shift_right_logical(_iota((LANES, LANES), 0), group_log2)
    c = lax.shift_right_logical(_iota((LANES, LANES), 1), group_log2)
    ones = jnp.where(r == c, 1.0, 0.0).astype(BF16)
    if split:
        hi, lo = _split_bf16(x * x)
    else:
        hi, lo = (x * x).astype(BF16), None
    cols = []
    for j in range(width // LANES):
        sl = slice(LANES * j, LANES * (j + 1))
        cols.append(_dot(hi[:, sl], ones) + (_dot(lo[:, sl], ones) if split else 0.0))
    ss = cols[0] if len(cols) == 1 else jnp.concatenate(cols, axis=-1)
    return ss * (1.0 / (1 << group_log2))


def _log_sigmoid(x):
    return jnp.minimum(x, 0.0) - jnp.log1p(jnp.exp(-jnp.abs(x)))


def _silu(x):
    return x * (1.0 / (1.0 + jnp.exp(-x)))


def _mod_kernel(c_ref, w_ref, b_ref, o_ref):
    a = _silu(c_ref[...]).astype(BF16)
    o_ref[...] = _dot(a, w_ref[...].astype(BF16)) + b_ref[...]


def _modulation(cvec, w_mod, b_mod):
    rows = cvec.shape[0]
    tile = D_MODEL // 2
    return pl.pallas_call(
        _mod_kernel,
        grid=(DEPTH, 3 * D_MODEL // tile),
        in_specs=[
            pl.BlockSpec((rows, D_MODEL), lambda l, n: (0, 0)),
            pl.BlockSpec((None, D_MODEL, tile), lambda l, n: (l, 0, n)),
            pl.BlockSpec((None, 1, tile), lambda l, n: (l, 0, n)),
        ],
        out_specs=pl.BlockSpec((None, rows, tile), lambda l, n: (l, 0, n)),
        out_shape=jax.ShapeDtypeStruct((DEPTH, rows, 3 * D_MODEL), F32),
        compiler_params=pltpu.CompilerParams(
            dimension_semantics=("arbitrary", "arbitrary"), vmem_limit_bytes=VMEM_LIMIT),
        name="modulation",
    )(cvec, w_mod, b_mod.reshape(DEPTH, 1, 3 * D_MODEL))


def _in_kernel(rope, layer, stacked_first, nb, tl, n_aliased, *refs):
    (x_ref, mod_ref, gpre_ref, wt_ref, aqg_ref, akg_ref, cwf_ref, cbf_ref, cwb_ref, cbb_ref) = refs[:10]
    refs = refs[10:]
    if rope:
        cosa_ref, sina_ref, cosb_ref, sinb_ref = refs[:4]
        refs = refs[4:]
    refs = refs[n_aliased:]
    (qa_ref, ka_ref, qb_ref, kb_ref, cq_ref, ck_ref, cv_ref, gf_ref, gb_ref, su_ref,
     vat_ref, vbt_ref) = refs[:12]
    kat_ref, kbt_ref = (None, None) if rope else refs[12:]
    qstat_ref = refs[12] if rope else None

    x = x_ref[...].reshape(nb * tl, D_MODEL)
    shift = mod_ref[:, 0:D_MODEL]
    scale = mod_ref[:, D_MODEL:2 * D_MODEL]
    ms = jnp.mean(x * x, axis=-1, keepdims=True)
    h = (x * lax.rsqrt(ms + EPS)) * gpre_ref[...] * (1.0 + scale) + shift
    hb = h.astype(BF16)

    def proj(off, width):
        return _dot_nt(hb, wt_ref[off:off + width, :])

    def proj_t(off, width):
        return _dot_nt(wt_ref[off:off + width, :], hb)

    def put_rows(ref, val):
        for bi in range(nb):
            ref[bi] = val[bi * tl:(bi + 1) * tl].astype(ref.dtype)

    def put_heads(ref, val, n_heads):
        for bi in range(nb):
            for hd in range(n_heads):
                ref[bi, hd] = val[bi * tl:(bi + 1) * tl,
                                  HEAD_DIM * hd:HEAD_DIM * (hd + 1)].astype(ref.dtype)

    def put_heads_t(ref, heads_t, stacked=False):
        for bi in range(nb):
            for hd, val_t in enumerate(heads_t):
                blk = val_t[:, bi * tl:(bi + 1) * tl].astype(ref.dtype)
                if stacked and stacked_first:
                    for l2 in range(DEPTH):
                        ref[bi, l2, hd] = blk if l2 == layer else jnp.zeros_like(blk)
                else:
                    ref[bi, hd] = blk

    def split_heads_t(val_t, n_heads):
        return [val_t[HEAD_DIM * hd:HEAD_DIM * (hd + 1)] for hd in range(n_heads)]

    def rms_t(head_t, gain_col):
        ms_h = jnp.mean(head_t * head_t, axis=0, keepdims=True)
        return head_t * lax.rsqrt(ms_h + EPS) * gain_col

    def rope_t(head_t, cos_t, sin_t, dist):
        blocks = [head_t[r0:r0 + dist] for r0 in range(0, HEAD_DIM, dist)]
        partner = jnp.concatenate([blocks[j ^ 1] for j in range(len(blocks))], axis=0)
        return head_t * cos_t + partner * sin_t

    def gate_tile(j):
        c0 = U_TILE * j
        val = _silu(proj(OFF_U + c0, U_TILE))
        for bi in range(nb):
            su_ref[bi, :, c0:c0 + U_TILE] = val[bi * tl:(bi + 1) * tl]

    za_t = proj_t(OFF_AQ, A_WIDTH + 2 * A_KV_WIDTH)
    zb_t = proj_t(OFF_BQ, 3 * B_WIDTH)
    lr_t = proj_t(OFF_LR, 2 * GATE_RANK).astype(BF16)
    gate_tile(0)

    aq_t = [rms_t(h_t, aqg_ref[...]) for h_t in split_heads_t(za_t[0:A_WIDTH], A_HEADS)]
    ak_t = [rms_t(h_t, akg_ref[...])
            for h_t in split_heads_t(za_t[A_WIDTH:A_WIDTH + A_KV_WIDTH], A_KV_HEADS)]
    if rope:
        aq_t = [rope_t(h_t, cosa_ref[...], sina_ref[...], 16) for h_t in aq_t]
        ak_t = [rope_t(h_t, cosa_ref[...], sina_ref[...], 16) for h_t in ak_t]
    put_heads_t(qa_ref, [h_t * (HEAD_DIM ** -0.5 * LOG2E) for h_t in aq_t])
    put_heads(ka_ref, jnp.concatenate(ak_t, axis=0).T, A_KV_HEADS)
    put_heads_t(vat_ref, split_heads_t(za_t[A_WIDTH + A_KV_WIDTH:], A_KV_HEADS), stacked=not rope)
    if kat_ref is not None:
        put_heads_t(kat_ref, ak_t, stacked=True)
    gate_tile(1)
    cqk = proj(OFF_CQ, 2 * C_KW)
    cv = proj(OFF_CV, C_WIDTH)

    bq_t = split_heads_t(zb_t[0:B_WIDTH], B_HEADS)
    bk_t = split_heads_t(zb_t[B_WIDTH:2 * B_WIDTH], B_HEADS)
    if rope:
        bq_t = [rope_t(h_t, cosb_ref[...], sinb_ref[...], 8) for h_t in bq_t]
        bk_t = [rope_t(h_t, cosb_ref[...], sinb_ref[...], 8) for h_t in bk_t]
    bq_t = [h_t * (B_QK_DIM ** -0.5 * LOG2E) for h_t in bq_t]
    put_heads_t(qb_ref, bq_t)
    put_heads(kb_ref, jnp.concatenate(bk_t, axis=0).T, B_HEADS)
    if rope:
        halves = [(hd, r0) for hd in range(B_HEADS) for r0 in (0, B_QK_DIM)]
        qstat_ref[0] = jnp.concatenate(
            [jnp.sum(bq_t[hd][r0:r0 + B_QK_DIM] * bq_t[hd][r0:r0 + B_QK_DIM], axis=0, keepdims=True)
             for hd, r0 in halves]
            + [jnp.sum(bq_t[hd][r0:r0 + B_QK_DIM] * bk_t[hd][r0:r0 + B_QK_DIM], axis=0, keepdims=True)
               for hd, r0 in halves], axis=0)
    put_heads_t(vbt_ref, split_heads_t(zb_t[2 * B_WIDTH:], B_HEADS), stacked=not rope)
    if kbt_ref is not None:
        put_heads_t(kbt_ref, bk_t, stacked=True)
    gate_tile(2)
    gf_pre = _dot_tn(lr_t, cwf_ref[...])
    gb_pre = _dot_tn(lr_t, cwb_ref[...])

    put_rows(cq_ref, cqk[:, 0:C_KW] * (C_DK ** -0.5))
    put_rows(ck_ref, cqk[:, C_KW:2 * C_KW])
    put_rows(cv_ref, cv)
    put_rows(gf_ref, _log_sigmoid(gf_pre + cbf_ref[...]) * (1.0 / GLA_TAU))
    put_rows(gb_ref, _log_sigmoid(gb_pre + cbb_ref[...]) * (1.0 / GLA_TAU))
    gate_tile(3)


def _in_projection(x, mod, per_batch_mod, layer, wts, rope_tabs, nb, tl, kv_prev=None):
    bsz, seq, _ = x.shape
    rope = rope_tabs is not None
    stacked = not rope
    stacked_first = stacked and kv_prev is None
    grid = (bsz // nb, seq // tl)

    def per_layer(shape):
        return pl.BlockSpec((None,) + shape, lambda b, t: (layer,) + (0,) * len(shape))

    mod_idx = (lambda b, t: (layer, b, 0, 0)) if per_batch_mod else (lambda b, t: (layer, 0, 0, 0))
    in_specs = [
        pl.BlockSpec((nb, tl, D_MODEL), lambda b, t: (b, t, 0)),
        pl.BlockSpec((None, None, 1, 3 * D_MODEL), mod_idx),
        per_layer((1, D_MODEL)),
        per_layer((IN_WIDTH, D_MODEL)),
        per_layer((HEAD_DIM, 1)),
        per_layer((HEAD_DIM, 1)),
        per_layer((2 * GATE_RANK, C_KW)),
        per_layer((1, C_KW)),
        per_layer((2 * GATE_RANK, C_KW)),
        per_layer((1, C_KW)),
    ]
    args = [x, mod, wts["g_pre"], wts["w_in_t"], wts["aq_gain_col"], wts["ak_gain_col"],
            wts["cw_f"], wts["cb_f"], wts["cw_b"], wts["cb_b"]]
    if rope:
        assert nb == 1
        in_specs += [
            pl.BlockSpec((HEAD_DIM, tl), lambda b, t: (0, t)),
            pl.BlockSpec((HEAD_DIM, tl), lambda b, t: (0, t)),
            pl.BlockSpec((HEAD_DIM, tl), lambda b, t: (0, t)),
            pl.BlockSpec((HEAD_DIM, tl), lambda b, t: (0, t)),
        ]
        args += list(rope_tabs)

    def heads(n):
        return pl.BlockSpec((nb, n, tl, HEAD_DIM), lambda b, t: (b, 0, t, 0))

    def heads_t(n):
        if stacked_first:
            return pl.BlockSpec((nb, DEPTH, n, HEAD_DIM, tl), lambda b, t: (b, 0, 0, 0, t))
        if stacked:
            return pl.BlockSpec((nb, None, n, HEAD_DIM, tl), lambda b, t: (b, layer, 0, 0, t))
        return pl.BlockSpec((nb, n, HEAD_DIM, tl), lambda b, t: (b, 0, 0, t))

    def q_heads_t(n):
        return pl.BlockSpec((nb, n, HEAD_DIM, tl), lambda b, t: (b, 0, 0, t))

    def rows(width):
        return pl.BlockSpec((nb, tl, width), lambda b, t: (b, t, 0))

    def hshape(n):
        return jax.ShapeDtypeStruct((bsz, n, seq, HEAD_DIM), BF16)

    def qshape(n):
        return jax.ShapeDtypeStruct((bsz, n, HEAD_DIM, seq), BF16)

    def tshape(n):
        if stacked:
            return jax.ShapeDtypeStruct((bsz, DEPTH, n, HEAD_DIM, seq), F32)
        return jax.ShapeDtypeStruct((bsz, n, HEAD_DIM, seq), BF16)

    def rshape(width):
        return jax.ShapeDtypeStruct((bsz, seq, width), F32)

    out_specs = [q_heads_t(A_HEADS), heads(A_KV_HEADS), q_heads_t(B_HEADS), heads(B_HEADS),
                 rows(C_KW), rows(C_KW), rows(C_WIDTH), rows(C_KW), rows(C_KW), rows(D_MIX),
                 heads_t(A_KV_HEADS), heads_t(B_HEADS)]
    out_shape = [qshape(A_HEADS), hshape(A_KV_HEADS), qshape(B_HEADS), hshape(B_HEADS),
                 rshape(C_KW), rshape(C_KW), rshape(C_WIDTH), rshape(C_KW), rshape(C_KW),
                 rshape(D_MIX), tshape(A_KV_HEADS), tshape(B_HEADS)]
    if stacked:
        out_specs += [heads_t(A_KV_HEADS), heads_t(B_HEADS)]
        out_shape += [tshape(A_KV_HEADS), tshape(B_HEADS)]
    else:
        out_specs += [pl.BlockSpec((nb, QSTAT_ROWS, tl), lambda b, t: (b, 0, t))]
        out_shape += [jax.ShapeDtypeStruct((bsz, QSTAT_ROWS, seq), F32)]
    aliases = {}
    if kv_prev is not None:
        for j, buf in enumerate(kv_prev):
            aliases[len(args)] = 10 + j
            in_specs.append(pl.BlockSpec(memory_space=pl.ANY))
            args.append(buf)
    return pl.pallas_call(
        functools.partial(_in_kernel, rope, layer, stacked_first, nb, tl, len(aliases)),
        grid=grid,
        in_specs=in_specs,
        out_specs=out_specs,
        out_shape=out_shape,
        input_output_aliases=aliases,
        compiler_params=pltpu.CompilerParams(
            dimension_semantics=("arbitrary", "arbitrary"), vmem_limit_bytes=VMEM_LIMIT),
        name="in_projection_rope" if rope else "in_projection",
    )(*args)


def _gla_bidirectional(cq_ref, ck_ref, cv_ref, gf_ref, gb_ref, s_f, s_b, seq, oc_ref):
    bl = GLA_BLOCK
    n_sub = bl // CHUNK
    nblk = seq // bl
    ri = _iota((bl, bl), 0)
    ci = _iota((bl, bl), 1)
    same_chunk = lax.shift_right_logical(ri, 6) == lax.shift_right_logical(ci, 6)
    bd = (lax.shift_right_logical(_iota((C_WIDTH, C_KW), 0), 6)
          == lax.shift_right_logical(_iota((C_WIDTH, C_KW), 1), 5))
    khead = lax.shift_right_logical(_iota((1, C_KW), 1), 5)
    vhead = lax.shift_right_logical(_iota((1, C_WIDTH), 1), 6)
    scans = []
    for reverse, g_ref in ((False, gf_ref), (True, gb_ref)):
        causal = same_chunk & ((ci >= ri) if reverse else (ci <= ri))
        scans.append((reverse, g_ref, causal, jnp.where(causal, 1.0, 0.0).astype(BF16)))
    states = [s_f, s_b]
    written = set()

    for step in range(nblk):
        rows0 = [step * bl, (nblk - 1 - step) * bl]
        cums = []
        for (reverse, g_ref, causal, tri), r0 in zip(scans, rows0):
            g_hi, g_lo = _split_bf16(g_ref[r0:r0 + bl, :])
            cums.append(_dot(tri, g_hi) + _dot(tri, g_lo))
        prep = []
        for (reverse, g_ref, causal, tri), r0, cum in zip(scans, rows0, cums):
            q = cq_ref[r0:r0 + bl, :]
            k = ck_ref[r0:r0 + bl, :]
            v = cv_ref[r0:r0 + bl, :]
            qt = q * jnp.exp(cum)
            ktb = (k * jnp.exp(-cum)).astype(BF16)
            vb = v.astype(BF16)
            lasts, kdecs = [], []
            for c in range(n_sub):
                c0 = CHUNK * c
                edge = c0 if reverse else c0 + CHUNK - 1
                last = cum[edge:edge + 1, :]
                lasts.append(last)
                kdecs.append((k[c0:c0 + CHUNK] * jnp.exp(last - cum[c0:c0 + CHUNK])).astype(BF16))
            prep.append((qt, ktb, v, vb, lasts, kdecs))
        scores, incs = [], []
        for qt, ktb, v, vb, lasts, kdecs in prep:
            scores.append([_dot_nt(jnp.where(khead == hd, qt, 0.0).astype(BF16), ktb)
                           for hd in range(C_HEADS)])
            incs.append([_dot_tn(vb[CHUNK * c:CHUNK * (c + 1)], kdecs[c]) for c in range(n_sub)])
        probs, entering = [], []
        for si, ((reverse, g_ref, causal, tri), (qt, ktb, v, vb, lasts, kdecs)) in enumerate(
                zip(scans, prep)):
            probs.append([jnp.where(causal, s, 0.0).astype(BF16) for s in scores[si]])
            s_t = states[si]
            before = [None] * n_sub
            subs = range(n_sub)
            for c in (reversed(subs) if reverse else subs):
                before[c] = s_t.astype(BF16)
                s_t = jnp.exp(lasts[c]) * s_t + jnp.where(bd, incs[si][c], 0.0)
            states[si] = s_t
            entering.append(before)
        outs = []
        for si, (qt, ktb, v, vb, lasts, kdecs) in enumerate(prep):
            qtb = qt.astype(BF16)
            o = jnp.concatenate([_dot_nt(qtb[CHUNK * c:CHUNK * (c + 1)], entering[si][c])
                                 for c in range(n_sub)], axis=0)
            for hd in range(C_HEADS):
                o = o + _dot(probs[si][hd], jnp.where(vhead == hd, v, 0.0).astype(BF16))
            outs.append(o)
        if rows0[0] == rows0[1]:
            outs, rows0 = [outs[0] + outs[1]], rows0[:1]
        for o, r0 in zip(outs, rows0):
            if r0 in written:
                oc_ref[r0:r0 + bl, :] = oc_ref[r0:r0 + bl, :] + o
            else:
                oc_ref[r0:r0 + bl, :] = o
                written.add(r0)
    return states[0], states[1]


def _attend_t(jobs, shifts=None):
    def scores(i):
        return _dot(jobs[i][0][...], jobs[i][1])

    outs = []
    pending = [scores(i) for i in range(min(SCORE_LOOKAHEAD, len(jobs)))]
    for i, (_, _, vt) in enumerate(jobs):
        st = pending.pop(0)
        if i + SCORE_LOOKAHEAD < len(jobs):
            pending.append(scores(i + SCORE_LOOKAHEAD))
        m = jnp.max(st, axis=0, keepdims=True) if shifts is None else shifts[i]
        p = jnp.exp2(st - m).astype(BF16)
        ot = _dot(vt[...], p)
        outs.append(ot[0:HEAD_DIM] * (1.0 / ot[HEAD_DIM:HEAD_DIM + 1]))
    return outs


def _pair_rows(a, b):
    return jnp.concatenate([a, b], axis=0).T


def _state_to_blockdiag_t(s_ref):
    rows = []
    for hd in range(C_HEADS):
        pieces = []
        if hd:
            pieces.append(jnp.zeros((C_DK, C_DV * hd), F32))
        pieces.append(s_ref[hd])
        if hd < C_HEADS - 1:
            pieces.append(jnp.zeros((C_DK, C_DV * (C_HEADS - 1 - hd)), F32))
        rows.append(jnp.concatenate(pieces, axis=-1))
    return jnp.concatenate(rows, axis=0).T


def _blockdiag_t_to_state(s_t, out_ref):
    s = s_t.T
    for hd in range(C_HEADS):
        out_ref[hd] = s[C_DK * hd:C_DK * (hd + 1), C_DV * hd:C_DV * (hd + 1)]


def _mix_sequence(cached, layer, first_state, n_aliased, lam_init, seq, qt, qb, *refs):
    it = iter(refs)
    x_ref, mod_ref = next(it), next(it)
    qa_ref, ka_ref, qb_ref, kb_ref = (next(it) for _ in range(4))
    cq_ref, ck_ref, cv_ref, gf_ref, gb_ref, su_ref = (next(it) for _ in range(6))
    vat_ref, vbt_ref = next(it), next(it)
    if cached:
        cakt_ref, cavt_ref, cbkt_ref, cbvt_ref, s0f_ref, s0b_ref = (next(it) for _ in range(6))
        qstat_ref, aqg_ref = next(it), next(it)
    wout_ref, gpost_ref, bog_ref, cog_ref, lamp_ref = (next(it) for _ in range(5))
    for _ in range(n_aliased):
        next(it)
    y_ref = next(it)
    if not cached:
        sf_ref, sb_ref = next(it), next(it)
    kA_s, vtA_s, kB_s, vtB_s, oc_s, mixed_s, kn2_s = (next(it) for _ in range(7))

    lk = kA_s.shape[1]
    past = lk - seq
    t = pl.program_id(1)

    def once_per_sequence(body):
        return body() if seq == qt else pl.when(t == 0)(body)

    @once_per_sequence
    def _per_sequence():
        ones_row = jnp.where(_iota((VT_ROWS - HEAD_DIM, lk), 0) == 0, 1.0, 0.0).astype(BF16)
        for k_new, kt_cache, k_dst, vt_new, vt_cache, vt_dst in (
                (ka_ref, cakt_ref if cached else None, kA_s, vat_ref, cavt_ref if cached else None, vtA_s),
                (kb_ref, cbkt_ref if cached else None, kB_s, vbt_ref, cbvt_ref if cached else None, vtB_s)):
            n_heads = k_dst.shape[0]
            if cached:
                for h0 in range(0, n_heads, 2):
                    pair = jnp.concatenate([kt_cache[h0], kt_cache[h0 + 1]], axis=0).T
                    k_dst[h0, 0:past, :] = pair[:, 0:HEAD_DIM].astype(BF16)
                    k_dst[h0 + 1, 0:past, :] = pair[:, HEAD_DIM:2 * HEAD_DIM].astype(BF16)
            for hd in range(n_heads):
                k_dst[hd, past:lk, :] = k_new[hd]
                if cached:
                    vt_dst[hd, 0:HEAD_DIM, 0:past] = vt_cache[hd].astype(BF16)
                vt_dst[hd, 0:HEAD_DIM, past:lk] = vt_new[hd].astype(BF16)
                vt_dst[hd, HEAD_DIM:VT_ROWS, :] = ones_row
        bounded_jobs = ([(kA_s, g) for g in range(A_KV_HEADS)] + [(kB_s, h) for h in range(B_HEADS)]
                        if lk > KEY_TILE else [])
        for j, (k_dst, hd) in enumerate(bounded_jobs):
            kf = k_dst[hd].astype(F32)
            kn2 = jnp.max(jnp.sum(kf * kf, axis=-1, keepdims=True), axis=0, keepdims=True)
            kn2_s[j:j + 1, :] = jnp.broadcast_to(kn2, (1, LANES))

        if cached:
            s0f, s0b = _state_to_blockdiag_t(s0f_ref), _state_to_blockdiag_t(s0b_ref)
        else:
            s0f = jnp.zeros((C_WIDTH, C_KW), F32)
            s0b = s0f
        s_f, s_b = _gla_bidirectional(cq_ref, ck_ref, cv_ref, gf_ref, gb_ref, s0f, s0b, seq, oc_s)
        if not cached:
            for ref, s_t in ((sf_ref, s_f), (sb_ref, s_b)):
                if first_state:
                    for l2 in range(DEPTH):
                        if l2 == layer:
                            _blockdiag_t_to_state(s_t, ref.at[l2])
                        else:
                            ref[l2] = jnp.zeros(ref.shape[1:], F32)
                else:
                    _blockdiag_t_to_state(s_t, ref)
        for r0 in range(0, seq, GLA_BLOCK):
            oc = oc_s[r0:r0 + GLA_BLOCK, :]
            oc_s[r0:r0 + GLA_BLOCK, :] = oc * lax.rsqrt(_group_mean_sq(oc, 6) + EPS) * cog_ref[...]

    lam = (jnp.exp(jnp.sum(lamp_ref[0:1, :] * lamp_ref[1:2, :], axis=-1, keepdims=True))
           - jnp.exp(jnp.sum(lamp_ref[2:3, :] * lamp_ref[3:4, :], axis=-1, keepdims=True))
           + lam_init)

    def project_out():
        seq_rows = pl.ds(pl.multiple_of(t * qt, qt), qt)
        mixed_s[:, A_WIDTH + B_WIDTH:D_MIX] = oc_s[seq_rows, :]
        gate = mod_ref[:, 2 * D_MODEL:3 * D_MODEL]
        mixed = (mixed_s[...] * su_ref[...]).astype(BF16)
        y = _dot(mixed, wout_ref[...])
        yn = y * lax.rsqrt(jnp.mean(y * y, axis=-1, keepdims=True) + EPS) * gpost_ref[...]
        y_ref[...] = x_ref[...] + gate * yn

    def attn_block(i, carry):
        rows = pl.ds(i * qb, qb) if isinstance(i, int) else pl.ds(pl.multiple_of(i * qb, qb), qb)
        jobs = []
        for grp in range(A_KV_HEADS):
            q4 = jnp.concatenate([qa_ref[A_GROUP * grp + j, :, rows] for j in range(A_GROUP)], axis=-1)
            jobs.append((kA_s.at[grp], q4, vtA_s.at[grp]))
        dim = _iota((HEAD_DIM, qb), 0)
        for hd in range(B_HEADS):
            q = qb_ref[hd, :, rows]
            zero = jnp.zeros_like(q)
            q2 = jnp.concatenate([jnp.where(dim < B_QK_DIM, q, zero),
                                  jnp.where(dim >= B_QK_DIM, q, zero)], axis=-1)
            jobs.append((kB_s.at[hd], q2, vtB_s.at[hd]))

        def finish(outs):
            for grp in range(A_KV_HEADS):
                ot = outs[grp]
                for pair in range(A_GROUP // 2):
                    c0 = 2 * pair * qb
                    col = A_GROUP * HEAD_DIM * grp + 2 * HEAD_DIM * pair
                    mixed_s[rows, col:col + 2 * HEAD_DIM] = _pair_rows(ot[:, c0:c0 + qb],
                                                                       ot[:, c0 + qb:c0 + 2 * qb])
            obs = []
            for hd in range(B_HEADS):
                ot = outs[A_KV_HEADS + hd]
                ob = ot[:, 0:qb] - lam * ot[:, qb:2 * qb]
                obs.append(ob * lax.rsqrt(jnp.mean(ob * ob, axis=0, keepdims=True) + EPS))
            for pair in range(B_HEADS // 2):
                col = A_WIDTH + 2 * B_V_DIM * pair
                mixed_s[rows, col:col + 2 * B_V_DIM] = (_pair_rows(obs[2 * pair], obs[2 * pair + 1])
                                                        * bog_ref[...] * (1.0 - lam_init))
            if qt == qb:
                project_out()

        if lk <= KEY_TILE:
            finish(_attend_t(jobs))
            return carry

        qa_norm = (jnp.max(jnp.abs(aqg_ref[...]), axis=-1, keepdims=True)
                   * (HEAD_DIM ** 0.5 * HEAD_DIM ** -0.5 * LOG2E))
        stat_t = qstat_ref[...]
        shifts, gaps = [], []
        for j, (_, q, _) in enumerate(jobs):
            k_norm = jnp.sqrt(kn2_s[j:j + 1, 0:1])
            if j < A_KV_HEADS:
                upper = qa_norm * k_norm * BOUND_SLACK
                shifts.append(jnp.broadcast_to(upper, (1, q.shape[1])))
                gaps.append(2.0 * upper)
            else:
                r0 = 2 * (j - A_KV_HEADS)
                r1 = r0 + 2 * B_HEADS
                qn2 = jnp.concatenate([stat_t[r0:r0 + 1], stat_t[r0 + 1:r0 + 2]], axis=-1)
                lower = jnp.concatenate([stat_t[r1:r1 + 1], stat_t[r1 + 1:r1 + 2]], axis=-1)
                upper = jnp.sqrt(qn2) * k_norm * BOUND_SLACK
                shifts.append(upper)
                gaps.append(jnp.max(upper - lower, axis=-1, keepdims=True))
        worst = functools.reduce(jnp.maximum, gaps)
        safe = worst[0, 0] <= SAFE_GAP

        @pl.when(safe)
        def _bounded():
            finish(_attend_t(jobs, shifts))

        @pl.when(jnp.logical_not(safe))
        def _exact_max():
            finish(_attend_t(jobs))

        return carry

    if qt != qb:
        lax.fori_loop(0, qt // qb, attn_block, 0)
        project_out()
    else:
        attn_block(0, 0)


def _mix_kernel(nseq, *params_and_refs):
    params, refs = params_and_refs[:N_MIX_PARAMS], params_and_refs[N_MIX_PARAMS:]
    if nseq == 1:
        return _mix_sequence(*params, *refs)
    n_aliased = params[3]
    shared = {1} | set(range(N_MIX_SEQ_INPUTS, N_MIX_SEQ_INPUTS + N_MIX_WEIGHTS + n_aliased))
    for bi in range(nseq):
        _mix_sequence(*params, *[r if idx in shared else r.at[bi] for idx, r in enumerate(refs)])


def _mixer(x, mod, per_batch_mod, layer, proj, wts, lam_init, cache, state_prev=None):
    bsz, seq, _ = x.shape
    cached = cache is not None
    past = cache[0].shape[4] if cached else 0
    qa, ka, qb, kb, cq, ck, cv, gf, gb, su, vat, vbt = proj[:12]
    qt = 256
    qb_rows = 256
    nseq = 1 if cached else 4
    lead = None if nseq == 1 else nseq

    def per_layer(shape):
        return pl.BlockSpec((None,) + shape, lambda b, t: (layer,) + (0,) * len(shape))

    def layer_heads_t(n, length):
        return pl.BlockSpec((lead, None, n, HEAD_DIM, length), lambda b, t: (b, layer, 0, 0, 0))

    def heads_t(n):
        if cached:
            return pl.BlockSpec((lead, n, HEAD_DIM, seq), lambda b, t: (b, 0, 0, 0))
        return layer_heads_t(n, seq)

    def heads(n):
        return pl.BlockSpec((lead, n, seq, HEAD_DIM), lambda b, t: (b, 0, 0, 0))

    def head_tile(n):
        return pl.BlockSpec((lead, n, HEAD_DIM, qt), lambda b, t: (b, 0, 0, t))

    def rows(width):
        return pl.BlockSpec((lead, seq, width), lambda b, t: (b, 0, 0))

    def row_tile(width):
        return pl.BlockSpec((lead, qt, width), lambda b, t: (b, t, 0))

    mod_idx = (lambda b, t: (layer, b, 0, 0)) if per_batch_mod else (lambda b, t: (layer, 0, 0, 0))
    in_specs = [row_tile(D_MODEL), pl.BlockSpec((None, None, 1, 3 * D_MODEL), mod_idx),
                head_tile(A_HEADS), heads(A_KV_HEADS), head_tile(B_HEADS), heads(B_HEADS),
                rows(C_KW), rows(C_KW), rows(C_WIDTH), rows(C_KW), rows(C_KW), row_tile(D_MIX),
                heads_t(A_KV_HEADS), heads_t(B_HEADS)]
    args = [x, mod, qa, ka, qb, kb, cq, ck, cv, gf, gb, su, vat, vbt]
    if cached:
        state_in = pl.BlockSpec((None, None, C_HEADS, C_DK, C_DV), lambda b, t: (b, layer, 0, 0, 0))
        in_specs += [layer_heads_t(A_KV_HEADS, past), layer_heads_t(A_KV_HEADS, past),
                     layer_heads_t(B_HEADS, past), layer_heads_t(B_HEADS, past), state_in, state_in,
                     pl.BlockSpec((None, QSTAT_ROWS, qt), lambda b, t: (b, 0, t)),
                     per_layer((1, HEAD_DIM))]
        args += list(cache) + [proj[12], wts["aq_gain"]]
    in_specs += [per_layer((D_MIX, D_MODEL)), per_layer((1, D_MODEL)), per_layer((1, 2 * B_V_DIM)),
                 per_layer((1, C_WIDTH)), per_layer((4, B_QK_DIM))]
    args += [wts["w_out"], wts["g_post"], wts["b_out_gain"], wts["c_out_gain"], wts["lam_params"]]

    out_specs = [row_tile(D_MODEL)]
    out_shape = [jax.ShapeDtypeStruct((bsz, seq, D_MODEL), F32)]
    aliases = {}
    first_state = not cached and state_prev is None
    if not cached:
        if first_state:
            state_out = pl.BlockSpec((lead, DEPTH, C_HEADS, C_DK, C_DV), lambda b, t: (b, 0, 0, 0, 0))
        else:
            state_out = pl.BlockSpec((lead, None, C_HEADS, C_DK, C_DV),
                                     lambda b, t: (b, layer, 0, 0, 0))
            for j, buf in enumerate(state_prev):
                aliases[len(args)] = 1 + j
                in_specs.append(pl.BlockSpec(memory_space=pl.ANY))
                args.append(buf)
        out_specs += [state_out, state_out]
        out_shape += [jax.ShapeDtypeStruct((bsz, DEPTH, C_HEADS, C_DK, C_DV), F32)] * 2

    lk = past + seq
    per_seq = () if nseq == 1 else (nseq,)
    scratch = [pltpu.VMEM(per_seq + shape, dtype) for shape, dtype in (
        ((A_KV_HEADS, lk, HEAD_DIM), BF16), ((A_KV_HEADS, VT_ROWS, lk), BF16),
        ((B_HEADS, lk, HEAD_DIM), BF16), ((B_HEADS, VT_ROWS, lk), BF16),
        ((seq, C_WIDTH), F32), ((qt, D_MIX), F32), ((8, LANES), F32))]
    return pl.pallas_call(
        functools.partial(_mix_kernel, nseq, cached, layer, first_state, len(aliases), lam_init, seq,
                          qt, qb_rows),
        grid=(bsz // nseq, seq // qt),
        in_specs=in_specs,
        out_specs=out_specs,
        out_shape=out_shape,
        scratch_shapes=scratch,
        input_output_aliases=aliases,
        compiler_params=pltpu.CompilerParams(
            dimension_semantics=("arbitrary", "arbitrary"), vmem_limit_bytes=VMEM_LIMIT),
        name="mixer_cached" if cached else "mixer",
    )(*args)


def _rope_tables(seq):
    t = jnp.arange(seq)
    pos_row = (t // GRID_W).astype(F32)
    pos_col = (t % GRID_W).astype(F32)

    def tables(half):
        freq = ROPE_THETA ** (-jnp.arange(half, dtype=F32) / half)
        ang_r = freq[:, None] * pos_row[None, :]
        ang_c = freq[:, None] * pos_col[None, :]
        cos = jnp.concatenate([jnp.cos(ang_r), jnp.cos(ang_r), jnp.cos(ang_c), jnp.cos(ang_c)], axis=0)
        sin = jnp.concatenate([-jnp.sin(ang_r), jnp.sin(ang_r), -jnp.sin(ang_c), jnp.sin(ang_c)], axis=0)
        reps = HEAD_DIM // (4 * half)
        return jnp.tile(cos, (reps, 1)), jnp.tile(sin, (reps, 1))

    cos_a, sin_a = tables(HEAD_DIM // 4)
    cos_b, sin_b = tables(B_QK_DIM // 4)
    return cos_a, sin_a, cos_b, sin_b


def _prepare_weights(g_pre, g_post, w_in, w_out, a_q_gain, a_k_gain, b_lambda_q1, b_lambda_k1,
                     b_lambda_q2, b_lambda_k2, b_out_gain, c_gate_w_fwd, c_gate_b_fwd, c_gate_w_bwd,
                     c_gate_b_bwd, c_out_gain):
    w_in_t = jnp.swapaxes(w_in, 1, 2).astype(BF16)
    pad = jnp.zeros((DEPTH, GATE_RANK, C_KW), F32)
    cw_f = jnp.concatenate([c_gate_w_fwd, pad], axis=1).astype(BF16)
    cw_b = jnp.concatenate([pad, c_gate_w_bwd], axis=1).astype(BF16)
    return {
        "g_pre": g_pre[:, None, :],
        "g_post": g_post[:, None, :],
        "w_in_t": w_in_t,
        "w_out": w_out.astype(BF16),
        "aq_gain": a_q_gain[:, None, :],
        "aq_gain_col": a_q_gain[:, :, None],
        "ak_gain_col": a_k_gain[:, :, None],
        "cw_f": cw_f,
        "cb_f": c_gate_b_fwd[:, None, :],
        "cw_b": cw_b,
        "cb_b": c_gate_b_bwd[:, None, :],
        "b_out_gain": jnp.tile(b_out_gain, (1, 2))[:, None, :],
        "c_out_gain": jnp.tile(c_out_gain, (1, C_HEADS))[:, None, :],
        "lam_params": jnp.stack([b_lambda_q1, b_lambda_k1, b_lambda_q2, b_lambda_k2], axis=1),
    }


def kernel(x_prompt, x_sample, c, cache_a_k, cache_a_v, cache_b_k, cache_b_v, state_c_fwd, state_c_bwd, c_ctx, w_mod, b_mod, g_pre, g_post, w_in, w_out, a_q_gain, a_k_gain, b_lambda_q1, b_lambda_k1, b_lambda_q2, b_lambda_k2, b_out_gain, c_gate_w_fwd, c_gate_b_fwd, c_gate_w_bwd, c_gate_b_bwd, c_out_gain):
    dec_batch = x_sample.shape[0]
    dec_seq = x_sample.shape[1]

    mod_rows = 16
    cvec = jnp.zeros((mod_rows, D_MODEL), F32).at[0:dec_batch].set(c).at[dec_batch].set(c_ctx)
    mod = _modulation(cvec, w_mod, b_mod)[:, :, None, :]
    mod_lat = mod[:, 0:dec_batch]
    mod_ctx = mod[:, dec_batch:dec_batch + 1]

    wts = _prepare_weights(g_pre, g_post, w_in, w_out, a_q_gain, a_k_gain, b_lambda_q1, b_lambda_k1,
                           b_lambda_q2, b_lambda_k2, b_out_gain, c_gate_w_fwd, c_gate_b_fwd,
                           c_gate_w_bwd, c_gate_b_bwd, c_out_gain)
    rope_tabs = _rope_tables(dec_seq)
    cache = tuple(jnp.swapaxes(a, -1, -2) for a in (cache_a_k, cache_a_v, cache_b_k, cache_b_v))
    cache += (state_c_fwd, state_c_bwd)

    y_p, y_s = x_prompt, x_sample
    kv_ctx = None
    states = None
    for l in range(DEPTH):
        lam_init = 0.8 - 0.6 * math.exp(-0.3 * l)
        proj_p = _in_projection(y_p, mod_ctx, False, l, wts, None, 4, x_prompt.shape[1], kv_ctx)
        kv_ctx = proj_p[10:14]
        y_p, *states = _mixer(y_p, mod_ctx, False, l, proj_p, wts, lam_init, None, states)

        proj_s = _in_projection(y_s, mod_lat, True, l, wts, rope_tabs, 1, 1024)
        (y_s,) = _mixer(y_s, mod_lat, True, l, proj_s, wts, lam_init, cache)

    va_t, vb_t, ka_t, kb_t = kv_ctx
    new_kv = [jnp.swapaxes(a, -1, -2) for a in (ka_t, va_t, kb_t, vb_t)]
    return (y_p, y_s, *new_kv, *states)
```

```python
import functools
import math

import jax
import jax.numpy as jnp
from jax import lax
from jax.experimental import pallas as pl
from jax.experimental.pallas import tpu as pltpu

F32 = jnp.float32
BF16 = jnp.bfloat16

D_MODEL = 1024
DEPTH = 2
GRID_W = 64
HEAD_DIM = 64
A_HEADS = 8
A_KV_HEADS = 2
A_GROUP = A_HEADS // A_KV_HEADS
A_WIDTH = A_HEADS * HEAD_DIM
A_KV_WIDTH = A_KV_HEADS * HEAD_DIM
B_HEADS = 4
B_QK_DIM = 32
B_V_DIM = 64
B_WIDTH = B_HEADS * B_V_DIM
C_HEADS = 4
C_DK = 32
C_DV = 64
C_KW = C_HEADS * C_DK
C_WIDTH = C_HEADS * C_DV
GATE_RANK = 16
GLA_TAU = 16.0
CHUNK = 64
D_MIX = A_WIDTH + B_WIDTH + C_WIDTH
ROPE_THETA = 10000.0
EPS = 1e-6

LANES = 128
GLA_BLOCK = 256
VT_ROWS = HEAD_DIM + 16
SCORE_LOOKAHEAD = 6
SAFE_GAP = 96.0
BOUND_SLACK = 1.02
QSTAT_ROWS = 4 * B_HEADS
U_TILE = D_MIX // 4
N_MIX_PARAMS = 8
N_MIX_SEQ_INPUTS = 14
N_MIX_WEIGHTS = 5
KEY_TILE = 256
LOG2E = math.log2(math.e)

OFF_AQ = 0
OFF_AK = OFF_AQ + A_WIDTH
OFF_AV = OFF_AK + A_KV_WIDTH
OFF_BQ = OFF_AV + A_KV_WIDTH
OFF_BK = OFF_BQ + B_WIDTH
OFF_BV = OFF_BK + B_WIDTH
OFF_CQ = OFF_BV + B_WIDTH
OFF_CK = OFF_CQ + C_KW
OFF_CV = OFF_CK + C_KW
OFF_LR = OFF_CV + C_WIDTH
OFF_U = OFF_LR + 2 * GATE_RANK
IN_WIDTH = OFF_U + D_MIX

VMEM_LIMIT = 60 * 1024 * 1024


def _dot(a, b):
    return jnp.dot(a, b, preferred_element_type=F32)


def _dot_nt(a, b):
    return lax.dot_general(a, b, (((1,), (1,)), ((), ())), preferred_element_type=F32)


def _dot_tn(a, b):
    return lax.dot_general(a, b, (((0,), (0,)), ((), ())), preferred_element_type=F32)


def _split_bf16(x):
    hi = x.astype(BF16)
    lo = (x - hi.astype(F32)).astype(BF16)
    return hi, lo


def _iota(shape, dim):
    return lax.broadcasted_iota(jnp.int32, shape, dim)


def _group_mean_sq(x, group_log2):
    width = x.shape[-1]
    r = lax.shift_right_logical(_iota((LANES, LANES), 0), group_log2)
    c = lax.shift_right_logical(_iota((LANES, LANES), 1), group_log2)
    ones = jnp.where(r == c, 1.0, 0.0).astype(BF16)
    sq = (x * x).astype(BF16)
    cols = []
    for j in range(width // LANES):
        sl = slice(LANES * j, LANES * (j + 1))
        cols.append(_dot(sq[:, sl], ones))
    ss = cols[0] if len(cols) == 1 else jnp.concatenate(cols, axis=-1)
    return ss * (1.0 / (1 << group_log2))


def _log_sigmoid(x):
    return jnp.minimum(x, 0.0) - jnp.log1p(jnp.exp(-jnp.abs(x)))


def _silu(x):
    return x * (1.0 / (1.0 + jnp.exp(-x)))


def _mod_kernel(c_ref, w_ref, b_ref, o_ref):
    a = _silu(c_ref[...]).astype(BF16)
    o_ref[...] = _dot(a, w_ref[...].astype(BF16)) + b_ref[...]


def _modulation(cvec, w_mod, b_mod):
    rows = cvec.shape[0]
    nblk = 3
    return pl.pallas_call(
        _mod_kernel,
        grid=(DEPTH, nblk),
        in_specs=[
            pl.BlockSpec((rows, D_MODEL), lambda l, n: (0, 0)),
            pl.BlockSpec((None, D_MODEL, D_MODEL), lambda l, n: (l, 0, n)),
            pl.BlockSpec((None, 1, D_MODEL), lambda l, n: (l, 0, n)),
        ],
        out_specs=pl.BlockSpec((None, rows, D_MODEL), lambda l, n: (l, 0, n)),
        out_shape=jax.ShapeDtypeStruct((DEPTH, rows, 3 * D_MODEL), F32),
        compiler_params=pltpu.CompilerParams(
            dimension_semantics=("arbitrary", "arbitrary"), vmem_limit_bytes=VMEM_LIMIT),
        name="modulation",
    )(cvec, w_mod, b_mod.reshape(DEPTH, 1, 3 * D_MODEL))


def _in_kernel(rope, layer, stacked_first, nb, tl, n_aliased, *refs):
    (x_ref, mod_ref, gpre_ref, wt_ref, aqg_ref, akg_ref, cwf_ref, cbf_ref, cwb_ref, cbb_ref) = refs[:10]
    refs = refs[10:]
    if rope:
        cosa_ref, sina_ref, cosb_ref, sinb_ref = refs[:4]
        refs = refs[4:]
    refs = refs[n_aliased:]
    (qa_ref, ka_ref, qb_ref, kb_ref, cq_ref, ck_ref, cv_ref, gf_ref, gb_ref, su_ref,
     vat_ref, vbt_ref) = refs[:12]
    kat_ref, kbt_ref = (None, None) if rope else refs[12:]
    qstat_ref = refs[12] if rope else None

    x = x_ref[...].reshape(nb * tl, D_MODEL)
    shift = mod_ref[:, 0:D_MODEL]
    scale = mod_ref[:, D_MODEL:2 * D_MODEL]
    ms = jnp.mean(x * x, axis=-1, keepdims=True)
    h = (x * lax.rsqrt(ms + EPS)) * gpre_ref[...] * (1.0 + scale) + shift
    hb = h.astype(BF16)

    def proj(off, width):
        return _dot_nt(hb, wt_ref[off:off + width, :])

    def proj_t(off, width):
        return _dot_nt(wt_ref[off:off + width, :], hb)

    def put_rows(ref, val):
        for bi in range(nb):
            ref[bi] = val[bi * tl:(bi + 1) * tl].astype(ref.dtype)

    def put_heads(ref, val, n_heads):
        for bi in range(nb):
            for hd in range(n_heads):
                ref[bi, hd] = val[bi * tl:(bi + 1) * tl,
                                  HEAD_DIM * hd:HEAD_DIM * (hd + 1)].astype(ref.dtype)

    def put_heads_t(ref, heads_t, stacked=False):
        for bi in range(nb):
            for hd, val_t in enumerate(heads_t):
                blk = val_t[:, bi * tl:(bi + 1) * tl].astype(ref.dtype)
                if stacked and stacked_first:
                    for l2 in range(DEPTH):
                        ref[bi, l2, hd] = blk if l2 == layer else jnp.zeros_like(blk)
                else:
                    ref[bi, hd] = blk

    def split_heads_t(val_t, n_heads):
        return [val_t[HEAD_DIM * hd:HEAD_DIM * (hd + 1)] for hd in range(n_heads)]

    def rms_t(head_t, gain_col):
        ms_h = jnp.mean(head_t * head_t, axis=0, keepdims=True)
        return head_t * lax.rsqrt(ms_h + EPS) * gain_col

    def rope_t(head_t, cos_t, sin_t, dist):
        blocks = [head_t[r0:r0 + dist] for r0 in range(0, HEAD_DIM, dist)]
        partner = jnp.concatenate([blocks[j ^ 1] for j in range(len(blocks))], axis=0)
        return head_t * cos_t + partner * sin_t

    def gate_tile(j):
        c0 = U_TILE * j
        val = _silu(proj(OFF_U + c0, U_TILE))
        for bi in range(nb):
            su_ref[bi, :, c0:c0 + U_TILE] = val[bi * tl:(bi + 1) * tl]

    za_t = proj_t(OFF_AQ, A_WIDTH + 2 * A_KV_WIDTH)
    zb_t = proj_t(OFF_BQ, 3 * B_WIDTH)
    lr_t = proj_t(OFF_LR, 2 * GATE_RANK).astype(BF16)
    gate_tile(0)

    aq_t = [rms_t(h_t, aqg_ref[...]) for h_t in split_heads_t(za_t[0:A_WIDTH], A_HEADS)]
    ak_t = [rms_t(h_t, akg_ref[...])
            for h_t in split_heads_t(za_t[A_WIDTH:A_WIDTH + A_KV_WIDTH], A_KV_HEADS)]
    if rope:
        aq_t = [rope_t(h_t, cosa_ref[...], sina_ref[...], 16) for h_t in aq_t]
        ak_t = [rope_t(h_t, cosa_ref[...], sina_ref[...], 16) for h_t in ak_t]
    put_heads_t(qa_ref, [h_t * (HEAD_DIM ** -0.5 * LOG2E) for h_t in aq_t])
    put_heads(ka_ref, jnp.concatenate(ak_t, axis=0).T, A_KV_HEADS)
    put_heads_t(vat_ref, split_heads_t(za_t[A_WIDTH + A_KV_WIDTH:], A_KV_HEADS), stacked=not rope)
    if kat_ref is not None:
        put_heads_t(kat_ref, ak_t, stacked=True)
    gate_tile(1)
    cqk = proj(OFF_CQ, 2 * C_KW)
    cv = proj(OFF_CV, C_WIDTH)

    bq_t = split_heads_t(zb_t[0:B_WIDTH], B_HEADS)
    bk_t = split_heads_t(zb_t[B_WIDTH:2 * B_WIDTH], B_HEADS)
    if rope:
        bq_t = [rope_t(h_t, cosb_ref[...], sinb_ref[...], 8) for h_t in bq_t]
        bk_t = [rope_t(h_t, cosb_ref[...], sinb_ref[...], 8) for h_t in bk_t]
    bq_t = [h_t * (B_QK_DIM ** -0.5 * LOG2E) for h_t in bq_t]
    put_heads_t(qb_ref, bq_t)
    put_heads(kb_ref, jnp.concatenate(bk_t, axis=0).T, B_HEADS)
    if rope:
        halves = [(hd, r0) for hd in range(B_HEADS) for r0 in (0, B_QK_DIM)]
        qstat_ref[0] = jnp.concatenate(
            [jnp.sum(bq_t[hd][r0:r0 + B_QK_DIM] * bq_t[hd][r0:r0 + B_QK_DIM], axis=0, keepdims=True)
             for hd, r0 in halves]
            + [jnp.sum(bq_t[hd][r0:r0 + B_QK_DIM] * bk_t[hd][r0:r0 + B_QK_DIM], axis=0, keepdims=True)
               for hd, r0 in halves], axis=0)
    put_heads_t(vbt_ref, split_heads_t(zb_t[2 * B_WIDTH:], B_HEADS), stacked=not rope)
    if kbt_ref is not None:
        put_heads_t(kbt_ref, bk_t, stacked=True)
    gate_tile(2)
    gf_pre = _dot_tn(lr_t, cwf_ref[...])
    gb_pre = _dot_tn(lr_t, cwb_ref[...])

    put_rows(cq_ref, cqk[:, 0:C_KW] * (C_DK ** -0.5))
    put_rows(ck_ref, cqk[:, C_KW:2 * C_KW])
    put_rows(cv_ref, cv)
    put_rows(gf_ref, _log_sigmoid(gf_pre + cbf_ref[...]) * (1.0 / GLA_TAU))
    put_rows(gb_ref, _log_sigmoid(gb_pre + cbb_ref[...]) * (1.0 / GLA_TAU))
    gate_tile(3)


def _in_projection(x, mod, per_batch_mod, layer, wts, rope_tabs, nb, tl, kv_prev=None):
    bsz, seq, _ = x.shape
    rope = rope_tabs is not None
    stacked = not rope
    stacked_first = stacked and kv_prev is None
    grid = (bsz // nb, seq // tl)

    def per_layer(shape):
        return pl.BlockSpec((None,) + shape, lambda b, t: (layer,) + (0,) * len(shape))

    mod_idx = (lambda b, t: (layer, b, 0, 0)) if per_batch_mod else (lambda b, t: (layer, 0, 0, 0))
    in_specs = [
        pl.BlockSpec((nb, tl, D_MODEL), lambda b, t: (b, t, 0)),
        pl.BlockSpec((None, None, 1, 3 * D_MODEL), mod_idx),
        per_layer((1, D_MODEL)),
        per_layer((IN_WIDTH, D_MODEL)),
        per_layer((HEAD_DIM, 1)),
        per_layer((HEAD_DIM, 1)),
        per_layer((2 * GATE_RANK, C_KW)),
        per_layer((1, C_KW)),
        per_layer((2 * GATE_RANK, C_KW)),
        per_layer((1, C_KW)),
    ]
    args = [x, mod, wts["g_pre"], wts["w_in_t"], wts["aq_gain_col"], wts["ak_gain_col"],
            wts["cw_f"], wts["cb_f"], wts["cw_b"], wts["cb_b"]]
    if rope:
        assert nb == 1
        in_specs += [
            pl.BlockSpec((HEAD_DIM, tl), lambda b, t: (0, t)),
            pl.BlockSpec((HEAD_DIM, tl), lambda b, t: (0, t)),
            pl.BlockSpec((HEAD_DIM, tl), lambda b, t: (0, t)),
            pl.BlockSpec((HEAD_DIM, tl), lambda b, t: (0, t)),
        ]
        args += list(rope_tabs)

    def heads(n):
        return pl.BlockSpec((nb, n, tl, HEAD_DIM), lambda b, t: (b, 0, t, 0))

    def heads_t(n):
        if stacked_first:
            return pl.BlockSpec((nb, DEPTH, n, HEAD_DIM, tl), lambda b, t: (b, 0, 0, 0, t))
        if stacked:
            return pl.BlockSpec((nb, None, n, HEAD_DIM, tl), lambda b, t: (b, layer, 0, 0, t))
        return pl.BlockSpec((nb, n, HEAD_DIM, tl), lambda b, t: (b, 0, 0, t))

    def q_heads_t(n):
        return pl.BlockSpec((nb, n, HEAD_DIM, tl), lambda b, t: (b, 0, 0, t))

    def rows(width):
        return pl.BlockSpec((nb, tl, width), lambda b, t: (b, t, 0))

    def hshape(n):
        return jax.ShapeDtypeStruct((bsz, n, seq, HEAD_DIM), BF16)

    def qshape(n):
        return jax.ShapeDtypeStruct((bsz, n, HEAD_DIM, seq), BF16)

    def tshape(n):
        if stacked:
            return jax.ShapeDtypeStruct((bsz, DEPTH, n, HEAD_DIM, seq), F32)
        return jax.ShapeDtypeStruct((bsz, n, HEAD_DIM, seq), BF16)

    def rshape(width):
        return jax.ShapeDtypeStruct((bsz, seq, width), F32)

    out_specs = [q_heads_t(A_HEADS), heads(A_KV_HEADS), q_heads_t(B_HEADS), heads(B_HEADS),
                 rows(C_KW), rows(C_KW), rows(C_WIDTH), rows(C_KW), rows(C_KW), rows(D_MIX),
                 heads_t(A_KV_HEADS), heads_t(B_HEADS)]
    out_shape = [qshape(A_HEADS), hshape(A_KV_HEADS), qshape(B_HEADS), hshape(B_HEADS),
                 rshape(C_KW), rshape(C_KW), rshape(C_WIDTH), rshape(C_KW), rshape(C_KW),
                 rshape(D_MIX), tshape(A_KV_HEADS), tshape(B_HEADS)]
    if stacked:
        out_specs += [heads_t(A_KV_HEADS), heads_t(B_HEADS)]
        out_shape += [tshape(A_KV_HEADS), tshape(B_HEADS)]
    else:
        out_specs += [pl.BlockSpec((nb, QSTAT_ROWS, tl), lambda b, t: (b, 0, t))]
        out_shape += [jax.ShapeDtypeStruct((bsz, QSTAT_ROWS, seq), F32)]
    aliases = {}
    if kv_prev is not None:
        for j, buf in enumerate(kv_prev):
            aliases[len(args)] = 10 + j
            in_specs.append(pl.BlockSpec(memory_space=pl.ANY))
            args.append(buf)
    return pl.pallas_call(
        functools.partial(_in_kernel, rope, layer, stacked_first, nb, tl, len(aliases)),
        grid=grid,
        in_specs=in_specs,
        out_specs=out_specs,
        out_shape=out_shape,
        input_output_aliases=aliases,
        compiler_params=pltpu.CompilerParams(
            dimension_semantics=("arbitrary", "arbitrary"), vmem_limit_bytes=VMEM_LIMIT),
        name="in_projection_rope" if rope else "in_projection",
    )(*args)


def _gla_bidirectional(cq_ref, ck_ref, cv_ref, gf_ref, gb_ref, s_f, s_b, seq, oc_ref):
    bl = GLA_BLOCK
    n_sub = bl // CHUNK
    nblk = seq // bl
    ri = _iota((bl, bl), 0)
    ci = _iota((bl, bl), 1)
    same_chunk = lax.shift_right_logical(ri, 6) == lax.shift_right_logical(ci, 6)
    bd = (lax.shift_right_logical(_iota((C_WIDTH, C_KW), 0), 6)
          == lax.shift_right_logical(_iota((C_WIDTH, C_KW), 1), 5))
    khead = lax.shift_right_logical(_iota((1, C_KW), 1), 5)
    vhead = lax.shift_right_logical(_iota((1, C_WIDTH), 1), 6)
    scans = []
    for reverse, g_ref in ((False, gf_ref), (True, gb_ref)):
        causal = same_chunk & ((ci >= ri) if reverse else (ci <= ri))
        scans.append((reverse, g_ref, causal, jnp.where(causal, 1.0, 0.0).astype(BF16)))
    states = [s_f, s_b]
    written = set()

    for step in range(nblk):
        rows0 = [step * bl, (nblk - 1 - step) * bl]
        cums = []
        for (reverse, g_ref, causal, tri), r0 in zip(scans, rows0):
            g_hi, g_lo = _split_bf16(g_ref[r0:r0 + bl, :])
            cums.append(_dot(tri, g_hi) + _dot(tri, g_lo))
        prep = []
        for (reverse, g_ref, causal, tri), r0, cum in zip(scans, rows0, cums):
            q = cq_ref[r0:r0 + bl, :]
            k = ck_ref[r0:r0 + bl, :]
            v = cv_ref[r0:r0 + bl, :]
            qt = q * jnp.exp(cum)
            ktb = (k * jnp.exp(-cum)).astype(BF16)
            vb = v.astype(BF16)
            lasts, kdecs = [], []
            for c in range(n_sub):
                c0 = CHUNK * c
                edge = c0 if reverse else c0 + CHUNK - 1
                last = cum[edge:edge + 1, :]
                lasts.append(last)
                kdecs.append((k[c0:c0 + CHUNK] * jnp.exp(last - cum[c0:c0 + CHUNK])).astype(BF16))
            prep.append((qt, ktb, v, vb, lasts, kdecs))
        scores, incs = [], []
        for qt, ktb, v, vb, lasts, kdecs in prep:
            scores.append([_dot_nt(jnp.where(khead == hd, qt, 0.0).astype(BF16), ktb)
                           for hd in range(C_HEADS)])
            incs.append([_dot_tn(vb[CHUNK * c:CHUNK * (c + 1)], kdecs[c]) for c in range(n_sub)])
        probs, entering = [], []
        for si, ((reverse, g_ref, causal, tri), (qt, ktb, v, vb, lasts, kdecs)) in enumerate(
                zip(scans, prep)):
            probs.append([jnp.where(causal, s, 0.0).astype(BF16) for s in scores[si]])
            s_t = states[si]
            before = [None] * n_sub
            subs = range(n_sub)
            for c in (reversed(subs) if reverse else subs):
                before[c] = s_t.astype(BF16)
                s_t = jnp.exp(lasts[c]) * s_t + jnp.where(bd, incs[si][c], 0.0)
            states[si] = s_t
            entering.append(before)
        outs = []
        for si, (qt, ktb, v, vb, lasts, kdecs) in enumerate(prep):
            qtb = qt.astype(BF16)
            o = jnp.concatenate([_dot_nt(qtb[CHUNK * c:CHUNK * (c + 1)], entering[si][c])
                                 for c in range(n_sub)], axis=0)
            for hd in range(C_HEADS):
                o = o + _dot(probs[si][hd], jnp.where(vhead == hd, v, 0.0).astype(BF16))
            outs.append(o)
        if rows0[0] == rows0[1]:
            outs, rows0 = [outs[0] + outs[1]], rows0[:1]
        for o, r0 in zip(outs, rows0):
            if r0 in written:
                oc_ref[r0:r0 + bl, :] = oc_ref[r0:r0 + bl, :] + o
            else:
                oc_ref[r0:r0 + bl, :] = o
                written.add(r0)
    return states[0], states[1]


def _attend_t(jobs, shifts=None):
    def scores(i):
        return _dot(jobs[i][0][...], jobs[i][1])

    outs = []
    pending = [scores(i) for i in range(min(SCORE_LOOKAHEAD, len(jobs)))]
    for i, (_, _, vt) in enumerate(jobs):
        st = pending.pop(0)
        if i + SCORE_LOOKAHEAD < len(jobs):
            pending.append(scores(i + SCORE_LOOKAHEAD))
        m = jnp.max(st, axis=0, keepdims=True) if shifts is None else shifts[i]
        p = jnp.exp2(st - m).astype(BF16)
        ot = _dot(vt[...], p)
        outs.append(ot[0:HEAD_DIM] * (1.0 / ot[HEAD_DIM:HEAD_DIM + 1]))
    return outs


def _pair_rows(a, b):
    return jnp.concatenate([a, b], axis=0).T


def _state_to_blockdiag_t(s_ref):
    rows = []
    for hd in range(C_HEADS):
        pieces = []
        if hd:
            pieces.append(jnp.zeros((C_DK, C_DV * hd), F32))
        pieces.append(s_ref[hd])
        if hd < C_HEADS - 1:
            pieces.append(jnp.zeros((C_DK, C_DV * (C_HEADS - 1 - hd)), F32))
        rows.append(jnp.concatenate(pieces, axis=-1))
    return jnp.concatenate(rows, axis=0).T


def _blockdiag_t_to_state(s_t, out_ref):
    s = s_t.T
    for hd in range(C_HEADS):
        out_ref[hd] = s[C_DK * hd:C_DK * (hd + 1), C_DV * hd:C_DV * (hd + 1)]


def _mix_sequence(cached, layer, first_state, n_aliased, lam_init, seq, qt, qb, *refs):
    it = iter(refs)
    x_ref, mod_ref = next(it), next(it)
    qa_ref, ka_ref, qb_ref, kb_ref = (next(it) for _ in range(4))
    cq_ref, ck_ref, cv_ref, gf_ref, gb_ref, su_ref = (next(it) for _ in range(6))
    vat_ref, vbt_ref = next(it), next(it)
    if cached:
        cakt_ref, cavt_ref, cbkt_ref, cbvt_ref, s0f_ref, s0b_ref = (next(it) for _ in range(6))
        qstat_ref, aqg_ref = next(it), next(it)
    wout_ref, gpost_ref, bog_ref, cog_ref, lamp_ref = (next(it) for _ in range(5))
    for _ in range(n_aliased):
        next(it)
    y_ref = next(it)
    if not cached:
        sf_ref, sb_ref = next(it), next(it)
    kA_s, vtA_s, kB_s, vtB_s, oc_s, mixed_s, kn2_s = (next(it) for _ in range(7))

    lk = kA_s.shape[1]
    past = lk - seq
    t = pl.program_id(1)

    def once_per_sequence(body):
        return body() if seq == qt else pl.when(t == 0)(body)

    @once_per_sequence
    def _per_sequence():
        ones_row = jnp.where(_iota((VT_ROWS - HEAD_DIM, lk), 0) == 0, 1.0, 0.0).astype(BF16)
        for k_new, kt_cache, k_dst, vt_new, vt_cache, vt_dst in (
                (ka_ref, cakt_ref if cached else None, kA_s, vat_ref, cavt_ref if cached else None, vtA_s),
                (kb_ref, cbkt_ref if cached else None, kB_s, vbt_ref, cbvt_ref if cached else None, vtB_s)):
            n_heads = k_dst.shape[0]
            if cached:
                for h0 in range(0, n_heads, 2):
                    pair = jnp.concatenate([kt_cache[h0], kt_cache[h0 + 1]], axis=0).T
                    k_dst[h0, 0:past, :] = pair[:, 0:HEAD_DIM].astype(BF16)
                    k_dst[h0 + 1, 0:past, :] = pair[:, HEAD_DIM:2 * HEAD_DIM].astype(BF16)
            for hd in range(n_heads):
                k_dst[hd, past:lk, :] = k_new[hd]
                if cached:
                    vt_dst[hd, 0:HEAD_DIM, 0:past] = vt_cache[hd].astype(BF16)
                vt_dst[hd, 0:HEAD_DIM, past:lk] = vt_new[hd].astype(BF16)
                vt_dst[hd, HEAD_DIM:VT_ROWS, :] = ones_row
        bounded_jobs = ([(kA_s, g) for g in range(A_KV_HEADS)] + [(kB_s, h) for h in range(B_HEADS)]
                        if lk > KEY_TILE else [])
        for j, (k_dst, hd) in enumerate(bounded_jobs):
            kf = k_dst[hd].astype(F32)
            kn2 = jnp.max(jnp.sum(kf * kf, axis=-1, keepdims=True), axis=0, keepdims=True)
            kn2_s[j:j + 1, :] = jnp.broadcast_to(kn2, (1, LANES))

        if cached:
            s0f, s0b = _state_to_blockdiag_t(s0f_ref), _state_to_blockdiag_t(s0b_ref)
        else:
            s0f = jnp.zeros((C_WIDTH, C_KW), F32)
            s0b = s0f
        s_f, s_b = _gla_bidirectional(cq_ref, ck_ref, cv_ref, gf_ref, gb_ref, s0f, s0b, seq, oc_s)
        if not cached:
            for ref, s_t in ((sf_ref, s_f), (sb_ref, s_b)):
                if first_state:
                    for l2 in range(DEPTH):
                        if l2 == layer:
                            _blockdiag_t_to_state(s_t, ref.at[l2])
                        else:
                            ref[l2] = jnp.zeros(ref.shape[1:], F32)
                else:
                    _blockdiag_t_to_state(s_t, ref)
        for r0 in range(0, seq, GLA_BLOCK):
            oc = oc_s[r0:r0 + GLA_BLOCK, :]
            oc_s[r0:r0 + GLA_BLOCK, :] = oc * lax.rsqrt(_group_mean_sq(oc, 6) + EPS) * cog_ref[...]

    lam = (jnp.exp(jnp.sum(lamp_ref[0:1, :] * lamp_ref[1:2, :], axis=-1, keepdims=True))
           - jnp.exp(jnp.sum(lamp_ref[2:3, :] * lamp_ref[3:4, :], axis=-1, keepdims=True))
           + lam_init)

    def project_out():
        seq_rows = pl.ds(pl.multiple_of(t * qt, qt), qt)
        mixed_s[:, A_WIDTH + B_WIDTH:D_MIX] = oc_s[seq_rows, :]
        gate = mod_ref[:, 2 * D_MODEL:3 * D_MODEL]
        mixed = (mixed_s[...] * su_ref[...]).astype(BF16)
        y = _dot(mixed, wout_ref[...])
        yn = y * lax.rsqrt(jnp.mean(y * y, axis=-1, keepdims=True) + EPS) * gpost_ref[...]
        y_ref[...] = x_ref[...] + gate * yn

    def attn_block(i, carry):
        rows = pl.ds(i * qb, qb) if isinstance(i, int) else pl.ds(pl.multiple_of(i * qb, qb), qb)
        jobs = []
        for grp in range(A_KV_HEADS):
            q4 = jnp.concatenate([qa_ref[A_GROUP * grp + j, :, rows] for j in range(A_GROUP)], axis=-1)
            jobs.append((kA_s.at[grp], q4, vtA_s.at[grp]))
        dim = _iota((HEAD_DIM, qb), 0)
        for hd in range(B_HEADS):
            q = qb_ref[hd, :, rows]
            zero = jnp.zeros_like(q)
            q2 = jnp.concatenate([jnp.where(dim < B_QK_DIM, q, zero),
                                  jnp.where(dim >= B_QK_DIM, q, zero)], axis=-1)
            jobs.append((kB_s.at[hd], q2, vtB_s.at[hd]))

        def finish(outs):
            for grp in range(A_KV_HEADS):
                ot = outs[grp]
                for pair in range(A_GROUP // 2):
                    c0 = 2 * pair * qb
                    col = A_GROUP * HEAD_DIM * grp + 2 * HEAD_DIM * pair
                    mixed_s[rows, col:col + 2 * HEAD_DIM] = _pair_rows(ot[:, c0:c0 + qb],
                                                                       ot[:, c0 + qb:c0 + 2 * qb])
            obs = []
            for hd in range(B_HEADS):
                ot = outs[A_KV_HEADS + hd]
                ob = ot[:, 0:qb] - lam * ot[:, qb:2 * qb]
                obs.append(ob * lax.rsqrt(jnp.mean(ob * ob, axis=0, keepdims=True) + EPS))
            for pair in range(B_HEADS // 2):
                col = A_WIDTH + 2 * B_V_DIM * pair
                mixed_s[rows, col:col + 2 * B_V_DIM] = (_pair_rows(obs[2 * pair], obs[2 * pair + 1])
                                                        * bog_ref[...] * (1.0 - lam_init))
            if qt == qb:
                project_out()

        if lk <= KEY_TILE:
            finish(_attend_t(jobs))
            return carry

        qa_norm = (jnp.max(jnp.abs(aqg_ref[...]), axis=-1, keepdims=True)
                   * (HEAD_DIM ** 0.5 * HEAD_DIM ** -0.5 * LOG2E))
        stat_t = qstat_ref[...]
        shifts, gaps = [], []
        for j, (_, q, _) in enumerate(jobs):
            k_norm = jnp.sqrt(kn2_s[j:j + 1, 0:1])
            if j < A_KV_HEADS:
                upper = qa_norm * k_norm * BOUND_SLACK
                shifts.append(jnp.broadcast_to(upper, (1, q.shape[1])))
                gaps.append(2.0 * upper)
            else:
                r0 = 2 * (j - A_KV_HEADS)
                r1 = r0 + 2 * B_HEADS
                qn2 = jnp.concatenate([stat_t[r0:r0 + 1], stat_t[r0 + 1:r0 + 2]], axis=-1)
                lower = jnp.concatenate([stat_t[r1:r1 + 1], stat_t[r1 + 1:r1 + 2]], axis=-1)
                upper = jnp.sqrt(qn2) * k_norm * BOUND_SLACK
                shifts.append(upper)
                gaps.append(jnp.max(upper - lower, axis=-1, keepdims=True))
        worst = functools.reduce(jnp.maximum, gaps)
        safe = worst[0, 0] <= SAFE_GAP

        @pl.when(safe)
        def _bounded():
            finish(_attend_t(jobs, shifts))

        @pl.when(jnp.logical_not(safe))
        def _exact_max():
            finish(_attend_t(jobs))

        return carry

    if qt != qb:
        lax.fori_loop(0, qt // qb, attn_block, 0)
        project_out()
    else:
        attn_block(0, 0)


def _mix_kernel(nseq, *params_and_refs):
    params, refs = params_and_refs[:N_MIX_PARAMS], params_and_refs[N_MIX_PARAMS:]
    if nseq == 1:
        return _mix_sequence(*params, *refs)
    n_aliased = params[3]
    shared = {1} | set(range(N_MIX_SEQ_INPUTS, N_MIX_SEQ_INPUTS + N_MIX_WEIGHTS + n_aliased))
    for bi in range(nseq):
        _mix_sequence(*params, *[r if idx in shared else r.at[bi] for idx, r in enumerate(refs)])


def _mixer(x, mod, per_batch_mod, layer, proj, wts, lam_init, cache, state_prev=None):
    bsz, seq, _ = x.shape
    cached = cache is not None
    past = cache[0].shape[4] if cached else 0
    qa, ka, qb, kb, cq, ck, cv, gf, gb, su, vat, vbt = proj[:12]
    qt = 256
    qb_rows = 256
    nseq = 1 if cached else 4
    lead = None if nseq == 1 else nseq

    def per_layer(shape):
        return pl.BlockSpec((None,) + shape, lambda b, t: (layer,) + (0,) * len(shape))

    def layer_heads_t(n, length):
        return pl.BlockSpec((lead, None, n, HEAD_DIM, length), lambda b, t: (b, layer, 0, 0, 0))

    def heads_t(n):
        if cached:
            return pl.BlockSpec((lead, n, HEAD_DIM, seq), lambda b, t: (b, 0, 0, 0))
        return layer_heads_t(n, seq)

    def heads(n):
        return pl.BlockSpec((lead, n, seq, HEAD_DIM), lambda b, t: (b, 0, 0, 0))

    def head_tile(n):
        return pl.BlockSpec((lead, n, HEAD_DIM, qt), lambda b, t: (b, 0, 0, t))

    def rows(width):
        return pl.BlockSpec((lead, seq, width), lambda b, t: (b, 0, 0))

    def row_tile(width):
        return pl.BlockSpec((lead, qt, width), lambda b, t: (b, t, 0))

    mod_idx = (lambda b, t: (layer, b, 0, 0)) if per_batch_mod else (lambda b, t: (layer, 0, 0, 0))
    in_specs = [row_tile(D_MODEL), pl.BlockSpec((None, None, 1, 3 * D_MODEL), mod_idx),
                head_tile(A_HEADS), heads(A_KV_HEADS), head_tile(B_HEADS), heads(B_HEADS),
                rows(C_KW), rows(C_KW), rows(C_WIDTH), rows(C_KW), rows(C_KW), row_tile(D_MIX),
                heads_t(A_KV_HEADS), heads_t(B_HEADS)]
    args = [x, mod, qa, ka, qb, kb, cq, ck, cv, gf, gb, su, vat, vbt]
    if cached:
        state_in = pl.BlockSpec((None, None, C_HEADS, C_DK, C_DV), lambda b, t: (b, layer, 0, 0, 0))
        in_specs += [layer_heads_t(A_KV_HEADS, past), layer_heads_t(A_KV_HEADS, past),
                     layer_heads_t(B_HEADS, past), layer_heads_t(B_HEADS, past), state_in, state_in,
                     pl.BlockSpec((None, QSTAT_ROWS, qt), lambda b, t: (b, 0, t)),
                     per_layer((1, HEAD_DIM))]
        args += list(cache) + [proj[12], wts["aq_gain"]]
    in_specs += [per_layer((D_MIX, D_MODEL)), per_layer((1, D_MODEL)), per_layer((1, 2 * B_V_DIM)),
                 per_layer((1, C_WIDTH)), per_layer((4, B_QK_DIM))]
    args += [wts["w_out"], wts["g_post"], wts["b_out_gain"], wts["c_out_gain"], wts["lam_params"]]

    out_specs = [row_tile(D_MODEL)]
    out_shape = [jax.ShapeDtypeStruct((bsz, seq, D_MODEL), F32)]
    aliases = {}
    first_state = not cached and state_prev is None
    if not cached:
        if first_state:
            state_out = pl.BlockSpec((lead, DEPTH, C_HEADS, C_DK, C_DV), lambda b, t: (b, 0, 0, 0, 0))
        else:
            state_out = pl.BlockSpec((lead, None, C_HEADS, C_DK, C_DV),
                                     lambda b, t: (b, layer, 0, 0, 0))
            for j, buf in enumerate(state_prev):
                aliases[len(args)] = 1 + j
                in_specs.append(pl.BlockSpec(memory_space=pl.ANY))
                args.append(buf)
        out_specs += [state_out, state_out]
        out_shape += [jax.ShapeDtypeStruct((bsz, DEPTH, C_HEADS, C_DK, C_DV), F32)] * 2

    lk = past + seq
    per_seq = () if nseq == 1 else (nseq,)
    scratch = [pltpu.VMEM(per_seq + shape, dtype) for shape, dtype in (
        ((A_KV_HEADS, lk, HEAD_DIM), BF16), ((A_KV_HEADS, VT_ROWS, lk), BF16),
        ((B_HEADS, lk, HEAD_DIM), BF16), ((B_HEADS, VT_ROWS, lk), BF16),
        ((seq, C_WIDTH), F32), ((qt, D_MIX), F32), ((8, LANES), F32))]
    return pl.pallas_call(
        functools.partial(_mix_kernel, nseq, cached, layer, first_state, len(aliases), lam_init, seq,
                          qt, qb_rows),
        grid=(bsz // nseq, seq // qt),
        in_specs=in_specs,
        out_specs=out_specs,
        out_shape=out_shape,
        scratch_shapes=scratch,
        input_output_aliases=aliases,
        compiler_params=pltpu.CompilerParams(
            dimension_semantics=("arbitrary", "arbitrary"), vmem_limit_bytes=VMEM_LIMIT),
        name="mixer_cached" if cached else "mixer",
    )(*args)


def _rope_tables(seq):
    t = jnp.arange(seq)
    pos_row = (t // GRID_W).astype(F32)
    pos_col = (t % GRID_W).astype(F32)

    def tables(half):
        freq = ROPE_THETA ** (-jnp.arange(half, dtype=F32) / half)
        ang_r = freq[:, None] * pos_row[None, :]
        ang_c = freq[:, None] * pos_col[None, :]
        cos = jnp.concatenate([jnp.cos(ang_r), jnp.cos(ang_r), jnp.cos(ang_c), jnp.cos(ang_c)], axis=0)
        sin = jnp.concatenate([-jnp.sin(ang_r), jnp.sin(ang_r), -jnp.sin(ang_c), jnp.sin(ang_c)], axis=0)
        reps = HEAD_DIM // (4 * half)
        return jnp.tile(cos, (reps, 1)), jnp.tile(sin, (reps, 1))

    cos_a, sin_a = tables(HEAD_DIM // 4)
    cos_b, sin_b = tables(B_QK_DIM // 4)
    return cos_a, sin_a, cos_b, sin_b


def _prepare_weights(g_pre, g_post, w_in, w_out, a_q_gain, a_k_gain, b_lambda_q1, b_lambda_k1,
                     b_lambda_q2, b_lambda_k2, b_out_gain, c_gate_w_fwd, c_gate_b_fwd, c_gate_w_bwd,
                     c_gate_b_bwd, c_out_gain):
    w_in_t = jnp.swapaxes(w_in, 1, 2).astype(BF16)
    pad = jnp.zeros((DEPTH, GATE_RANK, C_KW), F32)
    cw_f = jnp.concatenate([c_gate_w_fwd, pad], axis=1).astype(BF16)
    cw_b = jnp.concatenate([pad, c_gate_w_bwd], axis=1).astype(BF16)
    return {
        "g_pre": g_pre[:, None, :],
        "g_post": g_post[:, None, :],
        "w_in_t": w_in_t,
        "w_out": w_out.astype(BF16),
        "aq_gain": a_q_gain[:, None, :],
        "aq_gain_col": a_q_gain[:, :, None],
        "ak_gain_col": a_k_gain[:, :, None],
        "cw_f": cw_f,
        "cb_f": c_gate_b_fwd[:, None, :],
        "cw_b": cw_b,
        "cb_b": c_gate_b_bwd[:, None, :],
        "b_out_gain": jnp.tile(b_out_gain, (1, 2))[:, None, :],
        "c_out_gain": jnp.tile(c_out_gain, (1, C_HEADS))[:, None, :],
        "lam_params": jnp.stack([b_lambda_q1, b_lambda_k1, b_lambda_q2, b_lambda_k2], axis=1),
    }


def kernel(x_prompt, x_sample, c, cache_a_k, cache_a_v, cache_b_k, cache_b_v, state_c_fwd, state_c_bwd, c_ctx, w_mod, b_mod, g_pre, g_post, w_in, w_out, a_q_gain, a_k_gain, b_lambda_q1, b_lambda_k1, b_lambda_q2, b_lambda_k2, b_out_gain, c_gate_w_fwd, c_gate_b_fwd, c_gate_w_bwd, c_gate_b_bwd, c_out_gain):
    dec_batch = x_sample.shape[0]
    dec_seq = x_sample.shape[1]

    mod_rows = 16
    cvec = jnp.zeros((mod_rows, D_MODEL), F32).at[0:dec_batch].set(c).at[dec_batch].set(c_ctx)
    mod = _modulation(cvec, w_mod, b_mod)[:, :, None, :]
    mod_lat = mod[:, 0:dec_batch]
    mod_ctx = mod[:, dec_batch:dec_batch + 1]

    wts = _prepare_weights(g_pre, g_post, w_in, w_out, a_q_gain, a_k_gain, b_lambda_q1, b_lambda_k1,
                           b_lambda_q2, b_lambda_k2, b_out_gain, c_gate_w_fwd, c_gate_b_fwd,
                           c_gate_w_bwd, c_gate_b_bwd, c_out_gain)
    rope_tabs = _rope_tables(dec_seq)
    cache = tuple(jnp.swapaxes(a, -1, -2) for a in (cache_a_k, cache_a_v, cache_b_k, cache_b_v))
    cache += (state_c_fwd, state_c_bwd)

    y_p, y_s = x_prompt, x_sample
    kv_ctx = None
    states = None
    for l in range(DEPTH):
        lam_init = 0.8 - 0.6 * math.exp(-0.3 * l)
        proj_p = _in_projection(y_p, mod_ctx, False, l, wts, None, 4, x_prompt.shape[1], kv_ctx)
        kv_ctx = proj_p[10:14]
        y_p, *states = _mixer(y_p, mod_ctx, False, l, proj_p, wts, lam_init, None, states)

        proj_s = _in_projection(y_s, mod_lat, True, l, wts, rope_tabs, 1, 1024)
        (y_s,) = _mixer(y_s, mod_lat, True, l, proj_s, wts, lam_init, cache)

    va_t, vb_t, ka_t, kb_t = kv_ctx
    new_kv = [jnp.swapaxes(a, -1, -2) for a in (ka_t, va_t, kb_t, vb_t)]
    return (y_p, y_s, *new_kv, *states)
```

```python
import functools
import math

import jax
import jax.numpy as jnp
from jax import lax
from jax.experimental import pallas as pl
from jax.experimental.pallas import tpu as pltpu

F32 = jnp.float32
BF16 = jnp.bfloat16

D_MODEL = 1024
DEPTH = 2
GRID_W = 64
HEAD_DIM = 64
A_HEADS = 8
A_KV_HEADS = 2
A_GROUP = A_HEADS // A_KV_HEADS
A_WIDTH = A_HEADS * HEAD_DIM
A_KV_WIDTH = A_KV_HEADS * HEAD_DIM
B_HEADS = 4
B_QK_DIM = 32
B_V_DIM = 64
B_WIDTH = B_HEADS * B_V_DIM
C_HEADS = 4
C_DK = 32
C_DV = 64
C_KW = C_HEADS * C_DK
C_WIDTH = C_HEADS * C_DV
GATE_RANK = 16
GLA_TAU = 16.0
CHUNK = 64
D_MIX = A_WIDTH + B_WIDTH + C_WIDTH
ROPE_THETA = 10000.0
EPS = 1e-6

LANES = 128
GLA_BLOCK = 256
VT_ROWS = HEAD_DIM + 16
SCORE_LOOKAHEAD = 6
SAFE_GAP = 96.0
BOUND_SLACK = 1.02
QSTAT_ROWS = 4 * B_HEADS
U_TILE = D_MIX // 4
N_MIX_PARAMS = 8
N_MIX_SEQ_INPUTS = 14
N_MIX_WEIGHTS = 5
KEY_TILE = 256
LOG2E = math.log2(math.e)

OFF_AQ = 0
OFF_AK = OFF_AQ + A_WIDTH
OFF_AV = OFF_AK + A_KV_WIDTH
OFF_BQ = OFF_AV + A_KV_WIDTH
OFF_BK = OFF_BQ + B_WIDTH
OFF_BV = OFF_BK + B_WIDTH
OFF_CQ = OFF_BV + B_WIDTH
OFF_CK = OFF_CQ + C_KW
OFF_CV = OFF_CK + C_KW
OFF_LR = OFF_CV + C_WIDTH
OFF_U = OFF_LR + 2 * GATE_RANK
IN_WIDTH = OFF_U + D_MIX

VMEM_LIMIT = 60 * 1024 * 1024


def _dot(a, b):
    return jnp.dot(a, b, preferred_element_type=F32)


def _dot_nt(a, b):
    return lax.dot_general(a, b, (((1,), (1,)), ((), ())), preferred_element_type=F32)


def _dot_tn(a, b):
    return lax.dot_general(a, b, (((0,), (0,)), ((), ())), preferred_element_type=F32)


def _split_bf16(x):
    hi = x.astype(BF16)
    lo = (x - hi.astype(F32)).astype(BF16)
    return hi, lo


def _iota(shape, dim):
    return lax.broadcasted_iota(jnp.int32, shape, dim)


def _group_mean_sq(x, group_log2):
    width = x.shape[-1]
    r = lax.shift_right_logical(_iota((width, width), 0), group_log2)
    c = lax.shift_right_logical(_iota((width, width), 1), group_log2)
    ones = jnp.where(r == c, 1.0, 0.0).astype(BF16)
    return _dot((x * x).astype(BF16), ones) * (1.0 / (1 << group_log2))


def _log_sigmoid(x):
    return jnp.minimum(x, 0.0) - jnp.log1p(jnp.exp(-jnp.abs(x)))


def _silu(x):
    return x * (1.0 / (1.0 + jnp.exp(-x)))


def _mod_kernel(c_ref, w_ref, b_ref, o_ref):
    a = _silu(c_ref[...]).astype(BF16)
    o_ref[...] = _dot(a, w_ref[...].astype(BF16)) + b_ref[...]


def _modulation(cvec, w_mod, b_mod):
    rows = cvec.shape[0]
    nblk = 3
    return pl.pallas_call(
        _mod_kernel,
        grid=(DEPTH, nblk),
        in_specs=[
            pl.BlockSpec((rows, D_MODEL), lambda l, n: (0, 0)),
            pl.BlockSpec((None, D_MODEL, D_MODEL), lambda l, n: (l, 0, n)),
            pl.BlockSpec((None, 1, D_MODEL), lambda l, n: (l, 0, n)),
        ],
        out_specs=pl.BlockSpec((None, rows, D_MODEL), lambda l, n: (l, 0, n)),
        out_shape=jax.ShapeDtypeStruct((DEPTH, rows, 3 * D_MODEL), F32),
        compiler_params=pltpu.CompilerParams(
            dimension_semantics=("arbitrary", "arbitrary"), vmem_limit_bytes=VMEM_LIMIT),
        name="modulation",
    )(cvec, w_mod, b_mod.reshape(DEPTH, 1, 3 * D_MODEL))


def _in_kernel(rope, layer, stacked_first, nb, tl, n_aliased, *refs):
    (x_ref, mod_ref, gpre_ref, wt_ref, aqg_ref, akg_ref, cwf_ref, cbf_ref, cwb_ref, cbb_ref) = refs[:10]
    refs = refs[10:]
    if rope:
        cosa_ref, sina_ref, cosb_ref, sinb_ref = refs[:4]
        refs = refs[4:]
    refs = refs[n_aliased:]
    (qa_ref, ka_ref, qb_ref, kb_ref, cq_ref, ck_ref, cv_ref, gf_ref, gb_ref, su_ref,
     vat_ref, vbt_ref) = refs[:12]
    kat_ref, kbt_ref = (None, None) if rope else refs[12:]
    qstat_ref = refs[12] if rope else None

    x = x_ref[...].reshape(nb * tl, D_MODEL)
    shift = mod_ref[:, 0:D_MODEL]
    scale = mod_ref[:, D_MODEL:2 * D_MODEL]
    ms = jnp.mean(x * x, axis=-1, keepdims=True)
    h = (x * lax.rsqrt(ms + EPS)) * gpre_ref[...] * (1.0 + scale) + shift
    hb = h.astype(BF16)

    def proj(off, width):
        return _dot_nt(hb, wt_ref[off:off + width, :])

    def proj_t(off, width):
        return _dot_nt(wt_ref[off:off + width, :], hb)

    def put_rows(ref, val):
        for bi in range(nb):
            ref[bi] = val[bi * tl:(bi + 1) * tl].astype(ref.dtype)

    def put_heads(ref, val, n_heads):
        for bi in range(nb):
            for hd in range(n_heads):
                ref[bi, hd] = val[bi * tl:(bi + 1) * tl,
                                  HEAD_DIM * hd:HEAD_DIM * (hd + 1)].astype(ref.dtype)

    def put_heads_t(ref, heads_t, stacked=False):
        for bi in range(nb):
            for hd, val_t in enumerate(heads_t):
                blk = val_t[:, bi * tl:(bi + 1) * tl].astype(ref.dtype)
                if stacked and stacked_first:
                    for l2 in range(DEPTH):
                        ref[bi, l2, hd] = blk if l2 == layer else jnp.zeros_like(blk)
                else:
                    ref[bi, hd] = blk

    def split_heads_t(val_t, n_heads):
        return [val_t[HEAD_DIM * hd:HEAD_DIM * (hd + 1)] for hd in range(n_heads)]

    def rms_t(head_t, gain_col):
        ms_h = jnp.mean(head_t * head_t, axis=0, keepdims=True)
        return head_t * lax.rsqrt(ms_h + EPS) * gain_col

    def rope_t(head_t, cos_t, sin_t, dist):
        blocks = [head_t[r0:r0 + dist] for r0 in range(0, HEAD_DIM, dist)]
        partner = jnp.concatenate([blocks[j ^ 1] for j in range(len(blocks))], axis=0)
        return head_t * cos_t + partner * sin_t

    def gate_tile(j):
        c0 = U_TILE * j
        val = _silu(proj(OFF_U + c0, U_TILE))
        for bi in range(nb):
            su_ref[bi, :, c0:c0 + U_TILE] = val[bi * tl:(bi + 1) * tl]

    za_t = proj_t(OFF_AQ, A_WIDTH + 2 * A_KV_WIDTH)
    zb_t = proj_t(OFF_BQ, 3 * B_WIDTH)
    lr_t = proj_t(OFF_LR, 2 * GATE_RANK).astype(BF16)
    gate_tile(0)

    aq_t = [rms_t(h_t, aqg_ref[...]) for h_t in split_heads_t(za_t[0:A_WIDTH], A_HEADS)]
    ak_t = [rms_t(h_t, akg_ref[...])
            for h_t in split_heads_t(za_t[A_WIDTH:A_WIDTH + A_KV_WIDTH], A_KV_HEADS)]
    if rope:
        aq_t = [rope_t(h_t, cosa_ref[...], sina_ref[...], 16) for h_t in aq_t]
        ak_t = [rope_t(h_t, cosa_ref[...], sina_ref[...], 16) for h_t in ak_t]
    put_heads_t(qa_ref, [h_t * (HEAD_DIM ** -0.5 * LOG2E) for h_t in aq_t])
    put_heads(ka_ref, jnp.concatenate(ak_t, axis=0).T, A_KV_HEADS)
    put_heads_t(vat_ref, split_heads_t(za_t[A_WIDTH + A_KV_WIDTH:], A_KV_HEADS), stacked=not rope)
    if kat_ref is not None:
        put_heads_t(kat_ref, ak_t, stacked=True)
    gate_tile(1)
    cqk = proj(OFF_CQ, 2 * C_KW)
    cv = proj(OFF_CV, C_WIDTH)

    bq_t = split_heads_t(zb_t[0:B_WIDTH], B_HEADS)
    bk_t = split_heads_t(zb_t[B_WIDTH:2 * B_WIDTH], B_HEADS)
    if rope:
        bq_t = [rope_t(h_t, cosb_ref[...], sinb_ref[...], 8) for h_t in bq_t]
        bk_t = [rope_t(h_t, cosb_ref[...], sinb_ref[...], 8) for h_t in bk_t]
    bq_t = [h_t * (B_QK_DIM ** -0.5 * LOG2E) for h_t in bq_t]
    put_heads_t(qb_ref, bq_t)
    put_heads(kb_ref, jnp.concatenate(bk_t, axis=0).T, B_HEADS)
    if rope:
        halves = [(hd, r0) for hd in range(B_HEADS) for r0 in (0, B_QK_DIM)]
        qstat_ref[0] = jnp.concatenate(
            [jnp.sum(bq_t[hd][r0:r0 + B_QK_DIM] * bq_t[hd][r0:r0 + B_QK_DIM], axis=0, keepdims=True)
             for hd, r0 in halves]
            + [jnp.sum(bq_t[hd][r0:r0 + B_QK_DIM] * bk_t[hd][r0:r0 + B_QK_DIM], axis=0, keepdims=True)
               for hd, r0 in halves], axis=0)
    put_heads_t(vbt_ref, split_heads_t(zb_t[2 * B_WIDTH:], B_HEADS), stacked=not rope)
    if kbt_ref is not None:
        put_heads_t(kbt_ref, bk_t, stacked=True)
    gate_tile(2)
    g_pre = _dot_tn(lr_t, jnp.concatenate([cwf_ref[...], cwb_ref[...]], axis=-1))
    gf_pre, gb_pre = g_pre[:, 0:C_KW], g_pre[:, C_KW:2 * C_KW]

    put_rows(cq_ref, cqk[:, 0:C_KW] * (C_DK ** -0.5))
    put_rows(ck_ref, cqk[:, C_KW:2 * C_KW])
    put_rows(cv_ref, cv)
    put_rows(gf_ref, _log_sigmoid(gf_pre + cbf_ref[...]) * (1.0 / GLA_TAU))
    put_rows(gb_ref, _log_sigmoid(gb_pre + cbb_ref[...]) * (1.0 / GLA_TAU))
    gate_tile(3)


def _in_projection(x, mod, per_batch_mod, layer, wts, rope_tabs, nb, tl, kv_prev=None):
    bsz, seq, _ = x.shape
    rope = rope_tabs is not None
    stacked = not rope
    stacked_first = stacked and kv_prev is None
    grid = (bsz // nb, seq // tl)

    def per_layer(shape):
        return pl.BlockSpec((None,) + shape, lambda b, t: (layer,) + (0,) * len(shape))

    mod_idx = (lambda b, t: (layer, b, 0, 0)) if per_batch_mod else (lambda b, t: (layer, 0, 0, 0))
    in_specs = [
        pl.BlockSpec((nb, tl, D_MODEL), lambda b, t: (b, t, 0)),
        pl.BlockSpec((None, None, 1, 3 * D_MODEL), mod_idx),
        per_layer((1, D_MODEL)),
        per_layer((IN_WIDTH, D_MODEL)),
        per_layer((HEAD_DIM, 1)),
        per_layer((HEAD_DIM, 1)),
        per_layer((2 * GATE_RANK, C_KW)),
        per_layer((1, C_KW)),
        per_layer((2 * GATE_RANK, C_KW)),
        per_layer((1, C_KW)),
    ]
    args = [x, mod, wts["g_pre"], wts["w_in_t"], wts["aq_gain_col"], wts["ak_gain_col"],
            wts["cw_f"], wts["cb_f"], wts["cw_b"], wts["cb_b"]]
    if rope:
        assert nb == 1
        in_specs += [
            pl.BlockSpec((HEAD_DIM, tl), lambda b, t: (0, t)),
            pl.BlockSpec((HEAD_DIM, tl), lambda b, t: (0, t)),
            pl.BlockSpec((HEAD_DIM, tl), lambda b, t: (0, t)),
            pl.BlockSpec((HEAD_DIM, tl), lambda b, t: (0, t)),
        ]
        args += list(rope_tabs)

    def heads(n):
        return pl.BlockSpec((nb, n, tl, HEAD_DIM), lambda b, t: (b, 0, t, 0))

    def heads_t(n):
        if stacked_first:
            return pl.BlockSpec((nb, DEPTH, n, HEAD_DIM, tl), lambda b, t: (b, 0, 0, 0, t))
        if stacked:
            return pl.BlockSpec((nb, None, n, HEAD_DIM, tl), lambda b, t: (b, layer, 0, 0, t))
        return pl.BlockSpec((nb, n, HEAD_DIM, tl), lambda b, t: (b, 0, 0, t))

    def q_heads_t(n):
        return pl.BlockSpec((nb, n, HEAD_DIM, tl), lambda b, t: (b, 0, 0, t))

    def rows(width):
        return pl.BlockSpec((nb, tl, width), lambda b, t: (b, t, 0))

    def hshape(n):
        return jax.ShapeDtypeStruct((bsz, n, seq, HEAD_DIM), BF16)

    def qshape(n):
        return jax.ShapeDtypeStruct((bsz, n, HEAD_DIM, seq), BF16)

    def tshape(n):
        if stacked:
            return jax.ShapeDtypeStruct((bsz, DEPTH, n, HEAD_DIM, seq), F32)
        return jax.ShapeDtypeStruct((bsz, n, HEAD_DIM, seq), BF16)

    def rshape(width):
        return jax.ShapeDtypeStruct((bsz, seq, width), F32)

    out_specs = [q_heads_t(A_HEADS), heads(A_KV_HEADS), q_heads_t(B_HEADS), heads(B_HEADS),
                 rows(C_KW), rows(C_KW), rows(C_WIDTH), rows(C_KW), rows(C_KW), rows(D_MIX),
                 heads_t(A_KV_HEADS), heads_t(B_HEADS)]
    out_shape = [qshape(A_HEADS), hshape(A_KV_HEADS), qshape(B_HEADS), hshape(B_HEADS),
                 rshape(C_KW), rshape(C_KW), rshape(C_WIDTH), rshape(C_KW), rshape(C_KW),
                 rshape(D_MIX), tshape(A_KV_HEADS), tshape(B_HEADS)]
    if stacked:
        out_specs += [heads_t(A_KV_HEADS), heads_t(B_HEADS)]
        out_shape += [tshape(A_KV_HEADS), tshape(B_HEADS)]
    else:
        out_specs += [pl.BlockSpec((nb, QSTAT_ROWS, tl), lambda b, t: (b, 0, t))]
        out_shape += [jax.ShapeDtypeStruct((bsz, QSTAT_ROWS, seq), F32)]
    aliases = {}
    if kv_prev is not None:
        for j, buf in enumerate(kv_prev):
            aliases[len(args)] = 10 + j
            in_specs.append(pl.BlockSpec(memory_space=pl.ANY))
            args.append(buf)
    return pl.pallas_call(
        functools.partial(_in_kernel, rope, layer, stacked_first, nb, tl, len(aliases)),
        grid=grid,
        in_specs=in_specs,
        out_specs=out_specs,
        out_shape=out_shape,
        input_output_aliases=aliases,
        compiler_params=pltpu.CompilerParams(
            dimension_semantics=("arbitrary", "arbitrary"), vmem_limit_bytes=VMEM_LIMIT),
        name="in_projection_rope" if rope else "in_projection",
    )(*args)


def _gla_bidirectional(cq_ref, ck_ref, cv_ref, gf_ref, gb_ref, s_f, s_b, seq, oc_ref):
    bl = GLA_BLOCK
    n_sub = bl // CHUNK
    nblk = seq // bl
    ri = _iota((bl, bl), 0)
    ci = _iota((bl, bl), 1)
    same_chunk = lax.shift_right_logical(ri, 6) == lax.shift_right_logical(ci, 6)
    bd = (lax.shift_right_logical(_iota((C_WIDTH, C_KW), 0), 6)
          == lax.shift_right_logical(_iota((C_WIDTH, C_KW), 1), 5))
    khead = lax.shift_right_logical(_iota((1, C_KW), 1), 5)
    vhead = lax.shift_right_logical(_iota((1, C_WIDTH), 1), 6)
    scans = []
    for reverse, g_ref in ((False, gf_ref), (True, gb_ref)):
        causal = same_chunk & ((ci >= ri) if reverse else (ci <= ri))
        scans.append((reverse, g_ref, causal, jnp.where(causal, 1.0, 0.0).astype(BF16)))
    states = [s_f, s_b]
    written = set()

    for step in range(nblk):
        rows0 = [step * bl, (nblk - 1 - step) * bl]
        cums = []
        for (reverse, g_ref, causal, tri), r0 in zip(scans, rows0):
            both = _dot(tri, jnp.concatenate(_split_bf16(g_ref[r0:r0 + bl, :]), axis=-1))
            cums.append(both[:, 0:C_KW] + both[:, C_KW:2 * C_KW])
        prep = []
        for (reverse, g_ref, causal, tri), r0, cum in zip(scans, rows0, cums):
            q = cq_ref[r0:r0 + bl, :]
            k = ck_ref[r0:r0 + bl, :]
            v = cv_ref[r0:r0 + bl, :]
            qt = q * jnp.exp(cum)
            ktb = (k * jnp.exp(-cum)).astype(BF16)
            vb = v.astype(BF16)
            lasts, kdecs = [], []
            for c in range(n_sub):
                c0 = CHUNK * c
                edge = c0 if reverse else c0 + CHUNK - 1
                last = cum[edge:edge + 1, :]
                lasts.append(last)
                kdecs.append((k[c0:c0 + CHUNK] * jnp.exp(last - cum[c0:c0 + CHUNK])).astype(BF16))
            prep.append((qt, ktb, v, vb, lasts, kdecs))
        scores, incs = [], []
        for qt, ktb, v, vb, lasts, kdecs in prep:
            scores.append([_dot_nt(jnp.where(khead == hd, qt, 0.0).astype(BF16), ktb)
                           for hd in range(C_HEADS)])
            incs.append([_dot_tn(vb[CHUNK * c:CHUNK * (c + 1)], kdecs[c]) for c in range(n_sub)])
        probs, entering = [], []
        for si, ((reverse, g_ref, causal, tri), (qt, ktb, v, vb, lasts, kdecs)) in enumerate(
                zip(scans, prep)):
            probs.append([jnp.where(causal, s, 0.0).astype(BF16) for s in scores[si]])
            s_t = states[si]
            before = [None] * n_sub
            subs = range(n_sub)
            for c in (reversed(subs) if reverse else subs):
                before[c] = s_t.astype(BF16)
                s_t = jnp.exp(lasts[c]) * s_t + jnp.where(bd, incs[si][c], 0.0)
            states[si] = s_t
            entering.append(before)
        outs = []
        for si, (qt, ktb, v, vb, lasts, kdecs) in enumerate(prep):
            qtb = qt.astype(BF16)
            o = jnp.concatenate([_dot_nt(qtb[CHUNK * c:CHUNK * (c + 1)], entering[si][c])
                                 for c in range(n_sub)], axis=0)
            for hd in range(C_HEADS):
                o = o + _dot(probs[si][hd], jnp.where(vhead == hd, v, 0.0).astype(BF16))
            outs.append(o)
        if rows0[0] == rows0[1]:
            outs, rows0 = [outs[0] + outs[1]], rows0[:1]
        for o, r0 in zip(outs, rows0):
            if r0 in written:
                oc_ref[r0:r0 + bl, :] = oc_ref[r0:r0 + bl, :] + o
            else:
                oc_ref[r0:r0 + bl, :] = o
                written.add(r0)
    return states[0], states[1]


def _attend_t(jobs, shifts=None):
    def scores(i):
        return _dot(jobs[i][0][...], jobs[i][1])

    outs = []
    pending = [scores(i) for i in range(min(SCORE_LOOKAHEAD, len(jobs)))]
    for i, (_, _, vt) in enumerate(jobs):
        st = pending.pop(0)
        if i + SCORE_LOOKAHEAD < len(jobs):
            pending.append(scores(i + SCORE_LOOKAHEAD))
        m = jnp.max(st, axis=0, keepdims=True) if shifts is None else shifts[i]
        p = jnp.exp2(st - m).astype(BF16)
        ot = _dot(vt[...], p)
        outs.append(ot[0:HEAD_DIM] * (1.0 / ot[HEAD_DIM:HEAD_DIM + 1]))
    return outs


def _pair_rows(a, b):
    return jnp.concatenate([a, b], axis=0).T


def _state_to_blockdiag_t(s_ref):
    rows = []
    for hd in range(C_HEADS):
        pieces = []
        if hd:
            pieces.append(jnp.zeros((C_DK, C_DV * hd), F32))
        pieces.append(s_ref[hd])
        if hd < C_HEADS - 1:
            pieces.append(jnp.zeros((C_DK, C_DV * (C_HEADS - 1 - hd)), F32))
        rows.append(jnp.concatenate(pieces, axis=-1))
    return jnp.concatenate(rows, axis=0).T


def _blockdiag_t_to_state(s_t, out_ref):
    s = s_t.T
    for hd in range(C_HEADS):
        out_ref[hd] = s[C_DK * hd:C_DK * (hd + 1), C_DV * hd:C_DV * (hd + 1)]


def _mix_sequence(cached, layer, first_state, n_aliased, lam_init, seq, qt, qb, *refs):
    it = iter(refs)
    x_ref, mod_ref = next(it), next(it)
    qa_ref, ka_ref, qb_ref, kb_ref = (next(it) for _ in range(4))
    cq_ref, ck_ref, cv_ref, gf_ref, gb_ref, su_ref = (next(it) for _ in range(6))
    vat_ref, vbt_ref = next(it), next(it)
    if cached:
        cakt_ref, cavt_ref, cbkt_ref, cbvt_ref, s0f_ref, s0b_ref = (next(it) for _ in range(6))
        qstat_ref, aqg_ref = next(it), next(it)
    wout_ref, gpost_ref, bog_ref, cog_ref, lamp_ref = (next(it) for _ in range(5))
    for _ in range(n_aliased):
        next(it)
    y_ref = next(it)
    if not cached:
        sf_ref, sb_ref = next(it), next(it)
    kA_s, vtA_s, kB_s, vtB_s, oc_s, mixed_s, kn2_s = (next(it) for _ in range(7))

    lk = kA_s.shape[1]
    past = lk - seq
    t = pl.program_id(1)

    def once_per_sequence(body):
        return body() if seq == qt else pl.when(t == 0)(body)

    @once_per_sequence
    def _per_sequence():
        ones_row = jnp.where(_iota((VT_ROWS - HEAD_DIM, lk), 0) == 0, 1.0, 0.0).astype(BF16)
        for k_new, kt_cache, k_dst, vt_new, vt_cache, vt_dst in (
                (ka_ref, cakt_ref if cached else None, kA_s, vat_ref, cavt_ref if cached else None, vtA_s),
                (kb_ref, cbkt_ref if cached else None, kB_s, vbt_ref, cbvt_ref if cached else None, vtB_s)):
            n_heads = k_dst.shape[0]
            if cached:
                for h0 in range(0, n_heads, 2):
                    pair = jnp.concatenate([kt_cache[h0], kt_cache[h0 + 1]], axis=0).T
                    k_dst[h0, 0:past, :] = pair[:, 0:HEAD_DIM].astype(BF16)
                    k_dst[h0 + 1, 0:past, :] = pair[:, HEAD_DIM:2 * HEAD_DIM].astype(BF16)
            for hd in range(n_heads):
                k_dst[hd, past:lk, :] = k_new[hd]
                if cached:
                    vt_dst[hd, 0:HEAD_DIM, 0:past] = vt_cache[hd].astype(BF16)
                vt_dst[hd, 0:HEAD_DIM, past:lk] = vt_new[hd].astype(BF16)
                vt_dst[hd, HEAD_DIM:VT_ROWS, :] = ones_row
        bounded_jobs = ([(kA_s, g) for g in range(A_KV_HEADS)] + [(kB_s, h) for h in range(B_HEADS)]
                        if lk > KEY_TILE else [])
        for j, (k_dst, hd) in enumerate(bounded_jobs):
            kf = k_dst[hd].astype(F32)
            kn2 = jnp.max(jnp.sum(kf * kf, axis=-1, keepdims=True), axis=0, keepdims=True)
            kn2_s[j:j + 1, :] = jnp.broadcast_to(kn2, (1, LANES))

        if cached:
            s0f, s0b = _state_to_blockdiag_t(s0f_ref), _state_to_blockdiag_t(s0b_ref)
        else:
            s0f = jnp.zeros((C_WIDTH, C_KW), F32)
            s0b = s0f
        s_f, s_b = _gla_bidirectional(cq_ref, ck_ref, cv_ref, gf_ref, gb_ref, s0f, s0b, seq, oc_s)
        if not cached:
            for ref, s_t in ((sf_ref, s_f), (sb_ref, s_b)):
                if first_state:
                    for l2 in range(DEPTH):
                        if l2 == layer:
                            _blockdiag_t_to_state(s_t, ref.at[l2])
                        else:
                            ref[l2] = jnp.zeros(ref.shape[1:], F32)
                else:
                    _blockdiag_t_to_state(s_t, ref)
        for r0 in range(0, seq, GLA_BLOCK):
            oc = oc_s[r0:r0 + GLA_BLOCK, :]
            oc_s[r0:r0 + GLA_BLOCK, :] = oc * lax.rsqrt(_group_mean_sq(oc, 6) + EPS) * cog_ref[...]

    lam = (jnp.exp(jnp.sum(lamp_ref[0:1, :] * lamp_ref[1:2, :], axis=-1, keepdims=True))
           - jnp.exp(jnp.sum(lamp_ref[2:3, :] * lamp_ref[3:4, :], axis=-1, keepdims=True))
           + lam_init)

    def project_out():
        seq_rows = pl.ds(pl.multiple_of(t * qt, qt), qt)
        mixed_s[:, A_WIDTH + B_WIDTH:D_MIX] = oc_s[seq_rows, :]
        gate = mod_ref[:, 2 * D_MODEL:3 * D_MODEL]
        mixed = (mixed_s[...] * su_ref[...]).astype(BF16)
        y = _dot(mixed, wout_ref[...])
        yn = y * lax.rsqrt(jnp.mean(y * y, axis=-1, keepdims=True) + EPS) * gpost_ref[...]
        y_ref[...] = x_ref[...] + gate * yn

    def attn_block(i, carry):
        rows = pl.ds(i * qb, qb) if isinstance(i, int) else pl.ds(pl.multiple_of(i * qb, qb), qb)
        jobs = []
        for grp in range(A_KV_HEADS):
            q4 = jnp.concatenate([qa_ref[A_GROUP * grp + j, :, rows] for j in range(A_GROUP)], axis=-1)
            jobs.append((kA_s.at[grp], q4, vtA_s.at[grp]))
        dim = _iota((HEAD_DIM, qb), 0)
        for hd in range(B_HEADS):
            q = qb_ref[hd, :, rows]
            zero = jnp.zeros_like(q)
            q2 = jnp.concatenate([jnp.where(dim < B_QK_DIM, q, zero),
                                  jnp.where(dim >= B_QK_DIM, q, zero)], axis=-1)
            jobs.append((kB_s.at[hd], q2, vtB_s.at[hd]))

        def finish(outs):
            for grp in range(A_KV_HEADS):
                ot = outs[grp]
                for pair in range(A_GROUP // 2):
                    c0 = 2 * pair * qb
                    col = A_GROUP * HEAD_DIM * grp + 2 * HEAD_DIM * pair
                    mixed_s[rows, col:col + 2 * HEAD_DIM] = _pair_rows(ot[:, c0:c0 + qb],
                                                                       ot[:, c0 + qb:c0 + 2 * qb])
            obs = []
            for hd in range(B_HEADS):
                ot = outs[A_KV_HEADS + hd]
                ob = ot[:, 0:qb] - lam * ot[:, qb:2 * qb]
                obs.append(ob * lax.rsqrt(jnp.mean(ob * ob, axis=0, keepdims=True) + EPS))
            for pair in range(B_HEADS // 2):
                col = A_WIDTH + 2 * B_V_DIM * pair
                mixed_s[rows, col:col + 2 * B_V_DIM] = (_pair_rows(obs[2 * pair], obs[2 * pair + 1])
                                                        * bog_ref[...] * (1.0 - lam_init))
            if qt == qb:
                project_out()

        if lk <= KEY_TILE:
            finish(_attend_t(jobs))
            return carry

        qa_norm = (jnp.max(jnp.abs(aqg_ref[...]), axis=-1, keepdims=True)
                   * (HEAD_DIM ** 0.5 * HEAD_DIM ** -0.5 * LOG2E))
        stat_t = qstat_ref[...]
        shifts, gaps = [], []
        for j, (_, q, _) in enumerate(jobs):
            k_norm = jnp.sqrt(kn2_s[j:j + 1, 0:1])
            if j < A_KV_HEADS:
                upper = qa_norm * k_norm * BOUND_SLACK
                shifts.append(jnp.broadcast_to(upper, (1, q.shape[1])))
                gaps.append(2.0 * upper)
            else:
                r0 = 2 * (j - A_KV_HEADS)
                r1 = r0 + 2 * B_HEADS
                qn2 = jnp.concatenate([stat_t[r0:r0 + 1], stat_t[r0 + 1:r0 + 2]], axis=-1)
                lower = jnp.concatenate([stat_t[r1:r1 + 1], stat_t[r1 + 1:r1 + 2]], axis=-1)
                upper = jnp.sqrt(qn2) * k_norm * BOUND_SLACK
                shifts.append(upper)
                gaps.append(jnp.max(upper - lower, axis=-1, keepdims=True))
        worst = functools.reduce(jnp.maximum, gaps)
        safe = worst[0, 0] <= SAFE_GAP

        @pl.when(safe)
        def _bounded():
            finish(_attend_t(jobs, shifts))

        @pl.when(jnp.logical_not(safe))
        def _exact_max():
            finish(_attend_t(jobs))

        return carry

    if qt != qb:
        lax.fori_loop(0, qt // qb, attn_block, 0)
        project_out()
    else:
        attn_block(0, 0)


def _mix_kernel(nseq, *params_and_refs):
    params, refs = params_and_refs[:N_MIX_PARAMS], params_and_refs[N_MIX_PARAMS:]
    if nseq == 1:
        return _mix_sequence(*params, *refs)
    n_aliased = params[3]
    shared = {1} | set(range(N_MIX_SEQ_INPUTS, N_MIX_SEQ_INPUTS + N_MIX_WEIGHTS + n_aliased))
    for bi in range(nseq):
        _mix_sequence(*params, *[r if idx in shared else r.at[bi] for idx, r in enumerate(refs)])


def _mixer(x, mod, per_batch_mod, layer, proj, wts, lam_init, cache, state_prev=None):
    bsz, seq, _ = x.shape
    cached = cache is not None
    past = cache[0].shape[4] if cached else 0
    qa, ka, qb, kb, cq, ck, cv, gf, gb, su, vat, vbt = proj[:12]
    qt = 256
    qb_rows = 256
    nseq = 1 if cached else 4
    lead = None if nseq == 1 else nseq

    def per_layer(shape):
        return pl.BlockSpec((None,) + shape, lambda b, t: (layer,) + (0,) * len(shape))

    def layer_heads_t(n, length):
        return pl.BlockSpec((lead, None, n, HEAD_DIM, length), lambda b, t: (b, layer, 0, 0, 0))

    def heads_t(n):
        if cached:
            return pl.BlockSpec((lead, n, HEAD_DIM, seq), lambda b, t: (b, 0, 0, 0))
        return layer_heads_t(n, seq)

    def heads(n):
        return pl.BlockSpec((lead, n, seq, HEAD_DIM), lambda b, t: (b, 0, 0, 0))

    def head_tile(n):
        return pl.BlockSpec((lead, n, HEAD_DIM, qt), lambda b, t: (b, 0, 0, t))

    def rows(width):
        return pl.BlockSpec((lead, seq, width), lambda b, t: (b, 0, 0))

    def row_tile(width):
        return pl.BlockSpec((lead, qt, width), lambda b, t: (b, t, 0))

    mod_idx = (lambda b, t: (layer, b, 0, 0)) if per_batch_mod else (lambda b, t: (layer, 0, 0, 0))
    in_specs = [row_tile(D_MODEL), pl.BlockSpec((None, None, 1, 3 * D_MODEL), mod_idx),
                head_tile(A_HEADS), heads(A_KV_HEADS), head_tile(B_HEADS), heads(B_HEADS),
                rows(C_KW), rows(C_KW), rows(C_WIDTH), rows(C_KW), rows(C_KW), row_tile(D_MIX),
                heads_t(A_KV_HEADS), heads_t(B_HEADS)]
    args = [x, mod, qa, ka, qb, kb, cq, ck, cv, gf, gb, su, vat, vbt]
    if cached:
        state_in = pl.BlockSpec((None, None, C_HEADS, C_DK, C_DV), lambda b, t: (b, layer, 0, 0, 0))
        in_specs += [layer_heads_t(A_KV_HEADS, past), layer_heads_t(A_KV_HEADS, past),
                     layer_heads_t(B_HEADS, past), layer_heads_t(B_HEADS, past), state_in, state_in,
                     pl.BlockSpec((None, QSTAT_ROWS, qt), lambda b, t: (b, 0, t)),
                     per_layer((1, HEAD_DIM))]
        args += list(cache) + [proj[12], wts["aq_gain"]]
    in_specs += [per_layer((D_MIX, D_MODEL)), per_layer((1, D_MODEL)), per_layer((1, 2 * B_V_DIM)),
                 per_layer((1, C_WIDTH)), per_layer((4, B_QK_DIM))]
    args += [wts["w_out"], wts["g_post"], wts["b_out_gain"], wts["c_out_gain"], wts["lam_params"]]

    out_specs = [row_tile(D_MODEL)]
    out_shape = [jax.ShapeDtypeStruct((bsz, seq, D_MODEL), F32)]
    aliases = {}
    first_state = not cached and state_prev is None
    if not cached:
        if first_state:
            state_out = pl.BlockSpec((lead, DEPTH, C_HEADS, C_DK, C_DV), lambda b, t: (b, 0, 0, 0, 0))
        else:
            state_out = pl.BlockSpec((lead, None, C_HEADS, C_DK, C_DV),
                                     lambda b, t: (b, layer, 0, 0, 0))
            for j, buf in enumerate(state_prev):
                aliases[len(args)] = 1 + j
                in_specs.append(pl.BlockSpec(memory_space=pl.ANY))
                args.append(buf)
        out_specs += [state_out, state_out]
        out_shape += [jax.ShapeDtypeStruct((bsz, DEPTH, C_HEADS, C_DK, C_DV), F32)] * 2

    lk = past + seq
    per_seq = () if nseq == 1 else (nseq,)
    scratch = [pltpu.VMEM(per_seq + shape, dtype) for shape, dtype in (
        ((A_KV_HEADS, lk, HEAD_DIM), BF16), ((A_KV_HEADS, VT_ROWS, lk), BF16),
        ((B_HEADS, lk, HEAD_DIM), BF16), ((B_HEADS, VT_ROWS, lk), BF16),
        ((seq, C_WIDTH), F32), ((qt, D_MIX), F32), ((8, LANES), F32))]
    return pl.pallas_call(
        functools.partial(_mix_kernel, nseq, cached, layer, first_state, len(aliases), lam_init, seq,
                          qt, qb_rows),
        grid=(bsz // nseq, seq // qt),
        in_specs=in_specs,
        out_specs=out_specs,
        out_shape=out_shape,
        scratch_shapes=scratch,
        input_output_aliases=aliases,
        compiler_params=pltpu.CompilerParams(
            dimension_semantics=("arbitrary", "arbitrary"), vmem_limit_bytes=VMEM_LIMIT),
        name="mixer_cached" if cached else "mixer",
    )(*args)


def _rope_tables(seq):
    t = jnp.arange(seq)
    pos_row = (t // GRID_W).astype(F32)
    pos_col = (t % GRID_W).astype(F32)

    def tables(half):
        freq = ROPE_THETA ** (-jnp.arange(half, dtype=F32) / half)
        ang_r = freq[:, None] * pos_row[None, :]
        ang_c = freq[:, None] * pos_col[None, :]
        cos = jnp.concatenate([jnp.cos(ang_r), jnp.cos(ang_r), jnp.cos(ang_c), jnp.cos(ang_c)], axis=0)
        sin = jnp.concatenate([-jnp.sin(ang_r), jnp.sin(ang_r), -jnp.sin(ang_c), jnp.sin(ang_c)], axis=0)
        reps = HEAD_DIM // (4 * half)
        return jnp.tile(cos, (reps, 1)), jnp.tile(sin, (reps, 1))

    cos_a, sin_a = tables(HEAD_DIM // 4)
    cos_b, sin_b = tables(B_QK_DIM // 4)
    return cos_a, sin_a, cos_b, sin_b


def _prepare_weights(g_pre, g_post, w_in, w_out, a_q_gain, a_k_gain, b_lambda_q1, b_lambda_k1,
                     b_lambda_q2, b_lambda_k2, b_out_gain, c_gate_w_fwd, c_gate_b_fwd, c_gate_w_bwd,
                     c_gate_b_bwd, c_out_gain):
    w_in_t = jnp.swapaxes(w_in, 1, 2).astype(BF16)
    pad = jnp.zeros((DEPTH, GATE_RANK, C_KW), F32)
    cw_f = jnp.concatenate([c_gate_w_fwd, pad], axis=1).astype(BF16)
    cw_b = jnp.concatenate([pad, c_gate_w_bwd], axis=1).astype(BF16)
    return {
        "g_pre": g_pre[:, None, :],
        "g_post": g_post[:, None, :],
        "w_in_t": w_in_t,
        "w_out": w_out.astype(BF16),
        "aq_gain": a_q_gain[:, None, :],
        "aq_gain_col": a_q_gain[:, :, None],
        "ak_gain_col": a_k_gain[:, :, None],
        "cw_f": cw_f,
        "cb_f": c_gate_b_fwd[:, None, :],
        "cw_b": cw_b,
        "cb_b": c_gate_b_bwd[:, None, :],
        "b_out_gain": jnp.tile(b_out_gain, (1, 2))[:, None, :],
        "c_out_gain": jnp.tile(c_out_gain, (1, C_HEADS))[:, None, :],
        "lam_params": jnp.stack([b_lambda_q1, b_lambda_k1, b_lambda_q2, b_lambda_k2], axis=1),
    }


def kernel(x_prompt, x_sample, c, cache_a_k, cache_a_v, cache_b_k, cache_b_v, state_c_fwd, state_c_bwd, c_ctx, w_mod, b_mod, g_pre, g_post, w_in, w_out, a_q_gain, a_k_gain, b_lambda_q1, b_lambda_k1, b_lambda_q2, b_lambda_k2, b_out_gain, c_gate_w_fwd, c_gate_b_fwd, c_gate_w_bwd, c_gate_b_bwd, c_out_gain):
    dec_batch = x_sample.shape[0]
    dec_seq = x_sample.shape[1]

    mod_rows = 16
    cvec = jnp.zeros((mod_rows, D_MODEL), F32).at[0:dec_batch].set(c).at[dec_batch].set(c_ctx)
    mod = _modulation(cvec, w_mod, b_mod)[:, :, None, :]
    mod_lat = mod[:, 0:dec_batch]
    mod_ctx = mod[:, dec_batch:dec_batch + 1]

    wts = _prepare_weights(g_pre, g_post, w_in, w_out, a_q_gain, a_k_gain, b_lambda_q1, b_lambda_k1,
                           b_lambda_q2, b_lambda_k2, b_out_gain, c_gate_w_fwd, c_gate_b_fwd,
                           c_gate_w_bwd, c_gate_b_bwd, c_out_gain)
    rope_tabs = _rope_tables(dec_seq)
    cache = tuple(jnp.swapaxes(a, -1, -2) for a in (cache_a_k, cache_a_v, cache_b_k, cache_b_v))
    cache += (state_c_fwd, state_c_bwd)

    y_p, y_s = x_prompt, x_sample
    kv_ctx = None
    states = None
    for l in range(DEPTH):
        lam_init = 0.8 - 0.6 * math.exp(-0.3 * l)
        proj_p = _in_projection(y_p, mod_ctx, False, l, wts, None, 4, x_prompt.shape[1], kv_ctx)
        kv_ctx = proj_p[10:14]
        y_p, *states = _mixer(y_p, mod_ctx, False, l, proj_p, wts, lam_init, None, states)

        proj_s = _in_projection(y_s, mod_lat, True, l, wts, rope_tabs, 1, 1024)
        (y_s,) = _mixer(y_s, mod_lat, True, l, proj_s, wts, lam_init, cache)

    va_t, vb_t, ka_t, kb_t = kv_ctx
    new_kv = [jnp.swapaxes(a, -1, -2) for a in (ka_t, va_t, kb_t, vb_t)]
    return (y_p, y_s, *new_kv, *states)
```

```python
import functools
import math

import jax
import jax.numpy as jnp
from jax import lax
from jax.experimental import pallas as pl
from jax.experimental.pallas import tpu as pltpu

F32 = jnp.float32
BF16 = jnp.bfloat16

D_MODEL = 1024
DEPTH = 2
GRID_W = 64
HEAD_DIM = 64
A_HEADS = 8
A_KV_HEADS = 2
A_GROUP = A_HEADS // A_KV_HEADS
A_WIDTH = A_HEADS * HEAD_DIM
A_KV_WIDTH = A_KV_HEADS * HEAD_DIM
B_HEADS = 4
B_QK_DIM = 32
B_V_DIM = 64
B_WIDTH = B_HEADS * B_V_DIM
C_HEADS = 4
C_DK = 32
C_DV = 64
C_KW = C_HEADS * C_DK
C_WIDTH = C_HEADS * C_DV
GATE_RANK = 16
GLA_TAU = 16.0
CHUNK = 64
D_MIX = A_WIDTH + B_WIDTH + C_WIDTH
ROPE_THETA = 10000.0
EPS = 1e-6

LANES = 128
GLA_BLOCK = 256
VT_ROWS = HEAD_DIM + 16
SCORE_LOOKAHEAD = 6
SAFE_GAP = 96.0
BOUND_SLACK = 1.02
QSTAT_ROWS = 4 * B_HEADS
U_TILE = D_MIX // 4
N_MIX_PARAMS = 8
N_MIX_SEQ_INPUTS = 14
N_MIX_WEIGHTS = 5
KEY_TILE = 256
LOG2E = math.log2(math.e)

OFF_AQ = 0
OFF_AK = OFF_AQ + A_WIDTH
OFF_AV = OFF_AK + A_KV_WIDTH
OFF_BQ = OFF_AV + A_KV_WIDTH
OFF_BK = OFF_BQ + B_WIDTH
OFF_BV = OFF_BK + B_WIDTH
OFF_CQ = OFF_BV + B_WIDTH
OFF_CK = OFF_CQ + C_KW
OFF_CV = OFF_CK + C_KW
OFF_LR = OFF_CV + C_WIDTH
OFF_U = OFF_LR + 2 * GATE_RANK
IN_WIDTH = OFF_U + D_MIX

VMEM_LIMIT = 60 * 1024 * 1024


def _dot(a, b):
    return jnp.dot(a, b, preferred_element_type=F32)


def _dot_nt(a, b):
    return lax.dot_general(a, b, (((1,), (1,)), ((), ())), preferred_element_type=F32)


def _dot_tn(a, b):
    return lax.dot_general(a, b, (((0,), (0,)), ((), ())), preferred_element_type=F32)


def _split_bf16(x):
    hi = x.astype(BF16)
    lo = (x - hi.astype(F32)).astype(BF16)
    return hi, lo


def _iota(shape, dim):
    return lax.broadcasted_iota(jnp.int32, shape, dim)


def _group_mean_sq(x, group_log2):
    width = x.shape[-1]
    r = lax.shift_right_logical(_iota((width, width), 0), group_log2)
    c = lax.shift_right_logical(_iota((width, width), 1), group_log2)
    ones = jnp.where(r == c, 1.0, 0.0).astype(BF16)
    return _dot((x * x).astype(BF16), ones) * (1.0 / (1 << group_log2))


def _log_sigmoid(x):
    return jnp.minimum(x, 0.0) - jnp.log1p(jnp.exp(-jnp.abs(x)))


def _silu(x):
    return x * (1.0 / (1.0 + jnp.exp(-x)))


def _mod_kernel(c_ref, w_ref, b_ref, o_ref):
    a = _silu(c_ref[...]).astype(BF16)
    o_ref[...] = _dot(a, w_ref[...].astype(BF16)) + b_ref[...]


def _modulation(cvec, w_mod, b_mod):
    rows = cvec.shape[0]
    nblk = 3
    return pl.pallas_call(
        _mod_kernel,
        grid=(DEPTH, nblk),
        in_specs=[
            pl.BlockSpec((rows, D_MODEL), lambda l, n: (0, 0)),
            pl.BlockSpec((None, D_MODEL, D_MODEL), lambda l, n: (l, 0, n)),
            pl.BlockSpec((None, 1, D_MODEL), lambda l, n: (l, 0, n)),
        ],
        out_specs=pl.BlockSpec((None, rows, D_MODEL), lambda l, n: (l, 0, n)),
        out_shape=jax.ShapeDtypeStruct((DEPTH, rows, 3 * D_MODEL), F32),
        compiler_params=pltpu.CompilerParams(
            dimension_semantics=("arbitrary", "arbitrary"), vmem_limit_bytes=VMEM_LIMIT),
        name="modulation",
    )(cvec, w_mod, b_mod.reshape(DEPTH, 1, 3 * D_MODEL))


def _in_kernel(rope, layer, stacked_first, nb, tl, n_aliased, *refs):
    (x_ref, mod_ref, gpre_ref, wt_ref, aqg_ref, akg_ref, cwf_ref, cbf_ref, cwb_ref, cbb_ref) = refs[:10]
    refs = refs[10:]
    if rope:
        cosa_ref, sina_ref, cosb_ref, sinb_ref = refs[:4]
        refs = refs[4:]
    refs = refs[n_aliased:]
    (qa_ref, ka_ref, qb_ref, kb_ref, cq_ref, ck_ref, cv_ref, gf_ref, gb_ref, su_ref,
     vat_ref, vbt_ref) = refs[:12]
    kat_ref, kbt_ref = (None, None) if rope else refs[12:]
    qstat_ref = refs[12] if rope else None

    x = x_ref[...].reshape(nb * tl, D_MODEL)
    shift = mod_ref[:, 0:D_MODEL]
    scale = mod_ref[:, D_MODEL:2 * D_MODEL]
    ms = jnp.mean(x * x, axis=-1, keepdims=True)
    h = (x * lax.rsqrt(ms + EPS)) * gpre_ref[...] * (1.0 + scale) + shift
    hb = h.astype(BF16)

    def proj(off, width):
        return _dot_nt(hb, wt_ref[off:off + width, :])

    def proj_t(off, width):
        return _dot_nt(wt_ref[off:off + width, :], hb)

    def put_rows(ref, val):
        for bi in range(nb):
            ref[bi] = val[bi * tl:(bi + 1) * tl].astype(ref.dtype)

    def put_heads(ref, val, n_heads):
        for bi in range(nb):
            for hd in range(n_heads):
                ref[bi, hd] = val[bi * tl:(bi + 1) * tl,
                                  HEAD_DIM * hd:HEAD_DIM * (hd + 1)].astype(ref.dtype)

    def put_heads_t(ref, heads_t, stacked=False):
        for bi in range(nb):
            for hd, val_t in enumerate(heads_t):
                blk = val_t[:, bi * tl:(bi + 1) * tl].astype(ref.dtype)
                if stacked and stacked_first:
                    for l2 in range(DEPTH):
                        ref[bi, l2, hd] = blk if l2 == layer else jnp.zeros_like(blk)
                else:
                    ref[bi, hd] = blk

    def split_heads_t(val_t, n_heads):
        return [val_t[HEAD_DIM * hd:HEAD_DIM * (hd + 1)] for hd in range(n_heads)]

    def rms_t(head_t, gain_col):
        ms_h = jnp.mean(head_t * head_t, axis=0, keepdims=True)
        return head_t * lax.rsqrt(ms_h + EPS) * gain_col

    def rope_t(head_t, cos_t, sin_t, dist):
        blocks = [head_t[r0:r0 + dist] for r0 in range(0, HEAD_DIM, dist)]
        partner = jnp.concatenate([blocks[j ^ 1] for j in range(len(blocks))], axis=0)
        return head_t * cos_t + partner * sin_t

    def gate_tile(j):
        c0 = U_TILE * j
        val = _silu(proj(OFF_U + c0, U_TILE))
        for bi in range(nb):
            su_ref[bi, :, c0:c0 + U_TILE] = val[bi * tl:(bi + 1) * tl]

    za_t = proj_t(OFF_AQ, A_WIDTH + 2 * A_KV_WIDTH)
    zb_t = proj_t(OFF_BQ, 3 * B_WIDTH)
    lr_t = proj_t(OFF_LR, 2 * GATE_RANK).astype(BF16)
    gate_tile(0)

    aq_t = [rms_t(h_t, aqg_ref[...]) for h_t in split_heads_t(za_t[0:A_WIDTH], A_HEADS)]
    ak_t = [rms_t(h_t, akg_ref[...])
            for h_t in split_heads_t(za_t[A_WIDTH:A_WIDTH + A_KV_WIDTH], A_KV_HEADS)]
    if rope:
        aq_t = [rope_t(h_t, cosa_ref[...], sina_ref[...], 16) for h_t in aq_t]
        ak_t = [rope_t(h_t, cosa_ref[...], sina_ref[...], 16) for h_t in ak_t]
    put_heads_t(qa_ref, [h_t * (HEAD_DIM ** -0.5 * LOG2E) for h_t in aq_t])
    put_heads(ka_ref, jnp.concatenate(ak_t, axis=0).T, A_KV_HEADS)
    put_heads_t(vat_ref, split_heads_t(za_t[A_WIDTH + A_KV_WIDTH:], A_KV_HEADS), stacked=not rope)
    if kat_ref is not None:
        put_heads_t(kat_ref, ak_t, stacked=True)
    gate_tile(1)
    cqk = proj(OFF_CQ, 2 * C_KW)
    cv = proj(OFF_CV, C_WIDTH)

    bq_t = split_heads_t(zb_t[0:B_WIDTH], B_HEADS)
    bk_t = split_heads_t(zb_t[B_WIDTH:2 * B_WIDTH], B_HEADS)
    if rope:
        bq_t = [rope_t(h_t, cosb_ref[...], sinb_ref[...], 8) for h_t in bq_t]
        bk_t = [rope_t(h_t, cosb_ref[...], sinb_ref[...], 8) for h_t in bk_t]
    bq_t = [h_t * (B_QK_DIM ** -0.5 * LOG2E) for h_t in bq_t]
    put_heads_t(qb_ref, bq_t)
    put_heads(kb_ref, jnp.concatenate(bk_t, axis=0).T, B_HEADS)
    if rope:
        halves = [(hd, r0) for hd in range(B_HEADS) for r0 in (0, B_QK_DIM)]
        qstat_ref[0] = jnp.concatenate(
            [jnp.sum(bq_t[hd][r0:r0 + B_QK_DIM] * bq_t[hd][r0:r0 + B_QK_DIM], axis=0, keepdims=True)
             for hd, r0 in halves]
            + [jnp.sum(bq_t[hd][r0:r0 + B_QK_DIM] * bk_t[hd][r0:r0 + B_QK_DIM], axis=0, keepdims=True)
               for hd, r0 in halves], axis=0)
    put_heads_t(vbt_ref, split_heads_t(zb_t[2 * B_WIDTH:], B_HEADS), stacked=not rope)
    if kbt_ref is not None:
        put_heads_t(kbt_ref, bk_t, stacked=True)
    gate_tile(2)
    g_pre = _dot_tn(lr_t, jnp.concatenate([cwf_ref[...], cwb_ref[...]], axis=-1))
    gf_pre, gb_pre = g_pre[:, 0:C_KW], g_pre[:, C_KW:2 * C_KW]

    put_rows(cq_ref, cqk[:, 0:C_KW] * (C_DK ** -0.5))
    put_rows(ck_ref, cqk[:, C_KW:2 * C_KW])
    put_rows(cv_ref, cv)
    put_rows(gf_ref, _log_sigmoid(gf_pre + cbf_ref[...]) * (1.0 / GLA_TAU))
    put_rows(gb_ref, _log_sigmoid(gb_pre + cbb_ref[...]) * (1.0 / GLA_TAU))
    gate_tile(3)


def _in_projection(x, mod, per_batch_mod, layer, wts, rope_tabs, nb, tl, kv_prev=None):
    bsz, seq, _ = x.shape
    rope = rope_tabs is not None
    stacked = not rope
    stacked_first = stacked and kv_prev is None
    grid = (bsz // nb, seq // tl)

    def per_layer(shape):
        return pl.BlockSpec((None,) + shape, lambda b, t: (layer,) + (0,) * len(shape))

    mod_idx = (lambda b, t: (layer, b, 0, 0)) if per_batch_mod else (lambda b, t: (layer, 0, 0, 0))
    in_specs = [
        pl.BlockSpec((nb, tl, D_MODEL), lambda b, t: (b, t, 0)),
        pl.BlockSpec((None, None, 1, 3 * D_MODEL), mod_idx),
        per_layer((1, D_MODEL)),
        per_layer((IN_WIDTH, D_MODEL)),
        per_layer((HEAD_DIM, 1)),
        per_layer((HEAD_DIM, 1)),
        per_layer((2 * GATE_RANK, C_KW)),
        per_layer((1, C_KW)),
        per_layer((2 * GATE_RANK, C_KW)),
        per_layer((1, C_KW)),
    ]
    args = [x, mod, wts["g_pre"], wts["w_in_t"], wts["aq_gain_col"], wts["ak_gain_col"],
            wts["cw_f"], wts["cb_f"], wts["cw_b"], wts["cb_b"]]
    if rope:
        assert nb == 1
        in_specs += [
            pl.BlockSpec((HEAD_DIM, tl), lambda b, t: (0, t)),
            pl.BlockSpec((HEAD_DIM, tl), lambda b, t: (0, t)),
            pl.BlockSpec((HEAD_DIM, tl), lambda b, t: (0, t)),
            pl.BlockSpec((HEAD_DIM, tl), lambda b, t: (0, t)),
        ]
        args += list(rope_tabs)

    def heads(n):
        return pl.BlockSpec((nb, n, tl, HEAD_DIM), lambda b, t: (b, 0, t, 0))

    def heads_t(n):
        if stacked_first:
            return pl.BlockSpec((nb, DEPTH, n, HEAD_DIM, tl), lambda b, t: (b, 0, 0, 0, t))
        if stacked:
            return pl.BlockSpec((nb, None, n, HEAD_DIM, tl), lambda b, t: (b, layer, 0, 0, t))
        return pl.BlockSpec((nb, n, HEAD_DIM, tl), lambda b, t: (b, 0, 0, t))

    def q_heads_t(n):
        return pl.BlockSpec((nb, n, HEAD_DIM, tl), lambda b, t: (b, 0, 0, t))

    def rows(width):
        return pl.BlockSpec((nb, tl, width), lambda b, t: (b, t, 0))

    def hshape(n):
        return jax.ShapeDtypeStruct((bsz, n, seq, HEAD_DIM), BF16)

    def qshape(n):
        return jax.ShapeDtypeStruct((bsz, n, HEAD_DIM, seq), BF16)

    def tshape(n):
        if stacked:
            return jax.ShapeDtypeStruct((bsz, DEPTH, n, HEAD_DIM, seq), F32)
        return jax.ShapeDtypeStruct((bsz, n, HEAD_DIM, seq), BF16)

    def rshape(width):
        return jax.ShapeDtypeStruct((bsz, seq, width), F32)

    out_specs = [q_heads_t(A_HEADS), heads(A_KV_HEADS), q_heads_t(B_HEADS), heads(B_HEADS),
                 rows(C_KW), rows(C_KW), rows(C_WIDTH), rows(C_KW), rows(C_KW), rows(D_MIX),
                 heads_t(A_KV_HEADS), heads_t(B_HEADS)]
    out_shape = [qshape(A_HEADS), hshape(A_KV_HEADS), qshape(B_HEADS), hshape(B_HEADS),
                 rshape(C_KW), rshape(C_KW), rshape(C_WIDTH), rshape(C_KW), rshape(C_KW),
                 rshape(D_MIX), tshape(A_KV_HEADS), tshape(B_HEADS)]
    if stacked:
        out_specs += [heads_t(A_KV_HEADS), heads_t(B_HEADS)]
        out_shape += [tshape(A_KV_HEADS), tshape(B_HEADS)]
    else:
        out_specs += [pl.BlockSpec((nb, QSTAT_ROWS, tl), lambda b, t: (b, 0, t))]
        out_shape += [jax.ShapeDtypeStruct((bsz, QSTAT_ROWS, seq), F32)]
    aliases = {}
    if kv_prev is not None:
        for j, buf in enumerate(kv_prev):
            aliases[len(args)] = 10 + j
            in_specs.append(pl.BlockSpec(memory_space=pl.ANY))
            args.append(buf)
    return pl.pallas_call(
        functools.partial(_in_kernel, rope, layer, stacked_first, nb, tl, len(aliases)),
        grid=grid,
        in_specs=in_specs,
        out_specs=out_specs,
        out_shape=out_shape,
        input_output_aliases=aliases,
        compiler_params=pltpu.CompilerParams(
            dimension_semantics=("arbitrary", "arbitrary"), vmem_limit_bytes=VMEM_LIMIT),
        name="in_projection_rope" if rope else "in_projection",
    )(*args)


def _gla_bidirectional(cq_ref, ck_ref, cv_ref, gf_ref, gb_ref, s_f, s_b, seq, oc_ref):
    bl = GLA_BLOCK
    n_sub = bl // CHUNK
    nblk = seq // bl
    ri = _iota((bl, bl), 0)
    ci = _iota((bl, bl), 1)
    same_chunk = lax.shift_right_logical(ri, 6) == lax.shift_right_logical(ci, 6)
    bd = (lax.shift_right_logical(_iota((C_WIDTH, C_KW), 0), 6)
          == lax.shift_right_logical(_iota((C_WIDTH, C_KW), 1), 5))
    khead = lax.shift_right_logical(_iota((1, C_KW), 1), 5)
    vhead = lax.shift_right_logical(_iota((1, C_WIDTH), 1), 6)
    scans = []
    for reverse, g_ref in ((False, gf_ref), (True, gb_ref)):
        causal = same_chunk & ((ci >= ri) if reverse else (ci <= ri))
        scans.append((reverse, g_ref, causal, jnp.where(causal, 1.0, 0.0).astype(BF16)))
    states = [s_f, s_b]
    written = set()

    for step in range(nblk):
        rows0 = [step * bl, (nblk - 1 - step) * bl]
        cums = []
        for (reverse, g_ref, causal, tri), r0 in zip(scans, rows0):
            both = _dot(tri, jnp.concatenate(_split_bf16(g_ref[r0:r0 + bl, :]), axis=-1))
            cums.append(both[:, 0:C_KW] + both[:, C_KW:2 * C_KW])
        prep = []
        for (reverse, g_ref, causal, tri), r0, cum in zip(scans, rows0, cums):
            q = cq_ref[r0:r0 + bl, :]
            k = ck_ref[r0:r0 + bl, :]
            v = cv_ref[r0:r0 + bl, :]
            qt = q * jnp.exp(cum)
            ktb = (k * jnp.exp(-cum)).astype(BF16)
            vb = v.astype(BF16)
            lasts, kdecs = [], []
            for c in range(n_sub):
                c0 = CHUNK * c
                edge = c0 if reverse else c0 + CHUNK - 1
                last = cum[edge:edge + 1, :]
                lasts.append(last)
                kdecs.append((k[c0:c0 + CHUNK] * jnp.exp(last - cum[c0:c0 + CHUNK])).astype(BF16))
            prep.append((qt, ktb, v, vb, lasts, kdecs))
        scores, incs = [], []
        for qt, ktb, v, vb, lasts, kdecs in prep:
            scores.append([_dot_nt(jnp.where(khead == hd, qt, 0.0).astype(BF16), ktb)
                           for hd in range(C_HEADS)])
            incs.append([_dot_tn(vb[CHUNK * c:CHUNK * (c + 1)], kdecs[c]) for c in range(n_sub)])
        probs, entering = [], []
        for si, ((reverse, g_ref, causal, tri), (qt, ktb, v, vb, lasts, kdecs)) in enumerate(
                zip(scans, prep)):
            probs.append([jnp.where(causal, s, 0.0).astype(BF16) for s in scores[si]])
            s_t = states[si]
            before = [None] * n_sub
            subs = range(n_sub)
            for c in (reversed(subs) if reverse else subs):
                before[c] = s_t.astype(BF16)
                s_t = jnp.exp(lasts[c]) * s_t + jnp.where(bd, incs[si][c], 0.0)
            states[si] = s_t
            entering.append(before)
        outs = []
        for si, (qt, ktb, v, vb, lasts, kdecs) in enumerate(prep):
            qtb = qt.astype(BF16)
            o = jnp.concatenate([_dot_nt(qtb[CHUNK * c:CHUNK * (c + 1)], entering[si][c])
                                 for c in range(n_sub)], axis=0)
            for hd in range(C_HEADS):
                o = o + _dot(probs[si][hd], jnp.where(vhead == hd, v, 0.0).astype(BF16))
            outs.append(o)
        if rows0[0] == rows0[1]:
            outs, rows0 = [outs[0] + outs[1]], rows0[:1]
        for o, r0 in zip(outs, rows0):
            if r0 in written:
                oc_ref[r0:r0 + bl, :] = oc_ref[r0:r0 + bl, :] + o
            else:
                oc_ref[r0:r0 + bl, :] = o
                written.add(r0)
    return states[0], states[1]


def _attend_t(jobs, shifts=None):
    def scores(i):
        return _dot(jobs[i][0][...], jobs[i][1])

    outs = []
    pending = [scores(i) for i in range(min(SCORE_LOOKAHEAD, len(jobs)))]
    for i, (_, _, vt) in enumerate(jobs):
        st = pending.pop(0)
        if i + SCORE_LOOKAHEAD < len(jobs):
            pending.append(scores(i + SCORE_LOOKAHEAD))
        m = jnp.max(st, axis=0, keepdims=True) if shifts is None else shifts[i]
        p = jnp.exp2(st - m).astype(BF16)
        ot = _dot(vt[...], p)
        outs.append(ot[0:HEAD_DIM] * (1.0 / ot[HEAD_DIM:HEAD_DIM + 1]))
    return outs


def _pair_rows(a, b):
    return jnp.concatenate([a, b], axis=0).T


def _state_to_blockdiag_t(s_ref):
    rows = []
    for hd in range(C_HEADS):
        pieces = []
        if hd:
            pieces.append(jnp.zeros((C_DK, C_DV * hd), F32))
        pieces.append(s_ref[hd])
        if hd < C_HEADS - 1:
            pieces.append(jnp.zeros((C_DK, C_DV * (C_HEADS - 1 - hd)), F32))
        rows.append(jnp.concatenate(pieces, axis=-1))
    return jnp.concatenate(rows, axis=0).T


def _blockdiag_t_to_state(s_t, out_ref):
    s = s_t.T
    for hd in range(C_HEADS):
        out_ref[hd] = s[C_DK * hd:C_DK * (hd + 1), C_DV * hd:C_DV * (hd + 1)]


def _mix_sequence(cached, layer, first_state, n_aliased, lam_init, seq, qt, qb, *refs):
    it = iter(refs)
    x_ref, mod_ref = next(it), next(it)
    qa_ref, ka_ref, qb_ref, kb_ref = (next(it) for _ in range(4))
    cq_ref, ck_ref, cv_ref, gf_ref, gb_ref, su_ref = (next(it) for _ in range(6))
    vat_ref, vbt_ref = next(it), next(it)
    if cached:
        cakt_ref, cavt_ref, cbkt_ref, cbvt_ref, s0f_ref, s0b_ref = (next(it) for _ in range(6))
        qstat_ref, aqg_ref = next(it), next(it)
    wout_ref, gpost_ref, bog_ref, cog_ref, lamp_ref = (next(it) for _ in range(5))
    for _ in range(n_aliased):
        next(it)
    y_ref = next(it)
    if not cached:
        sf_ref, sb_ref = next(it), next(it)
    kA_s, vtA_s, kB_s, vtB_s, oc_s, mixed_s, kn2_s = (next(it) for _ in range(7))

    lk = kA_s.shape[1]
    past = lk - seq
    t = pl.program_id(1)

    def once_per_sequence(body):
        return body() if seq == qt else pl.when(t == 0)(body)

    @once_per_sequence
    def _per_sequence():
        ones_row = jnp.where(_iota((VT_ROWS - HEAD_DIM, lk), 0) == 0, 1.0, 0.0).astype(BF16)
        for k_new, kt_cache, k_dst, vt_new, vt_cache, vt_dst in (
                (ka_ref, cakt_ref if cached else None, kA_s, vat_ref, cavt_ref if cached else None, vtA_s),
                (kb_ref, cbkt_ref if cached else None, kB_s, vbt_ref, cbvt_ref if cached else None, vtB_s)):
            n_heads = k_dst.shape[0]
            if cached:
                for h0 in range(0, n_heads, 2):
                    pair = jnp.concatenate([kt_cache[h0], kt_cache[h0 + 1]], axis=0).T
                    k_dst[h0, 0:past, :] = pair[:, 0:HEAD_DIM].astype(BF16)
                    k_dst[h0 + 1, 0:past, :] = pair[:, HEAD_DIM:2 * HEAD_DIM].astype(BF16)
            for hd in range(n_heads):
                k_dst[hd, past:lk, :] = k_new[hd]
                if cached:
                    vt_dst[hd, 0:HEAD_DIM, 0:past] = vt_cache[hd].astype(BF16)
                vt_dst[hd, 0:HEAD_DIM, past:lk] = vt_new[hd].astype(BF16)
                vt_dst[hd, HEAD_DIM:VT_ROWS, :] = ones_row
        bounded_jobs = ([(kA_s, g) for g in range(A_KV_HEADS)] + [(kB_s, h) for h in range(B_HEADS)]
                        if lk > KEY_TILE else [])
        for j, (k_dst, hd) in enumerate(bounded_jobs):
            kf = k_dst[hd].astype(F32)
            kn2 = jnp.max(jnp.sum(kf * kf, axis=-1, keepdims=True), axis=0, keepdims=True)
            kn2_s[j:j + 1, :] = jnp.broadcast_to(kn2, (1, LANES))

        if cached:
            s0f, s0b = _state_to_blockdiag_t(s0f_ref), _state_to_blockdiag_t(s0b_ref)
        else:
            s0f = jnp.zeros((C_WIDTH, C_KW), F32)
            s0b = s0f
        s_f, s_b = _gla_bidirectional(cq_ref, ck_ref, cv_ref, gf_ref, gb_ref, s0f, s0b, seq, oc_s)
        if not cached:
            for ref, s_t in ((sf_ref, s_f), (sb_ref, s_b)):
                if first_state:
                    for l2 in range(DEPTH):
                        if l2 == layer:
                            _blockdiag_t_to_state(s_t, ref.at[l2])
                        else:
                            ref[l2] = jnp.zeros(ref.shape[1:], F32)
                else:
                    _blockdiag_t_to_state(s_t, ref)
        for r0 in range(0, seq, GLA_BLOCK):
            oc = oc_s[r0:r0 + GLA_BLOCK, :]
            oc_s[r0:r0 + GLA_BLOCK, :] = oc * lax.rsqrt(_group_mean_sq(oc, 6) + EPS) * cog_ref[...]

    lam = (jnp.exp(jnp.sum(lamp_ref[0:1, :] * lamp_ref[1:2, :], axis=-1, keepdims=True))
           - jnp.exp(jnp.sum(lamp_ref[2:3, :] * lamp_ref[3:4, :], axis=-1, keepdims=True))
           + lam_init)

    def project_out():
        seq_rows = pl.ds(pl.multiple_of(t * qt, qt), qt)
        mixed_s[:, A_WIDTH + B_WIDTH:D_MIX] = oc_s[seq_rows, :]
        gate = mod_ref[:, 2 * D_MODEL:3 * D_MODEL]
        mixed = (mixed_s[...] * su_ref[...]).astype(BF16)
        y = _dot(mixed, wout_ref[...])
        yn = y * lax.rsqrt(jnp.mean(y * y, axis=-1, keepdims=True) + EPS) * gpost_ref[...]
        y_ref[...] = x_ref[...] + gate * yn

    def attn_block(i, carry):
        rows = pl.ds(i * qb, qb) if isinstance(i, int) else pl.ds(pl.multiple_of(i * qb, qb), qb)
        jobs = []
        for grp in range(A_KV_HEADS):
            q4 = jnp.concatenate([qa_ref[A_GROUP * grp + j, :, rows] for j in range(A_GROUP)], axis=-1)
            jobs.append((kA_s.at[grp], q4, vtA_s.at[grp]))
        dim = _iota((HEAD_DIM, qb), 0)
        for hd in range(B_HEADS):
            q = qb_ref[hd, :, rows]
            zero = jnp.zeros_like(q)
            q2 = jnp.concatenate([jnp.where(dim < B_QK_DIM, q, zero),
                                  jnp.where(dim >= B_QK_DIM, q, zero)], axis=-1)
            jobs.append((kB_s.at[hd], q2, vtB_s.at[hd]))

        def finish(outs):
            for grp in range(A_KV_HEADS):
                ot = outs[grp]
                for pair in range(A_GROUP // 2):
                    c0 = 2 * pair * qb
                    col = A_GROUP * HEAD_DIM * grp + 2 * HEAD_DIM * pair
                    mixed_s[rows, col:col + 2 * HEAD_DIM] = _pair_rows(ot[:, c0:c0 + qb],
                                                                       ot[:, c0 + qb:c0 + 2 * qb])
            obs = []
            for hd in range(B_HEADS):
                ot = outs[A_KV_HEADS + hd]
                ob = ot[:, 0:qb] - lam * ot[:, qb:2 * qb]
                obs.append(ob * lax.rsqrt(jnp.mean(ob * ob, axis=0, keepdims=True) + EPS))
            for pair in range(B_HEADS // 2):
                col = A_WIDTH + 2 * B_V_DIM * pair
                mixed_s[rows, col:col + 2 * B_V_DIM] = (_pair_rows(obs[2 * pair], obs[2 * pair + 1])
                                                        * bog_ref[...] * (1.0 - lam_init))
            if qt == qb:
                project_out()

        if lk <= KEY_TILE:
            finish(_attend_t(jobs))
            return carry

        qa_norm = (jnp.max(jnp.abs(aqg_ref[...]), axis=-1, keepdims=True)
                   * (HEAD_DIM ** 0.5 * HEAD_DIM ** -0.5 * LOG2E))
        stat_t = qstat_ref[:, rows]
        shifts, gaps = [], []
        for j, (_, q, _) in enumerate(jobs):
            k_norm = jnp.sqrt(kn2_s[j:j + 1, 0:1])
            if j < A_KV_HEADS:
                upper = qa_norm * k_norm * BOUND_SLACK
                shifts.append(jnp.broadcast_to(upper, (1, q.shape[1])))
                gaps.append(2.0 * upper)
            else:
                r0 = 2 * (j - A_KV_HEADS)
                r1 = r0 + 2 * B_HEADS
                qn2 = jnp.concatenate([stat_t[r0:r0 + 1], stat_t[r0 + 1:r0 + 2]], axis=-1)
                lower = jnp.concatenate([stat_t[r1:r1 + 1], stat_t[r1 + 1:r1 + 2]], axis=-1)
                upper = jnp.sqrt(qn2) * k_norm * BOUND_SLACK
                shifts.append(upper)
                gaps.append(jnp.max(upper - lower, axis=-1, keepdims=True))
        worst = functools.reduce(jnp.maximum, gaps)
        safe = worst[0, 0] <= SAFE_GAP

        @pl.when(safe)
        def _bounded():
            finish(_attend_t(jobs, shifts))

        @pl.when(jnp.logical_not(safe))
        def _exact_max():
            finish(_attend_t(jobs))

        return carry

    if qt != qb:
        for i in range(qt // qb):
            attn_block(i, 0)
        project_out()
    else:
        attn_block(0, 0)


def _mix_kernel(nseq, *params_and_refs):
    params, refs = params_and_refs[:N_MIX_PARAMS], params_and_refs[N_MIX_PARAMS:]
    if nseq == 1:
        return _mix_sequence(*params, *refs)
    n_aliased = params[3]
    shared = {1} | set(range(N_MIX_SEQ_INPUTS, N_MIX_SEQ_INPUTS + N_MIX_WEIGHTS + n_aliased))
    for bi in range(nseq):
        _mix_sequence(*params, *[r if idx in shared else r.at[bi] for idx, r in enumerate(refs)])


def _mixer(x, mod, per_batch_mod, layer, proj, wts, lam_init, cache, state_prev=None):
    bsz, seq, _ = x.shape
    cached = cache is not None
    past = cache[0].shape[4] if cached else 0
    qa, ka, qb, kb, cq, ck, cv, gf, gb, su, vat, vbt = proj[:12]
    qt = 512 if cached else 256
    qb_rows = 256
    nseq = 1 if cached else 4
    lead = None if nseq == 1 else nseq

    def per_layer(shape):
        return pl.BlockSpec((None,) + shape, lambda b, t: (layer,) + (0,) * len(shape))

    def layer_heads_t(n, length):
        return pl.BlockSpec((lead, None, n, HEAD_DIM, length), lambda b, t: (b, layer, 0, 0, 0))

    def heads_t(n):
        if cached:
            return pl.BlockSpec((lead, n, HEAD_DIM, seq), lambda b, t: (b, 0, 0, 0))
        return layer_heads_t(n, seq)

    def heads(n):
        return pl.BlockSpec((lead, n, seq, HEAD_DIM), lambda b, t: (b, 0, 0, 0))

    def head_tile(n):
        return pl.BlockSpec((lead, n, HEAD_DIM, qt), lambda b, t: (b, 0, 0, t))

    def rows(width):
        return pl.BlockSpec((lead, seq, width), lambda b, t: (b, 0, 0))

    def row_tile(width):
        return pl.BlockSpec((lead, qt, width), lambda b, t: (b, t, 0))

    mod_idx = (lambda b, t: (layer, b, 0, 0)) if per_batch_mod else (lambda b, t: (layer, 0, 0, 0))
    in_specs = [row_tile(D_MODEL), pl.BlockSpec((None, None, 1, 3 * D_MODEL), mod_idx),
                head_tile(A_HEADS), heads(A_KV_HEADS), head_tile(B_HEADS), heads(B_HEADS),
                rows(C_KW), rows(C_KW), rows(C_WIDTH), rows(C_KW), rows(C_KW), row_tile(D_MIX),
                heads_t(A_KV_HEADS), heads_t(B_HEADS)]
    args = [x, mod, qa, ka, qb, kb, cq, ck, cv, gf, gb, su, vat, vbt]
    if cached:
        state_in = pl.BlockSpec((None, None, C_HEADS, C_DK, C_DV), lambda b, t: (b, layer, 0, 0, 0))
        in_specs += [layer_heads_t(A_KV_HEADS, past), layer_heads_t(A_KV_HEADS, past),
                     layer_heads_t(B_HEADS, past), layer_heads_t(B_HEADS, past), state_in, state_in,
                     pl.BlockSpec((None, QSTAT_ROWS, qt), lambda b, t: (b, 0, t)),
                     per_layer((1, HEAD_DIM))]
        args += list(cache) + [proj[12], wts["aq_gain"]]
    in_specs += [per_layer((D_MIX, D_MODEL)), per_layer((1, D_MODEL)), per_layer((1, 2 * B_V_DIM)),
                 per_layer((1, C_WIDTH)), per_layer((4, B_QK_DIM))]
    args += [wts["w_out"], wts["g_post"], wts["b_out_gain"], wts["c_out_gain"], wts["lam_params"]]

    out_specs = [row_tile(D_MODEL)]
    out_shape = [jax.ShapeDtypeStruct((bsz, seq, D_MODEL), F32)]
    aliases = {}
    first_state = not cached and state_prev is None
    if not cached:
        if first_state:
            state_out = pl.BlockSpec((lead, DEPTH, C_HEADS, C_DK, C_DV), lambda b, t: (b, 0, 0, 0, 0))
        else:
            state_out = pl.BlockSpec((lead, None, C_HEADS, C_DK, C_DV),
                                     lambda b, t: (b, layer, 0, 0, 0))
            for j, buf in enumerate(state_prev):
                aliases[len(args)] = 1 + j
                in_specs.append(pl.BlockSpec(memory_space=pl.ANY))
                args.append(buf)
        out_specs += [state_out, state_out]
        out_shape += [jax.ShapeDtypeStruct((bsz, DEPTH, C_HEADS, C_DK, C_DV), F32)] * 2

    lk = past + seq
    per_seq = () if nseq == 1 else (nseq,)
    scratch = [pltpu.VMEM(per_seq + shape, dtype) for shape, dtype in (
        ((A_KV_HEADS, lk, HEAD_DIM), BF16), ((A_KV_HEADS, VT_ROWS, lk), BF16),
        ((B_HEADS, lk, HEAD_DIM), BF16), ((B_HEADS, VT_ROWS, lk), BF16),
        ((seq, C_WIDTH), F32), ((qt, D_MIX), F32), ((8, LANES), F32))]
    return pl.pallas_call(
        functools.partial(_mix_kernel, nseq, cached, layer, first_state, len(aliases), lam_init, seq,
                          qt, qb_rows),
        grid=(bsz // nseq, seq // qt),
        in_specs=in_specs,
        out_specs=out_specs,
        out_shape=out_shape,
        scratch_shapes=scratch,
        input_output_aliases=aliases,
        compiler_params=pltpu.CompilerParams(
            dimension_semantics=("arbitrary", "arbitrary"), vmem_limit_bytes=VMEM_LIMIT),
        name="mixer_cached" if cached else "mixer",
    )(*args)


def _rope_tables(seq):
    t = jnp.arange(seq)
    pos_row = (t // GRID_W).astype(F32)
    pos_col = (t % GRID_W).astype(F32)

    def tables(half):
        freq = ROPE_THETA ** (-jnp.arange(half, dtype=F32) / half)
        ang_r = freq[:, None] * pos_row[None, :]
        ang_c = freq[:, None] * pos_col[None, :]
        cos = jnp.concatenate([jnp.cos(ang_r), jnp.cos(ang_r), jnp.cos(ang_c), jnp.cos(ang_c)], axis=0)
        sin = jnp.concatenate([-jnp.sin(ang_r), jnp.sin(ang_r), -jnp.sin(ang_c), jnp.sin(ang_c)], axis=0)
        reps = HEAD_DIM // (4 * half)
        return jnp.tile(cos, (reps, 1)), jnp.tile(sin, (reps, 1))

    cos_a, sin_a = tables(HEAD_DIM // 4)
    cos_b, sin_b = tables(B_QK_DIM // 4)
    return cos_a, sin_a, cos_b, sin_b


def _prepare_weights(g_pre, g_post, w_in, w_out, a_q_gain, a_k_gain, b_lambda_q1, b_lambda_k1,
                     b_lambda_q2, b_lambda_k2, b_out_gain, c_gate_w_fwd, c_gate_b_fwd, c_gate_w_bwd,
                     c_gate_b_bwd, c_out_gain):
    w_in_t = jnp.swapaxes(w_in, 1, 2).astype(BF16)
    pad = jnp.zeros((DEPTH, GATE_RANK, C_KW), F32)
    cw_f = jnp.concatenate([c_gate_w_fwd, pad], axis=1).astype(BF16)
    cw_b = jnp.concatenate([pad, c_gate_w_bwd], axis=1).astype(BF16)
    return {
        "g_pre": g_pre[:, None, :],
        "g_post": g_post[:, None, :],
        "w_in_t": w_in_t,
        "w_out": w_out.astype(BF16),
        "aq_gain": a_q_gain[:, None, :],
        "aq_gain_col": a_q_gain[:, :, None],
        "ak_gain_col": a_k_gain[:, :, None],
        "cw_f": cw_f,
        "cb_f": c_gate_b_fwd[:, None, :],
        "cw_b": cw_b,
        "cb_b": c_gate_b_bwd[:, None, :],
        "b_out_gain": jnp.tile(b_out_gain, (1, 2))[:, None, :],
        "c_out_gain": jnp.tile(c_out_gain, (1, C_HEADS))[:, None, :],
        "lam_params": jnp.stack([b_lambda_q1, b_lambda_k1, b_lambda_q2, b_lambda_k2], axis=1),
    }


def kernel(x_prompt, x_sample, c, cache_a_k, cache_a_v, cache_b_k, cache_b_v, state_c_fwd, state_c_bwd, c_ctx, w_mod, b_mod, g_pre, g_post, w_in, w_out, a_q_gain, a_k_gain, b_lambda_q1, b_lambda_k1, b_lambda_q2, b_lambda_k2, b_out_gain, c_gate_w_fwd, c_gate_b_fwd, c_gate_w_bwd, c_gate_b_bwd, c_out_gain):
    dec_batch = x_sample.shape[0]
    dec_seq = x_sample.shape[1]

    mod_rows = 16
    cvec = jnp.zeros((mod_rows, D_MODEL), F32).at[0:dec_batch].set(c).at[dec_batch].set(c_ctx)
    mod = _modulation(cvec, w_mod, b_mod)[:, :, None, :]
    mod_lat = mod[:, 0:dec_batch]
    mod_ctx = mod[:, dec_batch:dec_batch + 1]

    wts = _prepare_weights(g_pre, g_post, w_in, w_out, a_q_gain, a_k_gain, b_lambda_q1, b_lambda_k1,
                           b_lambda_q2, b_lambda_k2, b_out_gain, c_gate_w_fwd, c_gate_b_fwd,
                           c_gate_w_bwd, c_gate_b_bwd, c_out_gain)
    rope_tabs = _rope_tables(dec_seq)
    cache = tuple(jnp.swapaxes(a, -1, -2) for a in (cache_a_k, cache_a_v, cache_b_k, cache_b_v))
    cache += (state_c_fwd, state_c_bwd)

    y_p, y_s = x_prompt, x_sample
    kv_ctx = None
    states = None
    for l in range(DEPTH):
        lam_init = 0.8 - 0.6 * math.exp(-0.3 * l)
        proj_p = _in_projection(y_p, mod_ctx, False, l, wts, None, 4, x_prompt.shape[1], kv_ctx)
        kv_ctx = proj_p[10:14]
        y_p, *states = _mixer(y_p, mod_ctx, False, l, proj_p, wts, lam_init, None, states)

        proj_s = _in_projection(y_s, mod_lat, True, l, wts, rope_tabs, 1, 1024)
        (y_s,) = _mixer(y_s, mod_lat, True, l, proj_s, wts, lam_init, cache)

    va_t, vb_t, ka_t, kb_t = kv_ctx
    new_kv = [jnp.swapaxes(a, -1, -2) for a in (ka_t, va_t, kb_t, vb_t)]
    return (y_p, y_s, *new_kv, *states)
```

```python
import functools
import math

import jax
import jax.numpy as jnp
from jax import lax
from jax.experimental import pallas as pl
from jax.experimental.pallas import tpu as pltpu

F32 = jnp.float32
BF16 = jnp.bfloat16

D_MODEL = 1024
DEPTH = 2
GRID_W = 64
HEAD_DIM = 64
A_HEADS = 8
A_KV_HEADS = 2
A_GROUP = A_HEADS // A_KV_HEADS
A_WIDTH = A_HEADS * HEAD_DIM
A_KV_WIDTH = A_KV_HEADS * HEAD_DIM
B_HEADS = 4
B_QK_DIM = 32
B_V_DIM = 64
B_WIDTH = B_HEADS * B_V_DIM
C_HEADS = 4
C_DK = 32
C_DV = 64
C_KW = C_HEADS * C_DK
C_WIDTH = C_HEADS * C_DV
GATE_RANK = 16
GLA_TAU = 16.0
CHUNK = 64
D_MIX = A_WIDTH + B_WIDTH + C_WIDTH
ROPE_THETA = 10000.0
EPS = 1e-6

LANES = 128
GLA_BLOCK = 256
VT_ROWS = HEAD_DIM + 16
SCORE_LOOKAHEAD = 6
SAFE_GAP = 96.0
BOUND_SLACK = 1.02
QSTAT_ROWS = 4 * B_HEADS
U_TILE = D_MIX // 4
N_MIX_PARAMS = 8
N_MIX_SEQ_INPUTS = 14
N_MIX_WEIGHTS = 5
KEY_TILE = 256
LOG2E = math.log2(math.e)

OFF_AQ = 0
OFF_AK = OFF_AQ + A_WIDTH
OFF_AV = OFF_AK + A_KV_WIDTH
OFF_BQ = OFF_AV + A_KV_WIDTH
OFF_BK = OFF_BQ + B_WIDTH
OFF_BV = OFF_BK + B_WIDTH
OFF_CQ = OFF_BV + B_WIDTH
OFF_CK = OFF_CQ + C_KW
OFF_CV = OFF_CK + C_KW
OFF_LR = OFF_CV + C_WIDTH
OFF_U = OFF_LR + 2 * GATE_RANK
IN_WIDTH = OFF_U + D_MIX

VMEM_LIMIT = 60 * 1024 * 1024


def _dot(a, b):
    return jnp.dot(a, b, preferred_element_type=F32)


def _dot_nt(a, b):
    return lax.dot_general(a, b, (((1,), (1,)), ((), ())), preferred_element_type=F32)


def _dot_tn(a, b):
    return lax.dot_general(a, b, (((0,), (0,)), ((), ())), preferred_element_type=F32)


def _split_bf16(x):
    hi = x.astype(BF16)
    lo = (x - hi.astype(F32)).astype(BF16)
    return hi, lo


def _iota(shape, dim):
    return lax.broadcasted_iota(jnp.int32, shape, dim)


def _group_mean_sq(x, group_log2):
    width = x.shape[-1]
    r = lax.shift_right_logical(_iota((width, width), 0), group_log2)
    c = lax.shift_right_logical(_iota((width, width), 1), group_log2)
    ones = jnp.where(r == c, 1.0, 0.0).astype(BF16)
    return _dot((x * x).astype(BF16), ones) * (1.0 / (1 << group_log2))


def _log_sigmoid(x):
    return jnp.minimum(x, 0.0) - jnp.log1p(jnp.exp(-jnp.abs(x)))


def _silu(x):
    return x * (1.0 / (1.0 + jnp.exp(-x)))


def _mod_kernel(c_ref, w_ref, b_ref, o_ref):
    a = _silu(c_ref[...]).astype(BF16)
    o_ref[...] = _dot(a, w_ref[...].astype(BF16)) + b_ref[...]


def _modulation(cvec, w_mod, b_mod):
    rows = cvec.shape[0]
    nblk = 3
    return pl.pallas_call(
        _mod_kernel,
        grid=(DEPTH, nblk),
        in_specs=[
            pl.BlockSpec((rows, D_MODEL), lambda l, n: (0, 0)),
            pl.BlockSpec((None, D_MODEL, D_MODEL), lambda l, n: (l, 0, n)),
            pl.BlockSpec((None, 1, D_MODEL), lambda l, n: (l, 0, n)),
        ],
        out_specs=pl.BlockSpec((None, rows, D_MODEL), lambda l, n: (l, 0, n)),
        out_shape=jax.ShapeDtypeStruct((DEPTH, rows, 3 * D_MODEL), F32),
        compiler_params=pltpu.CompilerParams(
            dimension_semantics=("arbitrary", "arbitrary"), vmem_limit_bytes=VMEM_LIMIT),
        name="modulation",
    )(cvec, w_mod, b_mod.reshape(DEPTH, 1, 3 * D_MODEL))


def _in_kernel(rope, layer, stacked_first, nb, tl, n_aliased, *refs):
    (x_ref, mod_ref, gpre_ref, wt_ref, aqg_ref, akg_ref, cwf_ref, cbf_ref, cwb_ref, cbb_ref) = refs[:10]
    refs = refs[10:]
    if rope:
        cosa_ref, sina_ref, cosb_ref, sinb_ref = refs[:4]
        refs = refs[4:]
    refs = refs[n_aliased:]
    (qa_ref, ka_ref, qb_ref, kb_ref, cq_ref, ck_ref, cv_ref, gf_ref, gb_ref, su_ref,
     vat_ref, vbt_ref) = refs[:12]
    kat_ref, kbt_ref = (None, None) if rope else refs[12:]
    qstat_ref = refs[12] if rope else None

    x = x_ref[...].reshape(nb * tl, D_MODEL)
    shift = mod_ref[:, 0:D_MODEL]
    scale = mod_ref[:, D_MODEL:2 * D_MODEL]
    ms = jnp.mean(x * x, axis=-1, keepdims=True)
    h = (x * lax.rsqrt(ms + EPS)) * gpre_ref[...] * (1.0 + scale) + shift
    hb = h.astype(BF16)

    def proj(off, width):
        return _dot_nt(hb, wt_ref[off:off + width, :])

    def proj_t(off, width):
        return _dot_nt(wt_ref[off:off + width, :], hb)

    def put_rows(ref, val):
        for bi in range(nb):
            ref[bi] = val[bi * tl:(bi + 1) * tl].astype(ref.dtype)

    def put_heads(ref, val, n_heads):
        for bi in range(nb):
            for hd in range(n_heads):
                ref[bi, hd] = val[bi * tl:(bi + 1) * tl,
                                  HEAD_DIM * hd:HEAD_DIM * (hd + 1)].astype(ref.dtype)

    def put_heads_t(ref, heads_t, stacked=False):
        for bi in range(nb):
            for hd, val_t in enumerate(heads_t):
                blk = val_t[:, bi * tl:(bi + 1) * tl].astype(ref.dtype)
                if stacked and stacked_first:
                    for l2 in range(DEPTH):
                        ref[bi, l2, hd] = blk if l2 == layer else jnp.zeros_like(blk)
                else:
                    ref[bi, hd] = blk

    def split_heads_t(val_t, n_heads):
        return [val_t[HEAD_DIM * hd:HEAD_DIM * (hd + 1)] for hd in range(n_heads)]

    def rms_t(head_t, gain_col):
        ms_h = jnp.mean(head_t * head_t, axis=0, keepdims=True)
        return head_t * lax.rsqrt(ms_h + EPS) * gain_col

    def rope_t(head_t, cos_t, sin_t, dist):
        blocks = [head_t[r0:r0 + dist] for r0 in range(0, HEAD_DIM, dist)]
        partner = jnp.concatenate([blocks[j ^ 1] for j in range(len(blocks))], axis=0)
        return head_t * cos_t + partner * sin_t

    def gate_tile(j):
        c0 = U_TILE * j
        val = _silu(proj(OFF_U + c0, U_TILE))
        for bi in range(nb):
            su_ref[bi, :, c0:c0 + U_TILE] = val[bi * tl:(bi + 1) * tl]

    za_t = proj_t(OFF_AQ, A_WIDTH + 2 * A_KV_WIDTH)
    zb_lr_t = _dot_nt(jnp.concatenate([wt_ref[OFF_BQ:OFF_BQ + 3 * B_WIDTH, :],
                                       wt_ref[OFF_LR:OFF_LR + 2 * GATE_RANK, :]], axis=0), hb)
    zb_t = zb_lr_t[0:3 * B_WIDTH]
    lr_t = zb_lr_t[3 * B_WIDTH:].astype(BF16)
    gate_tile(0)

    aq_t = [rms_t(h_t, aqg_ref[...]) for h_t in split_heads_t(za_t[0:A_WIDTH], A_HEADS)]
    ak_t = [rms_t(h_t, akg_ref[...])
            for h_t in split_heads_t(za_t[A_WIDTH:A_WIDTH + A_KV_WIDTH], A_KV_HEADS)]
    if rope:
        aq_t = [rope_t(h_t, cosa_ref[...], sina_ref[...], 16) for h_t in aq_t]
        ak_t = [rope_t(h_t, cosa_ref[...], sina_ref[...], 16) for h_t in ak_t]
    put_heads_t(qa_ref, [h_t * (HEAD_DIM ** -0.5 * LOG2E) for h_t in aq_t])
    put_heads(ka_ref, jnp.concatenate(ak_t, axis=0).T, A_KV_HEADS)
    put_heads_t(vat_ref, split_heads_t(za_t[A_WIDTH + A_KV_WIDTH:], A_KV_HEADS), stacked=not rope)
    if kat_ref is not None:
        put_heads_t(kat_ref, ak_t, stacked=True)
    gate_tile(1)
    cqk = proj(OFF_CQ, 2 * C_KW)
    cv = proj(OFF_CV, C_WIDTH)

    bq_t = split_heads_t(zb_t[0:B_WIDTH], B_HEADS)
    bk_t = split_heads_t(zb_t[B_WIDTH:2 * B_WIDTH], B_HEADS)
    if rope:
        bq_t = [rope_t(h_t, cosb_ref[...], sinb_ref[...], 8) for h_t in bq_t]
        bk_t = [rope_t(h_t, cosb_ref[...], sinb_ref[...], 8) for h_t in bk_t]
    bq_t = [h_t * (B_QK_DIM ** -0.5 * LOG2E) for h_t in bq_t]
    put_heads_t(qb_ref, bq_t)
    put_heads(kb_ref, jnp.concatenate(bk_t, axis=0).T, B_HEADS)
    if rope:
        halves = [(hd, r0) for hd in range(B_HEADS) for r0 in (0, B_QK_DIM)]
        qstat_ref[0] = jnp.concatenate(
            [jnp.sum(bq_t[hd][r0:r0 + B_QK_DIM] * bq_t[hd][r0:r0 + B_QK_DIM], axis=0, keepdims=True)
             for hd, r0 in halves]
            + [jnp.sum(bq_t[hd][r0:r0 + B_QK_DIM] * bk_t[hd][r0:r0 + B_QK_DIM], axis=0, keepdims=True)
               for hd, r0 in halves], axis=0)
    put_heads_t(vbt_ref, split_heads_t(zb_t[2 * B_WIDTH:], B_HEADS), stacked=not rope)
    if kbt_ref is not None:
        put_heads_t(kbt_ref, bk_t, stacked=True)
    gate_tile(2)
    g_pre = _dot_tn(lr_t, jnp.concatenate([cwf_ref[...], cwb_ref[...]], axis=-1))
    gf_pre, gb_pre = g_pre[:, 0:C_KW], g_pre[:, C_KW:2 * C_KW]

    put_rows(cq_ref, cqk[:, 0:C_KW] * (C_DK ** -0.5))
    put_rows(ck_ref, cqk[:, C_KW:2 * C_KW])
    put_rows(cv_ref, cv)
    put_rows(gf_ref, _log_sigmoid(gf_pre + cbf_ref[...]) * (1.0 / GLA_TAU))
    put_rows(gb_ref, _log_sigmoid(gb_pre + cbb_ref[...]) * (1.0 / GLA_TAU))
    gate_tile(3)


def _in_projection(x, mod, per_batch_mod, layer, wts, rope_tabs, nb, tl, kv_prev=None):
    bsz, seq, _ = x.shape
    rope = rope_tabs is not None
    stacked = not rope
    stacked_first = stacked and kv_prev is None
    grid = (bsz // nb, seq // tl)

    def per_layer(shape):
        return pl.BlockSpec((None,) + shape, lambda b, t: (layer,) + (0,) * len(shape))

    mod_idx = (lambda b, t: (layer, b, 0, 0)) if per_batch_mod else (lambda b, t: (layer, 0, 0, 0))
    in_specs = [
        pl.BlockSpec((nb, tl, D_MODEL), lambda b, t: (b, t, 0)),
        pl.BlockSpec((None, None, 1, 3 * D_MODEL), mod_idx),
        per_layer((1, D_MODEL)),
        per_layer((IN_WIDTH, D_MODEL)),
        per_layer((HEAD_DIM, 1)),
        per_layer((HEAD_DIM, 1)),
        per_layer((2 * GATE_RANK, C_KW)),
        per_layer((1, C_KW)),
        per_layer((2 * GATE_RANK, C_KW)),
        per_layer((1, C_KW)),
    ]
    args = [x, mod, wts["g_pre"], wts["w_in_t"], wts["aq_gain_col"], wts["ak_gain_col"],
            wts["cw_f"], wts["cb_f"], wts["cw_b"], wts["cb_b"]]
    if rope:
        assert nb == 1
        in_specs += [
            pl.BlockSpec((HEAD_DIM, tl), lambda b, t: (0, t)),
            pl.BlockSpec((HEAD_DIM, tl), lambda b, t: (0, t)),
            pl.BlockSpec((HEAD_DIM, tl), lambda b, t: (0, t)),
            pl.BlockSpec((HEAD_DIM, tl), lambda b, t: (0, t)),
        ]
        args += list(rope_tabs)

    def heads(n):
        return pl.BlockSpec((nb, n, tl, HEAD_DIM), lambda b, t: (b, 0, t, 0))

    def heads_t(n):
        if stacked_first:
            return pl.BlockSpec((nb, DEPTH, n, HEAD_DIM, tl), lambda b, t: (b, 0, 0, 0, t))
        if stacked:
            return pl.BlockSpec((nb, None, n, HEAD_DIM, tl), lambda b, t: (b, layer, 0, 0, t))
        return pl.BlockSpec((nb, n, HEAD_DIM, tl), lambda b, t: (b, 0, 0, t))

    def q_heads_t(n):
        return pl.BlockSpec((nb, n, HEAD_DIM, tl), lambda b, t: (b, 0, 0, t))

    def rows(width):
        return pl.BlockSpec((nb, tl, width), lambda b, t: (b, t, 0))

    def hshape(n):
        return jax.ShapeDtypeStruct((bsz, n, seq, HEAD_DIM), BF16)

    def qshape(n):
        return jax.ShapeDtypeStruct((bsz, n, HEAD_DIM, seq), BF16)

    def tshape(n):
        if stacked:
            return jax.ShapeDtypeStruct((bsz, DEPTH, n, HEAD_DIM, seq), F32)
        return jax.ShapeDtypeStruct((bsz, n, HEAD_DIM, seq), BF16)

    def rshape(width):
        return jax.ShapeDtypeStruct((bsz, seq, width), F32)

    out_specs = [q_heads_t(A_HEADS), heads(A_KV_HEADS), q_heads_t(B_HEADS), heads(B_HEADS),
                 rows(C_KW), rows(C_KW), rows(C_WIDTH), rows(C_KW), rows(C_KW), rows(D_MIX),
                 heads_t(A_KV_HEADS), heads_t(B_HEADS)]
    out_shape = [qshape(A_HEADS), hshape(A_KV_HEADS), qshape(B_HEADS), hshape(B_HEADS),
                 rshape(C_KW), rshape(C_KW), rshape(C_WIDTH), rshape(C_KW), rshape(C_KW),
                 rshape(D_MIX), tshape(A_KV_HEADS), tshape(B_HEADS)]
    if stacked:
        out_specs += [heads_t(A_KV_HEADS), heads_t(B_HEADS)]
        out_shape += [tshape(A_KV_HEADS), tshape(B_HEADS)]
    else:
        out_specs += [pl.BlockSpec((nb, QSTAT_ROWS, tl), lambda b, t: (b, 0, t))]
        out_shape += [jax.ShapeDtypeStruct((bsz, QSTAT_ROWS, seq), F32)]
    aliases = {}
    if kv_prev is not None:
        for j, buf in enumerate(kv_prev):
            aliases[len(args)] = 10 + j
            in_specs.append(pl.BlockSpec(memory_space=pl.ANY))
            args.append(buf)
    return pl.pallas_call(
        functools.partial(_in_kernel, rope, layer, stacked_first, nb, tl, len(aliases)),
        grid=grid,
        in_specs=in_specs,
        out_specs=out_specs,
        out_shape=out_shape,
        input_output_aliases=aliases,
        compiler_params=pltpu.CompilerParams(
            dimension_semantics=("arbitrary", "arbitrary"), vmem_limit_bytes=VMEM_LIMIT),
        name="in_projection_rope" if rope else "in_projection",
    )(*args)


def _gla_bidirectional(cq_ref, ck_ref, cv_ref, gf_ref, gb_ref, s_f, s_b, seq, oc_ref):
    bl = GLA_BLOCK
    n_sub = bl // CHUNK
    nblk = seq // bl
    ri = _iota((bl, bl), 0)
    ci = _iota((bl, bl), 1)
    same_chunk = lax.shift_right_logical(ri, 6) == lax.shift_right_logical(ci, 6)
    bd = (lax.shift_right_logical(_iota((C_WIDTH, C_KW), 0), 6)
          == lax.shift_right_logical(_iota((C_WIDTH, C_KW), 1), 5))
    khead = lax.shift_right_logical(_iota((1, C_KW), 1), 5)
    vhead = lax.shift_right_logical(_iota((1, C_WIDTH), 1), 6)
    scans = []
    for reverse, g_ref in ((False, gf_ref), (True, gb_ref)):
        causal = same_chunk & ((ci >= ri) if reverse else (ci <= ri))
        scans.append((reverse, g_ref, causal, jnp.where(causal, 1.0, 0.0).astype(BF16)))
    states = [s_f, s_b]
    written = set()

    for step in range(nblk):
        rows0 = [step * bl, (nblk - 1 - step) * bl]
        cums = []
        for (reverse, g_ref, causal, tri), r0 in zip(scans, rows0):
            both = _dot(tri, jnp.concatenate(_split_bf16(g_ref[r0:r0 + bl, :]), axis=-1))
            cums.append(both[:, 0:C_KW] + both[:, C_KW:2 * C_KW])
        prep = []
        for (reverse, g_ref, causal, tri), r0, cum in zip(scans, rows0, cums):
            q = cq_ref[r0:r0 + bl, :]
            k = ck_ref[r0:r0 + bl, :]
            v = cv_ref[r0:r0 + bl, :]
            qt = q * jnp.exp(cum)
            ktb = (k * jnp.exp(-cum)).astype(BF16)
            vb = v.astype(BF16)
            lasts, kdecs = [], []
            for c in range(n_sub):
                c0 = CHUNK * c
                edge = c0 if reverse else c0 + CHUNK - 1
                last = cum[edge:edge + 1, :]
                lasts.append(last)
                kdecs.append((k[c0:c0 + CHUNK] * jnp.exp(last - cum[c0:c0 + CHUNK])).astype(BF16))
            prep.append((qt, ktb, v, vb, lasts, kdecs))
        scores, incs = [], []
        for qt, ktb, v, vb, lasts, kdecs in prep:
            scores.append([_dot_nt(jnp.where(khead == hd, qt, 0.0).astype(BF16), ktb)
                           for hd in range(C_HEADS)])
            incs.append([_dot_tn(vb[CHUNK * c:CHUNK * (c + 1)], kdecs[c]) for c in range(n_sub)])
        probs, entering = [], []
        for si, ((reverse, g_ref, causal, tri), (qt, ktb, v, vb, lasts, kdecs)) in enumerate(
                zip(scans, prep)):
            probs.append([jnp.where(causal, s, 0.0).astype(BF16) for s in scores[si]])
            s_t = states[si]
            before = [None] * n_sub
            subs = range(n_sub)
            for c in (reversed(subs) if reverse else subs):
                before[c] = s_t.astype(BF16)
                s_t = jnp.exp(lasts[c]) * s_t + jnp.where(bd, incs[si][c], 0.0)
            states[si] = s_t
            entering.append(before)
        outs = []
        for si, (qt, ktb, v, vb, lasts, kdecs) in enumerate(prep):
            qtb = qt.astype(BF16)
            o = jnp.concatenate([_dot_nt(qtb[CHUNK * c:CHUNK * (c + 1)], entering[si][c])
                                 for c in range(n_sub)], axis=0)
            for hd in range(C_HEADS):
                o = o + _dot(probs[si][hd], jnp.where(vhead == hd, v, 0.0).astype(BF16))
            outs.append(o)
        if rows0[0] == rows0[1]:
            outs, rows0 = [outs[0] + outs[1]], rows0[:1]
        for o, r0 in zip(outs, rows0):
            if r0 in written:
                oc_ref[r0:r0 + bl, :] = oc_ref[r0:r0 + bl, :] + o
            else:
                oc_ref[r0:r0 + bl, :] = o
                written.add(r0)
    return states[0], states[1]


def _attend_t(jobs, shifts=None):
    def scores(i):
        return _dot(jobs[i][0][...], jobs[i][1])

    outs = []
    pending = [scores(i) for i in range(min(SCORE_LOOKAHEAD, len(jobs)))]
    for i, (_, _, vt) in enumerate(jobs):
        st = pending.pop(0)
        if i + SCORE_LOOKAHEAD < len(jobs):
            pending.append(scores(i + SCORE_LOOKAHEAD))
        m = jnp.max(st, axis=0, keepdims=True) if shifts is None else shifts[i]
        p = jnp.exp2(st - m).astype(BF16)
        ot = _dot(vt[...], p)
        outs.append(ot[0:HEAD_DIM] * (1.0 / ot[HEAD_DIM:HEAD_DIM + 1]))
    return outs


def _pair_rows(a, b):
    return jnp.concatenate([a, b], axis=0).T


def _state_to_blockdiag_t(s_ref):
    rows = []
    for hd in range(C_HEADS):
        pieces = []
        if hd:
            pieces.append(jnp.zeros((C_DK, C_DV * hd), F32))
        pieces.append(s_ref[hd])
        if hd < C_HEADS - 1:
            pieces.append(jnp.zeros((C_DK, C_DV * (C_HEADS - 1 - hd)), F32))
        rows.append(jnp.concatenate(pieces, axis=-1))
    return jnp.concatenate(rows, axis=0).T


def _blockdiag_t_to_state(s_t, out_ref):
    s = s_t.T
    for hd in range(C_HEADS):
        out_ref[hd] = s[C_DK * hd:C_DK * (hd + 1), C_DV * hd:C_DV * (hd + 1)]


def _mix_sequence(cached, layer, first_state, n_aliased, lam_init, seq, qt, qb, *refs):
    it = iter(refs)
    x_ref, mod_ref = next(it), next(it)
    qa_ref, ka_ref, qb_ref, kb_ref = (next(it) for _ in range(4))
    cq_ref, ck_ref, cv_ref, gf_ref, gb_ref, su_ref = (next(it) for _ in range(6))
    vat_ref, vbt_ref = next(it), next(it)
    if cached:
        cakt_ref, cavt_ref, cbkt_ref, cbvt_ref, s0f_ref, s0b_ref = (next(it) for _ in range(6))
        qstat_ref, aqg_ref = next(it), next(it)
    wout_ref, gpost_ref, bog_ref, cog_ref, lamp_ref = (next(it) for _ in range(5))
    for _ in range(n_aliased):
        next(it)
    y_ref = next(it)
    if not cached:
        sf_ref, sb_ref = next(it), next(it)
    kA_s, vtA_s, kB_s, vtB_s, oc_s, mixed_s, kn2_s = (next(it) for _ in range(7))

    lk = kA_s.shape[1]
    past = lk - seq
    t = pl.program_id(1)

    def once_per_sequence(body):
        return body() if seq == qt else pl.when(t == 0)(body)

    @once_per_sequence
    def _per_sequence():
        ones_row = jnp.where(_iota((VT_ROWS - HEAD_DIM, lk), 0) == 0, 1.0, 0.0).astype(BF16)
        for k_new, kt_cache, k_dst, vt_new, vt_cache, vt_dst in (
                (ka_ref, cakt_ref if cached else None, kA_s, vat_ref, cavt_ref if cached else None, vtA_s),
                (kb_ref, cbkt_ref if cached else None, kB_s, vbt_ref, cbvt_ref if cached else None, vtB_s)):
            n_heads = k_dst.shape[0]
            if cached:
                for h0 in range(0, n_heads, 2):
                    pair = jnp.concatenate([kt_cache[h0], kt_cache[h0 + 1]], axis=0).T
                    k_dst[h0, 0:past, :] = pair[:, 0:HEAD_DIM].astype(BF16)
                    k_dst[h0 + 1, 0:past, :] = pair[:, HEAD_DIM:2 * HEAD_DIM].astype(BF16)
            for hd in range(n_heads):
                k_dst[hd, past:lk, :] = k_new[hd]
                if cached:
                    vt_dst[hd, 0:HEAD_DIM, 0:past] = vt_cache[hd].astype(BF16)
                vt_dst[hd, 0:HEAD_DIM, past:lk] = vt_new[hd].astype(BF16)
                vt_dst[hd, HEAD_DIM:VT_ROWS, :] = ones_row
        bounded_jobs = ([(kA_s, g) for g in range(A_KV_HEADS)] + [(kB_s, h) for h in range(B_HEADS)]
                        if lk > KEY_TILE else [])
        for j, (k_dst, hd) in enumerate(bounded_jobs):
            kf = k_dst[hd].astype(F32)
            kn2 = jnp.max(jnp.sum(kf * kf, axis=-1, keepdims=True), axis=0, keepdims=True)
            kn2_s[j:j + 1, :] = jnp.broadcast_to(kn2, (1, LANES))

        if cached:
            s0f, s0b = _state_to_blockdiag_t(s0f_ref), _state_to_blockdiag_t(s0b_ref)
        else:
            s0f = jnp.zeros((C_WIDTH, C_KW), F32)
            s0b = s0f
        s_f, s_b = _gla_bidirectional(cq_ref, ck_ref, cv_ref, gf_ref, gb_ref, s0f, s0b, seq, oc_s)
        if not cached:
            for ref, s_t in ((sf_ref, s_f), (sb_ref, s_b)):
                if first_state:
                    for l2 in range(DEPTH):
                        if l2 == layer:
                            _blockdiag_t_to_state(s_t, ref.at[l2])
                        else:
                            ref[l2] = jnp.zeros(ref.shape[1:], F32)
                else:
                    _blockdiag_t_to_state(s_t, ref)
        for r0 in range(0, seq, GLA_BLOCK):
            oc = oc_s[r0:r0 + GLA_BLOCK, :]
            oc_s[r0:r0 + GLA_BLOCK, :] = oc * lax.rsqrt(_group_mean_sq(oc, 6) + EPS) * cog_ref[...]

    lam = (jnp.exp(jnp.sum(lamp_ref[0:1, :] * lamp_ref[1:2, :], axis=-1, keepdims=True))
           - jnp.exp(jnp.sum(lamp_ref[2:3, :] * lamp_ref[3:4, :], axis=-1, keepdims=True))
           + lam_init)

    def project_out():
        seq_rows = pl.ds(pl.multiple_of(t * qt, qt), qt)
        mixed_s[:, A_WIDTH + B_WIDTH:D_MIX] = oc_s[seq_rows, :]
        gate = mod_ref[:, 2 * D_MODEL:3 * D_MODEL]
        mixed = (mixed_s[...] * su_ref[...]).astype(BF16)
        y = _dot(mixed, wout_ref[...])
        yn = y * lax.rsqrt(jnp.mean(y * y, axis=-1, keepdims=True) + EPS) * gpost_ref[...]
        y_ref[...] = x_ref[...] + gate * yn

    def attn_block(i, carry):
        rows = pl.ds(i * qb, qb) if isinstance(i, int) else pl.ds(pl.multiple_of(i * qb, qb), qb)
        jobs = []
        for grp in range(A_KV_HEADS):
            q4 = jnp.concatenate([qa_ref[A_GROUP * grp + j, :, rows] for j in range(A_GROUP)], axis=-1)
            jobs.append((kA_s.at[grp], q4, vtA_s.at[grp]))
        dim = _iota((HEAD_DIM, qb), 0)
        for hd in range(B_HEADS):
            q = qb_ref[hd, :, rows]
            zero = jnp.zeros_like(q)
            q2 = jnp.concatenate([jnp.where(dim < B_QK_DIM, q, zero),
                                  jnp.where(dim >= B_QK_DIM, q, zero)], axis=-1)
            jobs.append((kB_s.at[hd], q2, vtB_s.at[hd]))

        def finish(outs):
            for grp in range(A_KV_HEADS):
                ot = outs[grp]
                for pair in range(A_GROUP // 2):
                    c0 = 2 * pair * qb
                    col = A_GROUP * HEAD_DIM * grp + 2 * HEAD_DIM * pair
                    mixed_s[rows, col:col + 2 * HEAD_DIM] = _pair_rows(ot[:, c0:c0 + qb],
                                                                       ot[:, c0 + qb:c0 + 2 * qb])
            obs = []
            for hd in range(B_HEADS):
                ot = outs[A_KV_HEADS + hd]
                ob = ot[:, 0:qb] - lam * ot[:, qb:2 * qb]
                obs.append(ob * lax.rsqrt(jnp.mean(ob * ob, axis=0, keepdims=True) + EPS))
            for pair in range(B_HEADS // 2):
                col = A_WIDTH + 2 * B_V_DIM * pair
                mixed_s[rows, col:col + 2 * B_V_DIM] = (_pair_rows(obs[2 * pair], obs[2 * pair + 1])
                                                        * bog_ref[...] * (1.0 - lam_init))
            if qt == qb:
                project_out()

        if lk <= KEY_TILE:
            finish(_attend_t(jobs))
            return carry

        qa_norm = (jnp.max(jnp.abs(aqg_ref[...]), axis=-1, keepdims=True)
                   * (HEAD_DIM ** 0.5 * HEAD_DIM ** -0.5 * LOG2E))
        stat_t = qstat_ref[:, rows]
        shifts, gaps = [], []
        for j, (_, q, _) in enumerate(jobs):
            k_norm = jnp.sqrt(kn2_s[j:j + 1, 0:1])
            if j < A_KV_HEADS:
                upper = qa_norm * k_norm * BOUND_SLACK
                shifts.append(jnp.broadcast_to(upper, (1, q.shape[1])))
                gaps.append(2.0 * upper)
            else:
                r0 = 2 * (j - A_KV_HEADS)
                r1 = r0 + 2 * B_HEADS
                qn2 = jnp.concatenate([stat_t[r0:r0 + 1], stat_t[r0 + 1:r0 + 2]], axis=-1)
                lower = jnp.concatenate([stat_t[r1:r1 + 1], stat_t[r1 + 1:r1 + 2]], axis=-1)
                upper = jnp.sqrt(qn2) * k_norm * BOUND_SLACK
                shifts.append(upper)
                gaps.append(jnp.max(upper - lower, axis=-1, keepdims=True))
        worst = functools.reduce(jnp.maximum, gaps)
        safe = worst[0, 0] <= SAFE_GAP

        @pl.when(safe)
        def _bounded():
            finish(_attend_t(jobs, shifts))

        @pl.when(jnp.logical_not(safe))
        def _exact_max():
            finish(_attend_t(jobs))

        return carry

    if qt != qb:
        for i in range(qt // qb):
            attn_block(i, 0)
        project_out()
    else:
        attn_block(0, 0)


def _mix_kernel(nseq, *params_and_refs):
    params, refs = params_and_refs[:N_MIX_PARAMS], params_and_refs[N_MIX_PARAMS:]
    if nseq == 1:
        return _mix_sequence(*params, *refs)
    n_aliased = params[3]
    shared = {1} | set(range(N_MIX_SEQ_INPUTS, N_MIX_SEQ_INPUTS + N_MIX_WEIGHTS + n_aliased))
    for bi in range(nseq):
        _mix_sequence(*params, *[r if idx in shared else r.at[bi] for idx, r in enumerate(refs)])


def _mixer(x, mod, per_batch_mod, layer, proj, wts, lam_init, cache, state_prev=None):
    bsz, seq, _ = x.shape
    cached = cache is not None
    past = cache[0].shape[4] if cached else 0
    qa, ka, qb, kb, cq, ck, cv, gf, gb, su, vat, vbt = proj[:12]
    qt = 512 if cached else 256
    qb_rows = 256
    nseq = 1 if cached else 4
    lead = None if nseq == 1 else nseq

    def per_layer(shape):
        return pl.BlockSpec((None,) + shape, lambda b, t: (layer,) + (0,) * len(shape))

    def layer_heads_t(n, length):
        return pl.BlockSpec((lead, None, n, HEAD_DIM, length), lambda b, t: (b, layer, 0, 0, 0))

    def heads_t(n):
        if cached:
            return pl.BlockSpec((lead, n, HEAD_DIM, seq), lambda b, t: (b, 0, 0, 0))
        return layer_heads_t(n, seq)

    def heads(n):
        return pl.BlockSpec((lead, n, seq, HEAD_DIM), lambda b, t: (b, 0, 0, 0))

    def head_tile(n):
        return pl.BlockSpec((lead, n, HEAD_DIM, qt), lambda b, t: (b, 0, 0, t))

    def rows(width):
        return pl.BlockSpec((lead, seq, width), lambda b, t: (b, 0, 0))

    def row_tile(width):
        return pl.BlockSpec((lead, qt, width), lambda b, t: (b, t, 0))

    mod_idx = (lambda b, t: (layer, b, 0, 0)) if per_batch_mod else (lambda b, t: (layer, 0, 0, 0))
    in_specs = [row_tile(D_MODEL), pl.BlockSpec((None, None, 1, 3 * D_MODEL), mod_idx),
                head_tile(A_HEADS), heads(A_KV_HEADS), head_tile(B_HEADS), heads(B_HEADS),
                rows(C_KW), rows(C_KW), rows(C_WIDTH), rows(C_KW), rows(C_KW), row_tile(D_MIX),
                heads_t(A_KV_HEADS), heads_t(B_HEADS)]
    args = [x, mod, qa, ka, qb, kb, cq, ck, cv, gf, gb, su, vat, vbt]
    if cached:
        state_in = pl.BlockSpec((None, None, C_HEADS, C_DK, C_DV), lambda b, t: (b, layer, 0, 0, 0))
        in_specs += [layer_heads_t(A_KV_HEADS, past), layer_heads_t(A_KV_HEADS, past),
                     layer_heads_t(B_HEADS, past), layer_heads_t(B_HEADS, past), state_in, state_in,
                     pl.BlockSpec((None, QSTAT_ROWS, qt), lambda b, t: (b, 0, t)),
                     per_layer((1, HEAD_DIM))]
        args += list(cache) + [proj[12], wts["aq_gain"]]
    in_specs += [per_layer((D_MIX, D_MODEL)), per_layer((1, D_MODEL)), per_layer((1, 2 * B_V_DIM)),
                 per_layer((1, C_WIDTH)), per_layer((4, B_QK_DIM))]
    args += [wts["w_out"], wts["g_post"], wts["b_out_gain"], wts["c_out_gain"], wts["lam_params"]]

    out_specs = [row_tile(D_MODEL)]
    out_shape = [jax.ShapeDtypeStruct((bsz, seq, D_MODEL), F32)]
    aliases = {}
    first_state = not cached and state_prev is None
    if not cached:
        if first_state:
            state_out = pl.BlockSpec((lead, DEPTH, C_HEADS, C_DK, C_DV), lambda b, t: (b, 0, 0, 0, 0))
        else:
            state_out = pl.BlockSpec((lead, None, C_HEADS, C_DK, C_DV),
                                     lambda b, t: (b, layer, 0, 0, 0))
            for j, buf in enumerate(state_prev):
                aliases[len(args)] = 1 + j
                in_specs.append(pl.BlockSpec(memory_space=pl.ANY))
                args.append(buf)
        out_specs += [state_out, state_out]
        out_shape += [jax.ShapeDtypeStruct((bsz, DEPTH, C_HEADS, C_DK, C_DV), F32)] * 2

    lk = past + seq
    per_seq = () if nseq == 1 else (nseq,)
    scratch = [pltpu.VMEM(per_seq + shape, dtype) for shape, dtype in (
        ((A_KV_HEADS, lk, HEAD_DIM), BF16), ((A_KV_HEADS, VT_ROWS, lk), BF16),
        ((B_HEADS, lk, HEAD_DIM), BF16), ((B_HEADS, VT_ROWS, lk), BF16),
        ((seq, C_WIDTH), F32), ((qt, D_MIX), F32), ((8, LANES), F32))]
    return pl.pallas_call(
        functools.partial(_mix_kernel, nseq, cached, layer, first_state, len(aliases), lam_init, seq,
                          qt, qb_rows),
        grid=(bsz // nseq, seq // qt),
        in_specs=in_specs,
        out_specs=out_specs,
        out_shape=out_shape,
        scratch_shapes=scratch,
        input_output_aliases=aliases,
        compiler_params=pltpu.CompilerParams(
            dimension_semantics=("arbitrary", "arbitrary"), vmem_limit_bytes=VMEM_LIMIT),
        name="mixer_cached" if cached else "mixer",
    )(*args)


def _rope_tables(seq):
    t = jnp.arange(seq)
    pos_row = (t // GRID_W).astype(F32)
    pos_col = (t % GRID_W).astype(F32)

    def tables(half):
        freq = ROPE_THETA ** (-jnp.arange(half, dtype=F32) / half)
        ang_r = freq[:, None] * pos_row[None, :]
        ang_c = freq[:, None] * pos_col[None, :]
        cos = jnp.concatenate([jnp.cos(ang_r), jnp.cos(ang_r), jnp.cos(ang_c), jnp.cos(ang_c)], axis=0)
        sin = jnp.concatenate([-jnp.sin(ang_r), jnp.sin(ang_r), -jnp.sin(ang_c), jnp.sin(ang_c)], axis=0)
        reps = HEAD_DIM // (4 * half)
        return jnp.tile(cos, (reps, 1)), jnp.tile(sin, (reps, 1))

    cos_a, sin_a = tables(HEAD_DIM // 4)
    cos_b, sin_b = tables(B_QK_DIM // 4)
    return cos_a, sin_a, cos_b, sin_b


def _prepare_weights(g_pre, g_post, w_in, w_out, a_q_gain, a_k_gain, b_lambda_q1, b_lambda_k1,
                     b_lambda_q2, b_lambda_k2, b_out_gain, c_gate_w_fwd, c_gate_b_fwd, c_gate_w_bwd,
                     c_gate_b_bwd, c_out_gain):
    w_in_t = jnp.swapaxes(w_in, 1, 2).astype(BF16)
    pad = jnp.zeros((DEPTH, GATE_RANK, C_KW), F32)
    cw_f = jnp.concatenate([c_gate_w_fwd, pad], axis=1).astype(BF16)
    cw_b = jnp.concatenate([pad, c_gate_w_bwd], axis=1).astype(BF16)
    return {
        "g_pre": g_pre[:, None, :],
        "g_post": g_post[:, None, :],
        "w_in_t": w_in_t,
        "w_out": w_out.astype(BF16),
        "aq_gain": a_q_gain[:, None, :],
        "aq_gain_col": a_q_gain[:, :, None],
        "ak_gain_col": a_k_gain[:, :, None],
        "cw_f": cw_f,
        "cb_f": c_gate_b_fwd[:, None, :],
        "cw_b": cw_b,
        "cb_b": c_gate_b_bwd[:, None, :],
        "b_out_gain": jnp.tile(b_out_gain, (1, 2))[:, None, :],
        "c_out_gain": jnp.tile(c_out_gain, (1, C_HEADS))[:, None, :],
        "lam_params": jnp.stack([b_lambda_q1, b_lambda_k1, b_lambda_q2, b_lambda_k2], axis=1),
    }


def kernel(x_prompt, x_sample, c, cache_a_k, cache_a_v, cache_b_k, cache_b_v, state_c_fwd, state_c_bwd, c_ctx, w_mod, b_mod, g_pre, g_post, w_in, w_out, a_q_gain, a_k_gain, b_lambda_q1, b_lambda_k1, b_lambda_q2, b_lambda_k2, b_out_gain, c_gate_w_fwd, c_gate_b_fwd, c_gate_w_bwd, c_gate_b_bwd, c_out_gain):
    dec_batch = x_sample.shape[0]
    dec_seq = x_sample.shape[1]

    mod_rows = 16
    cvec = jnp.zeros((mod_rows, D_MODEL), F32).at[0:dec_batch].set(c).at[dec_batch].set(c_ctx)
    mod = _modulation(cvec, w_mod, b_mod)[:, :, None, :]
    mod_lat = mod[:, 0:dec_batch]
    mod_ctx = mod[:, dec_batch:dec_batch + 1]

    wts = _prepare_weights(g_pre, g_post, w_in, w_out, a_q_gain, a_k_gain, b_lambda_q1, b_lambda_k1,
                           b_lambda_q2, b_lambda_k2, b_out_gain, c_gate_w_fwd, c_gate_b_fwd,
                           c_gate_w_bwd, c_gate_b_bwd, c_out_gain)
    rope_tabs = _rope_tables(dec_seq)
    cache = tuple(jnp.swapaxes(a, -1, -2) for a in (cache_a_k, cache_a_v, cache_b_k, cache_b_v))
    cache += (state_c_fwd, state_c_bwd)

    y_p, y_s = x_prompt, x_sample
    kv_ctx = None
    states = None
    for l in range(DEPTH):
        lam_init = 0.8 - 0.6 * math.exp(-0.3 * l)
        proj_p = _in_projection(y_p, mod_ctx, False, l, wts, None, 4, x_prompt.shape[1], kv_ctx)
        kv_ctx = proj_p[10:14]
        y_p, *states = _mixer(y_p, mod_ctx, False, l, proj_p, wts, lam_init, None, states)

        proj_s = _in_projection(y_s, mod_lat, True, l, wts, rope_tabs, 1, 1024)
        (y_s,) = _mixer(y_s, mod_lat, True, l, proj_s, wts, lam_init, cache)

    va_t, vb_t, ka_t, kb_t = kv_ctx
    new_kv = [jnp.swapaxes(a, -1, -2) for a in (ka_t, va_t, kb_t, vb_t)]
    return (y_p, y_s, *new_kv, *states)
```

```python
import functools
import math

import jax
import jax.numpy as jnp
from jax import lax
from jax.experimental import pallas as pl
from jax.experimental.pallas import tpu as pltpu

F32 = jnp.float32
BF16 = jnp.bfloat16

D_MODEL = 1024
DEPTH = 2
GRID_W = 64
HEAD_DIM = 64
A_HEADS = 8
A_KV_HEADS = 2
A_GROUP = A_HEADS // A_KV_HEADS
A_WIDTH = A_HEADS * HEAD_DIM
A_KV_WIDTH = A_KV_HEADS * HEAD_DIM
B_HEADS = 4
B_QK_DIM = 32
B_V_DIM = 64
B_WIDTH = B_HEADS * B_V_DIM
C_HEADS = 4
C_DK = 32
C_DV = 64
C_KW = C_HEADS * C_DK
C_WIDTH = C_HEADS * C_DV
GATE_RANK = 16
GLA_TAU = 16.0
CHUNK = 64
D_MIX = A_WIDTH + B_WIDTH + C_WIDTH
ROPE_THETA = 10000.0
EPS = 1e-6

LANES = 128
GLA_BLOCK = 256
VT_ROWS = HEAD_DIM + 16
SCORE_LOOKAHEAD = 6
SAFE_GAP = 96.0
BOUND_SLACK = 1.02
QSTAT_ROWS = 4 * B_HEADS
U_TILE = D_MIX // 4
N_MIX_PARAMS = 8
N_MIX_SEQ_INPUTS = 14
N_MIX_WEIGHTS = 5
KEY_TILE = 256
LOG2E = math.log2(math.e)

OFF_AQ = 0
OFF_AK = OFF_AQ + A_WIDTH
OFF_AV = OFF_AK + A_KV_WIDTH
OFF_BQ = OFF_AV + A_KV_WIDTH
OFF_BK = OFF_BQ + B_WIDTH
OFF_BV = OFF_BK + B_WIDTH
OFF_CQ = OFF_BV + B_WIDTH
OFF_CK = OFF_CQ + C_KW
OFF_CV = OFF_CK + C_KW
OFF_LR = OFF_CV + C_WIDTH
OFF_U = OFF_LR + 2 * GATE_RANK
IN_WIDTH = OFF_U + D_MIX

VMEM_LIMIT = 60 * 1024 * 1024


def _dot(a, b):
    return jnp.dot(a, b, preferred_element_type=F32)


def _dot_nt(a, b):
    return lax.dot_general(a, b, (((1,), (1,)), ((), ())), preferred_element_type=F32)


def _dot_tn(a, b):
    return lax.dot_general(a, b, (((0,), (0,)), ((), ())), preferred_element_type=F32)


def _split_bf16(x):
    hi = x.astype(BF16)
    lo = (x - hi.astype(F32)).astype(BF16)
    return hi, lo


def _iota(shape, dim):
    return lax.broadcasted_iota(jnp.int32, shape, dim)


def _group_mean_sq(x, group_log2):
    width = x.shape[-1]
    r = lax.shift_right_logical(_iota((width, width), 0), group_log2)
    c = lax.shift_right_logical(_iota((width, width), 1), group_log2)
    ones = jnp.where(r == c, 1.0, 0.0).astype(BF16)
    return _dot((x * x).astype(BF16), ones) * (1.0 / (1 << group_log2))


def _log_sigmoid(x):
    return jnp.minimum(x, 0.0) - jnp.log1p(jnp.exp(-jnp.abs(x)))


def _silu(x):
    return x * (1.0 / (1.0 + jnp.exp(-x)))


def _mod_kernel(c_ref, w_ref, b_ref, o_ref):
    a = _silu(c_ref[...]).astype(BF16)
    o_ref[...] = _dot(a, w_ref[...].astype(BF16)) + b_ref[...]


def _modulation(cvec, w_mod, b_mod):
    rows = cvec.shape[0]
    nblk = 3
    return pl.pallas_call(
        _mod_kernel,
        grid=(DEPTH, nblk),
        in_specs=[
            pl.BlockSpec((rows, D_MODEL), lambda l, n: (0, 0)),
            pl.BlockSpec((None, D_MODEL, D_MODEL), lambda l, n: (l, 0, n)),
            pl.BlockSpec((None, 1, D_MODEL), lambda l, n: (l, 0, n)),
        ],
        out_specs=pl.BlockSpec((None, rows, D_MODEL), lambda l, n: (l, 0, n)),
        out_shape=jax.ShapeDtypeStruct((DEPTH, rows, 3 * D_MODEL), F32),
        compiler_params=pltpu.CompilerParams(
            dimension_semantics=("arbitrary", "arbitrary"), vmem_limit_bytes=VMEM_LIMIT),
        name="modulation",
    )(cvec, w_mod, b_mod.reshape(DEPTH, 1, 3 * D_MODEL))


def _in_kernel(rope, layer, stacked_first, nb, tl, n_aliased, *refs):
    (x_ref, mod_ref, gpre_ref, wt_ref, aqg_ref, akg_ref, cwf_ref, cbf_ref, cwb_ref, cbb_ref) = refs[:10]
    refs = refs[10:]
    if rope:
        cosa_ref, sina_ref, cosb_ref, sinb_ref = refs[:4]
        refs = refs[4:]
    refs = refs[n_aliased:]
    (qa_ref, ka_ref, qb_ref, kb_ref, cq_ref, ck_ref, cv_ref, gf_ref, gb_ref, su_ref,
     vat_ref, vbt_ref) = refs[:12]
    kat_ref, kbt_ref = (None, None) if rope else refs[12:]
    qstat_ref = refs[12] if rope else None

    x = x_ref[...].reshape(nb * tl, D_MODEL)
    shift = mod_ref[:, 0:D_MODEL]
    scale = mod_ref[:, D_MODEL:2 * D_MODEL]
    ms = jnp.mean(x * x, axis=-1, keepdims=True)
    h = (x * lax.rsqrt(ms + EPS)) * gpre_ref[...] * (1.0 + scale) + shift
    hb = h.astype(BF16)

    def proj(off, width):
        return _dot_nt(hb, wt_ref[off:off + width, :])

    def proj_t(off, width):
        return _dot_nt(wt_ref[off:off + width, :], hb)

    def put_rows(ref, val):
        for bi in range(nb):
            ref[bi] = val[bi * tl:(bi + 1) * tl].astype(ref.dtype)

    def put_heads(ref, val, n_heads):
        for bi in range(nb):
            for hd in range(n_heads):
                ref[bi, hd] = val[bi * tl:(bi + 1) * tl,
                                  HEAD_DIM * hd:HEAD_DIM * (hd + 1)].astype(ref.dtype)

    def put_heads_t(ref, heads_t, stacked=False):
        for bi in range(nb):
            for hd, val_t in enumerate(heads_t):
                blk = val_t[:, bi * tl:(bi + 1) * tl].astype(ref.dtype)
                if stacked and stacked_first:
                    for l2 in range(DEPTH):
                        ref[bi, l2, hd] = blk if l2 == layer else jnp.zeros_like(blk)
                else:
                    ref[bi, hd] = blk

    def split_heads_t(val_t, n_heads):
        return [val_t[HEAD_DIM * hd:HEAD_DIM * (hd + 1)] for hd in range(n_heads)]

    def rms_t(head_t, gain_col):
        ms_h = jnp.mean(head_t * head_t, axis=0, keepdims=True)
        return head_t * lax.rsqrt(ms_h + EPS) * gain_col

    def rope_t(head_t, cos_t, sin_t, dist):
        blocks = [head_t[r0:r0 + dist] for r0 in range(0, HEAD_DIM, dist)]
        partner = jnp.concatenate([blocks[j ^ 1] for j in range(len(blocks))], axis=0)
        return head_t * cos_t + partner * sin_t

    def gate_tile(j):
        c0 = U_TILE * j
        val = _silu(proj(OFF_U + c0, U_TILE))
        for bi in range(nb):
            su_ref[bi, :, c0:c0 + U_TILE] = val[bi * tl:(bi + 1) * tl]

    za_t = proj_t(OFF_AQ, A_WIDTH + 2 * A_KV_WIDTH)
    zb_lr_t = _dot_nt(jnp.concatenate([wt_ref[OFF_BQ:OFF_BQ + 3 * B_WIDTH, :],
                                       wt_ref[OFF_LR:OFF_LR + 2 * GATE_RANK, :]], axis=0), hb)
    zb_t = zb_lr_t[0:3 * B_WIDTH]
    lr_t = zb_lr_t[3 * B_WIDTH:].astype(BF16)
    gate_tile(0)

    aq_t = [rms_t(h_t, aqg_ref[...]) for h_t in split_heads_t(za_t[0:A_WIDTH], A_HEADS)]
    ak_t = [rms_t(h_t, akg_ref[...])
            for h_t in split_heads_t(za_t[A_WIDTH:A_WIDTH + A_KV_WIDTH], A_KV_HEADS)]
    if rope:
        aq_t = [rope_t(h_t, cosa_ref[...], sina_ref[...], 16) for h_t in aq_t]
        ak_t = [rope_t(h_t, cosa_ref[...], sina_ref[...], 16) for h_t in ak_t]
    put_heads_t(qa_ref, [h_t * (HEAD_DIM ** -0.5 * LOG2E) for h_t in aq_t])
    put_heads(ka_ref, jnp.concatenate(ak_t, axis=0).T, A_KV_HEADS)
    put_heads_t(vat_ref, split_heads_t(za_t[A_WIDTH + A_KV_WIDTH:], A_KV_HEADS), stacked=not rope)
    if kat_ref is not None:
        put_heads_t(kat_ref, ak_t, stacked=True)
    gate_tile(1)
    cqk = proj(OFF_CQ, 2 * C_KW)
    cv = proj(OFF_CV, C_WIDTH)

    bq_t = split_heads_t(zb_t[0:B_WIDTH], B_HEADS)
    bk_t = split_heads_t(zb_t[B_WIDTH:2 * B_WIDTH], B_HEADS)
    if rope:
        bq_t = [rope_t(h_t, cosb_ref[...], sinb_ref[...], 8) for h_t in bq_t]
        bk_t = [rope_t(h_t, cosb_ref[...], sinb_ref[...], 8) for h_t in bk_t]
    bq_t = [h_t * (B_QK_DIM ** -0.5 * LOG2E) for h_t in bq_t]
    put_heads_t(qb_ref, bq_t)
    put_heads(kb_ref, jnp.concatenate(bk_t, axis=0).T, B_HEADS)
    if rope:
        halves = [(hd, r0) for hd in range(B_HEADS) for r0 in (0, B_QK_DIM)]
        qstat_ref[0] = jnp.concatenate(
            [jnp.sum(bq_t[hd][r0:r0 + B_QK_DIM] * bq_t[hd][r0:r0 + B_QK_DIM], axis=0, keepdims=True)
             for hd, r0 in halves]
            + [jnp.sum(bq_t[hd][r0:r0 + B_QK_DIM] * bk_t[hd][r0:r0 + B_QK_DIM], axis=0, keepdims=True)
               for hd, r0 in halves], axis=0)
    put_heads_t(vbt_ref, split_heads_t(zb_t[2 * B_WIDTH:], B_HEADS), stacked=not rope)
    if kbt_ref is not None:
        put_heads_t(kbt_ref, bk_t, stacked=True)
    gate_tile(2)
    g_pre = _dot_tn(lr_t, jnp.concatenate([cwf_ref[...], cwb_ref[...]], axis=-1))
    gf_pre, gb_pre = g_pre[:, 0:C_KW], g_pre[:, C_KW:2 * C_KW]

    put_rows(cq_ref, cqk[:, 0:C_KW] * (C_DK ** -0.5))
    put_rows(ck_ref, cqk[:, C_KW:2 * C_KW])
    put_rows(cv_ref, cv)
    put_rows(gf_ref, _log_sigmoid(gf_pre + cbf_ref[...]) * (1.0 / GLA_TAU))
    put_rows(gb_ref, _log_sigmoid(gb_pre + cbb_ref[...]) * (1.0 / GLA_TAU))
    gate_tile(3)


def _in_projection(x, mod, per_batch_mod, layer, wts, rope_tabs, nb, tl, kv_prev=None):
    bsz, seq, _ = x.shape
    rope = rope_tabs is not None
    stacked = not rope
    stacked_first = stacked and kv_prev is None
    grid = (bsz // nb, seq // tl)

    def per_layer(shape):
        return pl.BlockSpec((None,) + shape, lambda b, t: (layer,) + (0,) * len(shape))

    mod_idx = (lambda b, t: (layer, b, 0, 0)) if per_batch_mod else (lambda b, t: (layer, 0, 0, 0))
    in_specs = [
        pl.BlockSpec((nb, tl, D_MODEL), lambda b, t: (b, t, 0)),
        pl.BlockSpec((None, None, 1, 3 * D_MODEL), mod_idx),
        per_layer((1, D_MODEL)),
        per_layer((IN_WIDTH, D_MODEL)),
        per_layer((HEAD_DIM, 1)),
        per_layer((HEAD_DIM, 1)),
        per_layer((2 * GATE_RANK, C_KW)),
        per_layer((1, C_KW)),
        per_layer((2 * GATE_RANK, C_KW)),
        per_layer((1, C_KW)),
    ]
    args = [x, mod, wts["g_pre"], wts["w_in_t"], wts["aq_gain_col"], wts["ak_gain_col"],
            wts["cw_f"], wts["cb_f"], wts["cw_b"], wts["cb_b"]]
    if rope:
        assert nb == 1
        in_specs += [
            pl.BlockSpec((HEAD_DIM, tl), lambda b, t: (0, t)),
            pl.BlockSpec((HEAD_DIM, tl), lambda b, t: (0, t)),
            pl.BlockSpec((HEAD_DIM, tl), lambda b, t: (0, t)),
            pl.BlockSpec((HEAD_DIM, tl), lambda b, t: (0, t)),
        ]
        args += list(rope_tabs)

    def heads(n):
        return pl.BlockSpec((nb, n, tl, HEAD_DIM), lambda b, t: (b, 0, t, 0))

    def heads_t(n):
        if stacked_first:
            return pl.BlockSpec((nb, DEPTH, n, HEAD_DIM, tl), lambda b, t: (b, 0, 0, 0, t))
        if stacked:
            return pl.BlockSpec((nb, None, n, HEAD_DIM, tl), lambda b, t: (b, layer, 0, 0, t))
        return pl.BlockSpec((nb, n, HEAD_DIM, tl), lambda b, t: (b, 0, 0, t))

    def q_heads_t(n):
        return pl.BlockSpec((nb, n, HEAD_DIM, tl), lambda b, t: (b, 0, 0, t))

    def rows(width):
        return pl.BlockSpec((nb, tl, width), lambda b, t: (b, t, 0))

    def hshape(n):
        return jax.ShapeDtypeStruct((bsz, n, seq, HEAD_DIM), BF16)

    def qshape(n):
        return jax.ShapeDtypeStruct((bsz, n, HEAD_DIM, seq), BF16)

    def tshape(n):
        if stacked:
            return jax.ShapeDtypeStruct((bsz, DEPTH, n, HEAD_DIM, seq), F32)
        return jax.ShapeDtypeStruct((bsz, n, HEAD_DIM, seq), BF16)

    def rshape(width):
        return jax.ShapeDtypeStruct((bsz, seq, width), F32)

    out_specs = [q_heads_t(A_HEADS), heads(A_KV_HEADS), q_heads_t(B_HEADS), heads(B_HEADS),
                 rows(C_KW), rows(C_KW), rows(C_WIDTH), rows(C_KW), rows(C_KW), rows(D_MIX),
                 heads_t(A_KV_HEADS), heads_t(B_HEADS)]
    out_shape = [qshape(A_HEADS), hshape(A_KV_HEADS), qshape(B_HEADS), hshape(B_HEADS),
                 rshape(C_KW), rshape(C_KW), rshape(C_WIDTH), rshape(C_KW), rshape(C_KW),
                 rshape(D_MIX), tshape(A_KV_HEADS), tshape(B_HEADS)]
    if stacked:
        out_specs += [heads_t(A_KV_HEADS), heads_t(B_HEADS)]
        out_shape += [tshape(A_KV_HEADS), tshape(B_HEADS)]
    else:
        out_specs += [pl.BlockSpec((nb, QSTAT_ROWS, tl), lambda b, t: (b, 0, t))]
        out_shape += [jax.ShapeDtypeStruct((bsz, QSTAT_ROWS, seq), F32)]
    aliases = {}
    if kv_prev is not None:
        for j, buf in enumerate(kv_prev):
            aliases[len(args)] = 10 + j
            in_specs.append(pl.BlockSpec(memory_space=pl.ANY))
            args.append(buf)
    return pl.pallas_call(
        functools.partial(_in_kernel, rope, layer, stacked_first, nb, tl, len(aliases)),
        grid=grid,
        in_specs=in_specs,
        out_specs=out_specs,
        out_shape=out_shape,
        input_output_aliases=aliases,
        compiler_params=pltpu.CompilerParams(
            dimension_semantics=("arbitrary", "arbitrary"), vmem_limit_bytes=VMEM_LIMIT),
        name="in_projection_rope" if rope else "in_projection",
    )(*args)


def _gla_bidirectional(cq_ref, ck_ref, cv_ref, gf_ref, gb_ref, s_f, s_b, seq, oc_ref):
    bl = GLA_BLOCK
    n_sub = bl // CHUNK
    nblk = seq // bl
    ri = _iota((bl, bl), 0)
    ci = _iota((bl, bl), 1)
    same_chunk = lax.shift_right_logical(ri, 6) == lax.shift_right_logical(ci, 6)
    bd = (lax.shift_right_logical(_iota((C_WIDTH, C_KW), 0), 6)
          == lax.shift_right_logical(_iota((C_WIDTH, C_KW), 1), 5))
    khead = lax.shift_right_logical(_iota((1, C_KW), 1), 5)
    vhead = lax.shift_right_logical(_iota((1, C_WIDTH), 1), 6)
    scans = []
    for reverse, g_ref in ((False, gf_ref), (True, gb_ref)):
        causal = same_chunk & ((ci >= ri) if reverse else (ci <= ri))
        scans.append((reverse, g_ref, causal, jnp.where(causal, 1.0, 0.0).astype(BF16)))
    states = [s_f, s_b]
    written = set()

    for step in range(nblk):
        rows0 = [step * bl, (nblk - 1 - step) * bl]
        cums = []
        for (reverse, g_ref, causal, tri), r0 in zip(scans, rows0):
            both = _dot(tri, jnp.concatenate(_split_bf16(g_ref[r0:r0 + bl, :]), axis=-1))
            cums.append(both[:, 0:C_KW] + both[:, C_KW:2 * C_KW])
        prep = []
        for (reverse, g_ref, causal, tri), r0, cum in zip(scans, rows0, cums):
            q = cq_ref[r0:r0 + bl, :]
            k = ck_ref[r0:r0 + bl, :]
            v = cv_ref[r0:r0 + bl, :]
            qt = q * jnp.exp(cum)
            ktb = (k * jnp.exp(-cum)).astype(BF16)
            vb = v.astype(BF16)
            lasts, kdecs = [], []
            for c in range(n_sub):
                c0 = CHUNK * c
                edge = c0 if reverse else c0 + CHUNK - 1
                last = cum[edge:edge + 1, :]
                lasts.append(last)
                kdecs.append((k[c0:c0 + CHUNK] * jnp.exp(last - cum[c0:c0 + CHUNK])).astype(BF16))
            prep.append((qt, ktb, v, vb, lasts, kdecs))
        scores, incs = [], []
        for qt, ktb, v, vb, lasts, kdecs in prep:
            q_heads = jnp.concatenate([jnp.where(khead == hd, qt, 0.0).astype(BF16)
                                       for hd in range(C_HEADS)], axis=0)
            s_all = _dot_nt(q_heads, ktb)
            scores.append([s_all[bl * hd:bl * (hd + 1)] for hd in range(C_HEADS)])
            incs.append([_dot_tn(vb[CHUNK * c:CHUNK * (c + 1)], kdecs[c]) for c in range(n_sub)])
        probs, entering = [], []
        for si, ((reverse, g_ref, causal, tri), (qt, ktb, v, vb, lasts, kdecs)) in enumerate(
                zip(scans, prep)):
            probs.append([jnp.where(causal, s, 0.0).astype(BF16) for s in scores[si]])
            s_t = states[si]
            before = [None] * n_sub
            subs = range(n_sub)
            for c in (reversed(subs) if reverse else subs):
                before[c] = s_t.astype(BF16)
                s_t = jnp.exp(lasts[c]) * s_t + jnp.where(bd, incs[si][c], 0.0)
            states[si] = s_t
            entering.append(before)
        outs = []
        for si, (qt, ktb, v, vb, lasts, kdecs) in enumerate(prep):
            qtb = qt.astype(BF16)
            o = jnp.concatenate([_dot_nt(qtb[CHUNK * c:CHUNK * (c + 1)], entering[si][c])
                                 for c in range(n_sub)], axis=0)
            for hd in range(C_HEADS):
                o = o + _dot(probs[si][hd], jnp.where(vhead == hd, v, 0.0).astype(BF16))
            outs.append(o)
        if rows0[0] == rows0[1]:
            outs, rows0 = [outs[0] + outs[1]], rows0[:1]
        for o, r0 in zip(outs, rows0):
            if r0 in written:
                oc_ref[r0:r0 + bl, :] = oc_ref[r0:r0 + bl, :] + o
            else:
                oc_ref[r0:r0 + bl, :] = o
                written.add(r0)
    return states[0], states[1]


def _attend_t(jobs, shifts=None):
    def scores(i):
        return _dot(jobs[i][0][...], jobs[i][1])

    outs = []
    pending = [scores(i) for i in range(min(SCORE_LOOKAHEAD, len(jobs)))]
    for i, (_, _, vt) in enumerate(jobs):
        st = pending.pop(0)
        if i + SCORE_LOOKAHEAD < len(jobs):
            pending.append(scores(i + SCORE_LOOKAHEAD))
        m = jnp.max(st, axis=0, keepdims=True) if shifts is None else shifts[i]
        p = jnp.exp2(st - m).astype(BF16)
        ot = _dot(vt[...], p)
        outs.append(ot[0:HEAD_DIM] * (1.0 / ot[HEAD_DIM:HEAD_DIM + 1]))
    return outs


def _pair_rows(a, b):
    return jnp.concatenate([a, b], axis=0).T


def _state_to_blockdiag_t(s_ref):
    rows = []
    for hd in range(C_HEADS):
        pieces = []
        if hd:
            pieces.append(jnp.zeros((C_DK, C_DV * hd), F32))
        pieces.append(s_ref[hd])
        if hd < C_HEADS - 1:
            pieces.append(jnp.zeros((C_DK, C_DV * (C_HEADS - 1 - hd)), F32))
        rows.append(jnp.concatenate(pieces, axis=-1))
    return jnp.concatenate(rows, axis=0).T


def _blockdiag_t_to_state(s_t, out_ref):
    s = s_t.T
    for hd in range(C_HEADS):
        out_ref[hd] = s[C_DK * hd:C_DK * (hd + 1), C_DV * hd:C_DV * (hd + 1)]


def _mix_sequence(cached, layer, first_state, n_aliased, lam_init, seq, qt, qb, *refs):
    it = iter(refs)
    x_ref, mod_ref = next(it), next(it)
    qa_ref, ka_ref, qb_ref, kb_ref = (next(it) for _ in range(4))
    cq_ref, ck_ref, cv_ref, gf_ref, gb_ref, su_ref = (next(it) for _ in range(6))
    vat_ref, vbt_ref = next(it), next(it)
    if cached:
        cakt_ref, cavt_ref, cbkt_ref, cbvt_ref, s0f_ref, s0b_ref = (next(it) for _ in range(6))
        qstat_ref, aqg_ref = next(it), next(it)
    wout_ref, gpost_ref, bog_ref, cog_ref, lamp_ref = (next(it) for _ in range(5))
    for _ in range(n_aliased):
        next(it)
    y_ref = next(it)
    if not cached:
        sf_ref, sb_ref = next(it), next(it)
    kA_s, vtA_s, kB_s, vtB_s, oc_s, mixed_s, kn2_s = (next(it) for _ in range(7))

    lk = kA_s.shape[1]
    past = lk - seq
    t = pl.program_id(1)

    def once_per_sequence(body):
        return body() if seq == qt else pl.when(t == 0)(body)

    @once_per_sequence
    def _per_sequence():
        ones_row = jnp.where(_iota((VT_ROWS - HEAD_DIM, lk), 0) == 0, 1.0, 0.0).astype(BF16)
        for k_new, kt_cache, k_dst, vt_new, vt_cache, vt_dst in (
                (ka_ref, cakt_ref if cached else None, kA_s, vat_ref, cavt_ref if cached else None, vtA_s),
                (kb_ref, cbkt_ref if cached else None, kB_s, vbt_ref, cbvt_ref if cached else None, vtB_s)):
            n_heads = k_dst.shape[0]
            if cached:
                for h0 in range(0, n_heads, 2):
                    pair = jnp.concatenate([kt_cache[h0], kt_cache[h0 + 1]], axis=0).T
                    k_dst[h0, 0:past, :] = pair[:, 0:HEAD_DIM].astype(BF16)
                    k_dst[h0 + 1, 0:past, :] = pair[:, HEAD_DIM:2 * HEAD_DIM].astype(BF16)
            for hd in range(n_heads):
                k_dst[hd, past:lk, :] = k_new[hd]
                if cached:
                    vt_dst[hd, 0:HEAD_DIM, 0:past] = vt_cache[hd].astype(BF16)
                vt_dst[hd, 0:HEAD_DIM, past:lk] = vt_new[hd].astype(BF16)
                vt_dst[hd, HEAD_DIM:VT_ROWS, :] = ones_row
        bounded_jobs = ([(kA_s, g) for g in range(A_KV_HEADS)] + [(kB_s, h) for h in range(B_HEADS)]
                        if lk > KEY_TILE else [])
        for j, (k_dst, hd) in enumerate(bounded_jobs):
            kf = k_dst[hd].astype(F32)
            kn2 = jnp.max(jnp.sum(kf * kf, axis=-1, keepdims=True), axis=0, keepdims=True)
            kn2_s[j:j + 1, :] = jnp.broadcast_to(kn2, (1, LANES))

        if cached:
            s0f, s0b = _state_to_blockdiag_t(s0f_ref), _state_to_blockdiag_t(s0b_ref)
        else:
            s0f = jnp.zeros((C_WIDTH, C_KW), F32)
            s0b = s0f
        s_f, s_b = _gla_bidirectional(cq_ref, ck_ref, cv_ref, gf_ref, gb_ref, s0f, s0b, seq, oc_s)
        if not cached:
            for ref, s_t in ((sf_ref, s_f), (sb_ref, s_b)):
                if first_state:
                    for l2 in range(DEPTH):
                        if l2 == layer:
                            _blockdiag_t_to_state(s_t, ref.at[l2])
                        else:
                            ref[l2] = jnp.zeros(ref.shape[1:], F32)
                else:
                    _blockdiag_t_to_state(s_t, ref)
        for r0 in range(0, seq, GLA_BLOCK):
            oc = oc_s[r0:r0 + GLA_BLOCK, :]
            oc_s[r0:r0 + GLA_BLOCK, :] = oc * lax.rsqrt(_group_mean_sq(oc, 6) + EPS) * cog_ref[...]

    lam = (jnp.exp(jnp.sum(lamp_ref[0:1, :] * lamp_ref[1:2, :], axis=-1, keepdims=True))
           - jnp.exp(jnp.sum(lamp_ref[2:3, :] * lamp_ref[3:4, :], axis=-1, keepdims=True))
           + lam_init)

    def project_out():
        seq_rows = pl.ds(pl.multiple_of(t * qt, qt), qt)
        mixed_s[:, A_WIDTH + B_WIDTH:D_MIX] = oc_s[seq_rows, :]
        gate = mod_ref[:, 2 * D_MODEL:3 * D_MODEL]
        mixed = (mixed_s[...] * su_ref[...]).astype(BF16)
        y = _dot(mixed, wout_ref[...])
        yn = y * lax.rsqrt(jnp.mean(y * y, axis=-1, keepdims=True) + EPS) * gpost_ref[...]
        y_ref[...] = x_ref[...] + gate * yn

    def attn_block(i, carry):
        rows = pl.ds(i * qb, qb) if isinstance(i, int) else pl.ds(pl.multiple_of(i * qb, qb), qb)
        jobs = []
        for grp in range(A_KV_HEADS):
            q4 = jnp.concatenate([qa_ref[A_GROUP * grp + j, :, rows] for j in range(A_GROUP)], axis=-1)
            jobs.append((kA_s.at[grp], q4, vtA_s.at[grp]))
        dim = _iota((HEAD_DIM, qb), 0)
        for hd in range(B_HEADS):
            q = qb_ref[hd, :, rows]
            zero = jnp.zeros_like(q)
            q2 = jnp.concatenate([jnp.where(dim < B_QK_DIM, q, zero),
                                  jnp.where(dim >= B_QK_DIM, q, zero)], axis=-1)
            jobs.append((kB_s.at[hd], q2, vtB_s.at[hd]))

        def finish(outs):
            for grp in range(A_KV_HEADS):
                ot = outs[grp]
                for pair in range(A_GROUP // 2):
                    c0 = 2 * pair * qb
                    col = A_GROUP * HEAD_DIM * grp + 2 * HEAD_DIM * pair
                    mixed_s[rows, col:col + 2 * HEAD_DIM] = _pair_rows(ot[:, c0:c0 + qb],
                                                                       ot[:, c0 + qb:c0 + 2 * qb])
            obs = []
            for hd in range(B_HEADS):
                ot = outs[A_KV_HEADS + hd]
                ob = ot[:, 0:qb] - lam * ot[:, qb:2 * qb]
                obs.append(ob * lax.rsqrt(jnp.mean(ob * ob, axis=0, keepdims=True) + EPS))
            for pair in range(B_HEADS // 2):
                col = A_WIDTH + 2 * B_V_DIM * pair
                mixed_s[rows, col:col + 2 * B_V_DIM] = (_pair_rows(obs[2 * pair], obs[2 * pair + 1])
                                                        * bog_ref[...] * (1.0 - lam_init))
            if qt == qb:
                project_out()

        if lk <= KEY_TILE:
            finish(_attend_t(jobs))
            return carry

        qa_norm = (jnp.max(jnp.abs(aqg_ref[...]), axis=-1, keepdims=True)
                   * (HEAD_DIM ** 0.5 * HEAD_DIM ** -0.5 * LOG2E))
        stat_t = qstat_ref[:, rows]
        shifts, gaps = [], []
        for j, (_, q, _) in enumerate(jobs):
            k_norm = jnp.sqrt(kn2_s[j:j + 1, 0:1])
            if j < A_KV_HEADS:
                upper = qa_norm * k_norm * BOUND_SLACK
                shifts.append(jnp.broadcast_to(upper, (1, q.shape[1])))
                gaps.append(2.0 * upper)
            else:
                r0 = 2 * (j - A_KV_HEADS)
                r1 = r0 + 2 * B_HEADS
                qn2 = jnp.concatenate([stat_t[r0:r0 + 1], stat_t[r0 + 1:r0 + 2]], axis=-1)
                lower = jnp.concatenate([stat_t[r1:r1 + 1], stat_t[r1 + 1:r1 + 2]], axis=-1)
                upper = jnp.sqrt(qn2) * k_norm * BOUND_SLACK
                shifts.append(upper)
                gaps.append(jnp.max(upper - lower, axis=-1, keepdims=True))
        worst = functools.reduce(jnp.maximum, gaps)
        safe = worst[0, 0] <= SAFE_GAP

        @pl.when(safe)
        def _bounded():
            finish(_attend_t(jobs, shifts))

        @pl.when(jnp.logical_not(safe))
        def _exact_max():
            finish(_attend_t(jobs))

        return carry

    if qt != qb:
        for i in range(qt // qb):
            attn_block(i, 0)
        project_out()
    else:
        attn_block(0, 0)


def _mix_kernel(nseq, *params_and_refs):
    params, refs = params_and_refs[:N_MIX_PARAMS], params_and_refs[N_MIX_PARAMS:]
    if nseq == 1:
        return _mix_sequence(*params, *refs)
    n_aliased = params[3]
    shared = {1} | set(range(N_MIX_SEQ_INPUTS, N_MIX_SEQ_INPUTS + N_MIX_WEIGHTS + n_aliased))
    for bi in range(nseq):
        _mix_sequence(*params, *[r if idx in shared else r.at[bi] for idx, r in enumerate(refs)])


def _mixer(x, mod, per_batch_mod, layer, proj, wts, lam_init, cache, state_prev=None):
    bsz, seq, _ = x.shape
    cached = cache is not None
    past = cache[0].shape[4] if cached else 0
    qa, ka, qb, kb, cq, ck, cv, gf, gb, su, vat, vbt = proj[:12]
    qt = 512 if cached else 256
    qb_rows = 256
    nseq = 1 if cached else 4
    lead = None if nseq == 1 else nseq

    def per_layer(shape):
        return pl.BlockSpec((None,) + shape, lambda b, t: (layer,) + (0,) * len(shape))

    def layer_heads_t(n, length):
        return pl.BlockSpec((lead, None, n, HEAD_DIM, length), lambda b, t: (b, layer, 0, 0, 0))

    def heads_t(n):
        if cached:
            return pl.BlockSpec((lead, n, HEAD_DIM, seq), lambda b, t: (b, 0, 0, 0))
        return layer_heads_t(n, seq)

    def heads(n):
        return pl.BlockSpec((lead, n, seq, HEAD_DIM), lambda b, t: (b, 0, 0, 0))

    def head_tile(n):
        return pl.BlockSpec((lead, n, HEAD_DIM, qt), lambda b, t: (b, 0, 0, t))

    def rows(width):
        return pl.BlockSpec((lead, seq, width), lambda b, t: (b, 0, 0))

    def row_tile(width):
        return pl.BlockSpec((lead, qt, width), lambda b, t: (b, t, 0))

    mod_idx = (lambda b, t: (layer, b, 0, 0)) if per_batch_mod else (lambda b, t: (layer, 0, 0, 0))
    in_specs = [row_tile(D_MODEL), pl.BlockSpec((None, None, 1, 3 * D_MODEL), mod_idx),
                head_tile(A_HEADS), heads(A_KV_HEADS), head_tile(B_HEADS), heads(B_HEADS),
                rows(C_KW), rows(C_KW), rows(C_WIDTH), rows(C_KW), rows(C_KW), row_tile(D_MIX),
                heads_t(A_KV_HEADS), heads_t(B_HEADS)]
    args = [x, mod, qa, ka, qb, kb, cq, ck, cv, gf, gb, su, vat, vbt]
    if cached:
        state_in = pl.BlockSpec((None, None, C_HEADS, C_DK, C_DV), lambda b, t: (b, layer, 0, 0, 0))
        in_specs += [layer_heads_t(A_KV_HEADS, past), layer_heads_t(A_KV_HEADS, past),
                     layer_heads_t(B_HEADS, past), layer_heads_t(B_HEADS, past), state_in, state_in,
                     pl.BlockSpec((None, QSTAT_ROWS, qt), lambda b, t: (b, 0, t)),
                     per_layer((1, HEAD_DIM))]
        args += list(cache) + [proj[12], wts["aq_gain"]]
    in_specs += [per_layer((D_MIX, D_MODEL)), per_layer((1, D_MODEL)), per_layer((1, 2 * B_V_DIM)),
                 per_layer((1, C_WIDTH)), per_layer((4, B_QK_DIM))]
    args += [wts["w_out"], wts["g_post"], wts["b_out_gain"], wts["c_out_gain"], wts["lam_params"]]

    out_specs = [row_tile(D_MODEL)]
    out_shape = [jax.ShapeDtypeStruct((bsz, seq, D_MODEL), F32)]
    aliases = {}
    first_state = not cached and state_prev is None
    if not cached:
        if first_state:
            state_out = pl.BlockSpec((lead, DEPTH, C_HEADS, C_DK, C_DV), lambda b, t: (b, 0, 0, 0, 0))
        else:
            state_out = pl.BlockSpec((lead, None, C_HEADS, C_DK, C_DV),
                                     lambda b, t: (b, layer, 0, 0, 0))
            for j, buf in enumerate(state_prev):
                aliases[len(args)] = 1 + j
                in_specs.append(pl.BlockSpec(memory_space=pl.ANY))
                args.append(buf)
        out_specs += [state_out, state_out]
        out_shape += [jax.ShapeDtypeStruct((bsz, DEPTH, C_HEADS, C_DK, C_DV), F32)] * 2

    lk = past + seq
    per_seq = () if nseq == 1 else (nseq,)
    scratch = [pltpu.VMEM(per_seq + shape, dtype) for shape, dtype in (
        ((A_KV_HEADS, lk, HEAD_DIM), BF16), ((A_KV_HEADS, VT_ROWS, lk), BF16),
        ((B_HEADS, lk, HEAD_DIM), BF16), ((B_HEADS, VT_ROWS, lk), BF16),
        ((seq, C_WIDTH), F32), ((qt, D_MIX), F32), ((8, LANES), F32))]
    return pl.pallas_call(
        functools.partial(_mix_kernel, nseq, cached, layer, first_state, len(aliases), lam_init, seq,
                          qt, qb_rows),
        grid=(bsz // nseq, seq // qt),
        in_specs=in_specs,
        out_specs=out_specs,
        out_shape=out_shape,
        scratch_shapes=scratch,
        input_output_aliases=aliases,
        compiler_params=pltpu.CompilerParams(
            dimension_semantics=("arbitrary", "arbitrary"), vmem_limit_bytes=VMEM_LIMIT),
        name="mixer_cached" if cached else "mixer",
    )(*args)


def _rope_tables(seq):
    t = jnp.arange(seq)
    pos_row = (t // GRID_W).astype(F32)
    pos_col = (t % GRID_W).astype(F32)

    def tables(half):
        freq = ROPE_THETA ** (-jnp.arange(half, dtype=F32) / half)
        ang_r = freq[:, None] * pos_row[None, :]
        ang_c = freq[:, None] * pos_col[None, :]
        cos = jnp.concatenate([jnp.cos(ang_r), jnp.cos(ang_r), jnp.cos(ang_c), jnp.cos(ang_c)], axis=0)
        sin = jnp.concatenate([-jnp.sin(ang_r), jnp.sin(ang_r), -jnp.sin(ang_c), jnp.sin(ang_c)], axis=0)
        reps = HEAD_DIM // (4 * half)
        return jnp.tile(cos, (reps, 1)), jnp.tile(sin, (reps, 1))

    cos_a, sin_a = tables(HEAD_DIM // 4)
    cos_b, sin_b = tables(B_QK_DIM // 4)
    return cos_a, sin_a, cos_b, sin_b


def _prepare_weights(g_pre, g_post, w_in, w_out, a_q_gain, a_k_gain, b_lambda_q1, b_lambda_k1,
                     b_lambda_q2, b_lambda_k2, b_out_gain, c_gate_w_fwd, c_gate_b_fwd, c_gate_w_bwd,
                     c_gate_b_bwd, c_out_gain):
    w_in_t = jnp.swapaxes(w_in, 1, 2).astype(BF16)
    pad = jnp.zeros((DEPTH, GATE_RANK, C_KW), F32)
    cw_f = jnp.concatenate([c_gate_w_fwd, pad], axis=1).astype(BF16)
    cw_b = jnp.concatenate([pad, c_gate_w_bwd], axis=1).astype(BF16)
    return {
        "g_pre": g_pre[:, None, :],
        "g_post": g_post[:, None, :],
        "w_in_t": w_in_t,
        "w_out": w_out.astype(BF16),
        "aq_gain": a_q_gain[:, None, :],
        "aq_gain_col": a_q_gain[:, :, None],
        "ak_gain_col": a_k_gain[:, :, None],
        "cw_f": cw_f,
        "cb_f": c_gate_b_fwd[:, None, :],
        "cw_b": cw_b,
        "cb_b": c_gate_b_bwd[:, None, :],
        "b_out_gain": jnp.tile(b_out_gain, (1, 2))[:, None, :],
        "c_out_gain": jnp.tile(c_out_gain, (1, C_HEADS))[:, None, :],
        "lam_params": jnp.stack([b_lambda_q1, b_lambda_k1, b_lambda_q2, b_lambda_k2], axis=1),
    }


def kernel(x_prompt, x_sample, c, cache_a_k, cache_a_v, cache_b_k, cache_b_v, state_c_fwd, state_c_bwd, c_ctx, w_mod, b_mod, g_pre, g_post, w_in, w_out, a_q_gain, a_k_gain, b_lambda_q1, b_lambda_k1, b_lambda_q2, b_lambda_k2, b_out_gain, c_gate_w_fwd, c_gate_b_fwd, c_gate_w_bwd, c_gate_b_bwd, c_out_gain):
    dec_batch = x_sample.shape[0]
    dec_seq = x_sample.shape[1]

    mod_rows = 16
    cvec = jnp.zeros((mod_rows, D_MODEL), F32).at[0:dec_batch].set(c).at[dec_batch].set(c_ctx)
    mod = _modulation(cvec, w_mod, b_mod)[:, :, None, :]
    mod_lat = mod[:, 0:dec_batch]
    mod_ctx = mod[:, dec_batch:dec_batch + 1]

    wts = _prepare_weights(g_pre, g_post, w_in, w_out, a_q_gain, a_k_gain, b_lambda_q1, b_lambda_k1,
                           b_lambda_q2, b_lambda_k2, b_out_gain, c_gate_w_fwd, c_gate_b_fwd,
                           c_gate_w_bwd, c_gate_b_bwd, c_out_gain)
    rope_tabs = _rope_tables(dec_seq)
    cache = tuple(jnp.swapaxes(a, -1, -2) for a in (cache_a_k, cache_a_v, cache_b_k, cache_b_v))
    cache += (state_c_fwd, state_c_bwd)

    y_p, y_s = x_prompt, x_sample
    kv_ctx = None
    states = None
    for l in range(DEPTH):
        lam_init = 0.8 - 0.6 * math.exp(-0.3 * l)
        proj_p = _in_projection(y_p, mod_ctx, False, l, wts, None, 4, x_prompt.shape[1], kv_ctx)
        kv_ctx = proj_p[10:14]
        y_p, *states = _mixer(y_p, mod_ctx, False, l, proj_p, wts, lam_init, None, states)

        proj_s = _in_projection(y_s, mod_lat, True, l, wts, rope_tabs, 1, 1024)
        (y_s,) = _mixer(y_s, mod_lat, True, l, proj_s, wts, lam_init, cache)

    va_t, vb_t, ka_t, kb_t = kv_ctx
    new_kv = [jnp.swapaxes(a, -1, -2) for a in (ka_t, va_t, kb_t, vb_t)]
    return (y_p, y_s, *new_kv, *states)
```

```python
import functools
import math

import jax
import jax.numpy as jnp
from jax import lax
from jax.experimental import pallas as pl
from jax.experimental.pallas import tpu as pltpu

F32 = jnp.float32
BF16 = jnp.bfloat16

D_MODEL = 1024
DEPTH = 2
GRID_W = 64
HEAD_DIM = 64
A_HEADS = 8
A_KV_HEADS = 2
A_GROUP = A_HEADS // A_KV_HEADS
A_WIDTH = A_HEADS * HEAD_DIM
A_KV_WIDTH = A_KV_HEADS * HEAD_DIM
B_HEADS = 4
B_QK_DIM = 32
B_V_DIM = 64
B_WIDTH = B_HEADS * B_V_DIM
C_HEADS = 4
C_DK = 32
C_DV = 64
C_KW = C_HEADS * C_DK
C_WIDTH = C_HEADS * C_DV
GATE_RANK = 16
GLA_TAU = 16.0
CHUNK = 64
D_MIX = A_WIDTH + B_WIDTH + C_WIDTH
ROPE_THETA = 10000.0
EPS = 1e-6

LANES = 128
GLA_BLOCK = 256
VT_ROWS = HEAD_DIM + 16
SCORE_LOOKAHEAD = 6
SAFE_GAP = 96.0
BOUND_SLACK = 1.02
QSTAT_ROWS = 4 * B_HEADS
U_TILE = D_MIX // 4
N_MIX_PARAMS = 8
N_MIX_SEQ_INPUTS = 14
N_MIX_WEIGHTS = 5
KN2_ROWS = 32
KEY_TILE = 256
LOG2E = math.log2(math.e)

OFF_AQ = 0
OFF_AK = OFF_AQ + A_WIDTH
OFF_AV = OFF_AK + A_KV_WIDTH
OFF_BQ = OFF_AV + A_KV_WIDTH
OFF_BK = OFF_BQ + B_WIDTH
OFF_BV = OFF_BK + B_WIDTH
OFF_CQ = OFF_BV + B_WIDTH
OFF_CK = OFF_CQ + C_KW
OFF_CV = OFF_CK + C_KW
OFF_LR = OFF_CV + C_WIDTH
OFF_U = OFF_LR + 2 * GATE_RANK
IN_WIDTH = OFF_U + D_MIX

VMEM_LIMIT = 60 * 1024 * 1024


def _dot(a, b):
    return jnp.dot(a, b, preferred_element_type=F32)


def _dot_nt(a, b):
    return lax.dot_general(a, b, (((1,), (1,)), ((), ())), preferred_element_type=F32)


def _dot_tn(a, b):
    return lax.dot_general(a, b, (((0,), (0,)), ((), ())), preferred_element_type=F32)


def _split_bf16(x):
    hi = x.astype(BF16)
    lo = (x - hi.astype(F32)).astype(BF16)
    return hi, lo


def _iota(shape, dim):
    return lax.broadcasted_iota(jnp.int32, shape, dim)


def _group_mean_sq(x, group_log2):
    width = x.shape[-1]
    r = lax.shift_right_logical(_iota((width, width), 0), group_log2)
    c = lax.shift_right_logical(_iota((width, width), 1), group_log2)
    ones = jnp.where(r == c, 1.0, 0.0).astype(BF16)
    return _dot((x * x).astype(BF16), ones) * (1.0 / (1 << group_log2))


def _log_sigmoid(x):
    return jnp.minimum(x, 0.0) - jnp.log1p(jnp.exp(-jnp.abs(x)))


def _silu(x):
    return x * (1.0 / (1.0 + jnp.exp(-x)))


def _mod_kernel(c_ref, w_ref, b_ref, o_ref):
    a = _silu(c_ref[...]).astype(BF16)
    o_ref[...] = _dot(a, w_ref[...].astype(BF16)) + b_ref[...]


def _modulation(cvec, w_mod, b_mod):
    rows = cvec.shape[0]
    nblk = 3
    return pl.pallas_call(
        _mod_kernel,
        grid=(DEPTH, nblk),
        in_specs=[
            pl.BlockSpec((rows, D_MODEL), lambda l, n: (0, 0)),
            pl.BlockSpec((None, D_MODEL, D_MODEL), lambda l, n: (l, 0, n)),
            pl.BlockSpec((None, 1, D_MODEL), lambda l, n: (l, 0, n)),
        ],
        out_specs=pl.BlockSpec((None, rows, D_MODEL), lambda l, n: (l, 0, n)),
        out_shape=jax.ShapeDtypeStruct((DEPTH, rows, 3 * D_MODEL), F32),
        compiler_params=pltpu.CompilerParams(
            dimension_semantics=("arbitrary", "arbitrary"), vmem_limit_bytes=VMEM_LIMIT),
        name="modulation",
    )(cvec, w_mod, b_mod.reshape(DEPTH, 1, 3 * D_MODEL))


def _in_kernel(rope, layer, stacked_first, nb, tl, n_aliased, *refs):
    (x_ref, mod_ref, gpre_ref, wt_ref, aqg_ref, akg_ref, cwf_ref, cbf_ref, cwb_ref, cbb_ref) = refs[:10]
    refs = refs[10:]
    if rope:
        cosa_ref, sina_ref, cosb_ref, sinb_ref = refs[:4]
        refs = refs[4:]
    refs = refs[n_aliased:]
    (qa_ref, ka_ref, qb_ref, kb_ref, cq_ref, ck_ref, cv_ref, gf_ref, gb_ref, su_ref,
     vat_ref, vbt_ref) = refs[:12]
    kat_ref, kbt_ref = (None, None) if rope else refs[12:]
    qstat_ref = refs[12] if rope else None

    x = x_ref[...].reshape(nb * tl, D_MODEL)
    shift = mod_ref[:, 0:D_MODEL]
    scale = mod_ref[:, D_MODEL:2 * D_MODEL]
    ms = jnp.mean(x * x, axis=-1, keepdims=True)
    h = (x * lax.rsqrt(ms + EPS)) * gpre_ref[...] * (1.0 + scale) + shift
    hb = h.astype(BF16)

    def proj(off, width):
        return _dot_nt(hb, wt_ref[off:off + width, :])

    def proj_t(off, width):
        return _dot_nt(wt_ref[off:off + width, :], hb)

    def put_rows(ref, val):
        for bi in range(nb):
            ref[bi] = val[bi * tl:(bi + 1) * tl].astype(ref.dtype)

    def put_heads(ref, val, n_heads):
        for bi in range(nb):
            for hd in range(n_heads):
                ref[bi, hd] = val[bi * tl:(bi + 1) * tl,
                                  HEAD_DIM * hd:HEAD_DIM * (hd + 1)].astype(ref.dtype)

    def put_heads_t(ref, heads_t, stacked=False):
        for bi in range(nb):
            for hd, val_t in enumerate(heads_t):
                blk = val_t[:, bi * tl:(bi + 1) * tl].astype(ref.dtype)
                if stacked and stacked_first:
                    for l2 in range(DEPTH):
                        ref[bi, l2, hd] = blk if l2 == layer else jnp.zeros_like(blk)
                else:
                    ref[bi, hd] = blk

    def split_heads_t(val_t, n_heads):
        return [val_t[HEAD_DIM * hd:HEAD_DIM * (hd + 1)] for hd in range(n_heads)]

    def rms_t(head_t, gain_col):
        ms_h = jnp.mean(head_t * head_t, axis=0, keepdims=True)
        return head_t * lax.rsqrt(ms_h + EPS) * gain_col

    def rope_t(head_t, cos_t, sin_t, dist):
        blocks = [head_t[r0:r0 + dist] for r0 in range(0, HEAD_DIM, dist)]
        partner = jnp.concatenate([blocks[j ^ 1] for j in range(len(blocks))], axis=0)
        return head_t * cos_t + partner * sin_t

    def gate_tile(j):
        c0 = U_TILE * j
        val = _silu(proj(OFF_U + c0, U_TILE))
        for bi in range(nb):
            su_ref[bi, :, c0:c0 + U_TILE] = val[bi * tl:(bi + 1) * tl]

    za_t = proj_t(OFF_AQ, A_WIDTH + 2 * A_KV_WIDTH)
    zb_lr_t = _dot_nt(jnp.concatenate([wt_ref[OFF_BQ:OFF_BQ + 3 * B_WIDTH, :],
                                       wt_ref[OFF_LR:OFF_LR + 2 * GATE_RANK, :]], axis=0), hb)
    zb_t = zb_lr_t[0:3 * B_WIDTH]
    lr_t = zb_lr_t[3 * B_WIDTH:].astype(BF16)
    gate_tile(0)

    aq_t = [rms_t(h_t, aqg_ref[...]) for h_t in split_heads_t(za_t[0:A_WIDTH], A_HEADS)]
    ak_t = [rms_t(h_t, akg_ref[...])
            for h_t in split_heads_t(za_t[A_WIDTH:A_WIDTH + A_KV_WIDTH], A_KV_HEADS)]
    if rope:
        aq_t = [rope_t(h_t, cosa_ref[...], sina_ref[...], 16) for h_t in aq_t]
        ak_t = [rope_t(h_t, cosa_ref[...], sina_ref[...], 16) for h_t in ak_t]
    put_heads_t(qa_ref, [h_t * (HEAD_DIM ** -0.5 * LOG2E) for h_t in aq_t])
    put_heads(ka_ref, jnp.concatenate(ak_t, axis=0).T, A_KV_HEADS)
    put_heads_t(vat_ref, split_heads_t(za_t[A_WIDTH + A_KV_WIDTH:], A_KV_HEADS), stacked=not rope)
    if kat_ref is not None:
        put_heads_t(kat_ref, ak_t, stacked=True)
    gate_tile(1)
    cqk = proj(OFF_CQ, 2 * C_KW)
    cv = proj(OFF_CV, C_WIDTH)

    bq_t = split_heads_t(zb_t[0:B_WIDTH], B_HEADS)
    bk_t = split_heads_t(zb_t[B_WIDTH:2 * B_WIDTH], B_HEADS)
    if rope:
        bq_t = [rope_t(h_t, cosb_ref[...], sinb_ref[...], 8) for h_t in bq_t]
        bk_t = [rope_t(h_t, cosb_ref[...], sinb_ref[...], 8) for h_t in bk_t]
    bq_t = [h_t * (B_QK_DIM ** -0.5 * LOG2E) for h_t in bq_t]
    put_heads_t(qb_ref, bq_t)
    put_heads(kb_ref, jnp.concatenate(bk_t, axis=0).T, B_HEADS)
    if rope:
        halves = [(hd, r0) for hd in range(B_HEADS) for r0 in (0, B_QK_DIM)]
        qstat_ref[0] = jnp.concatenate(
            [jnp.sum(bq_t[hd][r0:r0 + B_QK_DIM] * bq_t[hd][r0:r0 + B_QK_DIM], axis=0, keepdims=True)
             for hd, r0 in halves]
            + [jnp.sum(bq_t[hd][r0:r0 + B_QK_DIM] * bk_t[hd][r0:r0 + B_QK_DIM], axis=0, keepdims=True)
               for hd, r0 in halves], axis=0)
    put_heads_t(vbt_ref, split_heads_t(zb_t[2 * B_WIDTH:], B_HEADS), stacked=not rope)
    if kbt_ref is not None:
        put_heads_t(kbt_ref, bk_t, stacked=True)
    gate_tile(2)
    g_pre = _dot_tn(lr_t, jnp.concatenate([cwf_ref[...], cwb_ref[...]], axis=-1))
    gf_pre, gb_pre = g_pre[:, 0:C_KW], g_pre[:, C_KW:2 * C_KW]

    put_rows(cq_ref, cqk[:, 0:C_KW] * (C_DK ** -0.5))
    put_rows(ck_ref, cqk[:, C_KW:2 * C_KW])
    put_rows(cv_ref, cv)
    put_rows(gf_ref, _log_sigmoid(gf_pre + cbf_ref[...]) * (1.0 / GLA_TAU))
    put_rows(gb_ref, _log_sigmoid(gb_pre + cbb_ref[...]) * (1.0 / GLA_TAU))
    gate_tile(3)


def _in_projection(x, mod, per_batch_mod, layer, wts, rope_tabs, nb, tl, kv_prev=None):
    bsz, seq, _ = x.shape
    rope = rope_tabs is not None
    stacked = not rope
    stacked_first = stacked and kv_prev is None
    grid = (bsz // nb, seq // tl)

    def per_layer(shape):
        return pl.BlockSpec((None,) + shape, lambda b, t: (layer,) + (0,) * len(shape))

    mod_idx = (lambda b, t: (layer, b, 0, 0)) if per_batch_mod else (lambda b, t: (layer, 0, 0, 0))
    in_specs = [
        pl.BlockSpec((nb, tl, D_MODEL), lambda b, t: (b, t, 0)),
        pl.BlockSpec((None, None, 1, 3 * D_MODEL), mod_idx),
        per_layer((1, D_MODEL)),
        pl.BlockSpec((None, IN_WIDTH, D_MODEL), lambda b, t: (layer, 0, 0),
                     pipeline_mode=pl.Buffered(1)),
        per_layer((HEAD_DIM, 1)),
        per_layer((HEAD_DIM, 1)),
        per_layer((2 * GATE_RANK, C_KW)),
        per_layer((1, C_KW)),
        per_layer((2 * GATE_RANK, C_KW)),
        per_layer((1, C_KW)),
    ]
    args = [x, mod, wts["g_pre"], wts["w_in_t"], wts["aq_gain_col"], wts["ak_gain_col"],
            wts["cw_f"], wts["cb_f"], wts["cw_b"], wts["cb_b"]]
    if rope:
        assert nb == 1
        in_specs += [
            pl.BlockSpec((HEAD_DIM, tl), lambda b, t: (0, t)),
            pl.BlockSpec((HEAD_DIM, tl), lambda b, t: (0, t)),
            pl.BlockSpec((HEAD_DIM, tl), lambda b, t: (0, t)),
            pl.BlockSpec((HEAD_DIM, tl), lambda b, t: (0, t)),
        ]
        args += list(rope_tabs)

    def heads(n):
        return pl.BlockSpec((nb, n, tl, HEAD_DIM), lambda b, t: (b, 0, t, 0))

    def heads_t(n):
        if stacked_first:
            return pl.BlockSpec((nb, DEPTH, n, HEAD_DIM, tl), lambda b, t: (b, 0, 0, 0, t))
        if stacked:
            return pl.BlockSpec((nb, None, n, HEAD_DIM, tl), lambda b, t: (b, layer, 0, 0, t))
        return pl.BlockSpec((nb, n, HEAD_DIM, tl), lambda b, t: (b, 0, 0, t))

    def q_heads_t(n):
        return pl.BlockSpec((nb, n, HEAD_DIM, tl), lambda b, t: (b, 0, 0, t))

    def rows(width):
        return pl.BlockSpec((nb, tl, width), lambda b, t: (b, t, 0))

    def hshape(n):
        return jax.ShapeDtypeStruct((bsz, n, seq, HEAD_DIM), BF16)

    def qshape(n):
        return jax.ShapeDtypeStruct((bsz, n, HEAD_DIM, seq), BF16)

    def tshape(n):
        if stacked:
            return jax.ShapeDtypeStruct((bsz, DEPTH, n, HEAD_DIM, seq), F32)
        return jax.ShapeDtypeStruct((bsz, n, HEAD_DIM, seq), BF16)

    def rshape(width):
        return jax.ShapeDtypeStruct((bsz, seq, width), F32)

    out_specs = [q_heads_t(A_HEADS), heads(A_KV_HEADS), q_heads_t(B_HEADS), heads(B_HEADS),
                 rows(C_KW), rows(C_KW), rows(C_WIDTH), rows(C_KW), rows(C_KW), rows(D_MIX),
                 heads_t(A_KV_HEADS), heads_t(B_HEADS)]
    out_shape = [qshape(A_HEADS), hshape(A_KV_HEADS), qshape(B_HEADS), hshape(B_HEADS),
                 rshape(C_KW), rshape(C_KW), rshape(C_WIDTH), rshape(C_KW), rshape(C_KW),
                 rshape(D_MIX), tshape(A_KV_HEADS), tshape(B_HEADS)]
    if stacked:
        out_specs += [heads_t(A_KV_HEADS), heads_t(B_HEADS)]
        out_shape += [tshape(A_KV_HEADS), tshape(B_HEADS)]
    else:
        out_specs += [pl.BlockSpec((nb, QSTAT_ROWS, tl), lambda b, t: (b, 0, t))]
        out_shape += [jax.ShapeDtypeStruct((bsz, QSTAT_ROWS, seq), F32)]
    aliases = {}
    if kv_prev is not None:
        for j, buf in enumerate(kv_prev):
            aliases[len(args)] = 10 + j
            in_specs.append(pl.BlockSpec(memory_space=pl.ANY))
            args.append(buf)
    return pl.pallas_call(
        functools.partial(_in_kernel, rope, layer, stacked_first, nb, tl, len(aliases)),
        grid=grid,
        in_specs=in_specs,
        out_specs=out_specs,
        out_shape=out_shape,
        input_output_aliases=aliases,
        compiler_params=pltpu.CompilerParams(
            dimension_semantics=("arbitrary", "arbitrary"), vmem_limit_bytes=VMEM_LIMIT),
        name="in_projection_rope" if rope else "in_projection",
    )(*args)


def _gla_bidirectional(cq_ref, ck_ref, cv_ref, gf_ref, gb_ref, s_f, s_b, seq, oc_ref):
    bl = GLA_BLOCK
    n_sub = bl // CHUNK
    nblk = seq // bl
    ri = _iota((bl, bl), 0)
    ci = _iota((bl, bl), 1)
    same_chunk = lax.shift_right_logical(ri, 6) == lax.shift_right_logical(ci, 6)
    bd = (lax.shift_right_logical(_iota((C_WIDTH, C_KW), 0), 6)
          == lax.shift_right_logical(_iota((C_WIDTH, C_KW), 1), 5))
    khead = lax.shift_right_logical(_iota((1, C_KW), 1), 5)
    vhead = lax.shift_right_logical(_iota((1, C_WIDTH), 1), 6)
    scans = []
    for reverse, g_ref in ((False, gf_ref), (True, gb_ref)):
        causal = same_chunk & ((ci >= ri) if reverse else (ci <= ri))
        scans.append((reverse, g_ref, causal, jnp.where(causal, 1.0, 0.0).astype(BF16)))
    states = [s_f, s_b]
    written = set()

    for step in range(nblk):
        rows0 = [step * bl, (nblk - 1 - step) * bl]
        cums = []
        for (reverse, g_ref, causal, tri), r0 in zip(scans, rows0):
            both = _dot(tri, jnp.concatenate(_split_bf16(g_ref[r0:r0 + bl, :]), axis=-1))
            cums.append(both[:, 0:C_KW] + both[:, C_KW:2 * C_KW])
        prep = []
        for (reverse, g_ref, causal, tri), r0, cum in zip(scans, rows0, cums):
            q = cq_ref[r0:r0 + bl, :]
            k = ck_ref[r0:r0 + bl, :]
            v = cv_ref[r0:r0 + bl, :]
            qt = q * jnp.exp(cum)
            ktb = (k * jnp.exp(-cum)).astype(BF16)
            vb = v.astype(BF16)
            lasts, kdecs = [], []
            for c in range(n_sub):
                c0 = CHUNK * c
                edge = c0 if reverse else c0 + CHUNK - 1
                last = cum[edge:edge + 1, :]
                lasts.append(last)
                kdecs.append((k[c0:c0 + CHUNK] * jnp.exp(last - cum[c0:c0 + CHUNK])).astype(BF16))
            prep.append((qt, ktb, v, vb, lasts, kdecs))
        scores, incs = [], []
        for qt, ktb, v, vb, lasts, kdecs in prep:
            q_heads = jnp.concatenate([jnp.where(khead == hd, qt, 0.0).astype(BF16)
                                       for hd in range(C_HEADS)], axis=0)
            s_all = _dot_nt(q_heads, ktb)
            scores.append([s_all[bl * hd:bl * (hd + 1)] for hd in range(C_HEADS)])
            incs.append([_dot_tn(vb[CHUNK * c:CHUNK * (c + 1)], kdecs[c]) for c in range(n_sub)])
        probs, entering = [], []
        for si, ((reverse, g_ref, causal, tri), (qt, ktb, v, vb, lasts, kdecs)) in enumerate(
                zip(scans, prep)):
            probs.append([jnp.where(causal, s, 0.0).astype(BF16) for s in scores[si]])
            s_t = states[si]
            before = [None] * n_sub
            subs = range(n_sub)
            for c in (reversed(subs) if reverse else subs):
                before[c] = s_t.astype(BF16)
                s_t = jnp.exp(lasts[c]) * s_t + jnp.where(bd, incs[si][c], 0.0)
            states[si] = s_t
            entering.append(before)
        outs = []
        for si, (qt, ktb, v, vb, lasts, kdecs) in enumerate(prep):
            qtb = qt.astype(BF16)
            o = jnp.concatenate([_dot_nt(qtb[CHUNK * c:CHUNK * (c + 1)], entering[si][c])
                                 for c in range(n_sub)], axis=0)
            for hd in range(C_HEADS):
                o = o + _dot(probs[si][hd], jnp.where(vhead == hd, v, 0.0).astype(BF16))
            outs.append(o)
        if rows0[0] == rows0[1]:
            outs, rows0 = [outs[0] + outs[1]], rows0[:1]
        for o, r0 in zip(outs, rows0):
            if r0 in written:
                oc_ref[r0:r0 + bl, :] = oc_ref[r0:r0 + bl, :] + o
            else:
                oc_ref[r0:r0 + bl, :] = o
                written.add(r0)
    return states[0], states[1]


def _attend_t(jobs, shifts=None):
    def scores(i):
        return _dot(jobs[i][0][...], jobs[i][1])

    outs = []
    pending = [scores(i) for i in range(min(SCORE_LOOKAHEAD, len(jobs)))]
    for i, (_, _, vt) in enumerate(jobs):
        st = pending.pop(0)
        if i + SCORE_LOOKAHEAD < len(jobs):
            pending.append(scores(i + SCORE_LOOKAHEAD))
        m = jnp.max(st, axis=0, keepdims=True) if shifts is None else shifts[i]
        p = jnp.exp2(st - m).astype(BF16)
        ot = _dot(vt[...], p)
        outs.append(ot[0:HEAD_DIM] * (1.0 / ot[HEAD_DIM:HEAD_DIM + 1]))
    return outs


def _pair_rows(a, b):
    return jnp.concatenate([a, b], axis=0).T


def _state_to_blockdiag_t(s_ref):
    rows = []
    for hd in range(C_HEADS):
        pieces = []
        if hd:
            pieces.append(jnp.zeros((C_DK, C_DV * hd), F32))
        pieces.append(s_ref[hd])
        if hd < C_HEADS - 1:
            pieces.append(jnp.zeros((C_DK, C_DV * (C_HEADS - 1 - hd)), F32))
        rows.append(jnp.concatenate(pieces, axis=-1))
    return jnp.concatenate(rows, axis=0).T


def _blockdiag_t_to_state(s_t, out_ref):
    s = s_t.T
    for hd in range(C_HEADS):
        out_ref[hd] = s[C_DK * hd:C_DK * (hd + 1), C_DV * hd:C_DV * (hd + 1)]


def _mix_sequence(cached, layer, first_state, n_aliased, lam_init, seq, qt, qb, *refs):
    it = iter(refs)
    x_ref, mod_ref = next(it), next(it)
    qa_ref, ka_ref, qb_ref, kb_ref = (next(it) for _ in range(4))
    cq_ref, ck_ref, cv_ref, gf_ref, gb_ref, su_ref = (next(it) for _ in range(6))
    vat_ref, vbt_ref = next(it), next(it)
    if cached:
        cakt_ref, cavt_ref, cbkt_ref, cbvt_ref, s0f_ref, s0b_ref = (next(it) for _ in range(6))
        qstat_ref, aqg_ref = next(it), next(it)
    wout_ref, gpost_ref, bog_ref, cog_ref, lamp_ref = (next(it) for _ in range(5))
    for _ in range(n_aliased):
        next(it)
    y_ref = next(it)
    if not cached:
        sf_ref, sb_ref = next(it), next(it)
    kA_s, vtA_s, kB_s, vtB_s, oc_s, mixed_s, kn2_s = (next(it) for _ in range(7))

    lk = kA_s.shape[1]
    past = lk - seq
    t = pl.program_id(1)

    def once_per_sequence(body):
        return body() if seq == qt else pl.when(t == 0)(body)

    @once_per_sequence
    def _per_sequence():
        ones_row = jnp.where(_iota((VT_ROWS - HEAD_DIM, lk), 0) == 0, 1.0, 0.0).astype(BF16)
        for k_new, kt_cache, k_dst, vt_new, vt_cache, vt_dst in (
                (ka_ref, cakt_ref if cached else None, kA_s, vat_ref, cavt_ref if cached else None, vtA_s),
                (kb_ref, cbkt_ref if cached else None, kB_s, vbt_ref, cbvt_ref if cached else None, vtB_s)):
            n_heads = k_dst.shape[0]
            if cached:
                for h0 in range(0, n_heads, 2):
                    pair = jnp.concatenate([kt_cache[h0], kt_cache[h0 + 1]], axis=0).T
                    k_dst[h0, 0:past, :] = pair[:, 0:HEAD_DIM].astype(BF16)
                    k_dst[h0 + 1, 0:past, :] = pair[:, HEAD_DIM:2 * HEAD_DIM].astype(BF16)
            for hd in range(n_heads):
                k_dst[hd, past:lk, :] = k_new[hd]
                if cached:
                    vt_dst[hd, 0:HEAD_DIM, 0:past] = vt_cache[hd].astype(BF16)
                vt_dst[hd, 0:HEAD_DIM, past:lk] = vt_new[hd].astype(BF16)
                vt_dst[hd, HEAD_DIM:VT_ROWS, :] = ones_row
        bounded_jobs = ([(kA_s, g) for g in range(A_KV_HEADS)] + [(kB_s, h) for h in range(B_HEADS)]
                        if lk > KEY_TILE else [])
        for j, (k_dst, hd) in enumerate(bounded_jobs):
            kf = k_dst[hd].astype(F32)
            kn2 = jnp.max(jnp.sum(kf * kf, axis=-1, keepdims=True), axis=0, keepdims=True)
            kn2_s[j:j + 1, :] = jnp.broadcast_to(kn2, (1, LANES))

        if cached:
            s0f, s0b = _state_to_blockdiag_t(s0f_ref), _state_to_blockdiag_t(s0b_ref)
        else:
            s0f = jnp.zeros((C_WIDTH, C_KW), F32)
            s0b = s0f
        s_f, s_b = _gla_bidirectional(cq_ref, ck_ref, cv_ref, gf_ref, gb_ref, s0f, s0b, seq, oc_s)
        if not cached:
            for ref, s_t in ((sf_ref, s_f), (sb_ref, s_b)):
                if first_state:
                    for l2 in range(DEPTH):
                        if l2 == layer:
                            _blockdiag_t_to_state(s_t, ref.at[l2])
                        else:
                            ref[l2] = jnp.zeros(ref.shape[1:], F32)
                else:
                    _blockdiag_t_to_state(s_t, ref)
        for r0 in range(0, seq, GLA_BLOCK):
            oc = oc_s[r0:r0 + GLA_BLOCK, :]
            oc_s[r0:r0 + GLA_BLOCK, :] = oc * lax.rsqrt(_group_mean_sq(oc, 6) + EPS) * cog_ref[...]

    lam = (jnp.exp(jnp.sum(lamp_ref[0:1, :] * lamp_ref[1:2, :], axis=-1, keepdims=True))
           - jnp.exp(jnp.sum(lamp_ref[2:3, :] * lamp_ref[3:4, :], axis=-1, keepdims=True))
           + lam_init)

    def project_out():
        seq_rows = pl.ds(pl.multiple_of(t * qt, qt), qt)
        mixed_s[:, A_WIDTH + B_WIDTH:D_MIX] = oc_s[seq_rows, :]
        gate = mod_ref[:, 2 * D_MODEL:3 * D_MODEL]
        mixed = (mixed_s[...] * su_ref[...]).astype(BF16)
        y = _dot(mixed, wout_ref[...])
        yn = y * lax.rsqrt(jnp.mean(y * y, axis=-1, keepdims=True) + EPS) * gpost_ref[...]
        y_ref[...] = x_ref[...] + gate * yn

    def attn_block(i, carry):
        rows = pl.ds(i * qb, qb) if isinstance(i, int) else pl.ds(pl.multiple_of(i * qb, qb), qb)
        jobs = []
        for grp in range(A_KV_HEADS):
            q4 = jnp.concatenate([qa_ref[A_GROUP * grp + j, :, rows] for j in range(A_GROUP)], axis=-1)
            jobs.append((kA_s.at[grp], q4, vtA_s.at[grp]))
        dim = _iota((HEAD_DIM, qb), 0)
        for hd in range(B_HEADS):
            q = qb_ref[hd, :, rows]
            zero = jnp.zeros_like(q)
            q2 = jnp.concatenate([jnp.where(dim < B_QK_DIM, q, zero),
                                  jnp.where(dim >= B_QK_DIM, q, zero)], axis=-1)
            jobs.append((kB_s.at[hd], q2, vtB_s.at[hd]))

        def finish(outs):
            for grp in range(A_KV_HEADS):
                ot = outs[grp]
                for pair in range(A_GROUP // 2):
                    c0 = 2 * pair * qb
                    col = A_GROUP * HEAD_DIM * grp + 2 * HEAD_DIM * pair
                    mixed_s[rows, col:col + 2 * HEAD_DIM] = _pair_rows(ot[:, c0:c0 + qb],
                                                                       ot[:, c0 + qb:c0 + 2 * qb])
            obs = []
            for hd in range(B_HEADS):
                ot = outs[A_KV_HEADS + hd]
                ob = ot[:, 0:qb] - lam * ot[:, qb:2 * qb]
                obs.append(ob * lax.rsqrt(jnp.mean(ob * ob, axis=0, keepdims=True) + EPS))
            for pair in range(B_HEADS // 2):
                col = A_WIDTH + 2 * B_V_DIM * pair
                mixed_s[rows, col:col + 2 * B_V_DIM] = (_pair_rows(obs[2 * pair], obs[2 * pair + 1])
                                                        * bog_ref[...] * (1.0 - lam_init))
            if qt == qb:
                project_out()

        if lk <= KEY_TILE:
            finish(_attend_t(jobs))
            return carry

        qa_norm = (jnp.max(jnp.abs(aqg_ref[...]), axis=-1, keepdims=True)
                   * (HEAD_DIM ** 0.5 * HEAD_DIM ** -0.5 * LOG2E))
        stat_t = qstat_ref[:, rows]
        shifts, gaps = [], []
        for j, (_, q, _) in enumerate(jobs):
            k_norm = jnp.sqrt(kn2_s[j:j + 1, 0:1])
            if j < A_KV_HEADS:
                upper = qa_norm * k_norm * BOUND_SLACK
                shifts.append(jnp.broadcast_to(upper, (1, q.shape[1])))
                gaps.append(2.0 * upper)
            else:
                r0 = 2 * (j - A_KV_HEADS)
                r1 = r0 + 2 * B_HEADS
                qn2 = jnp.concatenate([stat_t[r0:r0 + 1], stat_t[r0 + 1:r0 + 2]], axis=-1)
                lower = jnp.concatenate([stat_t[r1:r1 + 1], stat_t[r1 + 1:r1 + 2]], axis=-1)
                upper = jnp.sqrt(qn2) * k_norm * BOUND_SLACK
                shifts.append(upper)
                gaps.append(jnp.max(upper - lower, axis=-1, keepdims=True))
        worst = functools.reduce(jnp.maximum, gaps)
        safe = worst[0, 0] <= SAFE_GAP

        @pl.when(safe)
        def _bounded():
            finish(_attend_t(jobs, shifts))

        @pl.when(jnp.logical_not(safe))
        def _exact_max():
            finish(_attend_t(jobs))

        return carry

    if qt != qb:
        for i in range(qt // qb):
            attn_block(i, 0)
        project_out()
    else:
        attn_block(0, 0)


def _mix_kernel(nseq, *params_and_refs):
    params, refs = params_and_refs[:N_MIX_PARAMS], params_and_refs[N_MIX_PARAMS:]
    if nseq == 1:
        return _mix_sequence(*params, *refs)
    n_aliased = params[3]
    shared = {1} | set(range(N_MIX_SEQ_INPUTS, N_MIX_SEQ_INPUTS + N_MIX_WEIGHTS + n_aliased))
    for bi in range(nseq):
        _mix_sequence(*params, *[r if idx in shared else r.at[bi] for idx, r in enumerate(refs)])


def _mixer(x, mod, per_batch_mod, layer, proj, wts, lam_init, cache, state_prev=None):
    bsz, seq, _ = x.shape
    cached = cache is not None
    past = cache[0].shape[4] if cached else 0
    qa, ka, qb, kb, cq, ck, cv, gf, gb, su, vat, vbt = proj[:12]
    qt = 512 if cached else 256
    qb_rows = 256
    nseq = 1 if cached else 4
    lead = None if nseq == 1 else nseq

    def per_layer(shape):
        return pl.BlockSpec((None,) + shape, lambda b, t: (layer,) + (0,) * len(shape))

    def layer_heads_t(n, length):
        return pl.BlockSpec((lead, None, n, HEAD_DIM, length), lambda b, t: (b, layer, 0, 0, 0))

    def heads_t(n):
        if cached:
            return pl.BlockSpec((lead, n, HEAD_DIM, seq), lambda b, t: (b, 0, 0, 0))
        return layer_heads_t(n, seq)

    def heads(n):
        return pl.BlockSpec((lead, n, seq, HEAD_DIM), lambda b, t: (b, 0, 0, 0))

    def head_tile(n):
        return pl.BlockSpec((lead, n, HEAD_DIM, qt), lambda b, t: (b, 0, 0, t))

    def rows(width):
        return pl.BlockSpec((lead, seq, width), lambda b, t: (b, 0, 0))

    def row_tile(width):
        return pl.BlockSpec((lead, qt, width), lambda b, t: (b, t, 0))

    mod_idx = (lambda b, t: (layer, b, 0, 0)) if per_batch_mod else (lambda b, t: (layer, 0, 0, 0))
    in_specs = [row_tile(D_MODEL), pl.BlockSpec((None, None, 1, 3 * D_MODEL), mod_idx),
                head_tile(A_HEADS), heads(A_KV_HEADS), head_tile(B_HEADS), heads(B_HEADS),
                rows(C_KW), rows(C_KW), rows(C_WIDTH), rows(C_KW), rows(C_KW), row_tile(D_MIX),
                heads_t(A_KV_HEADS), heads_t(B_HEADS)]
    args = [x, mod, qa, ka, qb, kb, cq, ck, cv, gf, gb, su, vat, vbt]
    if cached:
        state_in = pl.BlockSpec((None, None, C_HEADS, C_DK, C_DV), lambda b, t: (b, layer, 0, 0, 0))
        in_specs += [layer_heads_t(A_KV_HEADS, past), layer_heads_t(A_KV_HEADS, past),
                     layer_heads_t(B_HEADS, past), layer_heads_t(B_HEADS, past), state_in, state_in,
                     pl.BlockSpec((None, QSTAT_ROWS, qt), lambda b, t: (b, 0, t)),
                     per_layer((1, HEAD_DIM))]
        args += list(cache) + [proj[12], wts["aq_gain"]]
    in_specs += [pl.BlockSpec((None, D_MIX, D_MODEL), lambda b, t: (layer, 0, 0),
                              pipeline_mode=pl.Buffered(1)),
                 per_layer((1, D_MODEL)), per_layer((1, 2 * B_V_DIM)),
                 per_layer((1, C_WIDTH)), per_layer((4, B_QK_DIM))]
    args += [wts["w_out"], wts["g_post"], wts["b_out_gain"], wts["c_out_gain"], wts["lam_params"]]

    out_specs = [row_tile(D_MODEL)]
    out_shape = [jax.ShapeDtypeStruct((bsz, seq, D_MODEL), F32)]
    aliases = {}
    first_state = not cached and state_prev is None
    if not cached:
        if first_state:
            state_out = pl.BlockSpec((lead, DEPTH, C_HEADS, C_DK, C_DV), lambda b, t: (b, 0, 0, 0, 0))
        else:
            state_out = pl.BlockSpec((lead, None, C_HEADS, C_DK, C_DV),
                                     lambda b, t: (b, layer, 0, 0, 0))
            for j, buf in enumerate(state_prev):
                aliases[len(args)] = 1 + j
                in_specs.append(pl.BlockSpec(memory_space=pl.ANY))
                args.append(buf)
        out_specs += [state_out, state_out]
        out_shape += [jax.ShapeDtypeStruct((bsz, DEPTH, C_HEADS, C_DK, C_DV), F32)] * 2

    lk = past + seq
    per_seq = () if nseq == 1 else (nseq,)
    scratch = [pltpu.VMEM(per_seq + shape, dtype) for shape, dtype in (
        ((A_KV_HEADS, lk, HEAD_DIM), BF16), ((A_KV_HEADS, VT_ROWS, lk), BF16),
        ((B_HEADS, lk, HEAD_DIM), BF16), ((B_HEADS, VT_ROWS, lk), BF16),
        ((seq, C_WIDTH), F32), ((qt, D_MIX), F32), ((KN2_ROWS, LANES), F32))]
    return pl.pallas_call(
        functools.partial(_mix_kernel, nseq, cached, layer, first_state, len(aliases), lam_init, seq,
                          qt, qb_rows),
        grid=(bsz // nseq, seq // qt),
        in_specs=in_specs,
        out_specs=out_specs,
        out_shape=out_shape,
        scratch_shapes=scratch,
        input_output_aliases=aliases,
        compiler_params=pltpu.CompilerParams(
            dimension_semantics=("arbitrary", "arbitrary"), vmem_limit_bytes=VMEM_LIMIT),
        name="mixer_cached" if cached else "mixer",
    )(*args)


def _rope_tables(seq):
    t = jnp.arange(seq)
    pos_row = (t // GRID_W).astype(F32)
    pos_col = (t % GRID_W).astype(F32)

    def tables(half):
        freq = ROPE_THETA ** (-jnp.arange(half, dtype=F32) / half)
        ang_r = freq[:, None] * pos_row[None, :]
        ang_c = freq[:, None] * pos_col[None, :]
        cos = jnp.concatenate([jnp.cos(ang_r), jnp.cos(ang_r), jnp.cos(ang_c), jnp.cos(ang_c)], axis=0)
        sin = jnp.concatenate([-jnp.sin(ang_r), jnp.sin(ang_r), -jnp.sin(ang_c), jnp.sin(ang_c)], axis=0)
        reps = HEAD_DIM // (4 * half)
        return jnp.tile(cos, (reps, 1)), jnp.tile(sin, (reps, 1))

    cos_a, sin_a = tables(HEAD_DIM // 4)
    cos_b, sin_b = tables(B_QK_DIM // 4)
    return cos_a, sin_a, cos_b, sin_b


def _prepare_weights(g_pre, g_post, w_in, w_out, a_q_gain, a_k_gain, b_lambda_q1, b_lambda_k1,
                     b_lambda_q2, b_lambda_k2, b_out_gain, c_gate_w_fwd, c_gate_b_fwd, c_gate_w_bwd,
                     c_gate_b_bwd, c_out_gain):
    w_in_t = jnp.swapaxes(w_in, 1, 2).astype(BF16)
    pad = jnp.zeros((DEPTH, GATE_RANK, C_KW), F32)
    cw_f = jnp.concatenate([c_gate_w_fwd, pad], axis=1).astype(BF16)
    cw_b = jnp.concatenate([pad, c_gate_w_bwd], axis=1).astype(BF16)
    return {
        "g_pre": g_pre[:, None, :],
        "g_post": g_post[:, None, :],
        "w_in_t": w_in_t,
        "w_out": w_out.astype(BF16),
        "aq_gain": a_q_gain[:, None, :],
        "aq_gain_col": a_q_gain[:, :, None],
        "ak_gain_col": a_k_gain[:, :, None],
        "cw_f": cw_f,
        "cb_f": c_gate_b_fwd[:, None, :],
        "cw_b": cw_b,
        "cb_b": c_gate_b_bwd[:, None, :],
        "b_out_gain": jnp.tile(b_out_gain, (1, 2))[:, None, :],
        "c_out_gain": jnp.tile(c_out_gain, (1, C_HEADS))[:, None, :],
        "lam_params": jnp.stack([b_lambda_q1, b_lambda_k1, b_lambda_q2, b_lambda_k2], axis=1),
    }


def kernel(x_prompt, x_sample, c, cache_a_k, cache_a_v, cache_b_k, cache_b_v, state_c_fwd, state_c_bwd, c_ctx, w_mod, b_mod, g_pre, g_post, w_in, w_out, a_q_gain, a_k_gain, b_lambda_q1, b_lambda_k1, b_lambda_q2, b_lambda_k2, b_out_gain, c_gate_w_fwd, c_gate_b_fwd, c_gate_w_bwd, c_gate_b_bwd, c_out_gain):
    dec_batch = x_sample.shape[0]
    dec_seq = x_sample.shape[1]

    mod_rows = 16
    cvec = jnp.zeros((mod_rows, D_MODEL), F32).at[0:dec_batch].set(c).at[dec_batch].set(c_ctx)
    mod = _modulation(cvec, w_mod, b_mod)[:, :, None, :]
    mod_lat = mod[:, 0:dec_batch]
    mod_ctx = mod[:, dec_batch:dec_batch + 1]

    wts = _prepare_weights(g_pre, g_post, w_in, w_out, a_q_gain, a_k_gain, b_lambda_q1, b_lambda_k1,
                           b_lambda_q2, b_lambda_k2, b_out_gain, c_gate_w_fwd, c_gate_b_fwd,
                           c_gate_w_bwd, c_gate_b_bwd, c_out_gain)
    rope_tabs = _rope_tables(dec_seq)
    cache = tuple(jnp.swapaxes(a, -1, -2) for a in (cache_a_k, cache_a_v, cache_b_k, cache_b_v))
    cache += (state_c_fwd, state_c_bwd)

    y_p, y_s = x_prompt, x_sample
    kv_ctx = None
    states = None
    for l in range(DEPTH):
        lam_init = 0.8 - 0.6 * math.exp(-0.3 * l)
        proj_p = _in_projection(y_p, mod_ctx, False, l, wts, None, 4, x_prompt.shape[1], kv_ctx)
        kv_ctx = proj_p[10:14]
        y_p, *states = _mixer(y_p, mod_ctx, False, l, proj_p, wts, lam_init, None, states)

        proj_s = _in_projection(y_s, mod_lat, True, l, wts, rope_tabs, 1, 1024)
        (y_s,) = _mixer(y_s, mod_lat, True, l, proj_s, wts, lam_init, cache)

    va_t, vb_t, ka_t, kb_t = kv_ctx
    new_kv = [jnp.swapaxes(a, -1, -2) for a in (ka_t, va_t, kb_t, vb_t)]
    return (y_p, y_s, *new_kv, *states)
```

```python
import functools
import math

import jax
import jax.numpy as jnp
from jax import lax
from jax.experimental import pallas as pl
from jax.experimental.pallas import tpu as pltpu

F32 = jnp.float32
BF16 = jnp.bfloat16

D_MODEL = 1024
DEPTH = 2
GRID_W = 64
HEAD_DIM = 64
A_HEADS = 8
A_KV_HEADS = 2
A_GROUP = A_HEADS // A_KV_HEADS
A_WIDTH = A_HEADS * HEAD_DIM
A_KV_WIDTH = A_KV_HEADS * HEAD_DIM
B_HEADS = 4
B_QK_DIM = 32
B_V_DIM = 64
B_WIDTH = B_HEADS * B_V_DIM
C_HEADS = 4
C_DK = 32
C_DV = 64
C_KW = C_HEADS * C_DK
C_WIDTH = C_HEADS * C_DV
GATE_RANK = 16
GLA_TAU = 16.0
CHUNK = 64
D_MIX = A_WIDTH + B_WIDTH + C_WIDTH
ROPE_THETA = 10000.0
EPS = 1e-6

LANES = 128
GLA_BLOCK = 256
VT_ROWS = HEAD_DIM + 16
SCORE_LOOKAHEAD = 6
SAFE_GAP = 96.0
BOUND_SLACK = 1.02
QSTAT_ROWS = 4 * B_HEADS
U_TILE = D_MIX // 4
N_MIX_PARAMS = 8
N_MIX_SEQ_INPUTS = 14
N_MIX_WEIGHTS = 5
KN2_ROWS = 32
KEY_TILE = 256
LOG2E = math.log2(math.e)

OFF_AQ = 0
OFF_AK = OFF_AQ + A_WIDTH
OFF_AV = OFF_AK + A_KV_WIDTH
OFF_BQ = OFF_AV + A_KV_WIDTH
OFF_BK = OFF_BQ + B_WIDTH
OFF_BV = OFF_BK + B_WIDTH
OFF_CQ = OFF_BV + B_WIDTH
OFF_CK = OFF_CQ + C_KW
OFF_CV = OFF_CK + C_KW
OFF_LR = OFF_CV + C_WIDTH
OFF_U = OFF_LR + 2 * GATE_RANK
IN_WIDTH = OFF_U + D_MIX

VMEM_LIMIT = 60 * 1024 * 1024


def _dot(a, b):
    return jnp.dot(a, b, preferred_element_type=F32)


def _dot_nt(a, b):
    return lax.dot_general(a, b, (((1,), (1,)), ((), ())), preferred_element_type=F32)


def _dot_tn(a, b):
    return lax.dot_general(a, b, (((0,), (0,)), ((), ())), preferred_element_type=F32)


def _split_bf16(x):
    hi = x.astype(BF16)
    lo = (x - hi.astype(F32)).astype(BF16)
    return hi, lo


def _iota(shape, dim):
    return lax.broadcasted_iota(jnp.int32, shape, dim)


def _group_mean_sq(x, group_log2):
    width = x.shape[-1]
    r = lax.shift_right_logical(_iota((width, width), 0), group_log2)
    c = lax.shift_right_logical(_iota((width, width), 1), group_log2)
    ones = jnp.where(r == c, 1.0, 0.0).astype(BF16)
    return _dot((x * x).astype(BF16), ones) * (1.0 / (1 << group_log2))


def _log_sigmoid(x):
    return jnp.minimum(x, 0.0) - jnp.log1p(jnp.exp(-jnp.abs(x)))


def _silu(x):
    return x * (1.0 / (1.0 + jnp.exp(-x)))


def _mod_kernel(c_ref, w_ref, b_ref, o_ref):
    a = _silu(c_ref[...]).astype(BF16)
    o_ref[...] = _dot(a, w_ref[...].astype(BF16)) + b_ref[...]


def _modulation(cvec, w_mod, b_mod):
    rows = cvec.shape[0]
    nblk = 3
    return pl.pallas_call(
        _mod_kernel,
        grid=(DEPTH, nblk),
        in_specs=[
            pl.BlockSpec((rows, D_MODEL), lambda l, n: (0, 0)),
            pl.BlockSpec((None, D_MODEL, D_MODEL), lambda l, n: (l, 0, n)),
            pl.BlockSpec((None, 1, D_MODEL), lambda l, n: (l, 0, n)),
        ],
        out_specs=pl.BlockSpec((None, rows, D_MODEL), lambda l, n: (l, 0, n)),
        out_shape=jax.ShapeDtypeStruct((DEPTH, rows, 3 * D_MODEL), F32),
        compiler_params=pltpu.CompilerParams(
            dimension_semantics=("arbitrary", "arbitrary"), vmem_limit_bytes=VMEM_LIMIT),
        name="modulation",
    )(cvec, w_mod, b_mod.reshape(DEPTH, 1, 3 * D_MODEL))


def _in_kernel(rope, layer, stacked_first, nb, tl, n_aliased, *refs):
    (x_ref, mod_ref, gpre_ref, wt_ref, aqg_ref, akg_ref, cwf_ref, cbf_ref, cwb_ref, cbb_ref) = refs[:10]
    refs = refs[10:]
    if rope:
        cosa_ref, sina_ref, cosb_ref, sinb_ref = refs[:4]
        refs = refs[4:]
    refs = refs[n_aliased:]
    (qa_ref, ka_ref, qb_ref, kb_ref, cq_ref, ck_ref, cv_ref, gf_ref, gb_ref, su_ref,
     vat_ref, vbt_ref) = refs[:12]
    kat_ref, kbt_ref = (None, None) if rope else refs[12:]
    qstat_ref = refs[12] if rope else None

    x = x_ref[...].reshape(nb * tl, D_MODEL)
    shift = mod_ref[:, 0:D_MODEL]
    scale = mod_ref[:, D_MODEL:2 * D_MODEL]
    ms = jnp.mean(x * x, axis=-1, keepdims=True)
    h = (x * lax.rsqrt(ms + EPS)) * gpre_ref[...] * (1.0 + scale) + shift
    hb = h.astype(BF16)

    def proj(off, width):
        return _dot_nt(hb, wt_ref[off:off + width, :])

    def proj_t(off, width):
        return _dot_nt(wt_ref[off:off + width, :], hb)

    def put_rows(ref, val):
        for bi in range(nb):
            ref[bi] = val[bi * tl:(bi + 1) * tl].astype(ref.dtype)

    def put_heads(ref, val, n_heads):
        for bi in range(nb):
            for hd in range(n_heads):
                ref[bi, hd] = val[bi * tl:(bi + 1) * tl,
                                  HEAD_DIM * hd:HEAD_DIM * (hd + 1)].astype(ref.dtype)

    def put_heads_t(ref, heads_t, stacked=False):
        for bi in range(nb):
            for hd, val_t in enumerate(heads_t):
                blk = val_t[:, bi * tl:(bi + 1) * tl].astype(ref.dtype)
                if stacked and stacked_first:
                    for l2 in range(DEPTH):
                        ref[bi, l2, hd] = blk if l2 == layer else jnp.zeros_like(blk)
                else:
                    ref[bi, hd] = blk

    def split_heads_t(val_t, n_heads):
        return [val_t[HEAD_DIM * hd:HEAD_DIM * (hd + 1)] for hd in range(n_heads)]

    def rms_t(head_t, gain_col):
        ms_h = jnp.mean(head_t * head_t, axis=0, keepdims=True)
        return head_t * lax.rsqrt(ms_h + EPS) * gain_col

    def rope_t(head_t, cos_t, sin_t, dist):
        blocks = [head_t[r0:r0 + dist] for r0 in range(0, HEAD_DIM, dist)]
        partner = jnp.concatenate([blocks[j ^ 1] for j in range(len(blocks))], axis=0)
        return head_t * cos_t + partner * sin_t

    def gate_tile(j):
        c0 = U_TILE * j
        val = _silu(proj(OFF_U + c0, U_TILE))
        for bi in range(nb):
            su_ref[bi, :, c0:c0 + U_TILE] = val[bi * tl:(bi + 1) * tl]

    za_t = proj_t(OFF_AQ, A_WIDTH + 2 * A_KV_WIDTH)
    gate_tile(0)
    zb_lr_t = _dot_nt(jnp.concatenate([wt_ref[OFF_BQ:OFF_BQ + 3 * B_WIDTH, :],
                                       wt_ref[OFF_LR:OFF_LR + 2 * GATE_RANK, :]], axis=0), hb)
    zb_t = zb_lr_t[0:3 * B_WIDTH]
    lr_t = zb_lr_t[3 * B_WIDTH:].astype(BF16)

    aq_t = [rms_t(h_t, aqg_ref[...]) for h_t in split_heads_t(za_t[0:A_WIDTH], A_HEADS)]
    ak_t = [rms_t(h_t, akg_ref[...])
            for h_t in split_heads_t(za_t[A_WIDTH:A_WIDTH + A_KV_WIDTH], A_KV_HEADS)]
    if rope:
        aq_t = [rope_t(h_t, cosa_ref[...], sina_ref[...], 16) for h_t in aq_t]
        ak_t = [rope_t(h_t, cosa_ref[...], sina_ref[...], 16) for h_t in ak_t]
    put_heads_t(qa_ref, [h_t * (HEAD_DIM ** -0.5 * LOG2E) for h_t in aq_t])
    put_heads(ka_ref, jnp.concatenate(ak_t, axis=0).T, A_KV_HEADS)
    put_heads_t(vat_ref, split_heads_t(za_t[A_WIDTH + A_KV_WIDTH:], A_KV_HEADS), stacked=not rope)
    if kat_ref is not None:
        put_heads_t(kat_ref, ak_t, stacked=True)
    gate_tile(1)
    cqk = proj(OFF_CQ, 2 * C_KW)
    cv = proj(OFF_CV, C_WIDTH)

    bq_t = split_heads_t(zb_t[0:B_WIDTH], B_HEADS)
    bk_t = split_heads_t(zb_t[B_WIDTH:2 * B_WIDTH], B_HEADS)
    if rope:
        bq_t = [rope_t(h_t, cosb_ref[...], sinb_ref[...], 8) for h_t in bq_t]
        bk_t = [rope_t(h_t, cosb_ref[...], sinb_ref[...], 8) for h_t in bk_t]
    bq_t = [h_t * (B_QK_DIM ** -0.5 * LOG2E) for h_t in bq_t]
    put_heads_t(qb_ref, bq_t)
    put_heads(kb_ref, jnp.concatenate(bk_t, axis=0).T, B_HEADS)
    if rope:
        halves = [(hd, r0) for hd in range(B_HEADS) for r0 in (0, B_QK_DIM)]
        qstat_ref[0] = jnp.concatenate(
            [jnp.sum(bq_t[hd][r0:r0 + B_QK_DIM] * bq_t[hd][r0:r0 + B_QK_DIM], axis=0, keepdims=True)
             for hd, r0 in halves]
            + [jnp.sum(bq_t[hd][r0:r0 + B_QK_DIM] * bk_t[hd][r0:r0 + B_QK_DIM], axis=0, keepdims=True)
               for hd, r0 in halves], axis=0)
    put_heads_t(vbt_ref, split_heads_t(zb_t[2 * B_WIDTH:], B_HEADS), stacked=not rope)
    if kbt_ref is not None:
        put_heads_t(kbt_ref, bk_t, stacked=True)
    gate_tile(2)
    g_pre = _dot_tn(lr_t, jnp.concatenate([cwf_ref[...], cwb_ref[...]], axis=-1))
    gf_pre, gb_pre = g_pre[:, 0:C_KW], g_pre[:, C_KW:2 * C_KW]

    put_rows(cq_ref, cqk[:, 0:C_KW] * (C_DK ** -0.5))
    put_rows(ck_ref, cqk[:, C_KW:2 * C_KW])
    put_rows(cv_ref, cv)
    put_rows(gf_ref, _log_sigmoid(gf_pre + cbf_ref[...]) * (1.0 / GLA_TAU))
    put_rows(gb_ref, _log_sigmoid(gb_pre + cbb_ref[...]) * (1.0 / GLA_TAU))
    gate_tile(3)


def _in_projection(x, mod, per_batch_mod, layer, wts, rope_tabs, nb, tl, kv_prev=None):
    bsz, seq, _ = x.shape
    rope = rope_tabs is not None
    stacked = not rope
    stacked_first = stacked and kv_prev is None
    grid = (bsz // nb, seq // tl)

    def per_layer(shape):
        return pl.BlockSpec((None,) + shape, lambda b, t: (layer,) + (0,) * len(shape))

    mod_idx = (lambda b, t: (layer, b, 0, 0)) if per_batch_mod else (lambda b, t: (layer, 0, 0, 0))
    in_specs = [
        pl.BlockSpec((nb, tl, D_MODEL), lambda b, t: (b, t, 0)),
        pl.BlockSpec((None, None, 1, 3 * D_MODEL), mod_idx),
        per_layer((1, D_MODEL)),
        pl.BlockSpec((None, IN_WIDTH, D_MODEL), lambda b, t: (layer, 0, 0),
                     pipeline_mode=pl.Buffered(1)),
        per_layer((HEAD_DIM, 1)),
        per_layer((HEAD_DIM, 1)),
        per_layer((2 * GATE_RANK, C_KW)),
        per_layer((1, C_KW)),
        per_layer((2 * GATE_RANK, C_KW)),
        per_layer((1, C_KW)),
    ]
    args = [x, mod, wts["g_pre"], wts["w_in_t"], wts["aq_gain_col"], wts["ak_gain_col"],
            wts["cw_f"], wts["cb_f"], wts["cw_b"], wts["cb_b"]]
    if rope:
        assert nb == 1
        in_specs += [
            pl.BlockSpec((HEAD_DIM, tl), lambda b, t: (0, t)),
            pl.BlockSpec((HEAD_DIM, tl), lambda b, t: (0, t)),
            pl.BlockSpec((HEAD_DIM, tl), lambda b, t: (0, t)),
            pl.BlockSpec((HEAD_DIM, tl), lambda b, t: (0, t)),
        ]
        args += list(rope_tabs)

    def heads(n):
        return pl.BlockSpec((nb, n, tl, HEAD_DIM), lambda b, t: (b, 0, t, 0))

    def heads_t(n):
        if stacked_first:
            return pl.BlockSpec((nb, DEPTH, n, HEAD_DIM, tl), lambda b, t: (b, 0, 0, 0, t))
        if stacked:
            return pl.BlockSpec((nb, None, n, HEAD_DIM, tl), lambda b, t: (b, layer, 0, 0, t))
        return pl.BlockSpec((nb, n, HEAD_DIM, tl), lambda b, t: (b, 0, 0, t))

    def q_heads_t(n):
        return pl.BlockSpec((nb, n, HEAD_DIM, tl), lambda b, t: (b, 0, 0, t))

    def rows(width):
        return pl.BlockSpec((nb, tl, width), lambda b, t: (b, t, 0))

    def hshape(n):
        return jax.ShapeDtypeStruct((bsz, n, seq, HEAD_DIM), BF16)

    def qshape(n):
        return jax.ShapeDtypeStruct((bsz, n, HEAD_DIM, seq), BF16)

    def tshape(n):
        if stacked:
            return jax.ShapeDtypeStruct((bsz, DEPTH, n, HEAD_DIM, seq), F32)
        return jax.ShapeDtypeStruct((bsz, n, HEAD_DIM, seq), BF16)

    def rshape(width):
        return jax.ShapeDtypeStruct((bsz, seq, width), F32)

    out_specs = [q_heads_t(A_HEADS), heads(A_KV_HEADS), q_heads_t(B_HEADS), heads(B_HEADS),
                 rows(C_KW), rows(C_KW), rows(C_WIDTH), rows(C_KW), rows(C_KW), rows(D_MIX),
                 heads_t(A_KV_HEADS), heads_t(B_HEADS)]
    out_shape = [qshape(A_HEADS), hshape(A_KV_HEADS), qshape(B_HEADS), hshape(B_HEADS),
                 rshape(C_KW), rshape(C_KW), rshape(C_WIDTH), rshape(C_KW), rshape(C_KW),
                 rshape(D_MIX), tshape(A_KV_HEADS), tshape(B_HEADS)]
    if stacked:
        out_specs += [heads_t(A_KV_HEADS), heads_t(B_HEADS)]
        out_shape += [tshape(A_KV_HEADS), tshape(B_HEADS)]
    else:
        out_specs += [pl.BlockSpec((nb, QSTAT_ROWS, tl), lambda b, t: (b, 0, t))]
        out_shape += [jax.ShapeDtypeStruct((bsz, QSTAT_ROWS, seq), F32)]
    aliases = {}
    if kv_prev is not None:
        for j, buf in enumerate(kv_prev):
            aliases[len(args)] = 10 + j
            in_specs.append(pl.BlockSpec(memory_space=pl.ANY))
            args.append(buf)
    return pl.pallas_call(
        functools.partial(_in_kernel, rope, layer, stacked_first, nb, tl, len(aliases)),
        grid=grid,
        in_specs=in_specs,
        out_specs=out_specs,
        out_shape=out_shape,
        input_output_aliases=aliases,
        compiler_params=pltpu.CompilerParams(
            dimension_semantics=("arbitrary", "arbitrary"), vmem_limit_bytes=VMEM_LIMIT),
        name="in_projection_rope" if rope else "in_projection",
    )(*args)


def _gla_bidirectional(cq_ref, ck_ref, cv_ref, gf_ref, gb_ref, s_f, s_b, seq, oc_ref):
    bl = GLA_BLOCK
    n_sub = bl // CHUNK
    nblk = seq // bl
    ri = _iota((bl, bl), 0)
    ci = _iota((bl, bl), 1)
    same_chunk = lax.shift_right_logical(ri, 6) == lax.shift_right_logical(ci, 6)
    bd = (lax.shift_right_logical(_iota((C_WIDTH, C_KW), 0), 6)
          == lax.shift_right_logical(_iota((C_WIDTH, C_KW), 1), 5))
    khead = lax.shift_right_logical(_iota((1, C_KW), 1), 5)
    vhead = lax.shift_right_logical(_iota((1, C_WIDTH), 1), 6)
    scans = []
    for reverse, g_ref in ((False, gf_ref), (True, gb_ref)):
        causal = same_chunk & ((ci >= ri) if reverse else (ci <= ri))
        scans.append((reverse, g_ref, causal, jnp.where(causal, 1.0, 0.0).astype(BF16)))
    states = [s_f, s_b]
    written = set()

    for step in range(nblk):
        rows0 = [step * bl, (nblk - 1 - step) * bl]
        cums = []
        for (reverse, g_ref, causal, tri), r0 in zip(scans, rows0):
            both = _dot(tri, jnp.concatenate(_split_bf16(g_ref[r0:r0 + bl, :]), axis=-1))
            cums.append(both[:, 0:C_KW] + both[:, C_KW:2 * C_KW])
        prep = []
        for (reverse, g_ref, causal, tri), r0, cum in zip(scans, rows0, cums):
            q = cq_ref[r0:r0 + bl, :]
            k = ck_ref[r0:r0 + bl, :]
            v = cv_ref[r0:r0 + bl, :]
            qt = q * jnp.exp(cum)
            ktb = (k * jnp.exp(-cum)).astype(BF16)
            vb = v.astype(BF16)
            lasts, kdecs = [], []
            for c in range(n_sub):
                c0 = CHUNK * c
                edge = c0 if reverse else c0 + CHUNK - 1
                last = cum[edge:edge + 1, :]
                lasts.append(last)
                kdecs.append((k[c0:c0 + CHUNK] * jnp.exp(last - cum[c0:c0 + CHUNK])).astype(BF16))
            prep.append((qt, ktb, v, vb, lasts, kdecs))
        scores, incs = [], []
        for qt, ktb, v, vb, lasts, kdecs in prep:
            q_heads = jnp.concatenate([jnp.where(khead == hd, qt, 0.0).astype(BF16)
                                       for hd in range(C_HEADS)], axis=0)
            s_all = _dot_nt(q_heads, ktb)
            scores.append([s_all[bl * hd:bl * (hd + 1)] for hd in range(C_HEADS)])
            incs.append([_dot_tn(vb[CHUNK * c:CHUNK * (c + 1)], kdecs[c]) for c in range(n_sub)])
        probs, entering = [], []
        for si, ((reverse, g_ref, causal, tri), (qt, ktb, v, vb, lasts, kdecs)) in enumerate(
                zip(scans, prep)):
            probs.append([jnp.where(causal, s, 0.0).astype(BF16) for s in scores[si]])
            s_t = states[si]
            before = [None] * n_sub
            subs = range(n_sub)
            for c in (reversed(subs) if reverse else subs):
                before[c] = s_t.astype(BF16)
                s_t = jnp.exp(lasts[c]) * s_t + jnp.where(bd, incs[si][c], 0.0)
            states[si] = s_t
            entering.append(before)
        outs = []
        for si, (qt, ktb, v, vb, lasts, kdecs) in enumerate(prep):
            qtb = qt.astype(BF16)
            o = jnp.concatenate([_dot_nt(qtb[CHUNK * c:CHUNK * (c + 1)], entering[si][c])
                                 for c in range(n_sub)], axis=0)
            for hd in range(C_HEADS):
                o = o + _dot(probs[si][hd], jnp.where(vhead == hd, v, 0.0).astype(BF16))
            outs.append(o)
        if rows0[0] == rows0[1]:
            outs, rows0 = [outs[0] + outs[1]], rows0[:1]
        for o, r0 in zip(outs, rows0):
            if r0 in written:
                oc_ref[r0:r0 + bl, :] = oc_ref[r0:r0 + bl, :] + o
            else:
                oc_ref[r0:r0 + bl, :] = o
                written.add(r0)
    return states[0], states[1]


def _attend_t(jobs, shifts=None):
    def scores(i):
        return _dot(jobs[i][0][...], jobs[i][1])

    outs = []
    pending = [scores(i) for i in range(min(SCORE_LOOKAHEAD, len(jobs)))]
    for i, (_, _, vt) in enumerate(jobs):
        st = pending.pop(0)
        if i + SCORE_LOOKAHEAD < len(jobs):
            pending.append(scores(i + SCORE_LOOKAHEAD))
        m = jnp.max(st, axis=0, keepdims=True) if shifts is None else shifts[i]
        p = jnp.exp2(st - m).astype(BF16)
        ot = _dot(vt[...], p)
        outs.append(ot[0:HEAD_DIM] * (1.0 / ot[HEAD_DIM:HEAD_DIM + 1]))
    return outs


def _pair_rows(a, b):
    return jnp.concatenate([a, b], axis=0).T


def _state_to_blockdiag_t(s_ref):
    rows = []
    for hd in range(C_HEADS):
        pieces = []
        if hd:
            pieces.append(jnp.zeros((C_DK, C_DV * hd), F32))
        pieces.append(s_ref[hd])
        if hd < C_HEADS - 1:
            pieces.append(jnp.zeros((C_DK, C_DV * (C_HEADS - 1 - hd)), F32))
        rows.append(jnp.concatenate(pieces, axis=-1))
    return jnp.concatenate(rows, axis=0).T


def _blockdiag_t_to_state(s_t, out_ref):
    s = s_t.T
    for hd in range(C_HEADS):
        out_ref[hd] = s[C_DK * hd:C_DK * (hd + 1), C_DV * hd:C_DV * (hd + 1)]


def _mix_sequence(cached, layer, first_state, n_aliased, lam_init, seq, qt, qb, *refs):
    it = iter(refs)
    x_ref, mod_ref = next(it), next(it)
    qa_ref, ka_ref, qb_ref, kb_ref = (next(it) for _ in range(4))
    cq_ref, ck_ref, cv_ref, gf_ref, gb_ref, su_ref = (next(it) for _ in range(6))
    vat_ref, vbt_ref = next(it), next(it)
    if cached:
        cakt_ref, cavt_ref, cbkt_ref, cbvt_ref, s0f_ref, s0b_ref = (next(it) for _ in range(6))
        qstat_ref, aqg_ref = next(it), next(it)
    wout_ref, gpost_ref, bog_ref, cog_ref, lamp_ref = (next(it) for _ in range(5))
    for _ in range(n_aliased):
        next(it)
    y_ref = next(it)
    if not cached:
        sf_ref, sb_ref = next(it), next(it)
    kA_s, vtA_s, kB_s, vtB_s, oc_s, mixed_s, kn2_s = (next(it) for _ in range(7))

    lk = kA_s.shape[1]
    past = lk - seq
    t = pl.program_id(1)

    def once_per_sequence(body):
        return body() if seq == qt else pl.when(t == 0)(body)

    @once_per_sequence
    def _per_sequence():
        ones_row = jnp.where(_iota((VT_ROWS - HEAD_DIM, lk), 0) == 0, 1.0, 0.0).astype(BF16)
        for k_new, kt_cache, k_dst, vt_new, vt_cache, vt_dst in (
                (ka_ref, cakt_ref if cached else None, kA_s, vat_ref, cavt_ref if cached else None, vtA_s),
                (kb_ref, cbkt_ref if cached else None, kB_s, vbt_ref, cbvt_ref if cached else None, vtB_s)):
            n_heads = k_dst.shape[0]
            if cached:
                for h0 in range(0, n_heads, 2):
                    pair = jnp.concatenate([kt_cache[h0], kt_cache[h0 + 1]], axis=0).T
                    k_dst[h0, 0:past, :] = pair[:, 0:HEAD_DIM].astype(BF16)
                    k_dst[h0 + 1, 0:past, :] = pair[:, HEAD_DIM:2 * HEAD_DIM].astype(BF16)
            for hd in range(n_heads):
                k_dst[hd, past:lk, :] = k_new[hd]
                if cached:
                    vt_dst[hd, 0:HEAD_DIM, 0:past] = vt_cache[hd].astype(BF16)
                vt_dst[hd, 0:HEAD_DIM, past:lk] = vt_new[hd].astype(BF16)
                vt_dst[hd, HEAD_DIM:VT_ROWS, :] = ones_row
        bounded_jobs = ([(kA_s, g) for g in range(A_KV_HEADS)] + [(kB_s, h) for h in range(B_HEADS)]
                        if lk > KEY_TILE else [])
        for j, (k_dst, hd) in enumerate(bounded_jobs):
            kf = k_dst[hd].astype(F32)
            kn2 = jnp.max(jnp.sum(kf * kf, axis=-1, keepdims=True), axis=0, keepdims=True)
            kn2_s[j:j + 1, :] = jnp.broadcast_to(kn2, (1, LANES))

        if cached:
            s0f, s0b = _state_to_blockdiag_t(s0f_ref), _state_to_blockdiag_t(s0b_ref)
        else:
            s0f = jnp.zeros((C_WIDTH, C_KW), F32)
            s0b = s0f
        s_f, s_b = _gla_bidirectional(cq_ref, ck_ref, cv_ref, gf_ref, gb_ref, s0f, s0b, seq, oc_s)
        if not cached:
            for ref, s_t in ((sf_ref, s_f), (sb_ref, s_b)):
                if first_state:
                    for l2 in range(DEPTH):
                        if l2 == layer:
                            _blockdiag_t_to_state(s_t, ref.at[l2])
                        else:
                            ref[l2] = jnp.zeros(ref.shape[1:], F32)
                else:
                    _blockdiag_t_to_state(s_t, ref)
        for r0 in range(0, seq, GLA_BLOCK):
            oc = oc_s[r0:r0 + GLA_BLOCK, :]
            oc_s[r0:r0 + GLA_BLOCK, :] = oc * lax.rsqrt(_group_mean_sq(oc, 6) + EPS) * cog_ref[...]

    lam = (jnp.exp(jnp.sum(lamp_ref[0:1, :] * lamp_ref[1:2, :], axis=-1, keepdims=True))
           - jnp.exp(jnp.sum(lamp_ref[2:3, :] * lamp_ref[3:4, :], axis=-1, keepdims=True))
           + lam_init)

    def project_out():
        seq_rows = pl.ds(pl.multiple_of(t * qt, qt), qt)
        mixed_s[:, A_WIDTH + B_WIDTH:D_MIX] = oc_s[seq_rows, :]
        gate = mod_ref[:, 2 * D_MODEL:3 * D_MODEL]
        mixed = (mixed_s[...] * su_ref[...]).astype(BF16)
        y = _dot(mixed, wout_ref[...])
        yn = y * lax.rsqrt(jnp.mean(y * y, axis=-1, keepdims=True) + EPS) * gpost_ref[...]
        y_ref[...] = x_ref[...] + gate * yn

    def attn_block(i, carry):
        rows = pl.ds(i * qb, qb) if isinstance(i, int) else pl.ds(pl.multiple_of(i * qb, qb), qb)
        jobs = []
        for grp in range(A_KV_HEADS):
            q4 = jnp.concatenate([qa_ref[A_GROUP * grp + j, :, rows] for j in range(A_GROUP)], axis=-1)
            jobs.append((kA_s.at[grp], q4, vtA_s.at[grp]))
        dim = _iota((HEAD_DIM, qb), 0)
        for hd in range(B_HEADS):
            q = qb_ref[hd, :, rows]
            zero = jnp.zeros_like(q)
            q2 = jnp.concatenate([jnp.where(dim < B_QK_DIM, q, zero),
                                  jnp.where(dim >= B_QK_DIM, q, zero)], axis=-1)
            jobs.append((kB_s.at[hd], q2, vtB_s.at[hd]))

        def finish(outs):
            for grp in range(A_KV_HEADS):
                ot = outs[grp]
                for pair in range(A_GROUP // 2):
                    c0 = 2 * pair * qb
                    col = A_GROUP * HEAD_DIM * grp + 2 * HEAD_DIM * pair
                    mixed_s[rows, col:col + 2 * HEAD_DIM] = _pair_rows(ot[:, c0:c0 + qb],
                                                                       ot[:, c0 + qb:c0 + 2 * qb])
            obs = []
            for hd in range(B_HEADS):
                ot = outs[A_KV_HEADS + hd]
                ob = ot[:, 0:qb] - lam * ot[:, qb:2 * qb]
                obs.append(ob * lax.rsqrt(jnp.mean(ob * ob, axis=0, keepdims=True) + EPS))
            for pair in range(B_HEADS // 2):
                col = A_WIDTH + 2 * B_V_DIM * pair
                mixed_s[rows, col:col + 2 * B_V_DIM] = (_pair_rows(obs[2 * pair], obs[2 * pair + 1])
                                                        * bog_ref[...] * (1.0 - lam_init))
            if qt == qb:
                project_out()

        if lk <= KEY_TILE:
            finish(_attend_t(jobs))
            return carry

        qa_norm = (jnp.max(jnp.abs(aqg_ref[...]), axis=-1, keepdims=True)
                   * (HEAD_DIM ** 0.5 * HEAD_DIM ** -0.5 * LOG2E))
        stat_t = qstat_ref[:, rows]
        shifts, gaps = [], []
        for j, (_, q, _) in enumerate(jobs):
            k_norm = jnp.sqrt(kn2_s[j:j + 1, 0:1])
            if j < A_KV_HEADS:
                upper = qa_norm * k_norm * BOUND_SLACK
                shifts.append(jnp.broadcast_to(upper, (1, q.shape[1])))
                gaps.append(2.0 * upper)
            else:
                r0 = 2 * (j - A_KV_HEADS)
                r1 = r0 + 2 * B_HEADS
                qn2 = jnp.concatenate([stat_t[r0:r0 + 1], stat_t[r0 + 1:r0 + 2]], axis=-1)
                lower = jnp.concatenate([stat_t[r1:r1 + 1], stat_t[r1 + 1:r1 + 2]], axis=-1)
                upper = jnp.sqrt(qn2) * k_norm * BOUND_SLACK
                shifts.append(upper)
                gaps.append(jnp.max(upper - lower, axis=-1, keepdims=True))
        worst = functools.reduce(jnp.maximum, gaps)
        safe = worst[0, 0] <= SAFE_GAP

        @pl.when(safe)
        def _bounded():
            finish(_attend_t(jobs, shifts))

        @pl.when(jnp.logical_not(safe))
        def _exact_max():
            finish(_attend_t(jobs))

        return carry

    if qt != qb:
        for i in range(qt // qb):
            attn_block(i, 0)
        project_out()
    else:
        attn_block(0, 0)


def _mix_kernel(nseq, *params_and_refs):
    params, refs = params_and_refs[:N_MIX_PARAMS], params_and_refs[N_MIX_PARAMS:]
    if nseq == 1:
        return _mix_sequence(*params, *refs)
    n_aliased = params[3]
    shared = {1} | set(range(N_MIX_SEQ_INPUTS, N_MIX_SEQ_INPUTS + N_MIX_WEIGHTS + n_aliased))
    for bi in range(nseq):
        _mix_sequence(*params, *[r if idx in shared else r.at[bi] for idx, r in enumerate(refs)])


def _mixer(x, mod, per_batch_mod, layer, proj, wts, lam_init, cache, state_prev=None):
    bsz, seq, _ = x.shape
    cached = cache is not None
    past = cache[0].shape[4] if cached else 0
    qa, ka, qb, kb, cq, ck, cv, gf, gb, su, vat, vbt = proj[:12]
    qt = 512 if cached else 256
    qb_rows = 256
    nseq = 1 if cached else 4
    lead = None if nseq == 1 else nseq

    def per_layer(shape):
        return pl.BlockSpec((None,) + shape, lambda b, t: (layer,) + (0,) * len(shape))

    def layer_heads_t(n, length):
        return pl.BlockSpec((lead, None, n, HEAD_DIM, length), lambda b, t: (b, layer, 0, 0, 0))

    def heads_t(n):
        if cached:
            return pl.BlockSpec((lead, n, HEAD_DIM, seq), lambda b, t: (b, 0, 0, 0))
        return layer_heads_t(n, seq)

    def heads(n):
        return pl.BlockSpec((lead, n, seq, HEAD_DIM), lambda b, t: (b, 0, 0, 0))

    def head_tile(n):
        return pl.BlockSpec((lead, n, HEAD_DIM, qt), lambda b, t: (b, 0, 0, t))

    def rows(width):
        return pl.BlockSpec((lead, seq, width), lambda b, t: (b, 0, 0))

    def row_tile(width):
        return pl.BlockSpec((lead, qt, width), lambda b, t: (b, t, 0))

    mod_idx = (lambda b, t: (layer, b, 0, 0)) if per_batch_mod else (lambda b, t: (layer, 0, 0, 0))
    in_specs = [row_tile(D_MODEL), pl.BlockSpec((None, None, 1, 3 * D_MODEL), mod_idx),
                head_tile(A_HEADS), heads(A_KV_HEADS), head_tile(B_HEADS), heads(B_HEADS),
                rows(C_KW), rows(C_KW), rows(C_WIDTH), rows(C_KW), rows(C_KW), row_tile(D_MIX),
                heads_t(A_KV_HEADS), heads_t(B_HEADS)]
    args = [x, mod, qa, ka, qb, kb, cq, ck, cv, gf, gb, su, vat, vbt]
    if cached:
        state_in = pl.BlockSpec((None, None, C_HEADS, C_DK, C_DV), lambda b, t: (b, layer, 0, 0, 0))
        in_specs += [layer_heads_t(A_KV_HEADS, past), layer_heads_t(A_KV_HEADS, past),
                     layer_heads_t(B_HEADS, past), layer_heads_t(B_HEADS, past), state_in, state_in,
                     pl.BlockSpec((None, QSTAT_ROWS, qt), lambda b, t: (b, 0, t)),
                     per_layer((1, HEAD_DIM))]
        args += list(cache) + [proj[12], wts["aq_gain"]]
    in_specs += [pl.BlockSpec((None, D_MIX, D_MODEL), lambda b, t: (layer, 0, 0),
                              pipeline_mode=pl.Buffered(1)),
                 per_layer((1, D_MODEL)), per_layer((1, 2 * B_V_DIM)),
                 per_layer((1, C_WIDTH)), per_layer((4, B_QK_DIM))]
    args += [wts["w_out"], wts["g_post"], wts["b_out_gain"], wts["c_out_gain"], wts["lam_params"]]

    out_specs = [row_tile(D_MODEL)]
    out_shape = [jax.ShapeDtypeStruct((bsz, seq, D_MODEL), F32)]
    aliases = {}
    first_state = not cached and state_prev is None
    if not cached:
        if first_state:
            state_out = pl.BlockSpec((lead, DEPTH, C_HEADS, C_DK, C_DV), lambda b, t: (b, 0, 0, 0, 0))
        else:
            state_out = pl.BlockSpec((lead, None, C_HEADS, C_DK, C_DV),
                                     lambda b, t: (b, layer, 0, 0, 0))
            for j, buf in enumerate(state_prev):
                aliases[len(args)] = 1 + j
                in_specs.append(pl.BlockSpec(memory_space=pl.ANY))
                args.append(buf)
        out_specs += [state_out, state_out]
        out_shape += [jax.ShapeDtypeStruct((bsz, DEPTH, C_HEADS, C_DK, C_DV), F32)] * 2

    lk = past + seq
    per_seq = () if nseq == 1 else (nseq,)
    scratch = [pltpu.VMEM(per_seq + shape, dtype) for shape, dtype in (
        ((A_KV_HEADS, lk, HEAD_DIM), BF16), ((A_KV_HEADS, VT_ROWS, lk), BF16),
        ((B_HEADS, lk, HEAD_DIM), BF16), ((B_HEADS, VT_ROWS, lk), BF16),
        ((seq, C_WIDTH), F32), ((qt, D_MIX), F32), ((KN2_ROWS, LANES), F32))]
    return pl.pallas_call(
        functools.partial(_mix_kernel, nseq, cached, layer, first_state, len(aliases), lam_init, seq,
                          qt, qb_rows),
        grid=(bsz // nseq, seq // qt),
        in_specs=in_specs,
        out_specs=out_specs,
        out_shape=out_shape,
        scratch_shapes=scratch,
        input_output_aliases=aliases,
        compiler_params=pltpu.CompilerParams(
            dimension_semantics=("arbitrary", "arbitrary"), vmem_limit_bytes=VMEM_LIMIT),
        name="mixer_cached" if cached else "mixer",
    )(*args)


def _rope_tables(seq):
    t = jnp.arange(seq)
    pos_row = (t // GRID_W).astype(F32)
    pos_col = (t % GRID_W).astype(F32)

    def tables(half):
        freq = ROPE_THETA ** (-jnp.arange(half, dtype=F32) / half)
        ang_r = freq[:, None] * pos_row[None, :]
        ang_c = freq[:, None] * pos_col[None, :]
        cos = jnp.concatenate([jnp.cos(ang_r), jnp.cos(ang_r), jnp.cos(ang_c), jnp.cos(ang_c)], axis=0)
        sin = jnp.concatenate([-jnp.sin(ang_r), jnp.sin(ang_r), -jnp.sin(ang_c), jnp.sin(ang_c)], axis=0)
        reps = HEAD_DIM // (4 * half)
        return jnp.tile(cos, (reps, 1)), jnp.tile(sin, (reps, 1))

    cos_a, sin_a = tables(HEAD_DIM // 4)
    cos_b, sin_b = tables(B_QK_DIM // 4)
    return cos_a, sin_a, cos_b, sin_b


def _prepare_weights(g_pre, g_post, w_in, w_out, a_q_gain, a_k_gain, b_lambda_q1, b_lambda_k1,
                     b_lambda_q2, b_lambda_k2, b_out_gain, c_gate_w_fwd, c_gate_b_fwd, c_gate_w_bwd,
                     c_gate_b_bwd, c_out_gain):
    w_in_t = jnp.swapaxes(w_in, 1, 2).astype(BF16)
    pad = jnp.zeros((DEPTH, GATE_RANK, C_KW), F32)
    cw_f = jnp.concatenate([c_gate_w_fwd, pad], axis=1).astype(BF16)
    cw_b = jnp.concatenate([pad, c_gate_w_bwd], axis=1).astype(BF16)
    return {
        "g_pre": g_pre[:, None, :],
        "g_post": g_post[:, None, :],
        "w_in_t": w_in_t,
        "w_out": w_out.astype(BF16),
        "aq_gain": a_q_gain[:, None, :],
        "aq_gain_col": a_q_gain[:, :, None],
        "ak_gain_col": a_k_gain[:, :, None],
        "cw_f": cw_f,
        "cb_f": c_gate_b_fwd[:, None, :],
        "cw_b": cw_b,
        "cb_b": c_gate_b_bwd[:, None, :],
        "b_out_gain": jnp.tile(b_out_gain, (1, 2))[:, None, :],
        "c_out_gain": jnp.tile(c_out_gain, (1, C_HEADS))[:, None, :],
        "lam_params": jnp.stack([b_lambda_q1, b_lambda_k1, b_lambda_q2, b_lambda_k2], axis=1),
    }


def kernel(x_prompt, x_sample, c, cache_a_k, cache_a_v, cache_b_k, cache_b_v, state_c_fwd, state_c_bwd, c_ctx, w_mod, b_mod, g_pre, g_post, w_in, w_out, a_q_gain, a_k_gain, b_lambda_q1, b_lambda_k1, b_lambda_q2, b_lambda_k2, b_out_gain, c_gate_w_fwd, c_gate_b_fwd, c_gate_w_bwd, c_gate_b_bwd, c_out_gain):
    dec_batch = x_sample.shape[0]
    dec_seq = x_sample.shape[1]

    mod_rows = 16
    cvec = jnp.zeros((mod_rows, D_MODEL), F32).at[0:dec_batch].set(c).at[dec_batch].set(c_ctx)
    mod = _modulation(cvec, w_mod, b_mod)[:, :, None, :]
    mod_lat = mod[:, 0:dec_batch]
    mod_ctx = mod[:, dec_batch:dec_batch + 1]

    wts = _prepare_weights(g_pre, g_post, w_in, w_out, a_q_gain, a_k_gain, b_lambda_q1, b_lambda_k1,
                           b_lambda_q2, b_lambda_k2, b_out_gain, c_gate_w_fwd, c_gate_b_fwd,
                           c_gate_w_bwd, c_gate_b_bwd, c_out_gain)
    rope_tabs = _rope_tables(dec_seq)
    cache = tuple(jnp.swapaxes(a, -1, -2) for a in (cache_a_k, cache_a_v, cache_b_k, cache_b_v))
    cache += (state_c_fwd, state_c_bwd)

    y_p, y_s = x_prompt, x_sample
    kv_ctx = None
    states = None
    for l in range(DEPTH):
        lam_init = 0.8 - 0.6 * math.exp(-0.3 * l)
        proj_p = _in_projection(y_p, mod_ctx, False, l, wts, None, 4, x_prompt.shape[1], kv_ctx)
        kv_ctx = proj_p[10:14]
        y_p, *states = _mixer(y_p, mod_ctx, False, l, proj_p, wts, lam_init, None, states)

        proj_s = _in_projection(y_s, mod_lat, True, l, wts, rope_tabs, 1, 1024)
        (y_s,) = _mixer(y_s, mod_lat, True, l, proj_s, wts, lam_init, cache)

    va_t, vb_t, ka_t, kb_t = kv_ctx
    new_kv = [jnp.swapaxes(a, -1, -2) for a in (ka_t, va_t, kb_t, vb_t)]
    return (y_p, y_s, *new_kv, *states)
```
